```python
import jax, jax.numpy as jnp
from jax import lax
import numpy as np


D_MODEL = 1024
BATCH = 32
SEQ = 2048
DEPTH = 2

N_A_LAYERS = DEPTH // 2
N_B_LAYERS = DEPTH - N_A_LAYERS
HEAD_DIM = 64
MIX_WIDTH = D_MODEL
MEM_TOKENS = 256
MEM_HEADS = 4
MEM_WIDTH = MEM_HEADS * HEAD_DIM
POOL_WIDTH = MIX_WIDTH - MEM_WIDTH
POOL_WINDOWS = (2, 4, 8, 16)
POOL_GROUPS = len(POOL_WINDOWS)
POOL_GROUP_DIM = POOL_WIDTH // POOL_GROUPS
FOX_WIDTH = MIX_WIDTH - MEM_WIDTH
FOX_HEADS = FOX_WIDTH // HEAD_DIM
KV_SHARED_WIDTH = 2 * FOX_WIDTH + FOX_HEADS
Q_BLOCK = 128
N_EXPERTS = 32
TOP_K = 4
D_FF = D_MODEL
SWIGLU_ALPHA = 1.702
SWIGLU_LIMIT = 7.0
MOE_BLOCK = 512
DN_ALPHA = float((2 * DEPTH) ** 0.25)
DN_BETA = float((8 * DEPTH) ** -0.25)
LN_EPS = 1e-5
ATTN_SCALE = HEAD_DIM ** -0.5

kernel_name = 'hybrid_pool_fox_moe_deepnorm'


def layer_norm(x, g, b):
    xf = x.astype(jnp.float32)
    mu = jnp.mean(xf, axis=-1, keepdims=True)
    var = jnp.mean(jnp.square(xf - mu), axis=-1, keepdims=True)
    y = (xf - mu) * lax.rsqrt(var + LN_EPS) * g.astype(jnp.float32) + b.astype(jnp.float32)
    return y.astype(x.dtype)


def multi_scale_pool(u, pool_w, pool_scale):
    B, S, _ = u.shape
    ug = u.reshape(B, S, POOL_GROUPS, POOL_GROUP_DIM).astype(jnp.float32)
    cs = jnp.cumsum(ug, axis=1)
    t = jnp.arange(S)
    outs = []
    for g, w in enumerate(POOL_WINDOWS):
        c = cs[:, :, g]
        prev = jnp.pad(c, ((0, 0), (w, 0), (0, 0)))[:, :S]
        cnt = jnp.minimum(t + 1, w).astype(jnp.float32)[None, :, None]
        outs.append((c - prev) / cnt - ug[:, :, g])
    d = jnp.stack(outs, axis=2).astype(u.dtype)
    y = jnp.einsum('bsgc,gcd->bsgd', d, pool_w).reshape(B, S, POOL_WIDTH)
    return y * pool_scale


def memory_attention(q_flat, mem, w_mem_kv):
    B, S, _ = q_flat.shape
    M = mem.shape[1]
    q = q_flat.reshape(B, S, MEM_HEADS, HEAD_DIM)
    kv = mem @ w_mem_kv
    mk = kv[..., :MEM_WIDTH].reshape(B, M, MEM_HEADS, HEAD_DIM)
    mv = kv[..., MEM_WIDTH:].reshape(B, M, MEM_HEADS, HEAD_DIM)
    s = jnp.einsum('bshd,bmhd->bhsm', q, mk).astype(jnp.float32) * ATTN_SCALE
    p = jax.nn.softmax(s, axis=-1).astype(mv.dtype)
    o = jnp.einsum('bhsm,bmhd->bshd', p, mv)
    return o.reshape(B, S, MEM_WIDTH)


def shared_kv(h, kv_w, fgate_b):
    B, S, _ = h.shape
    proj = h @ kv_w
    k = proj[..., :FOX_WIDTH].reshape(B, S, FOX_HEADS, HEAD_DIM).transpose(0, 2, 1, 3)
    v = proj[..., FOX_WIDTH:2 * FOX_WIDTH].reshape(B, S, FOX_HEADS, HEAD_DIM).transpose(0, 2, 1, 3)
    logf = jax.nn.log_sigmoid((proj[..., 2 * FOX_WIDTH:] + fgate_b).astype(jnp.float32))
    c = jnp.cumsum(logf, axis=1).transpose(0, 2, 1)
    return k, v, c


def forgetting_attention(q_flat, k, v, c):
    B, S, _ = q_flat.shape
    q = q_flat.reshape(B, S, FOX_HEADS, HEAD_DIM).transpose(0, 2, 1, 3)
    outs = []
    for i in range(S // Q_BLOCK):
        q0 = i * Q_BLOCK
        q1 = q0 + Q_BLOCK
        s = jnp.einsum('bhqd,bhkd->bhqk', q[:, :, q0:q1], k[:, :, :q1]).astype(jnp.float32) * ATTN_SCALE
        s = s + c[:, :, q0:q1, None] - c[:, :, None, :q1]
        causal = jnp.arange(q1)[None, :] <= jnp.arange(q0, q1)[:, None]
        s = jnp.where(causal, s, -jnp.inf)
        p = jax.nn.softmax(s, axis=-1).astype(v.dtype)
        outs.append(jnp.einsum('bhqk,bhkd->bhqd', p, v[:, :, :q1]))
    o = jnp.concatenate(outs, axis=2)
    return o.transpose(0, 2, 1, 3).reshape(B, S, FOX_WIDTH)


def clamped_swiglu(hdn):
    x_glu = jnp.minimum(hdn[..., ::2], SWIGLU_LIMIT)
    x_lin = jnp.clip(hdn[..., 1::2], -SWIGLU_LIMIT, SWIGLU_LIMIT)
    return x_glu * jax.nn.sigmoid(SWIGLU_ALPHA * x_glu) * (x_lin + 1.0)


def moe(x, router_w, router_b, w1, b1, w2, b2):
    B, S, D = x.shape
    T = B * S
    xt = x.reshape(T, D)
    logits = (xt @ router_w + router_b).astype(jnp.float32)
    top_vals, top_idx = lax.top_k(logits, TOP_K)
    gates = jax.nn.softmax(top_vals, axis=-1).astype(x.dtype)
    n_assign = T * TOP_K
    e_flat = top_idx.reshape(-1).astype(jnp.int32)
    tok_flat = jnp.arange(n_assign, dtype=jnp.int32) // TOP_K
    g_flat = gates.reshape(-1)
    order = jnp.argsort(e_flat)
    e_sorted = e_flat[order]
    tok_sorted = tok_flat[order]
    g_sorted = g_flat[order]
    counts = jnp.bincount(e_flat, length=N_EXPERTS)
    padded = ((counts + MOE_BLOCK - 1) // MOE_BLOCK) * MOE_BLOCK
    start = jnp.cumsum(counts) - counts
    pad_end = jnp.cumsum(padded)
    pad_start = pad_end - padded
    dest = pad_start[e_sorted] + (jnp.arange(n_assign, dtype=jnp.int32) - start[e_sorted])
    n_blocks = -(-n_assign // MOE_BLOCK) + N_EXPERTS
    n_slots = n_blocks * MOE_BLOCK
    slot_tok = jnp.full((n_slots,), T, dtype=jnp.int32).at[dest].set(tok_sorted)
    slot_gate = jnp.zeros((n_slots,), x.dtype).at[dest].set(g_sorted)
    block_expert = jnp.clip(jnp.searchsorted(pad_end, jnp.arange(n_blocks) * MOE_BLOCK, side='right'),
                            0, N_EXPERTS - 1)
    xpad = jnp.concatenate([xt, jnp.zeros((1, D), x.dtype)], axis=0)

    def step(acc, blk):
        e, tok, g = blk
        hdn = xpad[tok] @ w1[e] + b1[e]
        y = (clamped_swiglu(hdn) @ w2[e] + b2[e]) * g[:, None]
        return acc.at[tok].add(y), None

    acc0 = jnp.zeros((T + 1, D), x.dtype)
    acc, _ = lax.scan(step, acc0, (block_expert, slot_tok.reshape(n_blocks, MOE_BLOCK),
                                   slot_gate.reshape(n_blocks, MOE_BLOCK)))
    return acc[:T].reshape(B, S, D)


def setup_inputs(seed: int = 0) -> dict:
    key = jax.random.key(seed)
    ks = jax.random.split(key, 24)

    def nrm(k, shape, scale):
        return jax.random.normal(k, shape, jnp.float32) * scale

    D, E, F = D_MODEL, N_EXPERTS, D_FF
    kv_cols = jnp.concatenate([jnp.ones((FOX_WIDTH,)), jnp.full((FOX_WIDTH,), DN_BETA),
                               jnp.ones((FOX_HEADS,))]).astype(jnp.float32)
    mem_cols = jnp.concatenate([jnp.ones((MEM_WIDTH,)), jnp.full((MEM_WIDTH,), DN_BETA)]).astype(jnp.float32)
    return {
        'x': nrm(ks[0], (BATCH, SEQ, D), 1.0),
        'mem': nrm(ks[1], (BATCH, MEM_TOKENS, D), 1.0),
        'ln1_g': 1.0 + nrm(ks[2], (DEPTH, D), 0.02),
        'ln1_b': nrm(ks[3], (DEPTH, D), 0.02),
        'ln2_g': 1.0 + nrm(ks[4], (DEPTH, D), 0.02),
        'ln2_b': nrm(ks[5], (DEPTH, D), 0.02),
        'a_w_in': nrm(ks[6], (N_A_LAYERS, D, MIX_WIDTH), D ** -0.5),
        'pool_w': nrm(ks[7], (N_A_LAYERS, POOL_GROUPS, POOL_GROUP_DIM, POOL_GROUP_DIM), POOL_GROUP_DIM ** -0.5),
        'pool_scale': 1.0 + nrm(ks[8], (N_A_LAYERS, POOL_WIDTH), 0.1),
        'a_mem_kv': nrm(ks[9], (N_A_LAYERS, D, 2 * MEM_WIDTH), D ** -0.5) * mem_cols,
        'a_w_out': nrm(ks[10], (N_A_LAYERS, MIX_WIDTH, D), MIX_WIDTH ** -0.5 * DN_BETA),
        'kv_w': nrm(ks[11], (D, KV_SHARED_WIDTH), D ** -0.5) * kv_cols,
        'fgate_b': jax.random.uniform(ks[12], (FOX_HEADS,), jnp.float32, 1.0, 4.0),
        'b_w_in': nrm(ks[13], (N_B_LAYERS, D, MIX_WIDTH), D ** -0.5),
        'b_mem_kv': nrm(ks[14], (N_B_LAYERS, D, 2 * MEM_WIDTH), D ** -0.5) * mem_cols,
        'b_w_out': nrm(ks[15], (N_B_LAYERS, MIX_WIDTH, D), MIX_WIDTH ** -0.5 * DN_BETA),
        'router_w': nrm(ks[16], (DEPTH, D, E), D ** -0.5),
        'router_b': nrm(ks[17], (DEPTH, E), 0.01),
        'moe_w1': nrm(ks[18], (DEPTH, E, D, 2 * F), D ** -0.5 * DN_BETA),
        'moe_b1': nrm(ks[19], (DEPTH, E, 2 * F), 0.01),
        'moe_w2': nrm(ks[20], (DEPTH, E, F, D), F ** -0.5 * DN_BETA),
        'moe_b2': nrm(ks[21], (DEPTH, E, D), 0.01),
    }


def reference(x, mem, ln1_g, ln1_b, ln2_g, ln2_b, a_w_in, pool_w, pool_scale, a_mem_kv, a_w_out,
              kv_w, fgate_b, b_w_in, b_mem_kv, b_w_out, router_w, router_b, moe_w1, moe_b1, moe_w2, moe_b2):
    h = x
    k_sh = v_sh = c_sh = None
    for l in range(DEPTH):
        if l < N_A_LAYERS:
            proj = h @ a_w_in[l]
            pooled = multi_scale_pool(proj[..., :POOL_WIDTH], pool_w[l], pool_scale[l])
            memo = memory_attention(proj[..., POOL_WIDTH:], mem, a_mem_kv[l])
            mix = jnp.concatenate([pooled, memo], axis=-1) @ a_w_out[l]
        else:
            j = l - N_A_LAYERS
            if j == 0:
                k_sh, v_sh, c_sh = shared_kv(h, kv_w, fgate_b)
            proj = h @ b_w_in[j]
            fox = forgetting_attention(proj[..., :FOX_WIDTH], k_sh, v_sh, c_sh)
            memo = memory_attention(proj[..., FOX_WIDTH:], mem, b_mem_kv[j])
            mix = jnp.concatenate([fox, memo], axis=-1) @ b_w_out[j]
        h = layer_norm(DN_ALPHA * h + mix, ln1_g[l], ln1_b[l])
        ffn = moe(h, router_w[l], router_b[l], moe_w1[l], moe_b1[l], moe_w2[l], moe_b2[l])
        h = layer_norm(DN_ALPHA * h + ffn, ln2_g[l], ln2_b[l])
    return h
```

```python
import functools
import jax
import jax.numpy as jnp
from jax import lax
from jax.experimental import pallas as pl
from jax.experimental.pallas import tpu as pltpu
from jax.experimental.pallas import tpu_sc as plsc

HEAD_DIM = 64
MEM_HEADS = 4
MEM_WIDTH = MEM_HEADS * HEAD_DIM
POOL_WINDOWS = (2, 4, 8, 16)
TOP_K = 4
SWIGLU_ALPHA = 1.702
SWIGLU_LIMIT = 7.0
MOE_BLOCK = 512
LN_EPS = 1e-5
ATTN_SCALE = HEAD_DIM ** -0.5

V7X_LANES = 128
V7X_VMEM_LIMIT_BYTES = 56 * 1024 * 1024
POOL_HISTORY = 32
SC_CHUNK = 32
SEQ_BLOCK = 512
Q_BLOCK = 256

F32 = jnp.float32
BF16 = jnp.bfloat16
NEG_INF = float("-inf")


def _dot(a, b):
    return jnp.dot(a, b, preferred_element_type=F32)


def _dot_nt(a, b):
    return lax.dot_general(a, b, (((1,), (1,)), ((), ())), preferred_element_type=F32)


def _lane_iota(shape):
    return lax.broadcasted_iota(jnp.int32, shape, len(shape) - 1)


def _pair_attention(q2, keys, vals, bias=None):
    lane = _lane_iota(q2.shape)
    first = lane < HEAD_DIM
    outs = []
    for sel in (first, jnp.logical_not(first)):
        qh = jnp.where(sel, q2, 0.0).astype(BF16)
        s = _dot_nt(qh, keys)
        m = jnp.max(s, axis=-1, keepdims=True)
        p = jnp.exp(s - m)
        l = jnp.sum(p, axis=-1, keepdims=True)
        outs.append(_dot(p.astype(BF16), vals) / l)
    return jnp.where(first, outs[0], outs[1])


def _memory_attention(proj, kv_ref, cat_ref, q_off):
    for p in range(MEM_HEADS // 2):
        lo = p * V7X_LANES
        q2 = proj[:, q_off + lo:q_off + lo + V7X_LANES] * ATTN_SCALE
        keys = kv_ref[:, lo:lo + V7X_LANES]
        vals = kv_ref[:, MEM_WIDTH + lo:MEM_WIDTH + lo + V7X_LANES]
        cat_ref[:, q_off + lo:q_off + lo + V7X_LANES] = _pair_attention(q2, keys, vals).astype(BF16)


def _layer_norm(z, g, b):
    mu = jnp.mean(z, axis=-1, keepdims=True)
    zc = z - mu
    var = jnp.mean(zc * zc, axis=-1, keepdims=True)
    return zc * lax.rsqrt(var + LN_EPS) * g + b


def _mix_epilogue(x, cat_ref, w_out_ref, g_ref, b_ref, rwt_ref, rb_ref, dn_alpha, h_out_ref, idx_ref, gcol_ref):
    mix = _dot(cat_ref[...], w_out_ref[...])
    h1 = _layer_norm(dn_alpha * x + mix, g_ref[...], b_ref[...])
    h_out_ref[...] = h1
    logits = lax.dot_general(rwt_ref[...], h1, (((1,), (1,)), ((), ())),
                             precision=lax.Precision.HIGHEST, preferred_element_type=F32) + rb_ref[...]
    n_exp, t = logits.shape
    rows = lax.broadcasted_iota(jnp.int32, (n_exp, t), 0)
    vals, idxs = [], []
    for _ in range(TOP_K):
        m = jnp.max(logits, axis=0, keepdims=True)
        ix = jnp.min(jnp.where(logits == m, rows, n_exp), axis=0, keepdims=True)
        vals.append(m)
        idxs.append(ix)
        logits = jnp.where(rows == ix, NEG_INF, logits)
    exps = [jnp.exp(v - vals[0]) for v in vals]
    denom = exps[0] + exps[1] + exps[2] + exps[3]
    for k in range(TOP_K):
        idx_ref[k:k + 1, :] = idxs[k]
    lane_rows = lax.broadcasted_iota(jnp.int32, (V7X_LANES, t), 0)
    gates_t = jnp.zeros((V7X_LANES, t), F32)
    for k in range(TOP_K):
        gates_t = jnp.where(lane_rows == k, exps[k] / denom, gates_t)
    gcol_ref[...] = gates_t.T


def _pool_layer_kernel(dn_alpha, h_ref, mem_ref, w_in_ref, wbd_ref, pscale_ref, mem_kv_ref, w_out_ref, g_ref, b_ref,
                       rwt_ref, rb_ref, h_out_ref, idx_ref, gcol_ref,
                       ue_ref, w2_ref, w4_ref, w8_ref, kv_ref, cat_ref):
    s = pl.program_id(1)
    ts = h_ref.shape[0]
    pool_width = wbd_ref.shape[0]
    hist = POOL_HISTORY

    @pl.when(s == 0)
    def _():
        ue_ref[0:hist, :] = jnp.zeros((hist, pool_width), F32)
        kv_ref[...] = _dot(mem_ref[0].astype(BF16), mem_kv_ref[...]).astype(BF16)

    x = h_ref[...]
    proj = _dot(x.astype(BF16), w_in_ref[...])
    u = proj[:, :pool_width]
    ue_ref[hist:, :] = u
    n = ts + hist
    w2_ref[8:n, :] = ue_ref[8:n, :] + ue_ref[7:n - 1, :]
    w4_ref[16:n, :] = w2_ref[16:n, :] + w2_ref[14:n - 2, :]
    w8_ref[24:n, :] = w4_ref[24:n, :] + w4_ref[20:n - 4, :]
    w16 = w8_ref[hist:n, :] + w8_ref[hist - 8:n - 8, :]
    group_dim = pool_width // len(POOL_WINDOWS)
    col = _lane_iota((ts, pool_width))
    wsum = jnp.where(col < group_dim, w2_ref[hist:n, :],
                     jnp.where(col < 2 * group_dim, w4_ref[hist:n, :],
                               jnp.where(col < 3 * group_dim, w8_ref[hist:n, :], w16)))
    window = jnp.where(col < group_dim, POOL_WINDOWS[0],
                       jnp.where(col < 2 * group_dim, POOL_WINDOWS[1],
                                 jnp.where(col < 3 * group_dim, POOL_WINDOWS[2], POOL_WINDOWS[3])))
    tpos = s * ts + lax.broadcasted_iota(jnp.int32, (ts, pool_width), 0) + 1
    cnt = jnp.minimum(tpos, window).astype(F32)
    d = wsum / cnt - u
    ue_ref[0:hist, :] = u[ts - hist:, :]
    pooled = _dot(d.astype(BF16), wbd_ref[...]) * pscale_ref[...]
    cat_ref[:, :pool_width] = pooled.astype(BF16)
    _memory_attention(proj, kv_ref, cat_ref, pool_width)
    _mix_epilogue(x, cat_ref, w_out_ref, g_ref, b_ref, rwt_ref, rb_ref, dn_alpha, h_out_ref, idx_ref, gcol_ref)


def _shared_kv_kernel(h_ref, kvw_ref, fb_ref, tri_ref, k_ref, v_ref, ccol_ref, crow_ref, carry_ref):
    s = pl.program_id(1)
    width = k_ref.shape[1]

    @pl.when(s == 0)
    def _():
        carry_ref[...] = jnp.zeros(carry_ref.shape, F32)

    proj = _dot(h_ref[...].astype(BF16), kvw_ref[...])
    k_ref[...] = proj[:, :width].astype(BF16)
    v_ref[...] = proj[:, width:2 * width].astype(BF16)
    z = proj[:, 2 * width:] + fb_ref[...]
    logf = jnp.minimum(z, 0.0) - jnp.log(1.0 + jnp.exp(-jnp.abs(z)))
    hi = logf.astype(BF16)
    r1 = logf - hi.astype(F32)
    mid = r1.astype(BF16)
    lo = (r1 - mid.astype(F32)).astype(BF16)
    tri = tri_ref[...]
    c = _dot(tri, hi) + _dot(tri, mid) + _dot(tri, lo) + carry_ref[0:1, :]
    ts = c.shape[0]
    carry_ref[0:1, :] = c[ts - 1:ts, :]
    ccol_ref[...] = c
    crow_ref[0] = c.T[:crow_ref.shape[1], :]


def _fox_layer_kernel(dn_alpha, h_ref, mem_ref, k_ref, v_ref, ccol_ref, crow_ref, w_in_ref, mem_kv_ref, w_out_ref,
                      g_ref, b_ref, rwt_ref, rb_ref, h_out_ref, idx_ref, gcol_ref, kv_ref, cat_ref):
    i = pl.program_id(1)
    tq = h_ref.shape[0]
    fox_width = k_ref.shape[2]
    n_pairs = fox_width // V7X_LANES

    @pl.when(i == 0)
    def _():
        kv_ref[...] = _dot(mem_ref[0].astype(BF16), mem_kv_ref[...]).astype(BF16)

    x = h_ref[...]
    proj = _dot(x.astype(BF16), w_in_ref[...])
    lane = _lane_iota((tq, V7X_LANES))
    first = lane < HEAD_DIM
    causal = lax.broadcasted_iota(jnp.int32, (tq, tq), 1) <= lax.broadcasted_iota(jnp.int32, (tq, tq), 0)
    c_t_all = ccol_ref[...]

    for p in range(n_pairs):
        lo = p * V7X_LANES
        q2 = proj[:, lo:lo + V7X_LANES] * ATTN_SCALE
        q_heads = (jnp.where(first, q2, 0.0).astype(BF16), jnp.where(first, 0.0, q2).astype(BF16))
        c_t = tuple(c_t_all[:, 2 * p + j:2 * p + j + 1] for j in range(2))

        def tile(j, carry, masked):
            start = pl.multiple_of(j * tq, tq)
            kb = k_ref[0, pl.ds(start, tq), lo:lo + V7X_LANES]
            vb = v_ref[0, pl.ds(start, tq), lo:lo + V7X_LANES]
            new = []
            for hd in range(2):
                m_old, l_old, acc_old = carry[3 * hd:3 * hd + 3]
                c_s = crow_ref[0, 2 * p + hd:2 * p + hd + 1, pl.ds(start, tq)]
                sc = _dot_nt(q_heads[hd], kb) + c_t[hd] - c_s
                if masked:
                    sc = jnp.where(causal, sc, NEG_INF)
                m_new = jnp.maximum(m_old, jnp.max(sc, axis=-1, keepdims=True))
                alpha = jnp.exp(m_old - m_new)
                pr = jnp.exp(sc - m_new)
                l_new = alpha * l_old + jnp.sum(pr, axis=-1, keepdims=True)
                acc_new = alpha * acc_old + _dot(pr.astype(BF16), vb)
                new += [m_new, l_new, acc_new]
            return tuple(new)

        init = (jnp.full((tq, 1), NEG_INF, F32), jnp.zeros((tq, 1), F32), jnp.zeros((tq, V7X_LANES), F32)) * 2
        carry = tile(i, init, True)
        carry = lax.fori_loop(0, i, lambda j, c: tile(j, c, False), carry)
        o2 = jnp.where(first, carry[2] / carry[1], carry[5] / carry[4])
        cat_ref[:, lo:lo + V7X_LANES] = o2.astype(BF16)

    _memory_attention(proj, kv_ref, cat_ref, fox_width)
    _mix_epilogue(x, cat_ref, w_out_ref, g_ref, b_ref, rwt_ref, rb_ref, dn_alpha, h_out_ref, idx_ref, gcol_ref)


def _expert_kernel(be_ref, nact_ref, x_ref, w1_ref, b1_ref, w2_ref, b2_ref, y_ref):
    i = pl.program_id(0)

    @pl.when(i < nact_ref[0])
    def _():
        d_ff = w2_ref.shape[1]
        hdn = _dot(x_ref[...].astype(BF16), w1_ref[0]) + b1_ref[0]
        x_glu = jnp.minimum(hdn[:, :d_ff], SWIGLU_LIMIT)
        x_lin = jnp.clip(hdn[:, d_ff:], -SWIGLU_LIMIT, SWIGLU_LIMIT)
        act = x_glu * jax.nn.sigmoid(SWIGLU_ALPHA * x_glu) * (x_lin + 1.0)
        y_ref[...] = _dot(act.astype(BF16), w2_ref[0]) + b2_ref[0]

    @pl.when(i >= nact_ref[0])
    def _():
        y_ref[...] = jnp.zeros(y_ref.shape, F32)


def _combine_kernel(dn_alpha, h_ref, y_ref, gcol_ref, g_ref, b_ref, o_ref):
    gates = gcol_ref[...]
    ffn = gates[:, 0:1] * y_ref[0]
    for k in range(1, TOP_K):
        ffn = ffn + gates[:, k:k + 1] * y_ref[k]
    o_ref[...] = _layer_norm(dn_alpha * h_ref[...] + ffn, g_ref[...], b_ref[...])


def _sc_gather_rows(table, idx):
    n, = idx.shape
    _, d = table.shape
    info = plsc.get_sparse_core_info()
    n_workers = info.num_cores * info.num_subcores
    per_worker = n // n_workers
    assert n % (n_workers * SC_CHUNK) == 0
    mesh = plsc.VectorSubcoreMesh(core_axis_name="core", subcore_axis_name="subcore")

    @functools.partial(
        pl.kernel,
        out_type=jax.ShapeDtypeStruct((n, d), table.dtype),
        mesh=mesh,
        scratch_types=[
            pltpu.VMEM((SC_CHUNK,), jnp.int32),
            pltpu.VMEM((SC_CHUNK, d), table.dtype),
            pltpu.SemaphoreType.DMA,
        ],
    )
    def gather_kernel(table_hbm, idx_hbm, out_hbm, idx_v, rows_v, sem):
        worker = lax.axis_index("subcore") * info.num_cores + lax.axis_index("core")
        base = worker * per_worker

        @pl.loop(0, per_worker // SC_CHUNK)
        def _(c):
            off = pl.multiple_of(base + c * SC_CHUNK, SC_CHUNK)
            pltpu.sync_copy(idx_hbm.at[pl.ds(off, SC_CHUNK)], idx_v)
            pltpu.async_copy(table_hbm.at[idx_v], rows_v, sem).wait()
            pltpu.sync_copy(rows_v, out_hbm.at[pl.ds(off, SC_CHUNK)])

    return gather_kernel(table, idx)


def _const_spec(shape):
    zeros = (0,) * len(shape)
    return pl.BlockSpec(shape, lambda *_: zeros)


def _tc_params(n_axes):
    return pltpu.CompilerParams(dimension_semantics=("arbitrary",) * n_axes,
                                vmem_limit_bytes=V7X_VMEM_LIMIT_BYTES)


def _mix_out_shapes(t, d):
    return (jax.ShapeDtypeStruct((t, d), F32),
            jax.ShapeDtypeStruct((TOP_K, t), jnp.int32),
            jax.ShapeDtypeStruct((t, V7X_LANES), F32))


def _pool_layer(h, mem, w_in, wbd, pscale, mem_kv, w_out, ln_g, ln_b, rwt, rb, dn_alpha, seq):
    t, d = h.shape
    n_mem = mem.shape[1]
    pool_width = wbd.shape[0]
    ts = SEQ_BLOCK
    nsb = seq // ts
    row = lambda b, s: (b * nsb + s, 0)
    return pl.pallas_call(
        functools.partial(_pool_layer_kernel, dn_alpha),
        out_shape=_mix_out_shapes(t, d),
        grid=(t // seq, nsb),
        in_specs=[
            pl.BlockSpec((ts, d), row),
            pl.BlockSpec((1, n_mem, d), lambda b, s: (b, 0, 0)),
            _const_spec(w_in.shape), _const_spec(wbd.shape), _const_spec(pscale.shape), _const_spec(mem_kv.shape),
            _const_spec(w_out.shape), _const_spec(ln_g.shape), _const_spec(ln_b.shape), _const_spec(rwt.shape),
            _const_spec(rb.shape),
        ],
        out_specs=(pl.BlockSpec((ts, d), row),
                   pl.BlockSpec((TOP_K, ts), lambda b, s: (0, b * nsb + s)),
                   pl.BlockSpec((ts, V7X_LANES), row)),
        scratch_shapes=[pltpu.VMEM((ts + POOL_HISTORY, pool_width), F32)] * 4
        + [pltpu.VMEM((n_mem, 2 * MEM_WIDTH), BF16), pltpu.VMEM((ts, d), BF16)],
        compiler_params=_tc_params(2),
        name="pool_layer",
    )(h, mem, w_in, wbd, pscale, mem_kv, w_out, ln_g, ln_b, rwt, rb)


def _shared_kv(h, kvw, fb, seq, fox_width, n_heads_pad):
    t, d = h.shape
    ts = SEQ_BLOCK
    nsb = seq // ts
    tri = (lax.broadcasted_iota(jnp.int32, (ts, ts), 1) <= lax.broadcasted_iota(jnp.int32, (ts, ts), 0)).astype(BF16)
    row = lambda b, s: (b * nsb + s, 0)
    return pl.pallas_call(
        _shared_kv_kernel,
        out_shape=(jax.ShapeDtypeStruct((t, fox_width), BF16), jax.ShapeDtypeStruct((t, fox_width), BF16),
                   jax.ShapeDtypeStruct((t, V7X_LANES), F32),
                   jax.ShapeDtypeStruct((t // seq, n_heads_pad, seq), F32)),
        grid=(t // seq, nsb),
        in_specs=[pl.BlockSpec((ts, d), row), _const_spec(kvw.shape), _const_spec(fb.shape), _const_spec(tri.shape)],
        out_specs=(pl.BlockSpec((ts, fox_width), row), pl.BlockSpec((ts, fox_width), row),
                   pl.BlockSpec((ts, V7X_LANES), row),
                   pl.BlockSpec((1, n_heads_pad, ts), lambda b, s: (b, 0, s))),
        scratch_shapes=[pltpu.VMEM((8, V7X_LANES), F32)],
        compiler_params=_tc_params(2),
        name="shared_kv",
    )(h, kvw, fb, tri)


def _fox_layer(h, mem, k, v, ccol, crow, w_in, mem_kv, w_out, ln_g, ln_b, rwt, rb, dn_alpha, seq):
    t, d = h.shape
    n_mem = mem.shape[1]
    fox_width = k.shape[1]
    tq = Q_BLOCK
    nqb = seq // tq
    n_batch = t // seq
    row = lambda b, i: (b * nqb + i, 0)
    per_batch = lambda b, i: (b, 0, 0)
    return pl.pallas_call(
        functools.partial(_fox_layer_kernel, dn_alpha),
        out_shape=_mix_out_shapes(t, d),
        grid=(n_batch, nqb),
        in_specs=[
            pl.BlockSpec((tq, d), row),
            pl.BlockSpec((1, n_mem, d), per_batch),
            pl.BlockSpec((1, seq, fox_width), per_batch),
            pl.BlockSpec((1, seq, fox_width), per_batch),
            pl.BlockSpec((tq, V7X_LANES), row),
            pl.BlockSpec((1, crow.shape[1], seq), per_batch),
            _const_spec(w_in.shape), _const_spec(mem_kv.shape), _const_spec(w_out.shape), _const_spec(ln_g.shape),
            _const_spec(ln_b.shape), _const_spec(rwt.shape), _const_spec(rb.shape),
        ],
        out_specs=(pl.BlockSpec((tq, d), row),
                   pl.BlockSpec((TOP_K, tq), lambda b, i: (0, b * nqb + i)),
                   pl.BlockSpec((tq, V7X_LANES), row)),
        scratch_shapes=[pltpu.VMEM((n_mem, 2 * MEM_WIDTH), BF16), pltpu.VMEM((tq, d), BF16)],
        compiler_params=_tc_params(2),
        name="fox_layer",
    )(h, mem, k.reshape(n_batch, seq, fox_width), v.reshape(n_batch, seq, fox_width), ccol, crow,
      w_in, mem_kv, w_out, ln_g, ln_b, rwt, rb)


def _route(idx_t, n_experts):
    top_k, t = idx_t.shape
    n_assign = top_k * t
    e_flat = idx_t.reshape(-1)
    iota = jnp.arange(n_assign, dtype=jnp.int32)
    _, order = lax.sort((e_flat, iota), num_keys=1)
    _, inv = lax.sort((order, iota), num_keys=1)
    counts = jnp.sum((e_flat[None, :] == jnp.arange(n_experts, dtype=jnp.int32)[:, None]).astype(jnp.int32), axis=1)
    padded = ((counts + MOE_BLOCK - 1) // MOE_BLOCK) * MOE_BLOCK
    start = jnp.cumsum(counts) - counts
    pad_end = jnp.cumsum(padded)
    pad_start = pad_end - padded
    n_blocks = n_assign // MOE_BLOCK + n_experts
    block_expert = jnp.clip(jnp.searchsorted(pad_end, jnp.arange(n_blocks, dtype=jnp.int32) * MOE_BLOCK, side="right"),
                            0, n_experts - 1).astype(jnp.int32)
    slot = jnp.arange(n_blocks * MOE_BLOCK, dtype=jnp.int32)
    slot_e = jnp.repeat(block_expert, MOE_BLOCK)
    j = slot - pad_start[slot_e]
    valid = j < counts[slot_e]
    src = jnp.clip(start[slot_e] + j, 0, n_assign - 1)
    slot_tok = jnp.where(valid, order[src] % t, 0).astype(jnp.int32)
    pos = (pad_start[e_flat] + inv - start[e_flat]).astype(jnp.int32)
    n_active = (pad_end[-1:] // MOE_BLOCK).astype(jnp.int32)
    return slot_tok, pos, block_expert, n_active


def _experts(xs, block_expert, n_active, w1, b1, w2, b2):
    n_slots, d = xs.shape
    n_exp, _, two_f = w1.shape
    d_ff = two_f // 2
    grid_spec = pltpu.PrefetchScalarGridSpec(
        num_scalar_prefetch=2,
        grid=(n_slots // MOE_BLOCK,),
        in_specs=[
            pl.BlockSpec((MOE_BLOCK, d), lambda i, be, na: (i, 0)),
            pl.BlockSpec((1, d, two_f), lambda i, be, na: (be[i], 0, 0)),
            pl.BlockSpec((1, 1, two_f), lambda i, be, na: (be[i], 0, 0)),
            pl.BlockSpec((1, d_ff, d), lambda i, be, na: (be[i], 0, 0)),
            pl.BlockSpec((1, 1, d), lambda i, be, na: (be[i], 0, 0)),
        ],
        out_specs=pl.BlockSpec((MOE_BLOCK, d), lambda i, be, na: (i, 0)),
    )
    return pl.pallas_call(
        _expert_kernel,
        out_shape=jax.ShapeDtypeStruct((n_slots, d), F32),
        grid_spec=grid_spec,
        compiler_params=_tc_params(1),
        name="experts",
    )(block_expert, n_active, xs, w1, b1.reshape(n_exp, 1, two_f), w2, b2.reshape(n_exp, 1, d))


def _combine(h, yg, gcol, ln_g, ln_b, dn_alpha):
    t, d = h.shape
    ts = SEQ_BLOCK
    row = lambda i: (i, 0)
    return pl.pallas_call(
        functools.partial(_combine_kernel, dn_alpha),
        out_shape=jax.ShapeDtypeStruct((t, d), F32),
        grid=(t // ts,),
        in_specs=[pl.BlockSpec((ts, d), row), pl.BlockSpec((TOP_K, ts, d), lambda i: (0, i, 0)),
                  pl.BlockSpec((ts, V7X_LANES), row), _const_spec(ln_g.shape), _const_spec(ln_b.shape)],
        out_specs=pl.BlockSpec((ts, d), row),
        compiler_params=_tc_params(1),
        name="combine",
    )(h, yg.reshape(TOP_K, t, d), gcol, ln_g, ln_b)


def _moe(h1, idx_t, gcol, w1, b1, w2, b2, ln_g, ln_b, dn_alpha):
    n_exp = w1.shape[0]
    d_ff = w2.shape[1]
    slot_tok, pos, block_expert, n_active = _route(idx_t, n_exp)
    w1p = jnp.concatenate([w1[..., 0::2], w1[..., 1::2]], axis=-1).astype(BF16)
    b1p = jnp.concatenate([b1[..., 0::2], b1[..., 1::2]], axis=-1)
    xs = _sc_gather_rows(h1, slot_tok)
    y = _experts(xs, block_expert, n_active, w1p, b1p, w2.astype(BF16), b2)
    yg = _sc_gather_rows(y, pos)
    return _combine(h1, yg, gcol, ln_g, ln_b, dn_alpha)


def kernel(x, mem, ln1_g, ln1_b, ln2_g, ln2_b, a_w_in, pool_w, pool_scale, a_mem_kv, a_w_out, kv_w, fgate_b, b_w_in, b_mem_kv, b_w_out, router_w, router_b, moe_w1, moe_b1, moe_w2, moe_b2):
    n_batch, seq, d = x.shape
    t = n_batch * seq
    depth = ln1_g.shape[0]
    n_a = a_w_in.shape[0]
    dn_alpha = float((2 * depth) ** 0.25)
    n_exp = router_w.shape[2]
    fox_heads = fgate_b.shape[0]
    fox_width = fox_heads * HEAD_DIM
    n_heads_pad = 16
    assert seq % SEQ_BLOCK == 0 and seq % Q_BLOCK == 0 and fox_heads <= n_heads_pad

    h = x.reshape(t, d)
    row = lambda a: a.reshape(1, -1)
    shared = None
    for l in range(depth):
        rwt = router_w[l].T
        rb = router_b[l].reshape(n_exp, 1)
        if l < n_a:
            wbd = jax.scipy.linalg.block_diag(*[pool_w[l, g] for g in range(pool_w.shape[1])]).astype(BF16)
            h1, idx_t, gcol = _pool_layer(
                h, mem, a_w_in[l].astype(BF16), wbd, row(pool_scale[l]), a_mem_kv[l].astype(BF16),
                a_w_out[l].astype(BF16), row(ln1_g[l]), row(ln1_b[l]), rwt, rb, dn_alpha, seq)
        else:
            j = l - n_a
            if shared is None:
                pad = V7X_LANES - fox_heads
                kvw = jnp.pad(kv_w, ((0, 0), (0, pad))).astype(BF16)
                fb = jnp.pad(fgate_b, (0, pad)).reshape(1, V7X_LANES)
                shared = _shared_kv(h, kvw, fb, seq, fox_width, n_heads_pad)
            k_sh, v_sh, ccol, crow = shared
            h1, idx_t, gcol = _fox_layer(
                h, mem, k_sh, v_sh, ccol, crow, b_w_in[j].astype(BF16), b_mem_kv[j].astype(BF16),
                b_w_out[j].astype(BF16), row(ln1_g[l]), row(ln1_b[l]), rwt, rb, dn_alpha, seq)
        h = _moe(h1, idx_t, gcol, moe_w1[l], moe_b1[l], moe_w2[l], moe_b2[l], row(ln2_g[l]), row(ln2_b[l]), dn_alpha)
    return h.reshape(n_batch, seq, d)
```

```python
import functools
import jax
import jax.numpy as jnp
from jax import lax
from jax.experimental import pallas as pl
from jax.experimental.pallas import tpu as pltpu
from jax.experimental.pallas import tpu_sc as plsc

HEAD_DIM = 64
MEM_HEADS = 4
MEM_WIDTH = MEM_HEADS * HEAD_DIM
POOL_WINDOWS = (2, 4, 8, 16)
TOP_K = 4
SWIGLU_ALPHA = 1.702
SWIGLU_LIMIT = 7.0
MOE_BLOCK = 512
LN_EPS = 1e-5
ATTN_SCALE = HEAD_DIM ** -0.5

V7X_LANES = 128
V7X_VMEM_LIMIT_BYTES = 56 * 1024 * 1024
POOL_HISTORY = 32
SC_CHUNK = 32
SEQ_BLOCK = 512
Q_BLOCK = 256

F32 = jnp.float32
BF16 = jnp.bfloat16
NEG_INF = float("-inf")


def _dot(a, b):
    return jnp.dot(a, b, preferred_element_type=F32)


def _dot_nt(a, b):
    return lax.dot_general(a, b, (((1,), (1,)), ((), ())), preferred_element_type=F32)


def _lane_iota(shape):
    return lax.broadcasted_iota(jnp.int32, shape, len(shape) - 1)


def _pair_attention(q2, keys, vals, bias=None):
    lane = _lane_iota(q2.shape)
    first = lane < HEAD_DIM
    outs = []
    for sel in (first, jnp.logical_not(first)):
        qh = jnp.where(sel, q2, 0.0).astype(BF16)
        s = _dot_nt(qh, keys)
        m = jnp.max(s, axis=-1, keepdims=True)
        p = jnp.exp(s - m)
        l = jnp.sum(p, axis=-1, keepdims=True)
        outs.append(_dot(p.astype(BF16), vals) / l)
    return jnp.where(first, outs[0], outs[1])


def _memory_attention(proj, kv_ref, cat_ref, q_off):
    for p in range(MEM_HEADS // 2):
        lo = p * V7X_LANES
        q2 = proj[:, q_off + lo:q_off + lo + V7X_LANES] * ATTN_SCALE
        keys = kv_ref[:, lo:lo + V7X_LANES]
        vals = kv_ref[:, MEM_WIDTH + lo:MEM_WIDTH + lo + V7X_LANES]
        cat_ref[:, q_off + lo:q_off + lo + V7X_LANES] = _pair_attention(q2, keys, vals).astype(BF16)


def _layer_norm(z, g, b):
    mu = jnp.mean(z, axis=-1, keepdims=True)
    zc = z - mu
    var = jnp.mean(zc * zc, axis=-1, keepdims=True)
    return zc * lax.rsqrt(var + LN_EPS) * g + b


def _mix_epilogue(x, cat_ref, w_out_ref, g_ref, b_ref, rwt_ref, rb_ref, dn_alpha, h_out_ref, idx_ref, gcol_ref):
    mix = _dot(cat_ref[...], w_out_ref[...])
    h1 = _layer_norm(dn_alpha * x + mix, g_ref[...], b_ref[...])
    h_out_ref[...] = h1
    logits = lax.dot_general(rwt_ref[...], h1, (((1,), (1,)), ((), ())),
                             precision=lax.Precision.HIGHEST, preferred_element_type=F32) + rb_ref[...]
    n_exp, t = logits.shape
    rows = lax.broadcasted_iota(jnp.int32, (n_exp, t), 0)
    vals, idxs = [], []
    for _ in range(TOP_K):
        m = jnp.max(logits, axis=0, keepdims=True)
        ix = jnp.min(jnp.where(logits == m, rows, n_exp), axis=0, keepdims=True)
        vals.append(m)
        idxs.append(ix)
        logits = jnp.where(rows == ix, NEG_INF, logits)
    exps = [jnp.exp(v - vals[0]) for v in vals]
    denom = exps[0] + exps[1] + exps[2] + exps[3]
    for k in range(TOP_K):
        idx_ref[k:k + 1, :] = idxs[k]
    lane_rows = lax.broadcasted_iota(jnp.int32, (V7X_LANES, t), 0)
    gates_t = jnp.zeros((V7X_LANES, t), F32)
    for k in range(TOP_K):
        gates_t = jnp.where(lane_rows == k, exps[k] / denom, gates_t)
    gcol_ref[...] = gates_t.T


def _pool_layer_kernel(dn_alpha, h_ref, mem_ref, w_in_ref, wbd_ref, pscale_ref, mem_kv_ref, w_out_ref, g_ref, b_ref,
                       rwt_ref, rb_ref, h_out_ref, idx_ref, gcol_ref,
                       ue_ref, w2_ref, w4_ref, w8_ref, kv_ref, cat_ref):
    s = pl.program_id(1)
    ts = h_ref.shape[0]
    pool_width = wbd_ref.shape[0]
    hist = POOL_HISTORY

    @pl.when(s == 0)
    def _():
        ue_ref[0:hist, :] = jnp.zeros((hist, pool_width), F32)
        kv_ref[...] = _dot(mem_ref[0].astype(BF16), mem_kv_ref[...]).astype(BF16)

    x = h_ref[...]
    proj = _dot(x.astype(BF16), w_in_ref[...])
    u = proj[:, :pool_width]
    ue_ref[hist:, :] = u
    n = ts + hist
    w2_ref[8:n, :] = ue_ref[8:n, :] + ue_ref[7:n - 1, :]
    w4_ref[16:n, :] = w2_ref[16:n, :] + w2_ref[14:n - 2, :]
    w8_ref[24:n, :] = w4_ref[24:n, :] + w4_ref[20:n - 4, :]
    w16 = w8_ref[hist:n, :] + w8_ref[hist - 8:n - 8, :]
    group_dim = pool_width // len(POOL_WINDOWS)
    col = _lane_iota((ts, pool_width))
    wsum = jnp.where(col < group_dim, w2_ref[hist:n, :],
                     jnp.where(col < 2 * group_dim, w4_ref[hist:n, :],
                               jnp.where(col < 3 * group_dim, w8_ref[hist:n, :], w16)))
    window = jnp.where(col < group_dim, POOL_WINDOWS[0],
                       jnp.where(col < 2 * group_dim, POOL_WINDOWS[1],
                                 jnp.where(col < 3 * group_dim, POOL_WINDOWS[2], POOL_WINDOWS[3])))
    tpos = s * ts + lax.broadcasted_iota(jnp.int32, (ts, pool_width), 0) + 1
    cnt = jnp.minimum(tpos, window).astype(F32)
    d = wsum / cnt - u
    ue_ref[0:hist, :] = u[ts - hist:, :]
    pooled = _dot(d.astype(BF16), wbd_ref[...]) * pscale_ref[...]
    cat_ref[:, :pool_width] = pooled.astype(BF16)
    _memory_attention(proj, kv_ref, cat_ref, pool_width)
    _mix_epilogue(x, cat_ref, w_out_ref, g_ref, b_ref, rwt_ref, rb_ref, dn_alpha, h_out_ref, idx_ref, gcol_ref)


def _shared_kv_kernel(h_ref, kvw_ref, fb_ref, tri_ref, k_ref, v_ref, ccol_ref, crow_ref, carry_ref):
    s = pl.program_id(1)
    width = k_ref.shape[1]

    @pl.when(s == 0)
    def _():
        carry_ref[...] = jnp.zeros(carry_ref.shape, F32)

    proj = _dot(h_ref[...].astype(BF16), kvw_ref[...])
    k_ref[...] = proj[:, :width].astype(BF16)
    v_ref[...] = proj[:, width:2 * width].astype(BF16)
    z = proj[:, 2 * width:] + fb_ref[...]
    logf = jnp.minimum(z, 0.0) - jnp.log(1.0 + jnp.exp(-jnp.abs(z)))
    hi = logf.astype(BF16)
    r1 = logf - hi.astype(F32)
    mid = r1.astype(BF16)
    lo = (r1 - mid.astype(F32)).astype(BF16)
    tri = tri_ref[...]
    c = _dot(tri, hi) + _dot(tri, mid) + _dot(tri, lo) + carry_ref[0:1, :]
    ts = c.shape[0]
    carry_ref[0:1, :] = c[ts - 1:ts, :]
    ccol_ref[...] = c
    crow_ref[0] = c.T[:crow_ref.shape[1], :]


def _fox_layer_kernel(dn_alpha, h_ref, mem_ref, k_ref, v_ref, ccol_ref, crow_ref, w_in_ref, mem_kv_ref, w_out_ref,
                      g_ref, b_ref, rwt_ref, rb_ref, h_out_ref, idx_ref, gcol_ref, kv_ref, cat_ref):
    i = pl.program_id(1)
    tq = h_ref.shape[0]
    fox_width = k_ref.shape[2]
    n_pairs = fox_width // V7X_LANES

    @pl.when(i == 0)
    def _():
        kv_ref[...] = _dot(mem_ref[0].astype(BF16), mem_kv_ref[...]).astype(BF16)

    x = h_ref[...]
    proj = _dot(x.astype(BF16), w_in_ref[...])
    lane = _lane_iota((tq, V7X_LANES))
    first = lane < HEAD_DIM
    causal = lax.broadcasted_iota(jnp.int32, (tq, tq), 1) <= lax.broadcasted_iota(jnp.int32, (tq, tq), 0)
    c_t_all = ccol_ref[...]

    for p in range(n_pairs):
        lo = p * V7X_LANES
        q2 = proj[:, lo:lo + V7X_LANES] * ATTN_SCALE
        q_heads = (jnp.where(first, q2, 0.0).astype(BF16), jnp.where(first, 0.0, q2).astype(BF16))
        c_t = tuple(c_t_all[:, 2 * p + j:2 * p + j + 1] for j in range(2))

        def tile(j, carry, masked):
            start = pl.multiple_of(j * tq, tq)
            kb = k_ref[0, pl.ds(start, tq), lo:lo + V7X_LANES]
            vb = v_ref[0, pl.ds(start, tq), lo:lo + V7X_LANES]
            new = []
            for hd in range(2):
                m_old, l_old, acc_old = carry[3 * hd:3 * hd + 3]
                c_s = crow_ref[0, 2 * p + hd:2 * p + hd + 1, pl.ds(start, tq)]
                sc = _dot_nt(q_heads[hd], kb) + c_t[hd] - c_s
                if masked:
                    sc = jnp.where(causal, sc, NEG_INF)
                m_new = jnp.maximum(m_old, jnp.max(sc, axis=-1, keepdims=True))
                alpha = jnp.exp(m_old - m_new)
                pr = jnp.exp(sc - m_new)
                l_new = alpha * l_old + jnp.sum(pr, axis=-1, keepdims=True)
                acc_new = alpha * acc_old + _dot(pr.astype(BF16), vb)
                new += [m_new, l_new, acc_new]
            return tuple(new)

        init = (jnp.full((tq, 1), NEG_INF, F32), jnp.zeros((tq, 1), F32), jnp.zeros((tq, V7X_LANES), F32)) * 2
        carry = tile(i, init, True)
        carry = lax.fori_loop(0, i, lambda j, c: tile(j, c, False), carry)
        o2 = jnp.where(first, carry[2] / carry[1], carry[5] / carry[4])
        cat_ref[:, lo:lo + V7X_LANES] = o2.astype(BF16)

    _memory_attention(proj, kv_ref, cat_ref, fox_width)
    _mix_epilogue(x, cat_ref, w_out_ref, g_ref, b_ref, rwt_ref, rb_ref, dn_alpha, h_out_ref, idx_ref, gcol_ref)


def _expert_kernel(be_ref, nact_ref, x_ref, w1_ref, b1_ref, w2_ref, b2_ref, y_ref, w1s_ref, w2s_ref, act_ref):
    i = pl.program_id(0)
    active = i < nact_ref[0]
    two_f = w1_ref.shape[2]
    group = 2 * V7X_LANES
    n_groups = two_f // group

    @pl.when(jnp.logical_and(active, jnp.logical_or(i == 0, be_ref[i] != be_ref[jnp.maximum(i - 1, 0)])))
    def _():
        src = lax.broadcasted_iota(jnp.int32, (group, group), 0)
        dst = lax.broadcasted_iota(jnp.int32, (group, group), 1)
        want = jnp.where(dst < V7X_LANES, 2 * dst, 2 * (dst - V7X_LANES) + 1)
        perm = (src == want).astype(BF16)
        for g in range(n_groups):
            cols = slice(g * group, (g + 1) * group)
            w1s_ref[:, cols] = _dot(w1_ref[0, :, cols].astype(BF16), perm).astype(BF16)
        w2s_ref[...] = w2_ref[0].astype(BF16)

    @pl.when(active)
    def _():
        hdn = _dot(x_ref[...].astype(BF16), w1s_ref[...]) + b1_ref[0]
        for g in range(n_groups):
            x_glu = jnp.minimum(hdn[:, g * group:g * group + V7X_LANES], SWIGLU_LIMIT)
            x_lin = jnp.clip(hdn[:, g * group + V7X_LANES:(g + 1) * group], -SWIGLU_LIMIT, SWIGLU_LIMIT)
            act = x_glu * jax.nn.sigmoid(SWIGLU_ALPHA * x_glu) * (x_lin + 1.0)
            act_ref[:, g * V7X_LANES:(g + 1) * V7X_LANES] = act.astype(BF16)
        y_ref[...] = _dot(act_ref[...], w2s_ref[...]) + b2_ref[0]

    @pl.when(jnp.logical_not(active))
    def _():
        y_ref[...] = jnp.zeros(y_ref.shape, F32)


def _combine_kernel(dn_alpha, h_ref, y_ref, gcol_ref, g_ref, b_ref, o_ref):
    gates = gcol_ref[...]
    ffn = gates[:, 0:1] * y_ref[0]
    for k in range(1, TOP_K):
        ffn = ffn + gates[:, k:k + 1] * y_ref[k]
    o_ref[...] = _layer_norm(dn_alpha * h_ref[...] + ffn, g_ref[...], b_ref[...])


def _sc_gather_rows(table, idx):
    n, = idx.shape
    _, d = table.shape
    info = plsc.get_sparse_core_info()
    n_workers = info.num_cores * info.num_subcores
    per_worker = n // n_workers
    assert n % (n_workers * SC_CHUNK) == 0
    mesh = plsc.VectorSubcoreMesh(core_axis_name="core", subcore_axis_name="subcore")

    @functools.partial(
        pl.kernel,
        out_type=jax.ShapeDtypeStruct((n, d), table.dtype),
        mesh=mesh,
        scratch_types=[
            pltpu.VMEM((SC_CHUNK,), jnp.int32),
            pltpu.VMEM((SC_CHUNK, d), table.dtype),
            pltpu.SemaphoreType.DMA,
        ],
    )
    def gather_kernel(table_hbm, idx_hbm, out_hbm, idx_v, rows_v, sem):
        worker = lax.axis_index("subcore") * info.num_cores + lax.axis_index("core")
        base = worker * per_worker

        @pl.loop(0, per_worker // SC_CHUNK)
        def _(c):
            off = pl.multiple_of(base + c * SC_CHUNK, SC_CHUNK)
            pltpu.sync_copy(idx_hbm.at[pl.ds(off, SC_CHUNK)], idx_v)
            pltpu.async_copy(table_hbm.at[idx_v], rows_v, sem).wait()
            pltpu.sync_copy(rows_v, out_hbm.at[pl.ds(off, SC_CHUNK)])

    return gather_kernel(table, idx)


def _const_spec(shape):
    zeros = (0,) * len(shape)
    return pl.BlockSpec(shape, lambda *_: zeros)


def _tc_params(n_axes):
    return pltpu.CompilerParams(dimension_semantics=("arbitrary",) * n_axes,
                                vmem_limit_bytes=V7X_VMEM_LIMIT_BYTES)


def _mix_out_shapes(t, d):
    return (jax.ShapeDtypeStruct((t, d), F32),
            jax.ShapeDtypeStruct((TOP_K, t), jnp.int32),
            jax.ShapeDtypeStruct((t, V7X_LANES), F32))


def _pool_layer(h, mem, w_in, wbd, pscale, mem_kv, w_out, ln_g, ln_b, rwt, rb, dn_alpha, seq):
    t, d = h.shape
    n_mem = mem.shape[1]
    pool_width = wbd.shape[0]
    ts = SEQ_BLOCK
    nsb = seq // ts
    row = lambda b, s: (b * nsb + s, 0)
    return pl.pallas_call(
        functools.partial(_pool_layer_kernel, dn_alpha),
        out_shape=_mix_out_shapes(t, d),
        grid=(t // seq, nsb),
        in_specs=[
            pl.BlockSpec((ts, d), row),
            pl.BlockSpec((1, n_mem, d), lambda b, s: (b, 0, 0)),
            _const_spec(w_in.shape), _const_spec(wbd.shape), _const_spec(pscale.shape), _const_spec(mem_kv.shape),
            _const_spec(w_out.shape), _const_spec(ln_g.shape), _const_spec(ln_b.shape), _const_spec(rwt.shape),
            _const_spec(rb.shape),
        ],
        out_specs=(pl.BlockSpec((ts, d), row),
                   pl.BlockSpec((TOP_K, ts), lambda b, s: (0, b * nsb + s)),
                   pl.BlockSpec((ts, V7X_LANES), row)),
        scratch_shapes=[pltpu.VMEM((ts + POOL_HISTORY, pool_width), F32)] * 4
        + [pltpu.VMEM((n_mem, 2 * MEM_WIDTH), BF16), pltpu.VMEM((ts, d), BF16)],
        compiler_params=_tc_params(2),
        name="pool_layer",
    )(h, mem, w_in, wbd, pscale, mem_kv, w_out, ln_g, ln_b, rwt, rb)


def _shared_kv(h, kvw, fb, seq, fox_width, n_heads_pad):
    t, d = h.shape
    ts = SEQ_BLOCK
    nsb = seq // ts
    tri = (lax.broadcasted_iota(jnp.int32, (ts, ts), 1) <= lax.broadcasted_iota(jnp.int32, (ts, ts), 0)).astype(BF16)
    row = lambda b, s: (b * nsb + s, 0)
    return pl.pallas_call(
        _shared_kv_kernel,
        out_shape=(jax.ShapeDtypeStruct((t, fox_width), BF16), jax.ShapeDtypeStruct((t, fox_width), BF16),
                   jax.ShapeDtypeStruct((t, V7X_LANES), F32),
                   jax.ShapeDtypeStruct((t // seq, n_heads_pad, seq), F32)),
        grid=(t // seq, nsb),
        in_specs=[pl.BlockSpec((ts, d), row), _const_spec(kvw.shape), _const_spec(fb.shape), _const_spec(tri.shape)],
        out_specs=(pl.BlockSpec((ts, fox_width), row), pl.BlockSpec((ts, fox_width), row),
                   pl.BlockSpec((ts, V7X_LANES), row),
                   pl.BlockSpec((1, n_heads_pad, ts), lambda b, s: (b, 0, s))),
        scratch_shapes=[pltpu.VMEM((8, V7X_LANES), F32)],
        compiler_params=_tc_params(2),
        name="shared_kv",
    )(h, kvw, fb, tri)


def _fox_layer(h, mem, k, v, ccol, crow, w_in, mem_kv, w_out, ln_g, ln_b, rwt, rb, dn_alpha, seq):
    t, d = h.shape
    n_mem = mem.shape[1]
    fox_width = k.shape[1]
    tq = Q_BLOCK
    nqb = seq // tq
    n_batch = t // seq
    row = lambda b, i: (b * nqb + i, 0)
    per_batch = lambda b, i: (b, 0, 0)
    return pl.pallas_call(
        functools.partial(_fox_layer_kernel, dn_alpha),
        out_shape=_mix_out_shapes(t, d),
        grid=(n_batch, nqb),
        in_specs=[
            pl.BlockSpec((tq, d), row),
            pl.BlockSpec((1, n_mem, d), per_batch),
            pl.BlockSpec((1, seq, fox_width), per_batch),
            pl.BlockSpec((1, seq, fox_width), per_batch),
            pl.BlockSpec((tq, V7X_LANES), row),
            pl.BlockSpec((1, crow.shape[1], seq), per_batch),
            _const_spec(w_in.shape), _const_spec(mem_kv.shape), _const_spec(w_out.shape), _const_spec(ln_g.shape),
            _const_spec(ln_b.shape), _const_spec(rwt.shape), _const_spec(rb.shape),
        ],
        out_specs=(pl.BlockSpec((tq, d), row),
                   pl.BlockSpec((TOP_K, tq), lambda b, i: (0, b * nqb + i)),
                   pl.BlockSpec((tq, V7X_LANES), row)),
        scratch_shapes=[pltpu.VMEM((n_mem, 2 * MEM_WIDTH), BF16), pltpu.VMEM((tq, d), BF16)],
        compiler_params=_tc_params(2),
        name="fox_layer",
    )(h, mem, k.reshape(n_batch, seq, fox_width), v.reshape(n_batch, seq, fox_width), ccol, crow,
      w_in, mem_kv, w_out, ln_g, ln_b, rwt, rb)


def _route(idx_t, n_experts):
    top_k, t = idx_t.shape
    n_assign = top_k * t
    e_flat = idx_t.reshape(-1)
    iota = jnp.arange(n_assign, dtype=jnp.int32)
    _, order = lax.sort((e_flat, iota), num_keys=1)
    _, inv = lax.sort((order, iota), num_keys=1)
    counts = jnp.sum((e_flat[None, :] == jnp.arange(n_experts, dtype=jnp.int32)[:, None]).astype(jnp.int32), axis=1)
    padded = ((counts + MOE_BLOCK - 1) // MOE_BLOCK) * MOE_BLOCK
    start = jnp.cumsum(counts) - counts
    pad_end = jnp.cumsum(padded)
    pad_start = pad_end - padded
    n_blocks = n_assign // MOE_BLOCK + n_experts
    block_expert = jnp.clip(jnp.searchsorted(pad_end, jnp.arange(n_blocks, dtype=jnp.int32) * MOE_BLOCK, side="right"),
                            0, n_experts - 1).astype(jnp.int32)
    slot = jnp.arange(n_blocks * MOE_BLOCK, dtype=jnp.int32)
    slot_e = jnp.repeat(block_expert, MOE_BLOCK)
    j = slot - pad_start[slot_e]
    valid = j < counts[slot_e]
    src = jnp.clip(start[slot_e] + j, 0, n_assign - 1)
    slot_tok = jnp.where(valid, order[src] % t, 0).astype(jnp.int32)
    pos = (pad_start[e_flat] + inv - start[e_flat]).astype(jnp.int32)
    n_active = (pad_end[-1:] // MOE_BLOCK).astype(jnp.int32)
    return slot_tok, pos, block_expert, n_active


def _experts(xs, block_expert, n_active, w1, b1, w2, b2):
    n_slots, d = xs.shape
    n_exp, _, two_f = w1.shape
    d_ff = two_f // 2
    grid_spec = pltpu.PrefetchScalarGridSpec(
        num_scalar_prefetch=2,
        grid=(n_slots // MOE_BLOCK,),
        in_specs=[
            pl.BlockSpec((MOE_BLOCK, d), lambda i, be, na: (i, 0)),
            pl.BlockSpec((1, d, two_f), lambda i, be, na: (be[i], 0, 0)),
            pl.BlockSpec((1, 1, two_f), lambda i, be, na: (be[i], 0, 0)),
            pl.BlockSpec((1, d_ff, d), lambda i, be, na: (be[i], 0, 0)),
            pl.BlockSpec((1, 1, d), lambda i, be, na: (be[i], 0, 0)),
        ],
        out_specs=pl.BlockSpec((MOE_BLOCK, d), lambda i, be, na: (i, 0)),
        scratch_shapes=[pltpu.VMEM((d, two_f), BF16), pltpu.VMEM((d_ff, d), BF16), pltpu.VMEM((MOE_BLOCK, d_ff), BF16)],
    )
    return pl.pallas_call(
        _expert_kernel,
        out_shape=jax.ShapeDtypeStruct((n_slots, d), F32),
        grid_spec=grid_spec,
        compiler_params=_tc_params(1),
        name="experts",
    )(block_expert, n_active, xs, w1, b1.reshape(n_exp, 1, two_f), w2, b2.reshape(n_exp, 1, d))


def _combine(h, yg, gcol, ln_g, ln_b, dn_alpha):
    t, d = h.shape
    ts = SEQ_BLOCK
    row = lambda i: (i, 0)
    return pl.pallas_call(
        functools.partial(_combine_kernel, dn_alpha),
        out_shape=jax.ShapeDtypeStruct((t, d), F32),
        grid=(t // ts,),
        in_specs=[pl.BlockSpec((ts, d), row), pl.BlockSpec((TOP_K, ts, d), lambda i: (0, i, 0)),
                  pl.BlockSpec((ts, V7X_LANES), row), _const_spec(ln_g.shape), _const_spec(ln_b.shape)],
        out_specs=pl.BlockSpec((ts, d), row),
        compiler_params=_tc_params(1),
        name="combine",
    )(h, yg.reshape(TOP_K, t, d), gcol, ln_g, ln_b)


def _moe(h1, idx_t, gcol, w1, b1, w2, b2, ln_g, ln_b, dn_alpha):
    n_exp = w1.shape[0]
    d_ff = w2.shape[1]
    slot_tok, pos, block_expert, n_active = _route(idx_t, n_exp)
    b1p = jnp.swapaxes(b1.reshape(n_exp, d_ff // V7X_LANES, V7X_LANES, 2), 2, 3).reshape(n_exp, 2 * d_ff)
    xs = _sc_gather_rows(h1, slot_tok)
    y = _experts(xs, block_expert, n_active, w1, b1p, w2, b2)
    yg = _sc_gather_rows(y, pos)
    return _combine(h1, yg, gcol, ln_g, ln_b, dn_alpha)


def kernel(x, mem, ln1_g, ln1_b, ln2_g, ln2_b, a_w_in, pool_w, pool_scale, a_mem_kv, a_w_out, kv_w, fgate_b, b_w_in, b_mem_kv, b_w_out, router_w, router_b, moe_w1, moe_b1, moe_w2, moe_b2):
    n_batch, seq, d = x.shape
    t = n_batch * seq
    depth = ln1_g.shape[0]
    n_a = a_w_in.shape[0]
    dn_alpha = float((2 * depth) ** 0.25)
    n_exp = router_w.shape[2]
    fox_heads = fgate_b.shape[0]
    fox_width = fox_heads * HEAD_DIM
    n_heads_pad = 16
    assert seq % SEQ_BLOCK == 0 and seq % Q_BLOCK == 0 and fox_heads <= n_heads_pad

    h = x.reshape(t, d)
    row = lambda a: a.reshape(1, -1)
    shared = None
    for l in range(depth):
        rwt = router_w[l].T
        rb = router_b[l].reshape(n_exp, 1)
        if l < n_a:
            wbd = jax.scipy.linalg.block_diag(*[pool_w[l, g] for g in range(pool_w.shape[1])]).astype(BF16)
            h1, idx_t, gcol = _pool_layer(
                h, mem, a_w_in[l].astype(BF16), wbd, row(pool_scale[l]), a_mem_kv[l].astype(BF16),
                a_w_out[l].astype(BF16), row(ln1_g[l]), row(ln1_b[l]), rwt, rb, dn_alpha, seq)
        else:
            j = l - n_a
            if shared is None:
                pad = V7X_LANES - fox_heads
                kvw = jnp.pad(kv_w, ((0, 0), (0, pad))).astype(BF16)
                fb = jnp.pad(fgate_b, (0, pad)).reshape(1, V7X_LANES)
                shared = _shared_kv(h, kvw, fb, seq, fox_width, n_heads_pad)
            k_sh, v_sh, ccol, crow = shared
            h1, idx_t, gcol = _fox_layer(
                h, mem, k_sh, v_sh, ccol, crow, b_w_in[j].astype(BF16), b_mem_kv[j].astype(BF16),
                b_w_out[j].astype(BF16), row(ln1_g[l]), row(ln1_b[l]), rwt, rb, dn_alpha, seq)
        h = _moe(h1, idx_t, gcol, moe_w1[l], moe_b1[l], moe_w2[l], moe_b2[l], row(ln2_g[l]), row(ln2_b[l]), dn_alpha)
    return h.reshape(n_batch, seq, d)
```

```python
import functools
import jax
import jax.numpy as jnp
from jax import lax
from jax.experimental import pallas as pl
from jax.experimental.pallas import tpu as pltpu
from jax.experimental.pallas import tpu_sc as plsc

HEAD_DIM = 64
MEM_HEADS = 4
MEM_WIDTH = MEM_HEADS * HEAD_DIM
POOL_WINDOWS = (2, 4, 8, 16)
TOP_K = 4
SWIGLU_ALPHA = 1.702
SWIGLU_LIMIT = 7.0
MOE_BLOCK = 512
LN_EPS = 1e-5
ATTN_SCALE = HEAD_DIM ** -0.5

V7X_LANES = 128
V7X_VMEM_LIMIT_BYTES = 56 * 1024 * 1024
POOL_HISTORY = 32
SC_CHUNK = 32
SEQ_BLOCK = 512
Q_BLOCK = 512

F32 = jnp.float32
BF16 = jnp.bfloat16
NEG_INF = float("-inf")


def _dot(a, b):
    return jnp.dot(a, b, preferred_element_type=F32)


def _dot_nt(a, b):
    return lax.dot_general(a, b, (((1,), (1,)), ((), ())), preferred_element_type=F32)


def _lane_iota(shape):
    return lax.broadcasted_iota(jnp.int32, shape, len(shape) - 1)


def _pair_attention(q2, keys, vals, bias=None):
    lane = _lane_iota(q2.shape)
    first = lane < HEAD_DIM
    outs = []
    for sel in (first, jnp.logical_not(first)):
        qh = jnp.where(sel, q2, 0.0).astype(BF16)
        s = _dot_nt(qh, keys)
        m = jnp.max(s, axis=-1, keepdims=True)
        p = jnp.exp(s - m)
        l = jnp.sum(p, axis=-1, keepdims=True)
        outs.append(_dot(p.astype(BF16), vals) / l)
    return jnp.where(first, outs[0], outs[1])


def _memory_attention(proj, kv_ref, cat_ref, out_off, q_off):
    for p in range(MEM_HEADS // 2):
        lo = p * V7X_LANES
        q2 = proj[:, q_off + lo:q_off + lo + V7X_LANES] * ATTN_SCALE
        keys = kv_ref[:, lo:lo + V7X_LANES]
        vals = kv_ref[:, MEM_WIDTH + lo:MEM_WIDTH + lo + V7X_LANES]
        cat_ref[:, out_off + lo:out_off + lo + V7X_LANES] = _pair_attention(q2, keys, vals).astype(BF16)


def _layer_norm(z, g, b):
    mu = jnp.mean(z, axis=-1, keepdims=True)
    zc = z - mu
    var = jnp.mean(zc * zc, axis=-1, keepdims=True)
    return zc * lax.rsqrt(var + LN_EPS) * g + b


def _mix_epilogue(x, cat_ref, w_out_ref, g_ref, b_ref, rwt_ref, rb_ref, dn_alpha, h_out_ref, idx_ref, gcol_ref):
    mix = _dot(cat_ref[...], w_out_ref[...])
    h1 = _layer_norm(dn_alpha * x + mix, g_ref[...], b_ref[...])
    h_out_ref[...] = h1
    logits = lax.dot_general(rwt_ref[...], h1, (((1,), (1,)), ((), ())),
                             precision=lax.Precision.HIGHEST, preferred_element_type=F32) + rb_ref[...]
    n_exp, t = logits.shape
    rows = lax.broadcasted_iota(jnp.int32, (n_exp, t), 0)
    vals, idxs = [], []
    for _ in range(TOP_K):
        m = jnp.max(logits, axis=0, keepdims=True)
        ix = jnp.min(jnp.where(logits == m, rows, n_exp), axis=0, keepdims=True)
        vals.append(m)
        idxs.append(ix)
        logits = jnp.where(rows == ix, NEG_INF, logits)
    exps = [jnp.exp(v - vals[0]) for v in vals]
    denom = exps[0] + exps[1] + exps[2] + exps[3]
    for k in range(TOP_K):
        idx_ref[k:k + 1, :] = idxs[k]
    lane_rows = lax.broadcasted_iota(jnp.int32, (V7X_LANES, t), 0)
    gates_t = jnp.zeros((V7X_LANES, t), F32)
    for k in range(TOP_K):
        gates_t = jnp.where(lane_rows == k, exps[k] / denom, gates_t)
    gcol_ref[...] = gates_t.T


def _pool_layer_kernel(dn_alpha, h_ref, mem_ref, w_in_ref, wbd_ref, pscale_ref, mem_kv_ref, w_out_ref, g_ref, b_ref,
                       rwt_ref, rb_ref, h_out_ref, idx_ref, gcol_ref,
                       ue_ref, w2_ref, w4_ref, w8_ref, kv_ref, cat_ref):
    s = pl.program_id(1)
    ts = h_ref.shape[0]
    pool_width = wbd_ref.shape[0]
    hist = POOL_HISTORY

    @pl.when(s == 0)
    def _():
        ue_ref[0:hist, :] = jnp.zeros((hist, pool_width), F32)
        kv_ref[...] = _dot(mem_ref[0].astype(BF16), mem_kv_ref[...]).astype(BF16)

    x = h_ref[...]
    proj = _dot(x.astype(BF16), w_in_ref[...])
    u = proj[:, :pool_width]
    ue_ref[hist:, :] = u
    n = ts + hist
    w2_ref[8:n, :] = ue_ref[8:n, :] + ue_ref[7:n - 1, :]
    w4_ref[16:n, :] = w2_ref[16:n, :] + w2_ref[14:n - 2, :]
    w8_ref[24:n, :] = w4_ref[24:n, :] + w4_ref[20:n - 4, :]
    w16 = w8_ref[hist:n, :] + w8_ref[hist - 8:n - 8, :]
    group_dim = pool_width // len(POOL_WINDOWS)
    col = _lane_iota((ts, pool_width))
    wsum = jnp.where(col < group_dim, w2_ref[hist:n, :],
                     jnp.where(col < 2 * group_dim, w4_ref[hist:n, :],
                               jnp.where(col < 3 * group_dim, w8_ref[hist:n, :], w16)))
    window = jnp.where(col < group_dim, POOL_WINDOWS[0],
                       jnp.where(col < 2 * group_dim, POOL_WINDOWS[1],
                                 jnp.where(col < 3 * group_dim, POOL_WINDOWS[2], POOL_WINDOWS[3])))
    tpos = s * ts + lax.broadcasted_iota(jnp.int32, (ts, pool_width), 0) + 1
    cnt = jnp.minimum(tpos, window).astype(F32)
    d = wsum / cnt - u
    ue_ref[0:hist, :] = u[ts - hist:, :]
    pooled = _dot(d.astype(BF16), wbd_ref[...]) * pscale_ref[...]
    cat_ref[:, :pool_width] = pooled.astype(BF16)
    _memory_attention(proj, kv_ref, cat_ref, pool_width, pool_width)
    _mix_epilogue(x, cat_ref, w_out_ref, g_ref, b_ref, rwt_ref, rb_ref, dn_alpha, h_out_ref, idx_ref, gcol_ref)


def _bias_pieces(c):
    hi = c.astype(BF16).astype(F32)
    r = c - hi
    mid = r.astype(BF16).astype(F32)
    lo = (r - mid).astype(BF16).astype(F32)
    return hi, mid, lo


def _augment_head(x, c_col, is_query):
    lane = _lane_iota(x.shape)
    hi, mid, lo = _bias_pieces(jnp.broadcast_to(c_col, x.shape))
    b = HEAD_DIM
    if is_query:
        aug = jnp.where(lane == b, hi, jnp.where(lane == b + 1, mid, jnp.where(lane == b + 2, lo,
                        jnp.where(lane < b + 6, 1.0, 0.0))))
    else:
        aug = jnp.where(lane == b + 3, -hi, jnp.where(lane == b + 4, -mid, jnp.where(lane == b + 5, -lo,
                        jnp.where(lane < b + 3, 1.0, 0.0))))
    return jnp.where(lane < b, x, aug).astype(BF16)


def _shared_kv_kernel(h_ref, kvw_ref, fb_ref, tri_ref, k_ref, vt_ref, ccol_ref, carry_ref):
    s = pl.program_id(1)
    k_width = k_ref.shape[1]
    v_width = vt_ref.shape[1]

    @pl.when(s == 0)
    def _():
        carry_ref[...] = jnp.zeros(carry_ref.shape, F32)

    proj = _dot(h_ref[...].astype(BF16), kvw_ref[...])
    z = proj[:, k_width + v_width:] + fb_ref[...]
    logf = jnp.minimum(z, 0.0) - jnp.log(1.0 + jnp.exp(-jnp.abs(z)))
    hi, mid, lo = _bias_pieces(logf)
    tri = tri_ref[...]
    c = _dot(tri, hi.astype(BF16)) + _dot(tri, mid.astype(BF16)) + _dot(tri, lo.astype(BF16)) + carry_ref[0:1, :]
    ts = c.shape[0]
    carry_ref[0:1, :] = c[ts - 1:ts, :]
    ccol_ref[...] = c
    for h in range(k_width // V7X_LANES):
        cols = slice(h * V7X_LANES, (h + 1) * V7X_LANES)
        k_ref[:, cols] = _augment_head(proj[:, cols], c[:, h:h + 1], False)
    vt_ref[0] = proj[:, k_width:k_width + v_width].T.astype(BF16)


def _fox_layer_kernel(dn_alpha, h_ref, mem_ref, k_ref, vt_ref, ccol_ref, w_in_ref, mem_kv_ref, w_out_ref,
                      g_ref, b_ref, rwt_ref, rb_ref, h_out_ref, idx_ref, gcol_ref,
                      kv_ref, cat_ref, qa_ref, m_ref, l_ref, acc_ref):
    i = pl.program_id(1)
    tq = h_ref.shape[0]
    n_heads = k_ref.shape[2] // V7X_LANES
    fox_width = vt_ref.shape[1]

    @pl.when(i == 0)
    def _():
        kv_ref[...] = _dot(mem_ref[0].astype(BF16), mem_kv_ref[...]).astype(BF16)

    x = h_ref[...]
    proj = _dot(x.astype(BF16), w_in_ref[...])
    c_t = ccol_ref[...]
    for h in range(n_heads):
        cols = slice(h * V7X_LANES, (h + 1) * V7X_LANES)
        qa_ref[h] = _augment_head(proj[:, cols] * ATTN_SCALE, c_t[:, h:h + 1], True)
    m_ref[...] = jnp.full(m_ref.shape, NEG_INF, F32)
    l_ref[...] = jnp.zeros(l_ref.shape, F32)
    acc_ref[...] = jnp.zeros(acc_ref.shape, F32)
    causal_t = lax.broadcasted_iota(jnp.int32, (tq, tq), 0) <= lax.broadcasted_iota(jnp.int32, (tq, tq), 1)

    def tile(j, masked):
        start = pl.multiple_of(j * tq, tq)
        for h in range(n_heads):
            kb = k_ref[0, pl.ds(start, tq), h * V7X_LANES:(h + 1) * V7X_LANES]
            st = _dot_nt(kb, qa_ref[h])
            if masked:
                st = jnp.where(causal_t, st, NEG_INF)
            m_old = m_ref[h:h + 1, :]
            m_new = jnp.maximum(m_old, jnp.max(st, axis=0, keepdims=True))
            alpha = jnp.exp(m_old - m_new)
            pt = jnp.exp(st - m_new)
            l_ref[h:h + 1, :] = alpha * l_ref[h:h + 1, :] + jnp.sum(pt, axis=0, keepdims=True)
            m_ref[h:h + 1, :] = m_new
            rows = slice(h * HEAD_DIM, (h + 1) * HEAD_DIM)
            vb = vt_ref[0, rows, pl.ds(start, tq)]
            acc_ref[rows, :] = alpha * acc_ref[rows, :] + _dot(vb, pt.astype(BF16))

    tile(i, True)

    def off_diagonal(j, carry):
        tile(j, False)
        return carry

    lax.fori_loop(0, i, off_diagonal, 0)
    for h in range(n_heads):
        rows = slice(h * HEAD_DIM, (h + 1) * HEAD_DIM)
        acc_ref[rows, :] = acc_ref[rows, :] / l_ref[h:h + 1, :]
    cat_ref[:, :fox_width] = acc_ref[...].T.astype(BF16)
    _memory_attention(proj, kv_ref, cat_ref, fox_width, n_heads * V7X_LANES)
    _mix_epilogue(x, cat_ref, w_out_ref, g_ref, b_ref, rwt_ref, rb_ref, dn_alpha, h_out_ref, idx_ref, gcol_ref)


def _expert_kernel(be_ref, nact_ref, x_ref, w1_ref, b1_ref, w2_ref, b2_ref, y_ref, w1s_ref, w2s_ref, act_ref):
    i = pl.program_id(0)
    active = i < nact_ref[0]
    two_f = w1_ref.shape[2]
    group = 2 * V7X_LANES
    n_groups = two_f // group

    @pl.when(jnp.logical_and(active, jnp.logical_or(i == 0, be_ref[i] != be_ref[jnp.maximum(i - 1, 0)])))
    def _():
        src = lax.broadcasted_iota(jnp.int32, (group, group), 0)
        dst = lax.broadcasted_iota(jnp.int32, (group, group), 1)
        want = jnp.where(dst < V7X_LANES, 2 * dst, 2 * (dst - V7X_LANES) + 1)
        perm = (src == want).astype(BF16)
        for g in range(n_groups):
            cols = slice(g * group, (g + 1) * group)
            w1s_ref[:, cols] = _dot(w1_ref[0, :, cols].astype(BF16), perm).astype(BF16)
        w2s_ref[...] = w2_ref[0].astype(BF16)

    @pl.when(active)
    def _():
        hdn = _dot(x_ref[...].astype(BF16), w1s_ref[...]) + b1_ref[0]
        for g in range(n_groups):
            x_glu = jnp.minimum(hdn[:, g * group:g * group + V7X_LANES], SWIGLU_LIMIT)
            x_lin = jnp.clip(hdn[:, g * group + V7X_LANES:(g + 1) * group], -SWIGLU_LIMIT, SWIGLU_LIMIT)
            act = x_glu * jax.nn.sigmoid(SWIGLU_ALPHA * x_glu) * (x_lin + 1.0)
            act_ref[:, g * V7X_LANES:(g + 1) * V7X_LANES] = act.astype(BF16)
        y_ref[...] = _dot(act_ref[...], w2s_ref[...]) + b2_ref[0]

    @pl.when(jnp.logical_not(active))
    def _():
        y_ref[...] = jnp.zeros(y_ref.shape, F32)


def _combine_kernel(dn_alpha, h_ref, y_ref, gcol_ref, g_ref, b_ref, o_ref):
    gates = gcol_ref[...]
    ffn = gates[:, 0:1] * y_ref[0]
    for k in range(1, TOP_K):
        ffn = ffn + gates[:, k:k + 1] * y_ref[k]
    o_ref[...] = _layer_norm(dn_alpha * h_ref[...] + ffn, g_ref[...], b_ref[...])


def _sc_gather_rows(table, idx):
    n, = idx.shape
    _, d = table.shape
    info = plsc.get_sparse_core_info()
    n_workers = info.num_cores * info.num_subcores
    per_worker = n // n_workers
    assert n % (n_workers * SC_CHUNK) == 0
    mesh = plsc.VectorSubcoreMesh(core_axis_name="core", subcore_axis_name="subcore")

    @functools.partial(
        pl.kernel,
        out_type=jax.ShapeDtypeStruct((n, d), table.dtype),
        mesh=mesh,
        scratch_types=[
            pltpu.VMEM((SC_CHUNK,), jnp.int32),
            pltpu.VMEM((SC_CHUNK, d), table.dtype),
            pltpu.SemaphoreType.DMA,
        ],
    )
    def gather_kernel(table_hbm, idx_hbm, out_hbm, idx_v, rows_v, sem):
        worker = lax.axis_index("subcore") * info.num_cores + lax.axis_index("core")
        base = worker * per_worker

        @pl.loop(0, per_worker // SC_CHUNK)
        def _(c):
            off = pl.multiple_of(base + c * SC_CHUNK, SC_CHUNK)
            pltpu.sync_copy(idx_hbm.at[pl.ds(off, SC_CHUNK)], idx_v)
            pltpu.async_copy(table_hbm.at[idx_v], rows_v, sem).wait()
            pltpu.sync_copy(rows_v, out_hbm.at[pl.ds(off, SC_CHUNK)])

    return gather_kernel(table, idx)


def _const_spec(shape):
    zeros = (0,) * len(shape)
    return pl.BlockSpec(shape, lambda *_: zeros, pipeline_mode=pl.Buffered(1))


def _tc_params(n_axes):
    return pltpu.CompilerParams(dimension_semantics=("arbitrary",) * n_axes,
                                vmem_limit_bytes=V7X_VMEM_LIMIT_BYTES)


def _mix_out_shapes(t, d):
    return (jax.ShapeDtypeStruct((t, d), F32),
            jax.ShapeDtypeStruct((TOP_K, t), jnp.int32),
            jax.ShapeDtypeStruct((t, V7X_LANES), F32))


def _pool_layer(h, mem, w_in, wbd, pscale, mem_kv, w_out, ln_g, ln_b, rwt, rb, dn_alpha, seq):
    t, d = h.shape
    n_mem = mem.shape[1]
    pool_width = wbd.shape[0]
    ts = SEQ_BLOCK
    nsb = seq // ts
    row = lambda b, s: (b * nsb + s, 0)
    return pl.pallas_call(
        functools.partial(_pool_layer_kernel, dn_alpha),
        out_shape=_mix_out_shapes(t, d),
        grid=(t // seq, nsb),
        in_specs=[
            pl.BlockSpec((ts, d), row),
            pl.BlockSpec((1, n_mem, d), lambda b, s: (b, 0, 0)),
            _const_spec(w_in.shape), _const_spec(wbd.shape), _const_spec(pscale.shape), _const_spec(mem_kv.shape),
            _const_spec(w_out.shape), _const_spec(ln_g.shape), _const_spec(ln_b.shape), _const_spec(rwt.shape),
            _const_spec(rb.shape),
        ],
        out_specs=(pl.BlockSpec((ts, d), row),
                   pl.BlockSpec((TOP_K, ts), lambda b, s: (0, b * nsb + s)),
                   pl.BlockSpec((ts, V7X_LANES), row)),
        scratch_shapes=[pltpu.VMEM((ts + POOL_HISTORY, pool_width), F32)] * 4
        + [pltpu.VMEM((n_mem, 2 * MEM_WIDTH), BF16), pltpu.VMEM((ts, d), BF16)],
        compiler_params=_tc_params(2),
        name="pool_layer",
    )(h, mem, w_in, wbd, pscale, mem_kv, w_out, ln_g, ln_b, rwt, rb)


def _expand_heads(w, n_heads):
    d = w.shape[0]
    w3 = w.reshape(d, n_heads, HEAD_DIM)
    return jnp.pad(w3, ((0, 0), (0, 0), (0, V7X_LANES - HEAD_DIM))).reshape(d, n_heads * V7X_LANES)


def _shared_kv(h, kvw, fb, seq, n_heads):
    t, d = h.shape
    ts = SEQ_BLOCK
    nsb = seq // ts
    k_width = n_heads * V7X_LANES
    v_width = n_heads * HEAD_DIM
    tri = (lax.broadcasted_iota(jnp.int32, (ts, ts), 1) <= lax.broadcasted_iota(jnp.int32, (ts, ts), 0)).astype(BF16)
    row = lambda b, s: (b * nsb + s, 0)
    return pl.pallas_call(
        _shared_kv_kernel,
        out_shape=(jax.ShapeDtypeStruct((t, k_width), BF16),
                   jax.ShapeDtypeStruct((t // seq, v_width, seq), BF16),
                   jax.ShapeDtypeStruct((t, V7X_LANES), F32)),
        grid=(t // seq, nsb),
        in_specs=[pl.BlockSpec((ts, d), row), _const_spec(kvw.shape), _const_spec(fb.shape), _const_spec(tri.shape)],
        out_specs=(pl.BlockSpec((ts, k_width), row),
                   pl.BlockSpec((1, v_width, ts), lambda b, s: (b, 0, s)),
                   pl.BlockSpec((ts, V7X_LANES), row)),
        scratch_shapes=[pltpu.VMEM((8, V7X_LANES), F32)],
        compiler_params=_tc_params(2),
        name="shared_kv",
    )(h, kvw, fb, tri)


def _fox_layer(h, mem, k_aug, v_t, ccol, w_in, mem_kv, w_out, ln_g, ln_b, rwt, rb, dn_alpha, seq):
    t, d = h.shape
    n_mem = mem.shape[1]
    k_width = k_aug.shape[1]
    v_width = v_t.shape[1]
    n_heads = k_width // V7X_LANES
    tq = Q_BLOCK
    nqb = seq // tq
    n_batch = t // seq
    row = lambda b, i: (b * nqb + i, 0)
    per_batch = lambda b, i: (b, 0, 0)
    return pl.pallas_call(
        functools.partial(_fox_layer_kernel, dn_alpha),
        out_shape=_mix_out_shapes(t, d),
        grid=(n_batch, nqb),
        in_specs=[
            pl.BlockSpec((tq, d), row),
            pl.BlockSpec((1, n_mem, d), per_batch),
            pl.BlockSpec((1, seq, k_width), per_batch),
            pl.BlockSpec((1, v_width, seq), per_batch),
            pl.BlockSpec((tq, V7X_LANES), row),
            _const_spec(w_in.shape), _const_spec(mem_kv.shape), _const_spec(w_out.shape), _const_spec(ln_g.shape),
            _const_spec(ln_b.shape), _const_spec(rwt.shape), _const_spec(rb.shape),
        ],
        out_specs=(pl.BlockSpec((tq, d), row),
                   pl.BlockSpec((TOP_K, tq), lambda b, i: (0, b * nqb + i)),
                   pl.BlockSpec((tq, V7X_LANES), row)),
        scratch_shapes=[pltpu.VMEM((n_mem, 2 * MEM_WIDTH), BF16), pltpu.VMEM((tq, d), BF16),
                        pltpu.VMEM((n_heads, tq, V7X_LANES), BF16),
                        pltpu.VMEM((16, tq), F32), pltpu.VMEM((16, tq), F32), pltpu.VMEM((v_width, tq), F32)],
        compiler_params=_tc_params(2),
        name="fox_layer",
    )(h, mem, k_aug.reshape(n_batch, seq, k_width), v_t, ccol, w_in, mem_kv, w_out, ln_g, ln_b, rwt, rb)


def _route(idx_t, n_experts):
    top_k, t = idx_t.shape
    n_assign = top_k * t
    e_flat = idx_t.reshape(-1)
    iota = jnp.arange(n_assign, dtype=jnp.int32)
    _, order = lax.sort((e_flat, iota), num_keys=1)
    _, inv = lax.sort((order, iota), num_keys=1)
    counts = jnp.sum((e_flat[None, :] == jnp.arange(n_experts, dtype=jnp.int32)[:, None]).astype(jnp.int32), axis=1)
    padded = ((counts + MOE_BLOCK - 1) // MOE_BLOCK) * MOE_BLOCK
    start = jnp.cumsum(counts) - counts
    pad_end = jnp.cumsum(padded)
    pad_start = pad_end - padded
    n_blocks = n_assign // MOE_BLOCK + n_experts
    block_start = jnp.arange(n_blocks, dtype=jnp.int32) * MOE_BLOCK
    block_expert = jnp.minimum(jnp.sum((pad_end[None, :] <= block_start[:, None]).astype(jnp.int32), axis=1),
                               n_experts - 1)
    slot = jnp.arange(n_blocks * MOE_BLOCK, dtype=jnp.int32)
    slot_e = jnp.repeat(block_expert, MOE_BLOCK)
    j = slot - pad_start[slot_e]
    valid = j < counts[slot_e]
    src = jnp.clip(start[slot_e] + j, 0, n_assign - 1)
    slot_tok = jnp.where(valid, order[src] % t, 0).astype(jnp.int32)
    pos = (pad_start[e_flat] + inv - start[e_flat]).astype(jnp.int32)
    n_active = (pad_end[-1:] // MOE_BLOCK).astype(jnp.int32)
    return slot_tok, pos, block_expert, n_active


def _experts(xs, block_expert, n_active, w1_all, b1, w2_all, b2, layer):
    n_slots, d = xs.shape
    _, n_exp, _, two_f = w1_all.shape
    d_ff = two_f // 2
    grid_spec = pltpu.PrefetchScalarGridSpec(
        num_scalar_prefetch=2,
        grid=(n_slots // MOE_BLOCK,),
        in_specs=[
            pl.BlockSpec((MOE_BLOCK, d), lambda i, be, na: (i, 0)),
            pl.BlockSpec((None, 1, d, two_f), lambda i, be, na: (layer, be[i], 0, 0)),
            pl.BlockSpec((1, 1, two_f), lambda i, be, na: (be[i], 0, 0)),
            pl.BlockSpec((None, 1, d_ff, d), lambda i, be, na: (layer, be[i], 0, 0)),
            pl.BlockSpec((1, 1, d), lambda i, be, na: (be[i], 0, 0)),
        ],
        out_specs=pl.BlockSpec((MOE_BLOCK, d), lambda i, be, na: (i, 0)),
        scratch_shapes=[pltpu.VMEM((d, two_f), BF16), pltpu.VMEM((d_ff, d), BF16), pltpu.VMEM((MOE_BLOCK, d_ff), BF16)],
    )
    return pl.pallas_call(
        _expert_kernel,
        out_shape=jax.ShapeDtypeStruct((n_slots, d), F32),
        grid_spec=grid_spec,
        compiler_params=_tc_params(1),
        name="experts",
    )(block_expert, n_active, xs, w1_all, b1.reshape(n_exp, 1, two_f), w2_all, b2.reshape(n_exp, 1, d))


def _combine(h, yg, gcol, ln_g, ln_b, dn_alpha):
    t, d = h.shape
    ts = SEQ_BLOCK
    row = lambda i: (i, 0)
    return pl.pallas_call(
        functools.partial(_combine_kernel, dn_alpha),
        out_shape=jax.ShapeDtypeStruct((t, d), F32),
        grid=(t // ts,),
        in_specs=[pl.BlockSpec((ts, d), row), pl.BlockSpec((TOP_K, ts, d), lambda i: (0, i, 0)),
                  pl.BlockSpec((ts, V7X_LANES), row), _const_spec(ln_g.shape), _const_spec(ln_b.shape)],
        out_specs=pl.BlockSpec((ts, d), row),
        compiler_params=_tc_params(1),
        name="combine",
    )(h, yg.reshape(TOP_K, t, d), gcol, ln_g, ln_b)


def _moe(h1, idx_t, gcol, w1_all, b1, w2_all, b2, layer, ln_g, ln_b, dn_alpha):
    n_exp = w1_all.shape[1]
    d_ff = w2_all.shape[2]
    slot_tok, pos, block_expert, n_active = _route(idx_t, n_exp)
    b1p = jnp.swapaxes(b1.reshape(n_exp, d_ff // V7X_LANES, V7X_LANES, 2), 2, 3).reshape(n_exp, 2 * d_ff)
    xs = _sc_gather_rows(h1, slot_tok)
    y = _experts(xs, block_expert, n_active, w1_all, b1p, w2_all, b2, layer)
    yg = _sc_gather_rows(y, pos)
    return _combine(h1, yg, gcol, ln_g, ln_b, dn_alpha)


def kernel(x, mem, ln1_g, ln1_b, ln2_g, ln2_b, a_w_in, pool_w, pool_scale, a_mem_kv, a_w_out, kv_w, fgate_b, b_w_in, b_mem_kv, b_w_out, router_w, router_b, moe_w1, moe_b1, moe_w2, moe_b2):
    n_batch, seq, d = x.shape
    t = n_batch * seq
    depth = ln1_g.shape[0]
    n_a = a_w_in.shape[0]
    dn_alpha = float((2 * depth) ** 0.25)
    n_exp = router_w.shape[2]
    fox_heads = fgate_b.shape[0]
    fox_width = fox_heads * HEAD_DIM
    assert seq % SEQ_BLOCK == 0 and seq % Q_BLOCK == 0 and fox_heads <= 16

    h = x.reshape(t, d)
    row = lambda a: a.reshape(1, -1)
    shared = None
    for l in range(depth):
        rwt = router_w[l].T
        rb = router_b[l].reshape(n_exp, 1)
        if l < n_a:
            wbd = jax.scipy.linalg.block_diag(*[pool_w[l, g] for g in range(pool_w.shape[1])]).astype(BF16)
            h1, idx_t, gcol = _pool_layer(
                h, mem, a_w_in[l].astype(BF16), wbd, row(pool_scale[l]), a_mem_kv[l].astype(BF16),
                a_w_out[l].astype(BF16), row(ln1_g[l]), row(ln1_b[l]), rwt, rb, dn_alpha, seq)
        else:
            j = l - n_a
            if shared is None:
                pad = V7X_LANES - fox_heads
                kvw = jnp.concatenate([_expand_heads(kv_w[:, :fox_width], fox_heads),
                                       kv_w[:, fox_width:2 * fox_width],
                                       jnp.pad(kv_w[:, 2 * fox_width:], ((0, 0), (0, pad)))], axis=1).astype(BF16)
                fb = jnp.pad(fgate_b, (0, pad)).reshape(1, V7X_LANES)
                shared = _shared_kv(h, kvw, fb, seq, fox_heads)
            k_aug, v_t, ccol = shared
            w_in = jnp.concatenate([_expand_heads(b_w_in[j][:, :fox_width], fox_heads),
                                    b_w_in[j][:, fox_width:]], axis=1).astype(BF16)
            h1, idx_t, gcol = _fox_layer(
                h, mem, k_aug, v_t, ccol, w_in, b_mem_kv[j].astype(BF16),
                b_w_out[j].astype(BF16), row(ln1_g[l]), row(ln1_b[l]), rwt, rb, dn_alpha, seq)
        h = _moe(h1, idx_t, gcol, moe_w1, moe_b1[l], moe_w2, moe_b2[l], l, row(ln2_g[l]), row(ln2_b[l]), dn_alpha)
    return h.reshape(n_batch, seq, d)
```

```python
import functools
import jax
import jax.numpy as jnp
from jax import lax
from jax.experimental import pallas as pl
from jax.experimental.pallas import tpu as pltpu
from jax.experimental.pallas import tpu_sc as plsc

HEAD_DIM = 64
MEM_HEADS = 4
MEM_WIDTH = MEM_HEADS * HEAD_DIM
POOL_WINDOWS = (2, 4, 8, 16)
TOP_K = 4
SWIGLU_ALPHA = 1.702
SWIGLU_LIMIT = 7.0
MOE_BLOCK = 512
LN_EPS = 1e-5
ATTN_SCALE = HEAD_DIM ** -0.5

V7X_LANES = 128
V7X_VMEM_LIMIT_BYTES = 56 * 1024 * 1024
POOL_HISTORY = 32
SC_CHUNK = 64
SEQ_BLOCK = 512
Q_BLOCK = 512

F32 = jnp.float32
BF16 = jnp.bfloat16
NEG_INF = float("-inf")


def _dot(a, b):
    return jnp.dot(a, b, preferred_element_type=F32)


def _dot_nt(a, b):
    return lax.dot_general(a, b, (((1,), (1,)), ((), ())), preferred_element_type=F32)


def _lane_iota(shape):
    return lax.broadcasted_iota(jnp.int32, shape, len(shape) - 1)


def _pair_attention(q2, keys, vals, bias=None):
    lane = _lane_iota(q2.shape)
    first = lane < HEAD_DIM
    outs = []
    for sel in (first, jnp.logical_not(first)):
        qh = jnp.where(sel, q2, 0.0).astype(BF16)
        s = _dot_nt(qh, keys)
        m = jnp.max(s, axis=-1, keepdims=True)
        p = jnp.exp(s - m)
        l = jnp.sum(p, axis=-1, keepdims=True)
        outs.append(_dot(p.astype(BF16), vals) / l)
    return jnp.where(first, outs[0], outs[1])


def _memory_attention(proj, kv_ref, cat_ref, out_off, q_off):
    for p in range(MEM_HEADS // 2):
        lo = p * V7X_LANES
        q2 = proj[:, q_off + lo:q_off + lo + V7X_LANES] * ATTN_SCALE
        keys = kv_ref[:, lo:lo + V7X_LANES]
        vals = kv_ref[:, MEM_WIDTH + lo:MEM_WIDTH + lo + V7X_LANES]
        cat_ref[:, out_off + lo:out_off + lo + V7X_LANES] = _pair_attention(q2, keys, vals).astype(BF16)


def _layer_norm(z, g, b):
    mu = jnp.mean(z, axis=-1, keepdims=True)
    zc = z - mu
    var = jnp.mean(zc * zc, axis=-1, keepdims=True)
    return zc * lax.rsqrt(var + LN_EPS) * g + b


def _pack_bf16_pairs(x):
    w = x.shape[1] // 2
    bits = lax.bitcast_convert_type(x.astype(BF16).astype(F32), jnp.uint32)
    return lax.shift_right_logical(bits[:, :w], jnp.uint32(16)) | (bits[:, w:] & jnp.uint32(0xFFFF0000))


def _unpack_bf16_pairs(p):
    low = lax.bitcast_convert_type(lax.shift_left(p, jnp.uint32(16)), F32)
    high = lax.bitcast_convert_type(p & jnp.uint32(0xFFFF0000), F32)
    return low, high


def _mix_epilogue(x, cat_ref, w_out_ref, g_ref, b_ref, rwt_ref, rb_ref, dn_alpha, h_out_ref, hp_ref, idx_ref,
                  gcol_ref):
    mix = _dot(cat_ref[...], w_out_ref[...])
    h1 = _layer_norm(dn_alpha * x + mix, g_ref[...], b_ref[...])
    h_out_ref[...] = h1
    hp_ref[...] = _pack_bf16_pairs(h1)
    logits = lax.dot_general(rwt_ref[...], h1, (((1,), (1,)), ((), ())),
                             precision=lax.Precision.HIGHEST, preferred_element_type=F32) + rb_ref[...]
    n_exp, t = logits.shape
    rows = lax.broadcasted_iota(jnp.int32, (n_exp, t), 0)
    vals, idxs = [], []
    for _ in range(TOP_K):
        m = jnp.max(logits, axis=0, keepdims=True)
        ix = jnp.min(jnp.where(logits == m, rows, n_exp), axis=0, keepdims=True)
        vals.append(m)
        idxs.append(ix)
        logits = jnp.where(rows == ix, NEG_INF, logits)
    exps = [jnp.exp(v - vals[0]) for v in vals]
    denom = exps[0] + exps[1] + exps[2] + exps[3]
    for k in range(TOP_K):
        idx_ref[k:k + 1, :] = idxs[k]
    lane_rows = lax.broadcasted_iota(jnp.int32, (V7X_LANES, t), 0)
    gates_t = jnp.zeros((V7X_LANES, t), F32)
    for k in range(TOP_K):
        gates_t = jnp.where(lane_rows == k, exps[k] / denom, gates_t)
    gcol_ref[...] = gates_t.T


def _pool_layer_kernel(dn_alpha, h_ref, mem_ref, w_in_ref, wbd_ref, pscale_ref, mem_kv_ref, w_out_ref, g_ref, b_ref,
                       rwt_ref, rb_ref, h_out_ref, hp_ref, idx_ref, gcol_ref,
                       ue_ref, w2_ref, w4_ref, w8_ref, kv_ref, cat_ref):
    s = pl.program_id(1)
    ts = h_ref.shape[0]
    pool_width = wbd_ref.shape[0]
    hist = POOL_HISTORY

    @pl.when(s == 0)
    def _():
        ue_ref[0:hist, :] = jnp.zeros((hist, pool_width), F32)
        kv_ref[...] = _dot(mem_ref[0].astype(BF16), mem_kv_ref[...]).astype(BF16)

    x = h_ref[...]
    proj = _dot(x.astype(BF16), w_in_ref[...])
    u = proj[:, :pool_width]
    ue_ref[hist:, :] = u
    n = ts + hist
    w2_ref[8:n, :] = ue_ref[8:n, :] + ue_ref[7:n - 1, :]
    w4_ref[16:n, :] = w2_ref[16:n, :] + w2_ref[14:n - 2, :]
    w8_ref[24:n, :] = w4_ref[24:n, :] + w4_ref[20:n - 4, :]
    w16 = w8_ref[hist:n, :] + w8_ref[hist - 8:n - 8, :]
    group_dim = pool_width // len(POOL_WINDOWS)
    col = _lane_iota((ts, pool_width))
    wsum = jnp.where(col < group_dim, w2_ref[hist:n, :],
                     jnp.where(col < 2 * group_dim, w4_ref[hist:n, :],
                               jnp.where(col < 3 * group_dim, w8_ref[hist:n, :], w16)))
    window = jnp.where(col < group_dim, POOL_WINDOWS[0],
                       jnp.where(col < 2 * group_dim, POOL_WINDOWS[1],
                                 jnp.where(col < 3 * group_dim, POOL_WINDOWS[2], POOL_WINDOWS[3])))
    tpos = s * ts + lax.broadcasted_iota(jnp.int32, (ts, pool_width), 0) + 1
    cnt = jnp.minimum(tpos, window).astype(F32)
    d = wsum / cnt - u
    ue_ref[0:hist, :] = u[ts - hist:, :]
    pooled = _dot(d.astype(BF16), wbd_ref[...]) * pscale_ref[...]
    cat_ref[:, :pool_width] = pooled.astype(BF16)
    _memory_attention(proj, kv_ref, cat_ref, pool_width, pool_width)
    _mix_epilogue(x, cat_ref, w_out_ref, g_ref, b_ref, rwt_ref, rb_ref, dn_alpha, h_out_ref, hp_ref, idx_ref,
                  gcol_ref)


def _bias_pieces(c):
    hi = c.astype(BF16).astype(F32)
    r = c - hi
    mid = r.astype(BF16).astype(F32)
    lo = (r - mid).astype(BF16).astype(F32)
    return hi, mid, lo


def _augment_head(x, c_col, is_query):
    lane = _lane_iota(x.shape)
    hi, mid, lo = _bias_pieces(jnp.broadcast_to(c_col, x.shape))
    b = HEAD_DIM
    if is_query:
        aug = jnp.where(lane == b, hi, jnp.where(lane == b + 1, mid, jnp.where(lane == b + 2, lo,
                        jnp.where(lane < b + 6, 1.0, 0.0))))
    else:
        aug = jnp.where(lane == b + 3, -hi, jnp.where(lane == b + 4, -mid, jnp.where(lane == b + 5, -lo,
                        jnp.where(lane < b + 3, 1.0, 0.0))))
    return jnp.where(lane < b, x, aug).astype(BF16)


def _shared_kv_kernel(h_ref, kvw_ref, fb_ref, tri_ref, k_ref, vt_ref, ccol_ref, carry_ref):
    s = pl.program_id(1)
    k_width = k_ref.shape[1]
    v_width = vt_ref.shape[1]

    @pl.when(s == 0)
    def _():
        carry_ref[...] = jnp.zeros(carry_ref.shape, F32)

    proj = _dot(h_ref[...].astype(BF16), kvw_ref[...])
    z = proj[:, k_width + v_width:] + fb_ref[...]
    logf = jnp.minimum(z, 0.0) - jnp.log(1.0 + jnp.exp(-jnp.abs(z)))
    hi, mid, lo = _bias_pieces(logf)
    tri = tri_ref[...]
    c = _dot(tri, hi.astype(BF16)) + _dot(tri, mid.astype(BF16)) + _dot(tri, lo.astype(BF16)) + carry_ref[0:1, :]
    ts = c.shape[0]
    carry_ref[0:1, :] = c[ts - 1:ts, :]
    ccol_ref[...] = c
    for h in range(k_width // V7X_LANES):
        cols = slice(h * V7X_LANES, (h + 1) * V7X_LANES)
        k_ref[:, cols] = _augment_head(proj[:, cols], c[:, h:h + 1], False)
    vt_ref[0] = proj[:, k_width:k_width + v_width].T.astype(BF16)


def _fox_layer_kernel(dn_alpha, h_ref, mem_ref, k_ref, vt_ref, ccol_ref, w_in_ref, mem_kv_ref, w_out_ref,
                      g_ref, b_ref, rwt_ref, rb_ref, h_out_ref, hp_ref, idx_ref, gcol_ref,
                      kv_ref, cat_ref, qa_ref, m_ref, l_ref, acc_ref):
    i = pl.program_id(1)
    tq = h_ref.shape[0]
    n_heads = k_ref.shape[2] // V7X_LANES
    fox_width = vt_ref.shape[1]

    @pl.when(i == 0)
    def _():
        kv_ref[...] = _dot(mem_ref[0].astype(BF16), mem_kv_ref[...]).astype(BF16)

    x = h_ref[...]
    proj = _dot(x.astype(BF16), w_in_ref[...])
    c_t = ccol_ref[...]
    for h in range(n_heads):
        cols = slice(h * V7X_LANES, (h + 1) * V7X_LANES)
        qa_ref[h] = _augment_head(proj[:, cols] * ATTN_SCALE, c_t[:, h:h + 1], True)
    m_ref[...] = jnp.full(m_ref.shape, NEG_INF, F32)
    l_ref[...] = jnp.zeros(l_ref.shape, F32)
    acc_ref[...] = jnp.zeros(acc_ref.shape, F32)
    causal_t = lax.broadcasted_iota(jnp.int32, (tq, tq), 0) <= lax.broadcasted_iota(jnp.int32, (tq, tq), 1)

    def tile(j, masked):
        start = pl.multiple_of(j * tq, tq)
        for h in range(n_heads):
            kb = k_ref[0, pl.ds(start, tq), h * V7X_LANES:(h + 1) * V7X_LANES]
            st = _dot_nt(kb, qa_ref[h])
            if masked:
                st = jnp.where(causal_t, st, NEG_INF)
            m_old = m_ref[h:h + 1, :]
            m_new = jnp.maximum(m_old, jnp.max(st, axis=0, keepdims=True))
            alpha = jnp.exp(m_old - m_new)
            pt = jnp.exp(st - m_new)
            l_ref[h:h + 1, :] = alpha * l_ref[h:h + 1, :] + jnp.sum(pt, axis=0, keepdims=True)
            m_ref[h:h + 1, :] = m_new
            rows = slice(h * HEAD_DIM, (h + 1) * HEAD_DIM)
            vb = vt_ref[0, rows, pl.ds(start, tq)]
            acc_ref[rows, :] = alpha * acc_ref[rows, :] + _dot(vb, pt.astype(BF16))

    tile(i, True)

    def off_diagonal(j, carry):
        tile(j, False)
        return carry

    lax.fori_loop(0, i, off_diagonal, 0)
    for h in range(n_heads):
        rows = slice(h * HEAD_DIM, (h + 1) * HEAD_DIM)
        acc_ref[rows, :] = acc_ref[rows, :] / l_ref[h:h + 1, :]
    cat_ref[:, :fox_width] = acc_ref[...].T.astype(BF16)
    _memory_attention(proj, kv_ref, cat_ref, fox_width, n_heads * V7X_LANES)
    _mix_epilogue(x, cat_ref, w_out_ref, g_ref, b_ref, rwt_ref, rb_ref, dn_alpha, h_out_ref, hp_ref, idx_ref,
                  gcol_ref)


def _expert_kernel(be_ref, nact_ref, x_ref, w1_ref, b1_ref, w2_ref, b2_ref, y_ref, w1s_ref, w2s_ref, act_ref):
    i = pl.program_id(0)
    active = i < nact_ref[0]
    two_f = w1_ref.shape[2]
    group = 2 * V7X_LANES
    n_groups = two_f // group

    @pl.when(jnp.logical_and(active, jnp.logical_or(i == 0, be_ref[i] != be_ref[jnp.maximum(i - 1, 0)])))
    def _():
        src = lax.broadcasted_iota(jnp.int32, (group, group), 0)
        dst = lax.broadcasted_iota(jnp.int32, (group, group), 1)
        want = jnp.where(dst < V7X_LANES, 2 * dst, 2 * (dst - V7X_LANES) + 1)
        perm = (src == want).astype(BF16)
        for g in range(n_groups):
            cols = slice(g * group, (g + 1) * group)
            w1s_ref[:, cols] = _dot(w1_ref[0, :, cols].astype(BF16), perm).astype(BF16)
        w2s_ref[...] = w2_ref[0].astype(BF16)

    @pl.when(active)
    def _():
        x_low, x_high = _unpack_bf16_pairs(x_ref[...])
        xb = jnp.concatenate([x_low.astype(BF16), x_high.astype(BF16)], axis=1)
        hdn = _dot(xb, w1s_ref[...]) + b1_ref[0]
        for g in range(n_groups):
            x_glu = jnp.minimum(hdn[:, g * group:g * group + V7X_LANES], SWIGLU_LIMIT)
            x_lin = jnp.clip(hdn[:, g * group + V7X_LANES:(g + 1) * group], -SWIGLU_LIMIT, SWIGLU_LIMIT)
            act = x_glu * jax.nn.sigmoid(SWIGLU_ALPHA * x_glu) * (x_lin + 1.0)
            act_ref[:, g * V7X_LANES:(g + 1) * V7X_LANES] = act.astype(BF16)
        y_ref[...] = _pack_bf16_pairs(_dot(act_ref[...], w2s_ref[...]) + b2_ref[0])

    @pl.when(jnp.logical_not(active))
    def _():
        y_ref[...] = jnp.zeros(y_ref.shape, jnp.uint32)


def _combine_kernel(dn_alpha, h_ref, y_ref, gcol_ref, g_ref, b_ref, o_ref):
    gates = gcol_ref[...]
    low, high = _unpack_bf16_pairs(y_ref[0])
    low, high = gates[:, 0:1] * low, gates[:, 0:1] * high
    for k in range(1, TOP_K):
        low_k, high_k = _unpack_bf16_pairs(y_ref[k])
        low = low + gates[:, k:k + 1] * low_k
        high = high + gates[:, k:k + 1] * high_k
    ffn = jnp.concatenate([low, high], axis=1)
    o_ref[...] = _layer_norm(dn_alpha * h_ref[...] + ffn, g_ref[...], b_ref[...])


def _sc_gather_rows(table, idx):
    n, = idx.shape
    _, d = table.shape
    info = plsc.get_sparse_core_info()
    n_workers = info.num_cores * info.num_subcores
    per_worker = n // n_workers
    n_chunks = per_worker // SC_CHUNK
    assert n % (n_workers * SC_CHUNK) == 0 and n_chunks % 2 == 0
    mesh = plsc.VectorSubcoreMesh(core_axis_name="core", subcore_axis_name="subcore")

    @functools.partial(
        pl.kernel,
        out_type=jax.ShapeDtypeStruct((n, d), table.dtype),
        mesh=mesh,
        scratch_types=[
            pltpu.VMEM((per_worker,), jnp.int32),
            pltpu.VMEM((2, SC_CHUNK, d), table.dtype),
            pltpu.SemaphoreType.DMA((2,)),
            pltpu.SemaphoreType.DMA((2,)),
        ],
    )
    def gather_kernel(table_hbm, idx_hbm, out_hbm, idx_v, rows_v, gather_sem, write_sem):
        worker = lax.axis_index("subcore") * info.num_cores + lax.axis_index("core")
        base = pl.multiple_of(worker * per_worker, SC_CHUNK)
        pltpu.sync_copy(idx_hbm.at[pl.ds(base, per_worker)], idx_v)

        def gather(c, buf):
            rows = idx_v.at[pl.ds(pl.multiple_of(c * SC_CHUNK, SC_CHUNK), SC_CHUNK)]
            return pltpu.make_async_copy(table_hbm.at[rows], rows_v.at[buf], gather_sem.at[buf])

        def write(c, buf):
            off = pl.multiple_of(base + c * SC_CHUNK, SC_CHUNK)
            return pltpu.make_async_copy(rows_v.at[buf], out_hbm.at[pl.ds(off, SC_CHUNK)], write_sem.at[buf])

        gather(0, 0).start()

        @pl.loop(0, n_chunks, step=2)
        def _(c0):
            for buf in range(2):
                c = c0 + buf
                gather(c, buf).wait()

                @pl.when(c >= 1)
                def _():
                    write(c - 1, 1 - buf).wait()

                @pl.when(c + 1 < n_chunks)
                def _():
                    gather(c + 1, 1 - buf).start()

                write(c, buf).start()

        write(n_chunks - 1, 1).wait()

    return gather_kernel(table, idx)


def _const_spec(shape):
    zeros = (0,) * len(shape)
    return pl.BlockSpec(shape, lambda *_: zeros, pipeline_mode=pl.Buffered(1))


def _tc_params(n_axes):
    return pltpu.CompilerParams(dimension_semantics=("arbitrary",) * n_axes,
                                vmem_limit_bytes=V7X_VMEM_LIMIT_BYTES)


def _mix_out_shapes(t, d):
    return (jax.ShapeDtypeStruct((t, d), F32),
            jax.ShapeDtypeStruct((t, d // 2), jnp.uint32),
            jax.ShapeDtypeStruct((TOP_K, t), jnp.int32),
            jax.ShapeDtypeStruct((t, V7X_LANES), F32))


def _mix_out_specs(rows, d, row, tok_col):
    return (pl.BlockSpec((rows, d), row),
            pl.BlockSpec((rows, d // 2), row),
            pl.BlockSpec((TOP_K, rows), tok_col),
            pl.BlockSpec((rows, V7X_LANES), row))


def _pool_layer(h, mem, w_in, wbd, pscale, mem_kv, w_out, ln_g, ln_b, rwt, rb, dn_alpha, seq):
    t, d = h.shape
    n_mem = mem.shape[1]
    pool_width = wbd.shape[0]
    ts = SEQ_BLOCK
    nsb = seq // ts
    row = lambda b, s: (b * nsb + s, 0)
    return pl.pallas_call(
        functools.partial(_pool_layer_kernel, dn_alpha),
        out_shape=_mix_out_shapes(t, d),
        grid=(t // seq, nsb),
        in_specs=[
            pl.BlockSpec((ts, d), row),
            pl.BlockSpec((1, n_mem, d), lambda b, s: (b, 0, 0)),
            _const_spec(w_in.shape), _const_spec(wbd.shape), _const_spec(pscale.shape), _const_spec(mem_kv.shape),
            _const_spec(w_out.shape), _const_spec(ln_g.shape), _const_spec(ln_b.shape), _const_spec(rwt.shape),
            _const_spec(rb.shape),
        ],
        out_specs=_mix_out_specs(ts, d, row, lambda b, s: (0, b * nsb + s)),
        scratch_shapes=[pltpu.VMEM((ts + POOL_HISTORY, pool_width), F32)] * 4
        + [pltpu.VMEM((n_mem, 2 * MEM_WIDTH), BF16), pltpu.VMEM((ts, d), BF16)],
        compiler_params=_tc_params(2),
        name="pool_layer",
    )(h, mem, w_in, wbd, pscale, mem_kv, w_out, ln_g, ln_b, rwt, rb)


def _expand_heads(w, n_heads):
    d = w.shape[0]
    w3 = w.reshape(d, n_heads, HEAD_DIM)
    return jnp.pad(w3, ((0, 0), (0, 0), (0, V7X_LANES - HEAD_DIM))).reshape(d, n_heads * V7X_LANES)


def _shared_kv(h, kvw, fb, seq, n_heads):
    t, d = h.shape
    ts = SEQ_BLOCK
    nsb = seq // ts
    k_width = n_heads * V7X_LANES
    v_width = n_heads * HEAD_DIM
    tri = (lax.broadcasted_iota(jnp.int32, (ts, ts), 1) <= lax.broadcasted_iota(jnp.int32, (ts, ts), 0)).astype(BF16)
    row = lambda b, s: (b * nsb + s, 0)
    return pl.pallas_call(
        _shared_kv_kernel,
        out_shape=(jax.ShapeDtypeStruct((t, k_width), BF16),
                   jax.ShapeDtypeStruct((t // seq, v_width, seq), BF16),
                   jax.ShapeDtypeStruct((t, V7X_LANES), F32)),
        grid=(t // seq, nsb),
        in_specs=[pl.BlockSpec((ts, d), row), _const_spec(kvw.shape), _const_spec(fb.shape), _const_spec(tri.shape)],
        out_specs=(pl.BlockSpec((ts, k_width), row),
                   pl.BlockSpec((1, v_width, ts), lambda b, s: (b, 0, s)),
                   pl.BlockSpec((ts, V7X_LANES), row)),
        scratch_shapes=[pltpu.VMEM((8, V7X_LANES), F32)],
        compiler_params=_tc_params(2),
        name="shared_kv",
    )(h, kvw, fb, tri)


def _fox_layer(h, mem, k_aug, v_t, ccol, w_in, mem_kv, w_out, ln_g, ln_b, rwt, rb, dn_alpha, seq):
    t, d = h.shape
    n_mem = mem.shape[1]
    k_width = k_aug.shape[1]
    v_width = v_t.shape[1]
    n_heads = k_width // V7X_LANES
    tq = Q_BLOCK
    nqb = seq // tq
    n_batch = t // seq
    row = lambda b, i: (b * nqb + i, 0)
    per_batch = lambda b, i: (b, 0, 0)
    return pl.pallas_call(
        functools.partial(_fox_layer_kernel, dn_alpha),
        out_shape=_mix_out_shapes(t, d),
        grid=(n_batch, nqb),
        in_specs=[
            pl.BlockSpec((tq, d), row),
            pl.BlockSpec((1, n_mem, d), per_batch),
            pl.BlockSpec((1, seq, k_width), per_batch),
            pl.BlockSpec((1, v_width, seq), per_batch),
            pl.BlockSpec((tq, V7X_LANES), row),
            _const_spec(w_in.shape), _const_spec(mem_kv.shape), _const_spec(w_out.shape), _const_spec(ln_g.shape),
            _const_spec(ln_b.shape), _const_spec(rwt.shape), _const_spec(rb.shape),
        ],
        out_specs=_mix_out_specs(tq, d, row, lambda b, i: (0, b * nqb + i)),
        scratch_shapes=[pltpu.VMEM((n_mem, 2 * MEM_WIDTH), BF16), pltpu.VMEM((tq, d), BF16),
                        pltpu.VMEM((n_heads, tq, V7X_LANES), BF16),
                        pltpu.VMEM((16, tq), F32), pltpu.VMEM((16, tq), F32), pltpu.VMEM((v_width, tq), F32)],
        compiler_params=_tc_params(2),
        name="fox_layer",
    )(h, mem, k_aug.reshape(n_batch, seq, k_width), v_t, ccol, w_in, mem_kv, w_out, ln_g, ln_b, rwt, rb)


def _route(idx_t, n_experts):
    top_k, t = idx_t.shape
    n_assign = top_k * t
    e_flat = idx_t.reshape(-1)
    iota = jnp.arange(n_assign, dtype=jnp.int32)
    _, order = lax.sort((e_flat, iota), num_keys=1)
    _, inv = lax.sort((order, iota), num_keys=1)
    counts = jnp.sum((e_flat[None, :] == jnp.arange(n_experts, dtype=jnp.int32)[:, None]).astype(jnp.int32), axis=1)
    padded = ((counts + MOE_BLOCK - 1) // MOE_BLOCK) * MOE_BLOCK
    start = jnp.cumsum(counts) - counts
    pad_end = jnp.cumsum(padded)
    pad_start = pad_end - padded
    n_blocks = n_assign // MOE_BLOCK + n_experts
    block_start = jnp.arange(n_blocks, dtype=jnp.int32) * MOE_BLOCK
    block_expert = jnp.minimum(jnp.sum((pad_end[None, :] <= block_start[:, None]).astype(jnp.int32), axis=1),
                               n_experts - 1)
    slot = jnp.arange(n_blocks * MOE_BLOCK, dtype=jnp.int32)
    slot_e = jnp.repeat(block_expert, MOE_BLOCK)
    j = slot - pad_start[slot_e]
    valid = j < counts[slot_e]
    src = jnp.clip(start[slot_e] + j, 0, n_assign - 1)
    slot_tok = jnp.where(valid, order[src] % t, slot % t).astype(jnp.int32)
    pos = (pad_start[e_flat] + inv - start[e_flat]).astype(jnp.int32)
    n_active = (pad_end[-1:] // MOE_BLOCK).astype(jnp.int32)
    return slot_tok, pos, block_expert, n_active


def _experts(xs, block_expert, n_active, w1_all, b1, w2_all, b2, layer):
    n_slots, half_d = xs.shape
    _, n_exp, d, two_f = w1_all.shape
    d_ff = two_f // 2
    assert d == 2 * half_d
    grid_spec = pltpu.PrefetchScalarGridSpec(
        num_scalar_prefetch=2,
        grid=(n_slots // MOE_BLOCK,),
        in_specs=[
            pl.BlockSpec((MOE_BLOCK, half_d), lambda i, be, na: (i, 0)),
            pl.BlockSpec((None, 1, d, two_f), lambda i, be, na: (layer, be[i], 0, 0)),
            pl.BlockSpec((1, 1, two_f), lambda i, be, na: (be[i], 0, 0)),
            pl.BlockSpec((None, 1, d_ff, d), lambda i, be, na: (layer, be[i], 0, 0)),
            pl.BlockSpec((1, 1, d), lambda i, be, na: (be[i], 0, 0)),
        ],
        out_specs=pl.BlockSpec((MOE_BLOCK, half_d), lambda i, be, na: (i, 0)),
        scratch_shapes=[pltpu.VMEM((d, two_f), BF16), pltpu.VMEM((d_ff, d), BF16), pltpu.VMEM((MOE_BLOCK, d_ff), BF16)],
    )
    return pl.pallas_call(
        _expert_kernel,
        out_shape=jax.ShapeDtypeStruct((n_slots, half_d), jnp.uint32),
        grid_spec=grid_spec,
        compiler_params=_tc_params(1),
        name="experts",
    )(block_expert, n_active, xs, w1_all, b1.reshape(n_exp, 1, two_f), w2_all, b2.reshape(n_exp, 1, d))


def _combine(h, yg, gcol, ln_g, ln_b, dn_alpha):
    t, d = h.shape
    ts = SEQ_BLOCK
    row = lambda i: (i, 0)
    return pl.pallas_call(
        functools.partial(_combine_kernel, dn_alpha),
        out_shape=jax.ShapeDtypeStruct((t, d), F32),
        grid=(t // ts,),
        in_specs=[pl.BlockSpec((ts, d), row), pl.BlockSpec((TOP_K, ts, d // 2), lambda i: (0, i, 0)),
                  pl.BlockSpec((ts, V7X_LANES), row), _const_spec(ln_g.shape), _const_spec(ln_b.shape)],
        out_specs=pl.BlockSpec((ts, d), row),
        compiler_params=_tc_params(1),
        name="combine",
    )(h, yg.reshape(TOP_K, t, d // 2), gcol, ln_g, ln_b)


def _moe(h1, h1_packed, idx_t, gcol, w1_all, b1, w2_all, b2, layer, ln_g, ln_b, dn_alpha):
    n_exp = w1_all.shape[1]
    d_ff = w2_all.shape[2]
    slot_tok, pos, block_expert, n_active = _route(idx_t, n_exp)
    b1p = jnp.swapaxes(b1.reshape(n_exp, d_ff // V7X_LANES, V7X_LANES, 2), 2, 3).reshape(n_exp, 2 * d_ff)
    xs = _sc_gather_rows(h1_packed, slot_tok)
    y = _experts(xs, block_expert, n_active, w1_all, b1p, w2_all, b2, layer)
    yg = _sc_gather_rows(y, pos)
    return _combine(h1, yg, gcol, ln_g, ln_b, dn_alpha)


def kernel(x, mem, ln1_g, ln1_b, ln2_g, ln2_b, a_w_in, pool_w, pool_scale, a_mem_kv, a_w_out, kv_w, fgate_b, b_w_in, b_mem_kv, b_w_out, router_w, router_b, moe_w1, moe_b1, moe_w2, moe_b2):
    n_batch, seq, d = x.shape
    t = n_batch * seq
    depth = ln1_g.shape[0]
    n_a = a_w_in.shape[0]
    dn_alpha = float((2 * depth) ** 0.25)
    n_exp = router_w.shape[2]
    fox_heads = fgate_b.shape[0]
    fox_width = fox_heads * HEAD_DIM
    assert seq % SEQ_BLOCK == 0 and seq % Q_BLOCK == 0 and fox_heads <= 16

    h = x.reshape(t, d)
    row = lambda a: a.reshape(1, -1)
    shared = None
    for l in range(depth):
        rwt = router_w[l].T
        rb = router_b[l].reshape(n_exp, 1)
        if l < n_a:
            wbd = jax.scipy.linalg.block_diag(*[pool_w[l, g] for g in range(pool_w.shape[1])]).astype(BF16)
            h1, h1_packed, idx_t, gcol = _pool_layer(
                h, mem, a_w_in[l].astype(BF16), wbd, row(pool_scale[l]), a_mem_kv[l].astype(BF16),
                a_w_out[l].astype(BF16), row(ln1_g[l]), row(ln1_b[l]), rwt, rb, dn_alpha, seq)
        else:
            j = l - n_a
            if shared is None:
                pad = V7X_LANES - fox_heads
                kvw = jnp.concatenate([_expand_heads(kv_w[:, :fox_width], fox_heads),
                                       kv_w[:, fox_width:2 * fox_width],
                                       jnp.pad(kv_w[:, 2 * fox_width:], ((0, 0), (0, pad)))], axis=1).astype(BF16)
                fb = jnp.pad(fgate_b, (0, pad)).reshape(1, V7X_LANES)
                shared = _shared_kv(h, kvw, fb, seq, fox_heads)
            k_aug, v_t, ccol = shared
            w_in = jnp.concatenate([_expand_heads(b_w_in[j][:, :fox_width], fox_heads),
                                    b_w_in[j][:, fox_width:]], axis=1).astype(BF16)
            h1, h1_packed, idx_t, gcol = _fox_layer(
                h, mem, k_aug, v_t, ccol, w_in, b_mem_kv[j].astype(BF16),
                b_w_out[j].astype(BF16), row(ln1_g[l]), row(ln1_b[l]), rwt, rb, dn_alpha, seq)
        h = _moe(h1, h1_packed, idx_t, gcol, moe_w1, moe_b1[l], moe_w2, moe_b2[l], l, row(ln2_g[l]), row(ln2_b[l]), dn_alpha)
    return h.reshape(n_batch, seq, d)
```

```python
import functools
import jax
import jax.numpy as jnp
from jax import lax
from jax.experimental import pallas as pl
from jax.experimental.pallas import tpu as pltpu
from jax.experimental.pallas import tpu_sc as plsc

HEAD_DIM = 64
MEM_HEADS = 4
MEM_WIDTH = MEM_HEADS * HEAD_DIM
POOL_WINDOWS = (2, 4, 8, 16)
TOP_K = 4
SWIGLU_ALPHA = 1.702
SWIGLU_LIMIT = 7.0
MOE_BLOCK = 512
LN_EPS = 1e-5
ATTN_SCALE = HEAD_DIM ** -0.5

V7X_LANES = 128
V7X_VMEM_LIMIT_BYTES = 56 * 1024 * 1024
POOL_HISTORY = 32
SC_CHUNK = 64
SEQ_BLOCK = 512
Q_BLOCK = 512

F32 = jnp.float32
BF16 = jnp.bfloat16
NEG_INF = float("-inf")


def _dot(a, b):
    return jnp.dot(a, b, preferred_element_type=F32)


def _dot_nt(a, b):
    return lax.dot_general(a, b, (((1,), (1,)), ((), ())), preferred_element_type=F32)


def _lane_iota(shape):
    return lax.broadcasted_iota(jnp.int32, shape, len(shape) - 1)


def _pair_attention(q2, keys, vals, bias=None):
    lane = _lane_iota(q2.shape)
    first = lane < HEAD_DIM
    outs = []
    for sel in (first, jnp.logical_not(first)):
        qh = jnp.where(sel, q2, 0.0).astype(BF16)
        s = _dot_nt(qh, keys)
        m = jnp.max(s, axis=-1, keepdims=True)
        p = jnp.exp(s - m)
        l = jnp.sum(p, axis=-1, keepdims=True)
        outs.append(_dot(p.astype(BF16), vals) / l)
    return jnp.where(first, outs[0], outs[1])


def _memory_attention(proj, kv_ref, cat_ref, out_off, q_off):
    for p in range(MEM_HEADS // 2):
        lo = p * V7X_LANES
        q2 = proj[:, q_off + lo:q_off + lo + V7X_LANES] * ATTN_SCALE
        keys = kv_ref[:, lo:lo + V7X_LANES]
        vals = kv_ref[:, MEM_WIDTH + lo:MEM_WIDTH + lo + V7X_LANES]
        cat_ref[:, out_off + lo:out_off + lo + V7X_LANES] = _pair_attention(q2, keys, vals).astype(BF16)


def _layer_norm(z, g, b):
    mu = jnp.mean(z, axis=-1, keepdims=True)
    zc = z - mu
    var = jnp.mean(zc * zc, axis=-1, keepdims=True)
    return zc * lax.rsqrt(var + LN_EPS) * g + b


def _pack_bf16_pairs(x):
    w = x.shape[1] // 2
    bits = lax.bitcast_convert_type(x.astype(BF16).astype(F32), jnp.uint32)
    return lax.shift_right_logical(bits[:, :w], jnp.uint32(16)) | (bits[:, w:] & jnp.uint32(0xFFFF0000))


def _unpack_bf16_pairs(p):
    low = lax.bitcast_convert_type(lax.shift_left(p, jnp.uint32(16)), F32)
    high = lax.bitcast_convert_type(p & jnp.uint32(0xFFFF0000), F32)
    return low, high


def _mix_epilogue(x, cat_ref, w_out_ref, g_ref, b_ref, rwt_ref, rb_ref, dn_alpha, h_out_ref, hp_ref, idx_ref,
                  gcol_ref):
    mix = _dot(cat_ref[...], w_out_ref[...])
    h1 = _layer_norm(dn_alpha * x + mix, g_ref[...], b_ref[...])
    h_out_ref[...] = h1
    hp_ref[...] = _pack_bf16_pairs(h1)
    logits = lax.dot_general(rwt_ref[...], h1, (((1,), (1,)), ((), ())),
                             precision=lax.Precision.HIGHEST, preferred_element_type=F32) + rb_ref[...]
    n_exp, t = logits.shape
    rows = lax.broadcasted_iota(jnp.int32, (n_exp, t), 0)
    vals, idxs = [], []
    for _ in range(TOP_K):
        m = jnp.max(logits, axis=0, keepdims=True)
        ix = jnp.min(jnp.where(logits == m, rows, n_exp), axis=0, keepdims=True)
        vals.append(m)
        idxs.append(ix)
        logits = jnp.where(rows == ix, NEG_INF, logits)
    exps = [jnp.exp(v - vals[0]) for v in vals]
    denom = exps[0] + exps[1] + exps[2] + exps[3]
    for k in range(TOP_K):
        idx_ref[k:k + 1, :] = idxs[k]
    lane_rows = lax.broadcasted_iota(jnp.int32, (V7X_LANES, t), 0)
    gates_t = jnp.zeros((V7X_LANES, t), F32)
    for k in range(TOP_K):
        gates_t = jnp.where(lane_rows == k, exps[k] / denom, gates_t)
    gcol_ref[...] = gates_t.T


def _pool_layer_kernel(dn_alpha, h_ref, mem_ref, w_in_ref, wbd_ref, pscale_ref, mem_kv_ref, w_out_ref, g_ref, b_ref,
                       rwt_ref, rb_ref, h_out_ref, hp_ref, idx_ref, gcol_ref,
                       ue_ref, w2_ref, w4_ref, w8_ref, kv_ref, cat_ref):
    s = pl.program_id(1)
    ts = h_ref.shape[0]
    pool_width = wbd_ref.shape[0]
    hist = POOL_HISTORY

    @pl.when(s == 0)
    def _():
        ue_ref[0:hist, :] = jnp.zeros((hist, pool_width), F32)
        kv_ref[...] = _dot(mem_ref[0].astype(BF16), mem_kv_ref[...]).astype(BF16)

    x = h_ref[...]
    proj = _dot(x.astype(BF16), w_in_ref[...])
    u = proj[:, :pool_width]
    ue_ref[hist:, :] = u
    n = ts + hist
    w2_ref[8:n, :] = ue_ref[8:n, :] + ue_ref[7:n - 1, :]
    w4_ref[16:n, :] = w2_ref[16:n, :] + w2_ref[14:n - 2, :]
    w8_ref[24:n, :] = w4_ref[24:n, :] + w4_ref[20:n - 4, :]
    w16 = w8_ref[hist:n, :] + w8_ref[hist - 8:n - 8, :]
    group_dim = pool_width // len(POOL_WINDOWS)
    col = _lane_iota((ts, pool_width))
    wsum = jnp.where(col < group_dim, w2_ref[hist:n, :],
                     jnp.where(col < 2 * group_dim, w4_ref[hist:n, :],
                               jnp.where(col < 3 * group_dim, w8_ref[hist:n, :], w16)))
    window = jnp.where(col < group_dim, POOL_WINDOWS[0],
                       jnp.where(col < 2 * group_dim, POOL_WINDOWS[1],
                                 jnp.where(col < 3 * group_dim, POOL_WINDOWS[2], POOL_WINDOWS[3])))
    tpos = s * ts + lax.broadcasted_iota(jnp.int32, (ts, pool_width), 0) + 1
    cnt = jnp.minimum(tpos, window).astype(F32)
    d = wsum / cnt - u
    ue_ref[0:hist, :] = u[ts - hist:, :]
    pooled = _dot(d.astype(BF16), wbd_ref[...]) * pscale_ref[...]
    cat_ref[:, :pool_width] = pooled.astype(BF16)
    _memory_attention(proj, kv_ref, cat_ref, pool_width, pool_width)
    _mix_epilogue(x, cat_ref, w_out_ref, g_ref, b_ref, rwt_ref, rb_ref, dn_alpha, h_out_ref, hp_ref, idx_ref,
                  gcol_ref)


def _bias_pieces(c):
    hi = c.astype(BF16).astype(F32)
    r = c - hi
    mid = r.astype(BF16).astype(F32)
    lo = (r - mid).astype(BF16).astype(F32)
    return hi, mid, lo


def _augment_head(x, c_col, is_query):
    lane = _lane_iota(x.shape)
    hi, mid, lo = _bias_pieces(jnp.broadcast_to(c_col, x.shape))
    b = HEAD_DIM
    if is_query:
        aug = jnp.where(lane == b, hi, jnp.where(lane == b + 1, mid, jnp.where(lane == b + 2, lo,
                        jnp.where(lane < b + 6, 1.0, 0.0))))
    else:
        aug = jnp.where(lane == b + 3, -hi, jnp.where(lane == b + 4, -mid, jnp.where(lane == b + 5, -lo,
                        jnp.where(lane < b + 3, 1.0, 0.0))))
    return jnp.where(lane < b, x, aug).astype(BF16)


def _shared_kv_kernel(h_ref, kvw_ref, fb_ref, tri_ref, k_ref, vt_ref, ccol_ref, carry_ref):
    s = pl.program_id(1)
    k_width = k_ref.shape[1]
    v_width = vt_ref.shape[1]

    @pl.when(s == 0)
    def _():
        carry_ref[...] = jnp.zeros(carry_ref.shape, F32)

    proj = _dot(h_ref[...].astype(BF16), kvw_ref[...])
    z = proj[:, k_width + v_width:] + fb_ref[...]
    logf = jnp.minimum(z, 0.0) - jnp.log(1.0 + jnp.exp(-jnp.abs(z)))
    hi, mid, lo = _bias_pieces(logf)
    tri = tri_ref[...]
    c = _dot(tri, hi.astype(BF16)) + _dot(tri, mid.astype(BF16)) + _dot(tri, lo.astype(BF16)) + carry_ref[0:1, :]
    ts = c.shape[0]
    carry_ref[0:1, :] = c[ts - 1:ts, :]
    ccol_ref[...] = c
    for h in range(k_width // V7X_LANES):
        cols = slice(h * V7X_LANES, (h + 1) * V7X_LANES)
        k_ref[:, cols] = _augment_head(proj[:, cols], c[:, h:h + 1], False)
    vt_ref[0] = proj[:, k_width:k_width + v_width].T.astype(BF16)


def _fox_layer_kernel(dn_alpha, h_ref, mem_ref, k_ref, vt_ref, ccol_ref, w_in_ref, mem_kv_ref, w_out_ref,
                      g_ref, b_ref, rwt_ref, rb_ref, h_out_ref, hp_ref, idx_ref, gcol_ref,
                      kv_ref, cat_ref, qa_ref, m_ref, l_ref, acc_ref):
    i = pl.program_id(1)
    tq = h_ref.shape[0]
    n_heads = k_ref.shape[2] // V7X_LANES
    fox_width = vt_ref.shape[1]

    @pl.when(i == 0)
    def _():
        kv_ref[...] = _dot(mem_ref[0].astype(BF16), mem_kv_ref[...]).astype(BF16)

    x = h_ref[...]
    proj = _dot(x.astype(BF16), w_in_ref[...])
    c_t = ccol_ref[...]
    for h in range(n_heads):
        cols = slice(h * V7X_LANES, (h + 1) * V7X_LANES)
        qa_ref[h] = _augment_head(proj[:, cols] * ATTN_SCALE, c_t[:, h:h + 1], True)
    m_ref[...] = jnp.full(m_ref.shape, NEG_INF, F32)
    l_ref[...] = jnp.zeros(l_ref.shape, F32)
    acc_ref[...] = jnp.zeros(acc_ref.shape, F32)
    causal_t = lax.broadcasted_iota(jnp.int32, (tq, tq), 0) <= lax.broadcasted_iota(jnp.int32, (tq, tq), 1)

    def tile(j, masked):
        start = pl.multiple_of(j * tq, tq)

        def scores(h):
            kb = k_ref[0, pl.ds(start, tq), h * V7X_LANES:(h + 1) * V7X_LANES]
            return _dot_nt(kb, qa_ref[h])

        st_next = scores(0)
        for h in range(n_heads):
            st = st_next
            if h + 1 < n_heads:
                st_next = scores(h + 1)
            if masked:
                st = jnp.where(causal_t, st, NEG_INF)
            m_old = m_ref[h:h + 1, :]
            m_new = jnp.maximum(m_old, jnp.max(st, axis=0, keepdims=True))
            alpha = jnp.exp(m_old - m_new)
            pt = jnp.exp(st - m_new)
            l_ref[h:h + 1, :] = alpha * l_ref[h:h + 1, :] + jnp.sum(pt, axis=0, keepdims=True)
            m_ref[h:h + 1, :] = m_new
            rows = slice(h * HEAD_DIM, (h + 1) * HEAD_DIM)
            vb = vt_ref[0, rows, pl.ds(start, tq)]
            acc_ref[rows, :] = alpha * acc_ref[rows, :] + _dot(vb, pt.astype(BF16))

    tile(i, True)

    def off_diagonal(j, carry):
        tile(j, False)
        return carry

    lax.fori_loop(0, i, off_diagonal, 0)
    for h in range(n_heads):
        rows = slice(h * HEAD_DIM, (h + 1) * HEAD_DIM)
        acc_ref[rows, :] = acc_ref[rows, :] / l_ref[h:h + 1, :]
    cat_ref[:, :fox_width] = acc_ref[...].T.astype(BF16)
    _memory_attention(proj, kv_ref, cat_ref, fox_width, n_heads * V7X_LANES)
    _mix_epilogue(x, cat_ref, w_out_ref, g_ref, b_ref, rwt_ref, rb_ref, dn_alpha, h_out_ref, hp_ref, idx_ref,
                  gcol_ref)


def _expert_kernel(be_ref, nact_ref, x_ref, w1_ref, b1_ref, w2_ref, b2_ref, y_ref, w1s_ref, w2s_ref, act_ref):
    i = pl.program_id(0)
    active = i < nact_ref[0]
    two_f = w1_ref.shape[2]
    group = 2 * V7X_LANES
    n_groups = two_f // group

    @pl.when(jnp.logical_and(active, jnp.logical_or(i == 0, be_ref[i] != be_ref[jnp.maximum(i - 1, 0)])))
    def _():
        src = lax.broadcasted_iota(jnp.int32, (group, group), 0)
        dst = lax.broadcasted_iota(jnp.int32, (group, group), 1)
        want = jnp.where(dst < V7X_LANES, 2 * dst, 2 * (dst - V7X_LANES) + 1)
        perm = (src == want).astype(BF16)
        for g in range(n_groups):
            cols = slice(g * group, (g + 1) * group)
            w1s_ref[:, cols] = _dot(w1_ref[0, :, cols].astype(BF16), perm).astype(BF16)
        w2s_ref[...] = w2_ref[0].astype(BF16)

    @pl.when(active)
    def _():
        x_low, x_high = _unpack_bf16_pairs(x_ref[...])
        xb = jnp.concatenate([x_low.astype(BF16), x_high.astype(BF16)], axis=1)
        hdn = _dot(xb, w1s_ref[...]) + b1_ref[0]
        for g in range(n_groups):
            x_glu = jnp.minimum(hdn[:, g * group:g * group + V7X_LANES], SWIGLU_LIMIT)
            x_lin = jnp.clip(hdn[:, g * group + V7X_LANES:(g + 1) * group], -SWIGLU_LIMIT, SWIGLU_LIMIT)
            act = x_glu * jax.nn.sigmoid(SWIGLU_ALPHA * x_glu) * (x_lin + 1.0)
            act_ref[:, g * V7X_LANES:(g + 1) * V7X_LANES] = act.astype(BF16)
        y_ref[...] = _pack_bf16_pairs(_dot(act_ref[...], w2s_ref[...]) + b2_ref[0])

    @pl.when(jnp.logical_not(active))
    def _():
        y_ref[...] = jnp.zeros(y_ref.shape, jnp.uint32)


def _combine_kernel(dn_alpha, h_ref, y_ref, gcol_ref, g_ref, b_ref, o_ref):
    gates = gcol_ref[...]
    low, high = _unpack_bf16_pairs(y_ref[0])
    low, high = gates[:, 0:1] * low, gates[:, 0:1] * high
    for k in range(1, TOP_K):
        low_k, high_k = _unpack_bf16_pairs(y_ref[k])
        low = low + gates[:, k:k + 1] * low_k
        high = high + gates[:, k:k + 1] * high_k
    ffn = jnp.concatenate([low, high], axis=1)
    o_ref[...] = _layer_norm(dn_alpha * h_ref[...] + ffn, g_ref[...], b_ref[...])


def _sc_gather_rows(table, idx):
    n, = idx.shape
    _, d = table.shape
    info = plsc.get_sparse_core_info()
    n_workers = info.num_cores * info.num_subcores
    per_worker = n // n_workers
    n_chunks = per_worker // SC_CHUNK
    assert n % (n_workers * SC_CHUNK) == 0 and n_chunks % 2 == 0
    mesh = plsc.VectorSubcoreMesh(core_axis_name="core", subcore_axis_name="subcore")

    @functools.partial(
        pl.kernel,
        out_type=jax.ShapeDtypeStruct((n, d), table.dtype),
        mesh=mesh,
        scratch_types=[
            pltpu.VMEM((per_worker,), jnp.int32),
            pltpu.VMEM((2, SC_CHUNK, d), table.dtype),
            pltpu.SemaphoreType.DMA((2,)),
            pltpu.SemaphoreType.DMA((2,)),
        ],
    )
    def gather_kernel(table_hbm, idx_hbm, out_hbm, idx_v, rows_v, gather_sem, write_sem):
        worker = lax.axis_index("subcore") * info.num_cores + lax.axis_index("core")
        base = pl.multiple_of(worker * per_worker, SC_CHUNK)
        pltpu.sync_copy(idx_hbm.at[pl.ds(base, per_worker)], idx_v)

        def gather(c, buf):
            rows = idx_v.at[pl.ds(pl.multiple_of(c * SC_CHUNK, SC_CHUNK), SC_CHUNK)]
            return pltpu.make_async_copy(table_hbm.at[rows], rows_v.at[buf], gather_sem.at[buf])

        def write(c, buf):
            off = pl.multiple_of(base + c * SC_CHUNK, SC_CHUNK)
            return pltpu.make_async_copy(rows_v.at[buf], out_hbm.at[pl.ds(off, SC_CHUNK)], write_sem.at[buf])

        gather(0, 0).start()

        @pl.loop(0, n_chunks, step=2)
        def _(c0):
            for buf in range(2):
                c = c0 + buf
                gather(c, buf).wait()

                @pl.when(c >= 1)
                def _():
                    write(c - 1, 1 - buf).wait()

                @pl.when(c + 1 < n_chunks)
                def _():
                    gather(c + 1, 1 - buf).start()

                write(c, buf).start()

        write(n_chunks - 1, 1).wait()

    return gather_kernel(table, idx)


def _const_spec(shape):
    zeros = (0,) * len(shape)
    return pl.BlockSpec(shape, lambda *_: zeros, pipeline_mode=pl.Buffered(1))


def _tc_params(n_axes):
    return pltpu.CompilerParams(dimension_semantics=("arbitrary",) * n_axes,
                                vmem_limit_bytes=V7X_VMEM_LIMIT_BYTES)


def _mix_out_shapes(t, d):
    return (jax.ShapeDtypeStruct((t, d), F32),
            jax.ShapeDtypeStruct((t, d // 2), jnp.uint32),
            jax.ShapeDtypeStruct((TOP_K, t), jnp.int32),
            jax.ShapeDtypeStruct((t, V7X_LANES), F32))


def _mix_out_specs(rows, d, row, tok_col):
    return (pl.BlockSpec((rows, d), row),
            pl.BlockSpec((rows, d // 2), row),
            pl.BlockSpec((TOP_K, rows), tok_col),
            pl.BlockSpec((rows, V7X_LANES), row))


def _pool_layer(h, mem, w_in, wbd, pscale, mem_kv, w_out, ln_g, ln_b, rwt, rb, dn_alpha, seq):
    t, d = h.shape
    n_mem = mem.shape[1]
    pool_width = wbd.shape[0]
    ts = SEQ_BLOCK
    nsb = seq // ts
    row = lambda b, s: (b * nsb + s, 0)
    return pl.pallas_call(
        functools.partial(_pool_layer_kernel, dn_alpha),
        out_shape=_mix_out_shapes(t, d),
        grid=(t // seq, nsb),
        in_specs=[
            pl.BlockSpec((ts, d), row),
            pl.BlockSpec((1, n_mem, d), lambda b, s: (b, 0, 0)),
            _const_spec(w_in.shape), _const_spec(wbd.shape), _const_spec(pscale.shape), _const_spec(mem_kv.shape),
            _const_spec(w_out.shape), _const_spec(ln_g.shape), _const_spec(ln_b.shape), _const_spec(rwt.shape),
            _const_spec(rb.shape),
        ],
        out_specs=_mix_out_specs(ts, d, row, lambda b, s: (0, b * nsb + s)),
        scratch_shapes=[pltpu.VMEM((ts + POOL_HISTORY, pool_width), F32)] * 4
        + [pltpu.VMEM((n_mem, 2 * MEM_WIDTH), BF16), pltpu.VMEM((ts, d), BF16)],
        compiler_params=_tc_params(2),
        name="pool_layer",
    )(h, mem, w_in, wbd, pscale, mem_kv, w_out, ln_g, ln_b, rwt, rb)


def _expand_heads(w, n_heads):
    d = w.shape[0]
    w3 = w.reshape(d, n_heads, HEAD_DIM)
    return jnp.pad(w3, ((0, 0), (0, 0), (0, V7X_LANES - HEAD_DIM))).reshape(d, n_heads * V7X_LANES)


def _shared_kv(h, kvw, fb, seq, n_heads):
    t, d = h.shape
    ts = SEQ_BLOCK
    nsb = seq // ts
    k_width = n_heads * V7X_LANES
    v_width = n_heads * HEAD_DIM
    tri = (lax.broadcasted_iota(jnp.int32, (ts, ts), 1) <= lax.broadcasted_iota(jnp.int32, (ts, ts), 0)).astype(BF16)
    row = lambda b, s: (b * nsb + s, 0)
    return pl.pallas_call(
        _shared_kv_kernel,
        out_shape=(jax.ShapeDtypeStruct((t, k_width), BF16),
                   jax.ShapeDtypeStruct((t // seq, v_width, seq), BF16),
                   jax.ShapeDtypeStruct((t, V7X_LANES), F32)),
        grid=(t // seq, nsb),
        in_specs=[pl.BlockSpec((ts, d), row), _const_spec(kvw.shape), _const_spec(fb.shape), _const_spec(tri.shape)],
        out_specs=(pl.BlockSpec((ts, k_width), row),
                   pl.BlockSpec((1, v_width, ts), lambda b, s: (b, 0, s)),
                   pl.BlockSpec((ts, V7X_LANES), row)),
        scratch_shapes=[pltpu.VMEM((8, V7X_LANES), F32)],
        compiler_params=_tc_params(2),
        name="shared_kv",
    )(h, kvw, fb, tri)


def _fox_layer(h, mem, k_aug, v_t, ccol, w_in, mem_kv, w_out, ln_g, ln_b, rwt, rb, dn_alpha, seq):
    t, d = h.shape
    n_mem = mem.shape[1]
    k_width = k_aug.shape[1]
    v_width = v_t.shape[1]
    n_heads = k_width // V7X_LANES
    tq = Q_BLOCK
    nqb = seq // tq
    n_batch = t // seq
    row = lambda b, i: (b * nqb + i, 0)
    per_batch = lambda b, i: (b, 0, 0)
    return pl.pallas_call(
        functools.partial(_fox_layer_kernel, dn_alpha),
        out_shape=_mix_out_shapes(t, d),
        grid=(n_batch, nqb),
        in_specs=[
            pl.BlockSpec((tq, d), row),
            pl.BlockSpec((1, n_mem, d), per_batch),
            pl.BlockSpec((1, seq, k_width), per_batch),
            pl.BlockSpec((1, v_width, seq), per_batch),
            pl.BlockSpec((tq, V7X_LANES), row),
            _const_spec(w_in.shape), _const_spec(mem_kv.shape), _const_spec(w_out.shape), _const_spec(ln_g.shape),
            _const_spec(ln_b.shape), _const_spec(rwt.shape), _const_spec(rb.shape),
        ],
        out_specs=_mix_out_specs(tq, d, row, lambda b, i: (0, b * nqb + i)),
        scratch_shapes=[pltpu.VMEM((n_mem, 2 * MEM_WIDTH), BF16), pltpu.VMEM((tq, d), BF16),
                        pltpu.VMEM((n_heads, tq, V7X_LANES), BF16),
                        pltpu.VMEM((16, tq), F32), pltpu.VMEM((16, tq), F32), pltpu.VMEM((v_width, tq), F32)],
        compiler_params=_tc_params(2),
        name="fox_layer",
    )(h, mem, k_aug.reshape(n_batch, seq, k_width), v_t, ccol, w_in, mem_kv, w_out, ln_g, ln_b, rwt, rb)


def _route(idx_t, n_experts):
    top_k, t = idx_t.shape
    n_assign = top_k * t
    e_flat = idx_t.reshape(-1)
    iota = jnp.arange(n_assign, dtype=jnp.int32)
    _, order = lax.sort((e_flat, iota), num_keys=1)
    _, inv = lax.sort((order, iota), num_keys=1)
    counts = jnp.sum((e_flat[None, :] == jnp.arange(n_experts, dtype=jnp.int32)[:, None]).astype(jnp.int32), axis=1)
    padded = ((counts + MOE_BLOCK - 1) // MOE_BLOCK) * MOE_BLOCK
    start = jnp.cumsum(counts) - counts
    pad_end = jnp.cumsum(padded)
    pad_start = pad_end - padded
    n_blocks = n_assign // MOE_BLOCK + n_experts
    block_start = jnp.arange(n_blocks, dtype=jnp.int32) * MOE_BLOCK
    block_expert = jnp.minimum(jnp.sum((pad_end[None, :] <= block_start[:, None]).astype(jnp.int32), axis=1),
                               n_experts - 1)
    slot = jnp.arange(n_blocks * MOE_BLOCK, dtype=jnp.int32)
    slot_e = jnp.repeat(block_expert, MOE_BLOCK)
    j = slot - pad_start[slot_e]
    valid = j < counts[slot_e]
    src = jnp.clip(start[slot_e] + j, 0, n_assign - 1)
    slot_tok = jnp.where(valid, order[src] % t, slot % t).astype(jnp.int32)
    pos = (pad_start[e_flat] + inv - start[e_flat]).astype(jnp.int32)
    n_active = (pad_end[-1:] // MOE_BLOCK).astype(jnp.int32)
    return slot_tok, pos, block_expert, n_active


def _experts(xs, block_expert, n_active, w1_all, b1, w2_all, b2, layer):
    n_slots, half_d = xs.shape
    _, n_exp, d, two_f = w1_all.shape
    d_ff = two_f // 2
    assert d == 2 * half_d
    grid_spec = pltpu.PrefetchScalarGridSpec(
        num_scalar_prefetch=2,
        grid=(n_slots // MOE_BLOCK,),
        in_specs=[
            pl.BlockSpec((MOE_BLOCK, half_d), lambda i, be, na: (i, 0)),
            pl.BlockSpec((None, 1, d, two_f), lambda i, be, na: (layer, be[i], 0, 0)),
            pl.BlockSpec((1, 1, two_f), lambda i, be, na: (be[i], 0, 0)),
            pl.BlockSpec((None, 1, d_ff, d), lambda i, be, na: (layer, be[i], 0, 0)),
            pl.BlockSpec((1, 1, d), lambda i, be, na: (be[i], 0, 0)),
        ],
        out_specs=pl.BlockSpec((MOE_BLOCK, half_d), lambda i, be, na: (i, 0)),
        scratch_shapes=[pltpu.VMEM((d, two_f), BF16), pltpu.VMEM((d_ff, d), BF16), pltpu.VMEM((MOE_BLOCK, d_ff), BF16)],
    )
    return pl.pallas_call(
        _expert_kernel,
        out_shape=jax.ShapeDtypeStruct((n_slots, half_d), jnp.uint32),
        grid_spec=grid_spec,
        compiler_params=_tc_params(1),
        name="experts",
    )(block_expert, n_active, xs, w1_all, b1.reshape(n_exp, 1, two_f), w2_all, b2.reshape(n_exp, 1, d))


def _combine(h, yg, gcol, ln_g, ln_b, dn_alpha):
    t, d = h.shape
    ts = SEQ_BLOCK
    row = lambda i: (i, 0)
    return pl.pallas_call(
        functools.partial(_combine_kernel, dn_alpha),
        out_shape=jax.ShapeDtypeStruct((t, d), F32),
        grid=(t // ts,),
        in_specs=[pl.BlockSpec((ts, d), row), pl.BlockSpec((TOP_K, ts, d // 2), lambda i: (0, i, 0)),
                  pl.BlockSpec((ts, V7X_LANES), row), _const_spec(ln_g.shape), _const_spec(ln_b.shape)],
        out_specs=pl.BlockSpec((ts, d), row),
        compiler_params=_tc_params(1),
        name="combine",
    )(h, yg.reshape(TOP_K, t, d // 2), gcol, ln_g, ln_b)


def _moe(h1, h1_packed, idx_t, gcol, w1_all, b1, w2_all, b2, layer, ln_g, ln_b, dn_alpha):
    n_exp = w1_all.shape[1]
    d_ff = w2_all.shape[2]
    slot_tok, pos, block_expert, n_active = _route(idx_t, n_exp)
    b1p = jnp.swapaxes(b1.reshape(n_exp, d_ff // V7X_LANES, V7X_LANES, 2), 2, 3).reshape(n_exp, 2 * d_ff)
    xs = _sc_gather_rows(h1_packed, slot_tok)
    y = _experts(xs, block_expert, n_active, w1_all, b1p, w2_all, b2, layer)
    yg = _sc_gather_rows(y, pos)
    return _combine(h1, yg, gcol, ln_g, ln_b, dn_alpha)


def kernel(x, mem, ln1_g, ln1_b, ln2_g, ln2_b, a_w_in, pool_w, pool_scale, a_mem_kv, a_w_out, kv_w, fgate_b, b_w_in, b_mem_kv, b_w_out, router_w, router_b, moe_w1, moe_b1, moe_w2, moe_b2):
    n_batch, seq, d = x.shape
    t = n_batch * seq
    depth = ln1_g.shape[0]
    n_a = a_w_in.shape[0]
    dn_alpha = float((2 * depth) ** 0.25)
    n_exp = router_w.shape[2]
    fox_heads = fgate_b.shape[0]
    fox_width = fox_heads * HEAD_DIM
    assert seq % SEQ_BLOCK == 0 and seq % Q_BLOCK == 0 and fox_heads <= 16

    h = x.reshape(t, d)
    row = lambda a: a.reshape(1, -1)
    shared = None
    for l in range(depth):
        rwt = router_w[l].T
        rb = router_b[l].reshape(n_exp, 1)
        if l < n_a:
            wbd = jax.scipy.linalg.block_diag(*[pool_w[l, g] for g in range(pool_w.shape[1])]).astype(BF16)
            h1, h1_packed, idx_t, gcol = _pool_layer(
                h, mem, a_w_in[l].astype(BF16), wbd, row(pool_scale[l]), a_mem_kv[l].astype(BF16),
                a_w_out[l].astype(BF16), row(ln1_g[l]), row(ln1_b[l]), rwt, rb, dn_alpha, seq)
        else:
            j = l - n_a
            if shared is None:
                pad = V7X_LANES - fox_heads
                kvw = jnp.concatenate([_expand_heads(kv_w[:, :fox_width], fox_heads),
                                       kv_w[:, fox_width:2 * fox_width],
                                       jnp.pad(kv_w[:, 2 * fox_width:], ((0, 0), (0, pad)))], axis=1).astype(BF16)
                fb = jnp.pad(fgate_b, (0, pad)).reshape(1, V7X_LANES)
                shared = _shared_kv(h, kvw, fb, seq, fox_heads)
            k_aug, v_t, ccol = shared
            w_in = jnp.concatenate([_expand_heads(b_w_in[j][:, :fox_width], fox_heads),
                                    b_w_in[j][:, fox_width:]], axis=1).astype(BF16)
            h1, h1_packed, idx_t, gcol = _fox_layer(
                h, mem, k_aug, v_t, ccol, w_in, b_mem_kv[j].astype(BF16),
                b_w_out[j].astype(BF16), row(ln1_g[l]), row(ln1_b[l]), rwt, rb, dn_alpha, seq)
        h = _moe(h1, h1_packed, idx_t, gcol, moe_w1, moe_b1[l], moe_w2, moe_b2[l], l, row(ln2_g[l]), row(ln2_b[l]), dn_alpha)
    return h.reshape(n_batch, seq, d)
```

```python
import functools
import jax
import jax.numpy as jnp
from jax import lax
from jax.experimental import pallas as pl
from jax.experimental.pallas import tpu as pltpu
from jax.experimental.pallas import tpu_sc as plsc

HEAD_DIM = 64
MEM_HEADS = 4
MEM_WIDTH = MEM_HEADS * HEAD_DIM
POOL_WINDOWS = (2, 4, 8, 16)
TOP_K = 4
SWIGLU_ALPHA = 1.702
SWIGLU_LIMIT = 7.0
MOE_BLOCK = 512
LN_EPS = 1e-5
ATTN_SCALE = HEAD_DIM ** -0.5

V7X_LANES = 128
V7X_VMEM_LIMIT_BYTES = 56 * 1024 * 1024
POOL_HISTORY = 32
SC_CHUNK = 64
SEQ_BLOCK = 512
Q_BLOCK = 512
SLOT_COLS = 8192

F32 = jnp.float32
BF16 = jnp.bfloat16
NEG_INF = float("-inf")


def _dot(a, b):
    return jnp.dot(a, b, preferred_element_type=F32)


def _dot_nt(a, b):
    return lax.dot_general(a, b, (((1,), (1,)), ((), ())), preferred_element_type=F32)


def _lane_iota(shape):
    return lax.broadcasted_iota(jnp.int32, shape, len(shape) - 1)


def _pair_attention(q2, keys, vals, bias=None):
    lane = _lane_iota(q2.shape)
    first = lane < HEAD_DIM
    outs = []
    for sel in (first, jnp.logical_not(first)):
        qh = jnp.where(sel, q2, 0.0).astype(BF16)
        s = _dot_nt(qh, keys)
        m = jnp.max(s, axis=-1, keepdims=True)
        p = jnp.exp(s - m)
        l = jnp.sum(p, axis=-1, keepdims=True)
        outs.append(_dot(p.astype(BF16), vals) / l)
    return jnp.where(first, outs[0], outs[1])


def _memory_attention(proj, kv_ref, cat_ref, out_off, q_off):
    for p in range(MEM_HEADS // 2):
        lo = p * V7X_LANES
        q2 = proj[:, q_off + lo:q_off + lo + V7X_LANES] * ATTN_SCALE
        keys = kv_ref[:, lo:lo + V7X_LANES]
        vals = kv_ref[:, MEM_WIDTH + lo:MEM_WIDTH + lo + V7X_LANES]
        cat_ref[:, out_off + lo:out_off + lo + V7X_LANES] = _pair_attention(q2, keys, vals).astype(BF16)


def _layer_norm(z, g, b):
    mu = jnp.mean(z, axis=-1, keepdims=True)
    zc = z - mu
    var = jnp.mean(zc * zc, axis=-1, keepdims=True)
    return zc * lax.rsqrt(var + LN_EPS) * g + b


def _pack_bf16_pairs(x):
    w = x.shape[1] // 2
    bits = lax.bitcast_convert_type(x.astype(BF16).astype(F32), jnp.uint32)
    return lax.shift_right_logical(bits[:, :w], jnp.uint32(16)) | (bits[:, w:] & jnp.uint32(0xFFFF0000))


def _unpack_bf16_pairs(p):
    low = lax.bitcast_convert_type(lax.shift_left(p, jnp.uint32(16)), F32)
    high = lax.bitcast_convert_type(p & jnp.uint32(0xFFFF0000), F32)
    return low, high


def _expert_ranks(idxs, tri_ref, rank_ref, counts_ref, base_ref):
    @pl.when(jnp.logical_and(pl.program_id(0) == 0, pl.program_id(1) == 0))
    def _():
        base_ref[...] = jnp.zeros(base_ref.shape, F32)

    n_exp = base_ref.shape[0]
    t = idxs[0].shape[1]
    rows = lax.broadcasted_iota(jnp.int32, (n_exp, t), 0)
    running = base_ref[:, 0:1]
    for k in range(TOP_K):
        onehot = rows == idxs[k]
        prefix = _dot(jnp.where(onehot, 1.0, 0.0).astype(BF16), tri_ref[...])
        rank = jnp.sum(jnp.where(onehot, prefix + running, 0.0), axis=0, keepdims=True)
        rank_ref[k:k + 1, :] = rank.astype(jnp.int32)
        running = running + jnp.sum(jnp.where(onehot, 1.0, 0.0), axis=1, keepdims=True)
    base_ref[...] = jnp.broadcast_to(running, base_ref.shape)
    counts_ref[...] = base_ref[...]


def _mix_epilogue(x, cat_ref, w_out_ref, g_ref, b_ref, rwt_ref, rb_ref, tri_ref, dn_alpha, h_out_ref, hp_ref, idx_ref,
                  rank_ref, counts_ref, gcol_ref, base_ref):
    mix = _dot(cat_ref[...], w_out_ref[...])
    h1 = _layer_norm(dn_alpha * x + mix, g_ref[...], b_ref[...])
    h_out_ref[...] = h1
    hp_ref[...] = _pack_bf16_pairs(h1)
    logits = lax.dot_general(rwt_ref[...], h1, (((1,), (1,)), ((), ())),
                             precision=lax.Precision.HIGHEST, preferred_element_type=F32) + rb_ref[...]
    n_exp, t = logits.shape
    rows = lax.broadcasted_iota(jnp.int32, (n_exp, t), 0)
    vals, idxs = [], []
    for _ in range(TOP_K):
        m = jnp.max(logits, axis=0, keepdims=True)
        ix = jnp.min(jnp.where(logits == m, rows, n_exp), axis=0, keepdims=True)
        vals.append(m)
        idxs.append(ix)
        logits = jnp.where(rows == ix, NEG_INF, logits)
    exps = [jnp.exp(v - vals[0]) for v in vals]
    denom = exps[0] + exps[1] + exps[2] + exps[3]
    for k in range(TOP_K):
        idx_ref[k:k + 1, :] = idxs[k]
    _expert_ranks(idxs, tri_ref, rank_ref, counts_ref, base_ref)
    lane_rows = lax.broadcasted_iota(jnp.int32, (V7X_LANES, t), 0)
    gates_t = jnp.zeros((V7X_LANES, t), F32)
    for k in range(TOP_K):
        gates_t = jnp.where(lane_rows == k, exps[k] / denom, gates_t)
    gcol_ref[...] = gates_t.T


def _pool_layer_kernel(dn_alpha, h_ref, mem_ref, w_in_ref, wbd_ref, pscale_ref, mem_kv_ref, w_out_ref, g_ref, b_ref,
                       rwt_ref, rb_ref, tri_ref, h_out_ref, hp_ref, idx_ref, rank_ref, counts_ref, gcol_ref,
                       ue_ref, w2_ref, w4_ref, w8_ref, kv_ref, cat_ref, base_ref):
    s = pl.program_id(1)
    ts = h_ref.shape[0]
    pool_width = wbd_ref.shape[0]
    hist = POOL_HISTORY

    @pl.when(s == 0)
    def _():
        ue_ref[0:hist, :] = jnp.zeros((hist, pool_width), F32)
        kv_ref[...] = _dot(mem_ref[0].astype(BF16), mem_kv_ref[...]).astype(BF16)

    x = h_ref[...]
    proj = _dot(x.astype(BF16), w_in_ref[...])
    u = proj[:, :pool_width]
    ue_ref[hist:, :] = u
    n = ts + hist
    w2_ref[8:n, :] = ue_ref[8:n, :] + ue_ref[7:n - 1, :]
    w4_ref[16:n, :] = w2_ref[16:n, :] + w2_ref[14:n - 2, :]
    w8_ref[24:n, :] = w4_ref[24:n, :] + w4_ref[20:n - 4, :]
    w16 = w8_ref[hist:n, :] + w8_ref[hist - 8:n - 8, :]
    group_dim = pool_width // len(POOL_WINDOWS)
    col = _lane_iota((ts, pool_width))
    wsum = jnp.where(col < group_dim, w2_ref[hist:n, :],
                     jnp.where(col < 2 * group_dim, w4_ref[hist:n, :],
                               jnp.where(col < 3 * group_dim, w8_ref[hist:n, :], w16)))
    window = jnp.where(col < group_dim, POOL_WINDOWS[0],
                       jnp.where(col < 2 * group_dim, POOL_WINDOWS[1],
                                 jnp.where(col < 3 * group_dim, POOL_WINDOWS[2], POOL_WINDOWS[3])))
    tpos = s * ts + lax.broadcasted_iota(jnp.int32, (ts, pool_width), 0) + 1
    cnt = jnp.minimum(tpos, window).astype(F32)
    d = wsum / cnt - u
    ue_ref[0:hist, :] = u[ts - hist:, :]
    pooled = _dot(d.astype(BF16), wbd_ref[...]) * pscale_ref[...]
    cat_ref[:, :pool_width] = pooled.astype(BF16)
    _memory_attention(proj, kv_ref, cat_ref, pool_width, pool_width)
    _mix_epilogue(x, cat_ref, w_out_ref, g_ref, b_ref, rwt_ref, rb_ref, tri_ref, dn_alpha, h_out_ref, hp_ref, idx_ref,
                  rank_ref, counts_ref, gcol_ref, base_ref)


def _bias_pieces(c):
    hi = c.astype(BF16).astype(F32)
    r = c - hi
    mid = r.astype(BF16).astype(F32)
    lo = (r - mid).astype(BF16).astype(F32)
    return hi, mid, lo


def _augment_head(x, c_col, is_query):
    lane = _lane_iota(x.shape)
    hi, mid, lo = _bias_pieces(jnp.broadcast_to(c_col, x.shape))
    b = HEAD_DIM
    if is_query:
        aug = jnp.where(lane == b, hi, jnp.where(lane == b + 1, mid, jnp.where(lane == b + 2, lo,
                        jnp.where(lane < b + 6, 1.0, 0.0))))
    else:
        aug = jnp.where(lane == b + 3, -hi, jnp.where(lane == b + 4, -mid, jnp.where(lane == b + 5, -lo,
                        jnp.where(lane < b + 3, 1.0, 0.0))))
    return jnp.where(lane < b, x, aug).astype(BF16)


def _shared_kv_kernel(h_ref, kvw_ref, fb_ref, tri_ref, k_ref, vt_ref, ccol_ref, carry_ref):
    s = pl.program_id(1)
    k_width = k_ref.shape[1]
    v_width = vt_ref.shape[1]

    @pl.when(s == 0)
    def _():
        carry_ref[...] = jnp.zeros(carry_ref.shape, F32)

    proj = _dot(h_ref[...].astype(BF16), kvw_ref[...])
    z = proj[:, k_width + v_width:] + fb_ref[...]
    logf = jnp.minimum(z, 0.0) - jnp.log(1.0 + jnp.exp(-jnp.abs(z)))
    hi, mid, lo = _bias_pieces(logf)
    tri = tri_ref[...]
    c = _dot(tri, hi.astype(BF16)) + _dot(tri, mid.astype(BF16)) + _dot(tri, lo.astype(BF16)) + carry_ref[0:1, :]
    ts = c.shape[0]
    carry_ref[0:1, :] = c[ts - 1:ts, :]
    ccol_ref[...] = c
    for h in range(k_width // V7X_LANES):
        cols = slice(h * V7X_LANES, (h + 1) * V7X_LANES)
        k_ref[:, cols] = _augment_head(proj[:, cols], c[:, h:h + 1], False)
    vt_ref[0] = proj[:, k_width:k_width + v_width].T.astype(BF16)


def _fox_layer_kernel(dn_alpha, h_ref, mem_ref, k_ref, vt_ref, ccol_ref, w_in_ref, mem_kv_ref, w_out_ref,
                      g_ref, b_ref, rwt_ref, rb_ref, tri_ref, h_out_ref, hp_ref, idx_ref, rank_ref, counts_ref, gcol_ref,
                      kv_ref, cat_ref, qa_ref, m_ref, l_ref, acc_ref, base_ref):
    i = pl.program_id(1)
    tq = h_ref.shape[0]
    n_heads = k_ref.shape[2] // V7X_LANES
    fox_width = vt_ref.shape[1]

    @pl.when(i == 0)
    def _():
        kv_ref[...] = _dot(mem_ref[0].astype(BF16), mem_kv_ref[...]).astype(BF16)

    x = h_ref[...]
    proj = _dot(x.astype(BF16), w_in_ref[...])
    c_t = ccol_ref[...]
    for h in range(n_heads):
        cols = slice(h * V7X_LANES, (h + 1) * V7X_LANES)
        qa_ref[h] = _augment_head(proj[:, cols] * ATTN_SCALE, c_t[:, h:h + 1], True)
    m_ref[...] = jnp.full(m_ref.shape, NEG_INF, F32)
    l_ref[...] = jnp.zeros(l_ref.shape, F32)
    acc_ref[...] = jnp.zeros(acc_ref.shape, F32)
    causal_t = lax.broadcasted_iota(jnp.int32, (tq, tq), 0) <= lax.broadcasted_iota(jnp.int32, (tq, tq), 1)

    def tile(j, masked):
        start = pl.multiple_of(j * tq, tq)

        def scores(h):
            kb = k_ref[0, pl.ds(start, tq), h * V7X_LANES:(h + 1) * V7X_LANES]
            return _dot_nt(kb, qa_ref[h])

        st_next = scores(0)
        for h in range(n_heads):
            st = st_next
            if h + 1 < n_heads:
                st_next = scores(h + 1)
            if masked:
                st = jnp.where(causal_t, st, NEG_INF)
            m_old = m_ref[h:h + 1, :]
            m_new = jnp.maximum(m_old, jnp.max(st, axis=0, keepdims=True))
            alpha = jnp.exp(m_old - m_new)
            pt = jnp.exp(st - m_new)
            l_ref[h:h + 1, :] = alpha * l_ref[h:h + 1, :] + jnp.sum(pt, axis=0, keepdims=True)
            m_ref[h:h + 1, :] = m_new
            rows = slice(h * HEAD_DIM, (h + 1) * HEAD_DIM)
            vb = vt_ref[0, rows, pl.ds(start, tq)]
            acc_ref[rows, :] = alpha * acc_ref[rows, :] + _dot(vb, pt.astype(BF16))

    tile(i, True)

    def off_diagonal(j, carry):
        tile(j, False)
        return carry

    lax.fori_loop(0, i, off_diagonal, 0)
    for h in range(n_heads):
        rows = slice(h * HEAD_DIM, (h + 1) * HEAD_DIM)
        acc_ref[rows, :] = acc_ref[rows, :] / l_ref[h:h + 1, :]
    cat_ref[:, :fox_width] = acc_ref[...].T.astype(BF16)
    _memory_attention(proj, kv_ref, cat_ref, fox_width, n_heads * V7X_LANES)
    _mix_epilogue(x, cat_ref, w_out_ref, g_ref, b_ref, rwt_ref, rb_ref, tri_ref, dn_alpha, h_out_ref, hp_ref, idx_ref,
                  rank_ref, counts_ref, gcol_ref, base_ref)


def _expert_kernel(be_ref, bv_ref, nact_ref, x_ref, w1_ref, b1_ref, w2_ref, b2_ref, y_ref, w1s_ref, w2s_ref, act_ref):
    i = pl.program_id(0)
    active = i < nact_ref[0]
    two_f = w1_ref.shape[2]
    group = 2 * V7X_LANES
    n_groups = two_f // group

    @pl.when(jnp.logical_and(active, jnp.logical_or(i == 0, be_ref[i] != be_ref[jnp.maximum(i - 1, 0)])))
    def _():
        src = lax.broadcasted_iota(jnp.int32, (group, group), 0)
        dst = lax.broadcasted_iota(jnp.int32, (group, group), 1)
        want = jnp.where(dst < V7X_LANES, 2 * dst, 2 * (dst - V7X_LANES) + 1)
        perm = (src == want).astype(BF16)
        for g in range(n_groups):
            cols = slice(g * group, (g + 1) * group)
            w1s_ref[:, cols] = _dot(w1_ref[0, :, cols].astype(BF16), perm).astype(BF16)
        w2s_ref[...] = w2_ref[0].astype(BF16)

    @pl.when(active)
    def _():
        row_id = lax.broadcasted_iota(jnp.int32, x_ref.shape, 0)
        x_low, x_high = _unpack_bf16_pairs(jnp.where(row_id < bv_ref[i], x_ref[...], jnp.uint32(0)))
        xb = jnp.concatenate([x_low.astype(BF16), x_high.astype(BF16)], axis=1)
        hdn = _dot(xb, w1s_ref[...]) + b1_ref[0]
        for g in range(n_groups):
            x_glu = jnp.minimum(hdn[:, g * group:g * group + V7X_LANES], SWIGLU_LIMIT)
            x_lin = jnp.clip(hdn[:, g * group + V7X_LANES:(g + 1) * group], -SWIGLU_LIMIT, SWIGLU_LIMIT)
            act = x_glu * jax.nn.sigmoid(SWIGLU_ALPHA * x_glu) * (x_lin + 1.0)
            act_ref[:, g * V7X_LANES:(g + 1) * V7X_LANES] = act.astype(BF16)
        y_ref[...] = _pack_bf16_pairs(_dot(act_ref[...], w2s_ref[...]) + b2_ref[0])

    @pl.when(jnp.logical_not(active))
    def _():
        y_ref[...] = jnp.zeros(y_ref.shape, jnp.uint32)


def _combine_kernel(dn_alpha, h_ref, y_ref, gcol_ref, g_ref, b_ref, o_ref):
    gates = gcol_ref[...]
    low, high = _unpack_bf16_pairs(y_ref[0])
    low, high = gates[:, 0:1] * low, gates[:, 0:1] * high
    for k in range(1, TOP_K):
        low_k, high_k = _unpack_bf16_pairs(y_ref[k])
        low = low + gates[:, k:k + 1] * low_k
        high = high + gates[:, k:k + 1] * high_k
    ffn = jnp.concatenate([low, high], axis=1)
    o_ref[...] = _layer_norm(dn_alpha * h_ref[...] + ffn, g_ref[...], b_ref[...])


def _sc_gather_rows(table, idx):
    n, = idx.shape
    _, d = table.shape
    info = plsc.get_sparse_core_info()
    n_workers = info.num_cores * info.num_subcores
    per_worker = n // n_workers
    n_chunks = per_worker // SC_CHUNK
    assert n % (n_workers * SC_CHUNK) == 0 and n_chunks % 2 == 0
    mesh = plsc.VectorSubcoreMesh(core_axis_name="core", subcore_axis_name="subcore")

    @functools.partial(
        pl.kernel,
        out_type=jax.ShapeDtypeStruct((n, d), table.dtype),
        mesh=mesh,
        scratch_types=[
            pltpu.VMEM((per_worker,), jnp.int32),
            pltpu.VMEM((2, SC_CHUNK, d), table.dtype),
            pltpu.SemaphoreType.DMA((2,)),
            pltpu.SemaphoreType.DMA((2,)),
        ],
    )
    def gather_kernel(table_hbm, idx_hbm, out_hbm, idx_v, rows_v, gather_sem, write_sem):
        worker = lax.axis_index("subcore") * info.num_cores + lax.axis_index("core")
        base = pl.multiple_of(worker * per_worker, SC_CHUNK)
        pltpu.sync_copy(idx_hbm.at[pl.ds(base, per_worker)], idx_v)

        def gather(c, buf):
            rows = idx_v.at[pl.ds(pl.multiple_of(c * SC_CHUNK, SC_CHUNK), SC_CHUNK)]
            return pltpu.make_async_copy(table_hbm.at[rows], rows_v.at[buf], gather_sem.at[buf])

        def write(c, buf):
            off = pl.multiple_of(base + c * SC_CHUNK, SC_CHUNK)
            return pltpu.make_async_copy(rows_v.at[buf], out_hbm.at[pl.ds(off, SC_CHUNK)], write_sem.at[buf])

        gather(0, 0).start()

        @pl.loop(0, n_chunks, step=2)
        def _(c0):
            for buf in range(2):
                c = c0 + buf
                gather(c, buf).wait()

                @pl.when(c >= 1)
                def _():
                    write(c - 1, 1 - buf).wait()

                @pl.when(c + 1 < n_chunks)
                def _():
                    gather(c + 1, 1 - buf).start()

                write(c, buf).start()

        write(n_chunks - 1, 1).wait()

    return gather_kernel(table, idx)


def _sc_scatter_rows(rows, pos, n_out):
    t, d = rows.shape
    top_k = pos.shape[0]
    pos_flat = pos.reshape(-1)
    info = plsc.get_sparse_core_info()
    n_workers = info.num_cores * info.num_subcores
    per_worker = t // n_workers
    n_chunks = per_worker // SC_CHUNK
    assert t % (n_workers * SC_CHUNK) == 0 and n_chunks % 2 == 0
    mesh = plsc.VectorSubcoreMesh(core_axis_name="core", subcore_axis_name="subcore")

    @functools.partial(
        pl.kernel,
        out_type=jax.ShapeDtypeStruct((n_out, d), rows.dtype),
        mesh=mesh,
        scratch_types=[pltpu.VMEM((2, SC_CHUNK, d), rows.dtype)]
        + [pltpu.VMEM((SC_CHUNK,), jnp.int32) for _ in range(2 * top_k)]
        + [pltpu.SemaphoreType.DMA((2,)), pltpu.SemaphoreType.DMA((2,))],
    )
    def scatter_kernel(rows_hbm, pos_hbm, out_hbm, rows_v, *rest):
        idx_bufs = rest[:2 * top_k]
        load_sem, write_sem = rest[2 * top_k:]
        worker = lax.axis_index("subcore") * info.num_cores + lax.axis_index("core")
        base = pl.multiple_of(worker * per_worker, SC_CHUNK)

        def loads(c, buf):
            off = pl.multiple_of(base + c * SC_CHUNK, SC_CHUNK)
            copies = [pltpu.make_async_copy(rows_hbm.at[pl.ds(off, SC_CHUNK)], rows_v.at[buf], load_sem.at[buf])]
            for k in range(top_k):
                src = pos_hbm.at[pl.ds(pl.multiple_of(k * t + off, SC_CHUNK), SC_CHUNK)]
                copies.append(pltpu.make_async_copy(src, idx_bufs[buf * top_k + k], load_sem.at[buf]))
            return copies

        def scatters(buf):
            return [pltpu.make_async_copy(rows_v.at[buf], out_hbm.at[idx_bufs[buf * top_k + k]], write_sem.at[buf])
                    for k in range(top_k)]

        for cp in loads(0, 0):
            cp.start()

        @pl.loop(0, n_chunks, step=2)
        def _(c0):
            for buf in range(2):
                c = c0 + buf
                for cp in loads(c, buf):
                    cp.wait()

                @pl.when(c >= 1)
                def _():
                    for cp in scatters(1 - buf):
                        cp.wait()

                @pl.when(c + 1 < n_chunks)
                def _():
                    for cp in loads(c + 1, 1 - buf):
                        cp.start()

                for cp in scatters(buf):
                    cp.start()

        for cp in scatters(1):
            cp.wait()

    return scatter_kernel(rows, pos_flat)


def _const_spec(shape):
    zeros = (0,) * len(shape)
    return pl.BlockSpec(shape, lambda *_: zeros, pipeline_mode=pl.Buffered(1))


def _tc_params(n_axes):
    return pltpu.CompilerParams(dimension_semantics=("arbitrary",) * n_axes,
                                vmem_limit_bytes=V7X_VMEM_LIMIT_BYTES)


def _mix_out_shapes(t, d, n_exp):
    return (jax.ShapeDtypeStruct((t, d), F32),
            jax.ShapeDtypeStruct((t, d // 2), jnp.uint32),
            jax.ShapeDtypeStruct((TOP_K, t), jnp.int32),
            jax.ShapeDtypeStruct((TOP_K, t), jnp.int32),
            jax.ShapeDtypeStruct((n_exp, V7X_LANES), F32),
            jax.ShapeDtypeStruct((t, V7X_LANES), F32))


def _mix_out_specs(rows, d, n_exp, row, tok_col):
    return (pl.BlockSpec((rows, d), row),
            pl.BlockSpec((rows, d // 2), row),
            pl.BlockSpec((TOP_K, rows), tok_col),
            pl.BlockSpec((TOP_K, rows), tok_col),
            pl.BlockSpec((n_exp, V7X_LANES), lambda *_: (0, 0)),
            pl.BlockSpec((rows, V7X_LANES), row))


def _strict_upper_tri(n):
    return (lax.broadcasted_iota(jnp.int32, (n, n), 0) < lax.broadcasted_iota(jnp.int32, (n, n), 1)).astype(BF16)


def _pool_layer(h, mem, w_in, wbd, pscale, mem_kv, w_out, ln_g, ln_b, rwt, rb, dn_alpha, seq):
    t, d = h.shape
    n_mem = mem.shape[1]
    pool_width = wbd.shape[0]
    ts = SEQ_BLOCK
    nsb = seq // ts
    row = lambda b, s: (b * nsb + s, 0)
    n_exp = rwt.shape[0]
    tri = _strict_upper_tri(ts)
    return pl.pallas_call(
        functools.partial(_pool_layer_kernel, dn_alpha),
        out_shape=_mix_out_shapes(t, d, n_exp),
        grid=(t // seq, nsb),
        in_specs=[
            pl.BlockSpec((ts, d), row),
            pl.BlockSpec((1, n_mem, d), lambda b, s: (b, 0, 0)),
            _const_spec(w_in.shape), _const_spec(wbd.shape), _const_spec(pscale.shape), _const_spec(mem_kv.shape),
            _const_spec(w_out.shape), _const_spec(ln_g.shape), _const_spec(ln_b.shape), _const_spec(rwt.shape),
            _const_spec(rb.shape), _const_spec(tri.shape),
        ],
        out_specs=_mix_out_specs(ts, d, n_exp, row, lambda b, s: (0, b * nsb + s)),
        scratch_shapes=[pltpu.VMEM((ts + POOL_HISTORY, pool_width), F32)] * 4
        + [pltpu.VMEM((n_mem, 2 * MEM_WIDTH), BF16), pltpu.VMEM((ts, d), BF16), pltpu.VMEM((n_exp, V7X_LANES), F32)],
        compiler_params=_tc_params(2),
        name="pool_layer",
    )(h, mem, w_in, wbd, pscale, mem_kv, w_out, ln_g, ln_b, rwt, rb, tri)


def _expand_heads(w, n_heads):
    d = w.shape[0]
    w3 = w.reshape(d, n_heads, HEAD_DIM)
    return jnp.pad(w3, ((0, 0), (0, 0), (0, V7X_LANES - HEAD_DIM))).reshape(d, n_heads * V7X_LANES)


def _shared_kv(h, kvw, fb, seq, n_heads):
    t, d = h.shape
    ts = SEQ_BLOCK
    nsb = seq // ts
    k_width = n_heads * V7X_LANES
    v_width = n_heads * HEAD_DIM
    tri = (lax.broadcasted_iota(jnp.int32, (ts, ts), 1) <= lax.broadcasted_iota(jnp.int32, (ts, ts), 0)).astype(BF16)
    row = lambda b, s: (b * nsb + s, 0)
    return pl.pallas_call(
        _shared_kv_kernel,
        out_shape=(jax.ShapeDtypeStruct((t, k_width), BF16),
                   jax.ShapeDtypeStruct((t // seq, v_width, seq), BF16),
                   jax.ShapeDtypeStruct((t, V7X_LANES), F32)),
        grid=(t // seq, nsb),
        in_specs=[pl.BlockSpec((ts, d), row), _const_spec(kvw.shape), _const_spec(fb.shape), _const_spec(tri.shape)],
        out_specs=(pl.BlockSpec((ts, k_width), row),
                   pl.BlockSpec((1, v_width, ts), lambda b, s: (b, 0, s)),
                   pl.BlockSpec((ts, V7X_LANES), row)),
        scratch_shapes=[pltpu.VMEM((8, V7X_LANES), F32)],
        compiler_params=_tc_params(2),
        name="shared_kv",
    )(h, kvw, fb, tri)


def _fox_layer(h, mem, k_aug, v_t, ccol, w_in, mem_kv, w_out, ln_g, ln_b, rwt, rb, dn_alpha, seq):
    t, d = h.shape
    n_mem = mem.shape[1]
    k_width = k_aug.shape[1]
    v_width = v_t.shape[1]
    n_heads = k_width // V7X_LANES
    tq = Q_BLOCK
    nqb = seq // tq
    n_batch = t // seq
    row = lambda b, i: (b * nqb + i, 0)
    per_batch = lambda b, i: (b, 0, 0)
    n_exp = rwt.shape[0]
    tri = _strict_upper_tri(tq)
    return pl.pallas_call(
        functools.partial(_fox_layer_kernel, dn_alpha),
        out_shape=_mix_out_shapes(t, d, n_exp),
        grid=(n_batch, nqb),
        in_specs=[
            pl.BlockSpec((tq, d), row),
            pl.BlockSpec((1, n_mem, d), per_batch),
            pl.BlockSpec((1, seq, k_width), per_batch),
            pl.BlockSpec((1, v_width, seq), per_batch),
            pl.BlockSpec((tq, V7X_LANES), row),
            _const_spec(w_in.shape), _const_spec(mem_kv.shape), _const_spec(w_out.shape), _const_spec(ln_g.shape),
            _const_spec(ln_b.shape), _const_spec(rwt.shape), _const_spec(rb.shape), _const_spec(tri.shape),
        ],
        out_specs=_mix_out_specs(tq, d, n_exp, row, lambda b, i: (0, b * nqb + i)),
        scratch_shapes=[pltpu.VMEM((n_mem, 2 * MEM_WIDTH), BF16), pltpu.VMEM((tq, d), BF16),
                        pltpu.VMEM((n_heads, tq, V7X_LANES), BF16),
                        pltpu.VMEM((16, tq), F32), pltpu.VMEM((16, tq), F32), pltpu.VMEM((v_width, tq), F32),
                        pltpu.VMEM((n_exp, V7X_LANES), F32)],
        compiler_params=_tc_params(2),
        name="fox_layer",
    )(h, mem, k_aug.reshape(n_batch, seq, k_width), v_t, ccol, w_in, mem_kv, w_out, ln_g, ln_b, rwt, rb, tri)


def _slot_kernel(pad_start_ref, idx_ref, rank_ref, pos_ref):
    pos = rank_ref[...]
    idx = idx_ref[...]
    for e in range(pad_start_ref.shape[0]):
        pos = pos + jnp.where(idx == e, pad_start_ref[e], 0)
    pos_ref[...] = pos


def _route(idx_t, rank_t, counts, n_experts):
    top_k, t = idx_t.shape
    n_assign = top_k * t
    padded = ((counts + MOE_BLOCK - 1) // MOE_BLOCK) * MOE_BLOCK
    pad_end = jnp.cumsum(padded)
    pad_start = pad_end - padded
    n_blocks = n_assign // MOE_BLOCK + n_experts
    block_start = jnp.arange(n_blocks, dtype=jnp.int32) * MOE_BLOCK
    block_expert = jnp.minimum(jnp.sum((pad_end[None, :] <= block_start[:, None]).astype(jnp.int32), axis=1),
                               n_experts - 1)
    block_valid = jnp.clip(counts[block_expert] - (block_start - pad_start[block_expert]), 0, MOE_BLOCK)
    n_active = (pad_end[-1:] // MOE_BLOCK).astype(jnp.int32)
    cols = min(SLOT_COLS, t)
    pos = pl.pallas_call(
        _slot_kernel,
        out_shape=jax.ShapeDtypeStruct((top_k, t), jnp.int32),
        grid_spec=pltpu.PrefetchScalarGridSpec(
            num_scalar_prefetch=1,
            grid=(t // cols,),
            in_specs=[pl.BlockSpec((top_k, cols), lambda i, ps: (0, i))] * 2,
            out_specs=pl.BlockSpec((top_k, cols), lambda i, ps: (0, i)),
        ),
        compiler_params=_tc_params(1),
        name="slots",
    )(pad_start.astype(jnp.int32), idx_t, rank_t)
    return pos, block_expert, block_valid.astype(jnp.int32), n_active


def _experts(xs, block_expert, block_valid, n_active, w1_all, b1, w2_all, b2, layer):
    n_slots, half_d = xs.shape
    _, n_exp, d, two_f = w1_all.shape
    d_ff = two_f // 2
    assert d == 2 * half_d
    grid_spec = pltpu.PrefetchScalarGridSpec(
        num_scalar_prefetch=3,
        grid=(n_slots // MOE_BLOCK,),
        in_specs=[
            pl.BlockSpec((MOE_BLOCK, half_d), lambda i, be, bv, na: (i, 0)),
            pl.BlockSpec((None, 1, d, two_f), lambda i, be, bv, na: (layer, be[i], 0, 0)),
            pl.BlockSpec((1, 1, two_f), lambda i, be, bv, na: (be[i], 0, 0)),
            pl.BlockSpec((None, 1, d_ff, d), lambda i, be, bv, na: (layer, be[i], 0, 0)),
            pl.BlockSpec((1, 1, d), lambda i, be, bv, na: (be[i], 0, 0)),
        ],
        out_specs=pl.BlockSpec((MOE_BLOCK, half_d), lambda i, be, bv, na: (i, 0)),
        scratch_shapes=[pltpu.VMEM((d, two_f), BF16), pltpu.VMEM((d_ff, d), BF16), pltpu.VMEM((MOE_BLOCK, d_ff), BF16)],
    )
    return pl.pallas_call(
        _expert_kernel,
        out_shape=jax.ShapeDtypeStruct((n_slots, half_d), jnp.uint32),
        grid_spec=grid_spec,
        compiler_params=_tc_params(1),
        name="experts",
    )(block_expert, block_valid, n_active, xs, w1_all, b1.reshape(n_exp, 1, two_f), w2_all, b2.reshape(n_exp, 1, d))


def _combine(h, yg, gcol, ln_g, ln_b, dn_alpha):
    t, d = h.shape
    ts = SEQ_BLOCK
    row = lambda i: (i, 0)
    return pl.pallas_call(
        functools.partial(_combine_kernel, dn_alpha),
        out_shape=jax.ShapeDtypeStruct((t, d), F32),
        grid=(t // ts,),
        in_specs=[pl.BlockSpec((ts, d), row), pl.BlockSpec((TOP_K, ts, d // 2), lambda i: (0, i, 0)),
                  pl.BlockSpec((ts, V7X_LANES), row), _const_spec(ln_g.shape), _const_spec(ln_b.shape)],
        out_specs=pl.BlockSpec((ts, d), row),
        compiler_params=_tc_params(1),
        name="combine",
    )(h, yg.reshape(TOP_K, t, d // 2), gcol, ln_g, ln_b)


def _moe(mixed, w1_all, b1, w2_all, b2, layer, ln_g, ln_b, dn_alpha):
    h1, h1_packed, idx_t, rank_t, counts, gcol = mixed
    n_exp = w1_all.shape[1]
    d_ff = w2_all.shape[2]
    pos, block_expert, block_valid, n_active = _route(idx_t, rank_t, counts[:, 0].astype(jnp.int32), n_exp)
    b1p = jnp.swapaxes(b1.reshape(n_exp, d_ff // V7X_LANES, V7X_LANES, 2), 2, 3).reshape(n_exp, 2 * d_ff)
    xs = _sc_scatter_rows(h1_packed, pos, block_expert.shape[0] * MOE_BLOCK)
    y = _experts(xs, block_expert, block_valid, n_active, w1_all, b1p, w2_all, b2, layer)
    yg = _sc_gather_rows(y, pos.reshape(-1))
    return _combine(h1, yg, gcol, ln_g, ln_b, dn_alpha)


def kernel(x, mem, ln1_g, ln1_b, ln2_g, ln2_b, a_w_in, pool_w, pool_scale, a_mem_kv, a_w_out, kv_w, fgate_b, b_w_in, b_mem_kv, b_w_out, router_w, router_b, moe_w1, moe_b1, moe_w2, moe_b2):
    n_batch, seq, d = x.shape
    t = n_batch * seq
    depth = ln1_g.shape[0]
    n_a = a_w_in.shape[0]
    dn_alpha = float((2 * depth) ** 0.25)
    n_exp = router_w.shape[2]
    fox_heads = fgate_b.shape[0]
    fox_width = fox_heads * HEAD_DIM
    assert seq % SEQ_BLOCK == 0 and seq % Q_BLOCK == 0 and fox_heads <= 16

    h = x.reshape(t, d)
    row = lambda a: a.reshape(1, -1)
    shared = None
    for l in range(depth):
        rwt = router_w[l].T
        rb = router_b[l].reshape(n_exp, 1)
        if l < n_a:
            wbd = jax.scipy.linalg.block_diag(*[pool_w[l, g] for g in range(pool_w.shape[1])]).astype(BF16)
            mixed = _pool_layer(
                h, mem, a_w_in[l].astype(BF16), wbd, row(pool_scale[l]), a_mem_kv[l].astype(BF16),
                a_w_out[l].astype(BF16), row(ln1_g[l]), row(ln1_b[l]), rwt, rb, dn_alpha, seq)
        else:
            j = l - n_a
            if shared is None:
                pad = V7X_LANES - fox_heads
                kvw = jnp.concatenate([_expand_heads(kv_w[:, :fox_width], fox_heads),
                                       kv_w[:, fox_width:2 * fox_width],
                                       jnp.pad(kv_w[:, 2 * fox_width:], ((0, 0), (0, pad)))], axis=1).astype(BF16)
                fb = jnp.pad(fgate_b, (0, pad)).reshape(1, V7X_LANES)
                shared = _shared_kv(h, kvw, fb, seq, fox_heads)
            k_aug, v_t, ccol = shared
            w_in = jnp.concatenate([_expand_heads(b_w_in[j][:, :fox_width], fox_heads),
                                    b_w_in[j][:, fox_width:]], axis=1).astype(BF16)
            mixed = _fox_layer(
                h, mem, k_aug, v_t, ccol, w_in, b_mem_kv[j].astype(BF16),
                b_w_out[j].astype(BF16), row(ln1_g[l]), row(ln1_b[l]), rwt, rb, dn_alpha, seq)
        h = _moe(mixed, moe_w1, moe_b1[l], moe_w2, moe_b2[l], l, row(ln2_g[l]), row(ln2_b[l]), dn_alpha)
    return h.reshape(n_batch, seq, d)
```

```python
import functools
import jax
import jax.numpy as jnp
from jax import lax
from jax.experimental import pallas as pl
from jax.experimental.pallas import tpu as pltpu
from jax.experimental.pallas import tpu_sc as plsc

HEAD_DIM = 64
MEM_HEADS = 4
MEM_WIDTH = MEM_HEADS * HEAD_DIM
POOL_WINDOWS = (2, 4, 8, 16)
TOP_K = 4
SWIGLU_ALPHA = 1.702
SWIGLU_LIMIT = 7.0
MOE_BLOCK = 512
LN_EPS = 1e-5
ATTN_SCALE = HEAD_DIM ** -0.5

V7X_LANES = 128
V7X_VMEM_LIMIT_BYTES = 56 * 1024 * 1024
POOL_HISTORY = 32
SC_CHUNK = 64
SEQ_BLOCK = 512
Q_BLOCK = 512
SLOT_COLS = 8192

F32 = jnp.float32
BF16 = jnp.bfloat16
NEG_INF = float("-inf")


def _dot(a, b):
    return jnp.dot(a, b, preferred_element_type=F32)


def _dot_nt(a, b):
    return lax.dot_general(a, b, (((1,), (1,)), ((), ())), preferred_element_type=F32)


def _lane_iota(shape):
    return lax.broadcasted_iota(jnp.int32, shape, len(shape) - 1)


def _pair_attention(q2, keys, vals, bias=None):
    lane = _lane_iota(q2.shape)
    first = lane < HEAD_DIM
    outs = []
    for sel in (first, jnp.logical_not(first)):
        qh = jnp.where(sel, q2, 0.0).astype(BF16)
        s = _dot_nt(qh, keys)
        m = jnp.max(s, axis=-1, keepdims=True)
        p = jnp.exp(s - m)
        l = jnp.sum(p, axis=-1, keepdims=True)
        outs.append(_dot(p.astype(BF16), vals) / l)
    return jnp.where(first, outs[0], outs[1])


def _memory_attention(proj, kv_ref, cat_ref, out_off, q_off):
    for p in range(MEM_HEADS // 2):
        lo = p * V7X_LANES
        q2 = proj[:, q_off + lo:q_off + lo + V7X_LANES] * ATTN_SCALE
        keys = kv_ref[:, lo:lo + V7X_LANES]
        vals = kv_ref[:, MEM_WIDTH + lo:MEM_WIDTH + lo + V7X_LANES]
        cat_ref[:, out_off + lo:out_off + lo + V7X_LANES] = _pair_attention(q2, keys, vals).astype(BF16)


def _layer_norm(z, g, b):
    mu = jnp.mean(z, axis=-1, keepdims=True)
    zc = z - mu
    var = jnp.mean(zc * zc, axis=-1, keepdims=True)
    return zc * lax.rsqrt(var + LN_EPS) * g + b


def _pack_bf16_pairs(x):
    w = x.shape[1] // 2
    bits = lax.bitcast_convert_type(x.astype(BF16).astype(F32), jnp.uint32)
    return lax.shift_right_logical(bits[:, :w], jnp.uint32(16)) | (bits[:, w:] & jnp.uint32(0xFFFF0000))


def _unpack_bf16_pairs(p):
    low = lax.bitcast_convert_type(lax.shift_left(p, jnp.uint32(16)), F32)
    high = lax.bitcast_convert_type(p & jnp.uint32(0xFFFF0000), F32)
    return low, high


def _expert_ranks(idxs, tri_ref, rank_ref, counts_ref, base_ref):
    @pl.when(jnp.logical_and(pl.program_id(0) == 0, pl.program_id(1) == 0))
    def _():
        base_ref[...] = jnp.zeros(base_ref.shape, F32)

    n_exp = base_ref.shape[0]
    t = idxs[0].shape[1]
    rows = lax.broadcasted_iota(jnp.int32, (n_exp, t), 0)
    running = base_ref[:, 0:1]
    for k in range(TOP_K):
        onehot = rows == idxs[k]
        prefix = _dot(jnp.where(onehot, 1.0, 0.0).astype(BF16), tri_ref[...])
        rank = jnp.sum(jnp.where(onehot, prefix + running, 0.0), axis=0, keepdims=True)
        rank_ref[k:k + 1, :] = rank.astype(jnp.int32)
        running = running + jnp.sum(jnp.where(onehot, 1.0, 0.0), axis=1, keepdims=True)
    base_ref[...] = jnp.broadcast_to(running, base_ref.shape)
    counts_ref[...] = base_ref[...]


def _mix_epilogue(x, cat_ref, w_out_ref, g_ref, b_ref, rwt_ref, rb_ref, tri_ref, dn_alpha, h_out_ref, hp_ref, idx_ref,
                  rank_ref, counts_ref, gcol_ref, base_ref):
    mix = _dot(cat_ref[...], w_out_ref[...])
    h1 = _layer_norm(dn_alpha * x + mix, g_ref[...], b_ref[...])
    h_out_ref[...] = h1
    hp_ref[...] = _pack_bf16_pairs(h1)
    logits = lax.dot_general(rwt_ref[...], h1, (((1,), (1,)), ((), ())),
                             precision=lax.Precision.HIGHEST, preferred_element_type=F32) + rb_ref[...]
    n_exp, t = logits.shape
    rows = lax.broadcasted_iota(jnp.int32, (n_exp, t), 0)
    vals, idxs = [], []
    for _ in range(TOP_K):
        m = jnp.max(logits, axis=0, keepdims=True)
        ix = jnp.min(jnp.where(logits == m, rows, n_exp), axis=0, keepdims=True)
        vals.append(m)
        idxs.append(ix)
        logits = jnp.where(rows == ix, NEG_INF, logits)
    exps = [jnp.exp(v - vals[0]) for v in vals]
    denom = exps[0] + exps[1] + exps[2] + exps[3]
    for k in range(TOP_K):
        idx_ref[k:k + 1, :] = idxs[k]
    _expert_ranks(idxs, tri_ref, rank_ref, counts_ref, base_ref)
    lane_rows = lax.broadcasted_iota(jnp.int32, (V7X_LANES, t), 0)
    gates_t = jnp.zeros((V7X_LANES, t), F32)
    for k in range(TOP_K):
        gates_t = jnp.where(lane_rows == k, exps[k] / denom, gates_t)
    gcol_ref[...] = gates_t.T


def _pool_layer_kernel(dn_alpha, h_ref, mem_ref, w_in_ref, wbd_ref, pscale_ref, mem_kv_ref, w_out_ref, g_ref, b_ref,
                       rwt_ref, rb_ref, tri_ref, h_out_ref, hp_ref, idx_ref, rank_ref, counts_ref, gcol_ref,
                       ue_ref, w2_ref, w4_ref, w8_ref, kv_ref, cat_ref, base_ref):
    s = pl.program_id(1)
    ts = h_ref.shape[0]
    pool_width = wbd_ref.shape[0]
    hist = POOL_HISTORY

    @pl.when(s == 0)
    def _():
        ue_ref[0:hist, :] = jnp.zeros((hist, pool_width), F32)
        kv_ref[...] = _dot(mem_ref[0].astype(BF16), mem_kv_ref[...]).astype(BF16)

    x = h_ref[...]
    proj = _dot(x.astype(BF16), w_in_ref[...])
    u = proj[:, :pool_width]
    ue_ref[hist:, :] = u
    n = ts + hist
    w2_ref[8:n, :] = ue_ref[8:n, :] + ue_ref[7:n - 1, :]
    w4_ref[16:n, :] = w2_ref[16:n, :] + w2_ref[14:n - 2, :]
    w8_ref[24:n, :] = w4_ref[24:n, :] + w4_ref[20:n - 4, :]
    w16 = w8_ref[hist:n, :] + w8_ref[hist - 8:n - 8, :]
    group_dim = pool_width // len(POOL_WINDOWS)
    col = _lane_iota((ts, pool_width))
    wsum = jnp.where(col < group_dim, w2_ref[hist:n, :],
                     jnp.where(col < 2 * group_dim, w4_ref[hist:n, :],
                               jnp.where(col < 3 * group_dim, w8_ref[hist:n, :], w16)))
    window = jnp.where(col < group_dim, POOL_WINDOWS[0],
                       jnp.where(col < 2 * group_dim, POOL_WINDOWS[1],
                                 jnp.where(col < 3 * group_dim, POOL_WINDOWS[2], POOL_WINDOWS[3])))
    tpos = s * ts + lax.broadcasted_iota(jnp.int32, (ts, pool_width), 0) + 1
    cnt = jnp.minimum(tpos, window).astype(F32)
    d = wsum / cnt - u
    ue_ref[0:hist, :] = u[ts - hist:, :]
    pooled = _dot(d.astype(BF16), wbd_ref[...]) * pscale_ref[...]
    cat_ref[:, :pool_width] = pooled.astype(BF16)
    _memory_attention(proj, kv_ref, cat_ref, pool_width, pool_width)
    _mix_epilogue(x, cat_ref, w_out_ref, g_ref, b_ref, rwt_ref, rb_ref, tri_ref, dn_alpha, h_out_ref, hp_ref, idx_ref,
                  rank_ref, counts_ref, gcol_ref, base_ref)


def _bias_pieces(c):
    hi = c.astype(BF16).astype(F32)
    r = c - hi
    mid = r.astype(BF16).astype(F32)
    lo = (r - mid).astype(BF16).astype(F32)
    return hi, mid, lo


def _augment_head(x, c_col, is_query):
    lane = _lane_iota(x.shape)
    hi, mid, lo = _bias_pieces(jnp.broadcast_to(c_col, x.shape))
    b = HEAD_DIM
    if is_query:
        aug = jnp.where(lane == b, hi, jnp.where(lane == b + 1, mid, jnp.where(lane == b + 2, lo,
                        jnp.where(lane < b + 6, 1.0, 0.0))))
    else:
        aug = jnp.where(lane == b + 3, -hi, jnp.where(lane == b + 4, -mid, jnp.where(lane == b + 5, -lo,
                        jnp.where(lane < b + 3, 1.0, 0.0))))
    return jnp.where(lane < b, x, aug).astype(BF16)


def _shared_kv_kernel(h_ref, kvw_ref, fb_ref, tri_ref, k_ref, vt_ref, ccol_ref, carry_ref):
    s = pl.program_id(1)
    k_width = k_ref.shape[1]
    v_width = vt_ref.shape[1]

    @pl.when(s == 0)
    def _():
        carry_ref[...] = jnp.zeros(carry_ref.shape, F32)

    proj = _dot(h_ref[...].astype(BF16), kvw_ref[...])
    z = proj[:, k_width + v_width:] + fb_ref[...]
    logf = jnp.minimum(z, 0.0) - jnp.log(1.0 + jnp.exp(-jnp.abs(z)))
    hi, mid, lo = _bias_pieces(logf)
    tri = tri_ref[...]
    c = _dot(tri, hi.astype(BF16)) + _dot(tri, mid.astype(BF16)) + _dot(tri, lo.astype(BF16)) + carry_ref[0:1, :]
    ts = c.shape[0]
    carry_ref[0:1, :] = c[ts - 1:ts, :]
    ccol_ref[...] = c
    for h in range(k_width // V7X_LANES):
        cols = slice(h * V7X_LANES, (h + 1) * V7X_LANES)
        k_ref[:, cols] = _augment_head(proj[:, cols], c[:, h:h + 1], False)
    vt_ref[0] = proj[:, k_width:k_width + v_width].T.astype(BF16)


def _fox_layer_kernel(dn_alpha, h_ref, mem_ref, k_ref, vt_ref, ccol_ref, w_in_ref, mem_kv_ref, w_out_ref,
                      g_ref, b_ref, rwt_ref, rb_ref, tri_ref, h_out_ref, hp_ref, idx_ref, rank_ref, counts_ref, gcol_ref,
                      kv_ref, cat_ref, qa_ref, m_ref, l_ref, acc_ref, base_ref):
    i = pl.program_id(1)
    tq = h_ref.shape[0]
    n_heads = k_ref.shape[2] // V7X_LANES
    fox_width = vt_ref.shape[1]

    @pl.when(i == 0)
    def _():
        kv_ref[...] = _dot(mem_ref[0].astype(BF16), mem_kv_ref[...]).astype(BF16)

    x = h_ref[...]
    proj = _dot(x.astype(BF16), w_in_ref[...])
    c_t = ccol_ref[...]
    for h in range(n_heads):
        cols = slice(h * V7X_LANES, (h + 1) * V7X_LANES)
        qa_ref[h] = _augment_head(proj[:, cols] * ATTN_SCALE, c_t[:, h:h + 1], True)
    m_ref[...] = jnp.full(m_ref.shape, NEG_INF, F32)
    l_ref[...] = jnp.zeros(l_ref.shape, F32)
    acc_ref[...] = jnp.zeros(acc_ref.shape, F32)
    causal_t = lax.broadcasted_iota(jnp.int32, (tq, tq), 0) <= lax.broadcasted_iota(jnp.int32, (tq, tq), 1)

    def tile(j, masked):
        start = pl.multiple_of(j * tq, tq)

        def scores(h):
            kb = k_ref[0, pl.ds(start, tq), h * V7X_LANES:(h + 1) * V7X_LANES]
            return _dot_nt(kb, qa_ref[h])

        st_next = scores(0)
        for h in range(n_heads):
            st = st_next
            if h + 1 < n_heads:
                st_next = scores(h + 1)
            if masked:
                st = jnp.where(causal_t, st, NEG_INF)
            m_old = m_ref[h:h + 1, :]
            m_new = jnp.maximum(m_old, jnp.max(st, axis=0, keepdims=True))
            alpha = jnp.exp(m_old - m_new)
            pt = jnp.exp(st - m_new)
            l_ref[h:h + 1, :] = alpha * l_ref[h:h + 1, :] + jnp.sum(pt, axis=0, keepdims=True)
            m_ref[h:h + 1, :] = m_new
            rows = slice(h * HEAD_DIM, (h + 1) * HEAD_DIM)
            vb = vt_ref[0, rows, pl.ds(start, tq)]
            acc_ref[rows, :] = alpha * acc_ref[rows, :] + _dot(vb, pt.astype(BF16))

    tile(i, True)

    def off_diagonal(j, carry):
        tile(j, False)
        return carry

    lax.fori_loop(0, i, off_diagonal, 0)
    for h in range(n_heads):
        rows = slice(h * HEAD_DIM, (h + 1) * HEAD_DIM)
        acc_ref[rows, :] = acc_ref[rows, :] / l_ref[h:h + 1, :]
    cat_ref[:, :fox_width] = acc_ref[...].T.astype(BF16)
    _memory_attention(proj, kv_ref, cat_ref, fox_width, n_heads * V7X_LANES)
    _mix_epilogue(x, cat_ref, w_out_ref, g_ref, b_ref, rwt_ref, rb_ref, tri_ref, dn_alpha, h_out_ref, hp_ref, idx_ref,
                  rank_ref, counts_ref, gcol_ref, base_ref)


def _expert_kernel(layer, be_ref, bv_ref, bslot_ref, bnext_ref, nact_ref, x_ref, w1_hbm, b1_ref, w2_hbm, b2_ref, y_ref,
                   w1s_ref, w2s_ref, act_ref, w1f_ref, w2f_ref, sem_ref):
    i = pl.program_id(0)
    active = i < nact_ref[0]
    two_f = w1f_ref.shape[2]
    group = 2 * V7X_LANES
    n_groups = two_f // group

    def weight_copies(expert, slot):
        return (pltpu.make_async_copy(w1_hbm.at[layer, expert], w1f_ref.at[slot], sem_ref.at[0, slot]),
                pltpu.make_async_copy(w2_hbm.at[layer, expert], w2f_ref.at[slot], sem_ref.at[1, slot]))

    @pl.when(i == 0)
    def _():
        for cp in weight_copies(be_ref[0], bslot_ref[0]):
            cp.start()

    @pl.when(jnp.logical_and(active, jnp.logical_or(i == 0, be_ref[i] != be_ref[jnp.maximum(i - 1, 0)])))
    def _():
        slot = bslot_ref[i]
        for cp in weight_copies(be_ref[i], slot):
            cp.wait()

        @pl.when(bnext_ref[i] >= 0)
        def _():
            for cp in weight_copies(bnext_ref[i], 1 - slot):
                cp.start()

        src = lax.broadcasted_iota(jnp.int32, (group, group), 0)
        dst = lax.broadcasted_iota(jnp.int32, (group, group), 1)
        want = jnp.where(dst < V7X_LANES, 2 * dst, 2 * (dst - V7X_LANES) + 1)
        perm = (src == want).astype(BF16)
        for g in range(n_groups):
            cols = slice(g * group, (g + 1) * group)
            w1s_ref[:, cols] = _dot(w1f_ref[slot, :, cols].astype(BF16), perm).astype(BF16)
        w2s_ref[...] = w2f_ref[slot].astype(BF16)

    @pl.when(active)
    def _():
        row_id = lax.broadcasted_iota(jnp.int32, x_ref.shape, 0)
        x_low, x_high = _unpack_bf16_pairs(jnp.where(row_id < bv_ref[i], x_ref[...], jnp.uint32(0)))
        xb = jnp.concatenate([x_low.astype(BF16), x_high.astype(BF16)], axis=1)
        hdn = _dot(xb, w1s_ref[...]) + b1_ref[0]
        for g in range(n_groups):
            x_glu = jnp.minimum(hdn[:, g * group:g * group + V7X_LANES], SWIGLU_LIMIT)
            x_lin = jnp.clip(hdn[:, g * group + V7X_LANES:(g + 1) * group], -SWIGLU_LIMIT, SWIGLU_LIMIT)
            act = x_glu * jax.nn.sigmoid(SWIGLU_ALPHA * x_glu) * (x_lin + 1.0)
            act_ref[:, g * V7X_LANES:(g + 1) * V7X_LANES] = act.astype(BF16)
        y_ref[...] = _pack_bf16_pairs(_dot(act_ref[...], w2s_ref[...]) + b2_ref[0])

    @pl.when(jnp.logical_not(active))
    def _():
        y_ref[...] = jnp.zeros(y_ref.shape, jnp.uint32)


def _combine_kernel(dn_alpha, h_ref, y_ref, gcol_ref, g_ref, b_ref, o_ref):
    gates = gcol_ref[...]
    low, high = _unpack_bf16_pairs(y_ref[0])
    low, high = gates[:, 0:1] * low, gates[:, 0:1] * high
    for k in range(1, TOP_K):
        low_k, high_k = _unpack_bf16_pairs(y_ref[k])
        low = low + gates[:, k:k + 1] * low_k
        high = high + gates[:, k:k + 1] * high_k
    ffn = jnp.concatenate([low, high], axis=1)
    o_ref[...] = _layer_norm(dn_alpha * h_ref[...] + ffn, g_ref[...], b_ref[...])


def _sc_gather_rows(table, idx):
    n, = idx.shape
    _, d = table.shape
    info = plsc.get_sparse_core_info()
    n_workers = info.num_cores * info.num_subcores
    per_worker = n // n_workers
    n_chunks = per_worker // SC_CHUNK
    assert n % (n_workers * SC_CHUNK) == 0 and n_chunks % 2 == 0
    mesh = plsc.VectorSubcoreMesh(core_axis_name="core", subcore_axis_name="subcore")

    @functools.partial(
        pl.kernel,
        out_type=jax.ShapeDtypeStruct((n, d), table.dtype),
        mesh=mesh,
        scratch_types=[
            pltpu.VMEM((per_worker,), jnp.int32),
            pltpu.VMEM((2, SC_CHUNK, d), table.dtype),
            pltpu.SemaphoreType.DMA((2,)),
            pltpu.SemaphoreType.DMA((2,)),
        ],
    )
    def gather_kernel(table_hbm, idx_hbm, out_hbm, idx_v, rows_v, gather_sem, write_sem):
        worker = lax.axis_index("subcore") * info.num_cores + lax.axis_index("core")
        base = pl.multiple_of(worker * per_worker, SC_CHUNK)
        pltpu.sync_copy(idx_hbm.at[pl.ds(base, per_worker)], idx_v)

        def gather(c, buf):
            rows = idx_v.at[pl.ds(pl.multiple_of(c * SC_CHUNK, SC_CHUNK), SC_CHUNK)]
            return pltpu.make_async_copy(table_hbm.at[rows], rows_v.at[buf], gather_sem.at[buf])

        def write(c, buf):
            off = pl.multiple_of(base + c * SC_CHUNK, SC_CHUNK)
            return pltpu.make_async_copy(rows_v.at[buf], out_hbm.at[pl.ds(off, SC_CHUNK)], write_sem.at[buf])

        gather(0, 0).start()

        @pl.loop(0, n_chunks, step=2)
        def _(c0):
            for buf in range(2):
                c = c0 + buf
                gather(c, buf).wait()

                @pl.when(c >= 1)
                def _():
                    write(c - 1, 1 - buf).wait()

                @pl.when(c + 1 < n_chunks)
                def _():
                    gather(c + 1, 1 - buf).start()

                write(c, buf).start()

        write(n_chunks - 1, 1).wait()

    return gather_kernel(table, idx)


def _sc_scatter_rows(rows, pos, n_out):
    t, d = rows.shape
    top_k = pos.shape[0]
    pos_flat = pos.reshape(-1)
    info = plsc.get_sparse_core_info()
    n_workers = info.num_cores * info.num_subcores
    per_worker = t // n_workers
    n_chunks = per_worker // SC_CHUNK
    assert t % (n_workers * SC_CHUNK) == 0 and n_chunks % 2 == 0
    mesh = plsc.VectorSubcoreMesh(core_axis_name="core", subcore_axis_name="subcore")

    @functools.partial(
        pl.kernel,
        out_type=jax.ShapeDtypeStruct((n_out, d), rows.dtype),
        mesh=mesh,
        scratch_types=[pltpu.VMEM((2, SC_CHUNK, d), rows.dtype)]
        + [pltpu.VMEM((SC_CHUNK,), jnp.int32) for _ in range(2 * top_k)]
        + [pltpu.SemaphoreType.DMA((2,)), pltpu.SemaphoreType.DMA((2,))],
    )
    def scatter_kernel(rows_hbm, pos_hbm, out_hbm, rows_v, *rest):
        idx_bufs = rest[:2 * top_k]
        load_sem, write_sem = rest[2 * top_k:]
        worker = lax.axis_index("subcore") * info.num_cores + lax.axis_index("core")
        base = pl.multiple_of(worker * per_worker, SC_CHUNK)

        def loads(c, buf):
            off = pl.multiple_of(base + c * SC_CHUNK, SC_CHUNK)
            copies = [pltpu.make_async_copy(rows_hbm.at[pl.ds(off, SC_CHUNK)], rows_v.at[buf], load_sem.at[buf])]
            for k in range(top_k):
                src = pos_hbm.at[pl.ds(pl.multiple_of(k * t + off, SC_CHUNK), SC_CHUNK)]
                copies.append(pltpu.make_async_copy(src, idx_bufs[buf * top_k + k], load_sem.at[buf]))
            return copies

        def scatters(buf):
            return [pltpu.make_async_copy(rows_v.at[buf], out_hbm.at[idx_bufs[buf * top_k + k]], write_sem.at[buf])
                    for k in range(top_k)]

        for cp in loads(0, 0):
            cp.start()

        @pl.loop(0, n_chunks, step=2)
        def _(c0):
            for buf in range(2):
                c = c0 + buf
                for cp in loads(c, buf):
                    cp.wait()

                @pl.when(c >= 1)
                def _():
                    for cp in scatters(1 - buf):
                        cp.wait()

                @pl.when(c + 1 < n_chunks)
                def _():
                    for cp in loads(c + 1, 1 - buf):
                        cp.start()

                for cp in scatters(buf):
                    cp.start()

        for cp in scatters(1):
            cp.wait()

    return scatter_kernel(rows, pos_flat)


def _const_spec(shape):
    zeros = (0,) * len(shape)
    return pl.BlockSpec(shape, lambda *_: zeros, pipeline_mode=pl.Buffered(1))


def _tc_params(n_axes):
    return pltpu.CompilerParams(dimension_semantics=("arbitrary",) * n_axes,
                                vmem_limit_bytes=V7X_VMEM_LIMIT_BYTES)


def _mix_out_shapes(t, d, n_exp):
    return (jax.ShapeDtypeStruct((t, d), F32),
            jax.ShapeDtypeStruct((t, d // 2), jnp.uint32),
            jax.ShapeDtypeStruct((TOP_K, t), jnp.int32),
            jax.ShapeDtypeStruct((TOP_K, t), jnp.int32),
            jax.ShapeDtypeStruct((n_exp, V7X_LANES), F32),
            jax.ShapeDtypeStruct((t, V7X_LANES), F32))


def _mix_out_specs(rows, d, n_exp, row, tok_col):
    return (pl.BlockSpec((rows, d), row),
            pl.BlockSpec((rows, d // 2), row),
            pl.BlockSpec((TOP_K, rows), tok_col),
            pl.BlockSpec((TOP_K, rows), tok_col),
            pl.BlockSpec((n_exp, V7X_LANES), lambda *_: (0, 0)),
            pl.BlockSpec((rows, V7X_LANES), row))


def _strict_upper_tri(n):
    return (lax.broadcasted_iota(jnp.int32, (n, n), 0) < lax.broadcasted_iota(jnp.int32, (n, n), 1)).astype(BF16)


def _pool_layer(h, mem, w_in, wbd, pscale, mem_kv, w_out, ln_g, ln_b, rwt, rb, dn_alpha, seq):
    t, d = h.shape
    n_mem = mem.shape[1]
    pool_width = wbd.shape[0]
    ts = SEQ_BLOCK
    nsb = seq // ts
    row = lambda b, s: (b * nsb + s, 0)
    n_exp = rwt.shape[0]
    tri = _strict_upper_tri(ts)
    return pl.pallas_call(
        functools.partial(_pool_layer_kernel, dn_alpha),
        out_shape=_mix_out_shapes(t, d, n_exp),
        grid=(t // seq, nsb),
        in_specs=[
            pl.BlockSpec((ts, d), row),
            pl.BlockSpec((1, n_mem, d), lambda b, s: (b, 0, 0)),
            _const_spec(w_in.shape), _const_spec(wbd.shape), _const_spec(pscale.shape), _const_spec(mem_kv.shape),
            _const_spec(w_out.shape), _const_spec(ln_g.shape), _const_spec(ln_b.shape), _const_spec(rwt.shape),
            _const_spec(rb.shape), _const_spec(tri.shape),
        ],
        out_specs=_mix_out_specs(ts, d, n_exp, row, lambda b, s: (0, b * nsb + s)),
        scratch_shapes=[pltpu.VMEM((ts + POOL_HISTORY, pool_width), F32)] * 4
        + [pltpu.VMEM((n_mem, 2 * MEM_WIDTH), BF16), pltpu.VMEM((ts, d), BF16), pltpu.VMEM((n_exp, V7X_LANES), F32)],
        compiler_params=_tc_params(2),
        name="pool_layer",
    )(h, mem, w_in, wbd, pscale, mem_kv, w_out, ln_g, ln_b, rwt, rb, tri)


def _expand_heads(w, n_heads):
    d = w.shape[0]
    w3 = w.reshape(d, n_heads, HEAD_DIM)
    return jnp.pad(w3, ((0, 0), (0, 0), (0, V7X_LANES - HEAD_DIM))).reshape(d, n_heads * V7X_LANES)


def _shared_kv(h, kvw, fb, seq, n_heads):
    t, d = h.shape
    ts = SEQ_BLOCK
    nsb = seq // ts
    k_width = n_heads * V7X_LANES
    v_width = n_heads * HEAD_DIM
    tri = (lax.broadcasted_iota(jnp.int32, (ts, ts), 1) <= lax.broadcasted_iota(jnp.int32, (ts, ts), 0)).astype(BF16)
    row = lambda b, s: (b * nsb + s, 0)
    return pl.pallas_call(
        _shared_kv_kernel,
        out_shape=(jax.ShapeDtypeStruct((t, k_width), BF16),
                   jax.ShapeDtypeStruct((t // seq, v_width, seq), BF16),
                   jax.ShapeDtypeStruct((t, V7X_LANES), F32)),
        grid=(t // seq, nsb),
        in_specs=[pl.BlockSpec((ts, d), row), _const_spec(kvw.shape), _const_spec(fb.shape), _const_spec(tri.shape)],
        out_specs=(pl.BlockSpec((ts, k_width), row),
                   pl.BlockSpec((1, v_width, ts), lambda b, s: (b, 0, s)),
                   pl.BlockSpec((ts, V7X_LANES), row)),
        scratch_shapes=[pltpu.VMEM((8, V7X_LANES), F32)],
        compiler_params=_tc_params(2),
        name="shared_kv",
    )(h, kvw, fb, tri)


def _fox_layer(h, mem, k_aug, v_t, ccol, w_in, mem_kv, w_out, ln_g, ln_b, rwt, rb, dn_alpha, seq):
    t, d = h.shape
    n_mem = mem.shape[1]
    k_width = k_aug.shape[1]
    v_width = v_t.shape[1]
    n_heads = k_width // V7X_LANES
    tq = Q_BLOCK
    nqb = seq // tq
    n_batch = t // seq
    row = lambda b, i: (b * nqb + i, 0)
    per_batch = lambda b, i: (b, 0, 0)
    n_exp = rwt.shape[0]
    tri = _strict_upper_tri(tq)
    return pl.pallas_call(
        functools.partial(_fox_layer_kernel, dn_alpha),
        out_shape=_mix_out_shapes(t, d, n_exp),
        grid=(n_batch, nqb),
        in_specs=[
            pl.BlockSpec((tq, d), row),
            pl.BlockSpec((1, n_mem, d), per_batch),
            pl.BlockSpec((1, seq, k_width), per_batch),
            pl.BlockSpec((1, v_width, seq), per_batch),
            pl.BlockSpec((tq, V7X_LANES), row),
            _const_spec(w_in.shape), _const_spec(mem_kv.shape), _const_spec(w_out.shape), _const_spec(ln_g.shape),
            _const_spec(ln_b.shape), _const_spec(rwt.shape), _const_spec(rb.shape), _const_spec(tri.shape),
        ],
        out_specs=_mix_out_specs(tq, d, n_exp, row, lambda b, i: (0, b * nqb + i)),
        scratch_shapes=[pltpu.VMEM((n_mem, 2 * MEM_WIDTH), BF16), pltpu.VMEM((tq, d), BF16),
                        pltpu.VMEM((n_heads, tq, V7X_LANES), BF16),
                        pltpu.VMEM((16, tq), F32), pltpu.VMEM((16, tq), F32), pltpu.VMEM((v_width, tq), F32),
                        pltpu.VMEM((n_exp, V7X_LANES), F32)],
        compiler_params=_tc_params(2),
        name="fox_layer",
    )(h, mem, k_aug.reshape(n_batch, seq, k_width), v_t, ccol, w_in, mem_kv, w_out, ln_g, ln_b, rwt, rb, tri)


def _slot_kernel(pad_start_ref, idx_ref, rank_ref, pos_ref):
    pos = rank_ref[...]
    idx = idx_ref[...]
    for e in range(pad_start_ref.shape[0]):
        pos = pos + jnp.where(idx == e, pad_start_ref[e], 0)
    pos_ref[...] = pos


def _route(idx_t, rank_t, counts, n_experts):
    top_k, t = idx_t.shape
    n_assign = top_k * t
    padded = ((counts + MOE_BLOCK - 1) // MOE_BLOCK) * MOE_BLOCK
    pad_end = jnp.cumsum(padded)
    pad_start = pad_end - padded
    n_blocks = n_assign // MOE_BLOCK + n_experts
    block_start = jnp.arange(n_blocks, dtype=jnp.int32) * MOE_BLOCK
    block_expert = jnp.minimum(jnp.sum((pad_end[None, :] <= block_start[:, None]).astype(jnp.int32), axis=1),
                               n_experts - 1)
    block_valid = jnp.clip(counts[block_expert] - (block_start - pad_start[block_expert]), 0, MOE_BLOCK)
    n_active = (pad_end[-1:] // MOE_BLOCK).astype(jnp.int32)
    expert_ids = jnp.arange(n_experts, dtype=jnp.int32)
    in_use = counts > 0
    expert_slot = (jnp.cumsum(in_use.astype(jnp.int32)) - in_use.astype(jnp.int32)) % 2
    later = lax.cummin(jnp.where(in_use, expert_ids, n_experts), axis=0, reverse=True)
    next_in_use = jnp.concatenate([later[1:], jnp.full((1,), n_experts, jnp.int32)])
    next_in_use = jnp.where(next_in_use < n_experts, next_in_use, -1)
    block_slot = expert_slot[block_expert].astype(jnp.int32)
    block_next = next_in_use[block_expert].astype(jnp.int32)
    cols = min(SLOT_COLS, t)
    pos = pl.pallas_call(
        _slot_kernel,
        out_shape=jax.ShapeDtypeStruct((top_k, t), jnp.int32),
        grid_spec=pltpu.PrefetchScalarGridSpec(
            num_scalar_prefetch=1,
            grid=(t // cols,),
            in_specs=[pl.BlockSpec((top_k, cols), lambda i, ps: (0, i))] * 2,
            out_specs=pl.BlockSpec((top_k, cols), lambda i, ps: (0, i)),
        ),
        compiler_params=_tc_params(1),
        name="slots",
    )(pad_start.astype(jnp.int32), idx_t, rank_t)
    return pos, (block_expert, block_valid.astype(jnp.int32), block_slot, block_next, n_active)


def _experts(xs, block_expert, block_valid, block_slot, block_next, n_active, w1_all, b1, w2_all, b2, layer):
    n_slots, half_d = xs.shape
    _, n_exp, d, two_f = w1_all.shape
    d_ff = two_f // 2
    assert d == 2 * half_d
    block = lambda i, *_: (i, 0)
    per_expert = lambda i, be, *_: (be[i], 0, 0)
    grid_spec = pltpu.PrefetchScalarGridSpec(
        num_scalar_prefetch=5,
        grid=(n_slots // MOE_BLOCK,),
        in_specs=[
            pl.BlockSpec((MOE_BLOCK, half_d), block),
            pl.BlockSpec(memory_space=pl.ANY),
            pl.BlockSpec((1, 1, two_f), per_expert),
            pl.BlockSpec(memory_space=pl.ANY),
            pl.BlockSpec((1, 1, d), per_expert),
        ],
        out_specs=pl.BlockSpec((MOE_BLOCK, half_d), block),
        scratch_shapes=[pltpu.VMEM((d, two_f), BF16), pltpu.VMEM((d_ff, d), BF16), pltpu.VMEM((MOE_BLOCK, d_ff), BF16),
                        pltpu.VMEM((2, d, two_f), F32), pltpu.VMEM((2, d_ff, d), F32),
                        pltpu.SemaphoreType.DMA((2, 2))],
    )
    return pl.pallas_call(
        functools.partial(_expert_kernel, layer),
        out_shape=jax.ShapeDtypeStruct((n_slots, half_d), jnp.uint32),
        grid_spec=grid_spec,
        compiler_params=_tc_params(1),
        name="experts",
    )(block_expert, block_valid, block_slot, block_next, n_active, xs, w1_all, b1.reshape(n_exp, 1, two_f), w2_all,
      b2.reshape(n_exp, 1, d))


def _combine(h, yg, gcol, ln_g, ln_b, dn_alpha):
    t, d = h.shape
    ts = SEQ_BLOCK
    row = lambda i: (i, 0)
    return pl.pallas_call(
        functools.partial(_combine_kernel, dn_alpha),
        out_shape=jax.ShapeDtypeStruct((t, d), F32),
        grid=(t // ts,),
        in_specs=[pl.BlockSpec((ts, d), row), pl.BlockSpec((TOP_K, ts, d // 2), lambda i: (0, i, 0)),
                  pl.BlockSpec((ts, V7X_LANES), row), _const_spec(ln_g.shape), _const_spec(ln_b.shape)],
        out_specs=pl.BlockSpec((ts, d), row),
        compiler_params=_tc_params(1),
        name="combine",
    )(h, yg.reshape(TOP_K, t, d // 2), gcol, ln_g, ln_b)


def _moe(mixed, w1_all, b1, w2_all, b2, layer, ln_g, ln_b, dn_alpha):
    h1, h1_packed, idx_t, rank_t, counts, gcol = mixed
    n_exp = w1_all.shape[1]
    d_ff = w2_all.shape[2]
    pos, block_meta = _route(idx_t, rank_t, counts[:, 0].astype(jnp.int32), n_exp)
    b1p = jnp.swapaxes(b1.reshape(n_exp, d_ff // V7X_LANES, V7X_LANES, 2), 2, 3).reshape(n_exp, 2 * d_ff)
    xs = _sc_scatter_rows(h1_packed, pos, block_meta[0].shape[0] * MOE_BLOCK)
    y = _experts(xs, *block_meta, w1_all, b1p, w2_all, b2, layer)
    yg = _sc_gather_rows(y, pos.reshape(-1))
    return _combine(h1, yg, gcol, ln_g, ln_b, dn_alpha)


def kernel(x, mem, ln1_g, ln1_b, ln2_g, ln2_b, a_w_in, pool_w, pool_scale, a_mem_kv, a_w_out, kv_w, fgate_b, b_w_in, b_mem_kv, b_w_out, router_w, router_b, moe_w1, moe_b1, moe_w2, moe_b2):
    n_batch, seq, d = x.shape
    t = n_batch * seq
    depth = ln1_g.shape[0]
    n_a = a_w_in.shape[0]
    dn_alpha = float((2 * depth) ** 0.25)
    n_exp = router_w.shape[2]
    fox_heads = fgate_b.shape[0]
    fox_width = fox_heads * HEAD_DIM
    assert seq % SEQ_BLOCK == 0 and seq % Q_BLOCK == 0 and fox_heads <= 16

    h = x.reshape(t, d)
    row = lambda a: a.reshape(1, -1)
    shared = None
    for l in range(depth):
        rwt = router_w[l].T
        rb = router_b[l].reshape(n_exp, 1)
        if l < n_a:
            wbd = jax.scipy.linalg.block_diag(*[pool_w[l, g] for g in range(pool_w.shape[1])]).astype(BF16)
            mixed = _pool_layer(
                h, mem, a_w_in[l].astype(BF16), wbd, row(pool_scale[l]), a_mem_kv[l].astype(BF16),
                a_w_out[l].astype(BF16), row(ln1_g[l]), row(ln1_b[l]), rwt, rb, dn_alpha, seq)
        else:
            j = l - n_a
            if shared is None:
                pad = V7X_LANES - fox_heads
                kvw = jnp.concatenate([_expand_heads(kv_w[:, :fox_width], fox_heads),
                                       kv_w[:, fox_width:2 * fox_width],
                                       jnp.pad(kv_w[:, 2 * fox_width:], ((0, 0), (0, pad)))], axis=1).astype(BF16)
                fb = jnp.pad(fgate_b, (0, pad)).reshape(1, V7X_LANES)
                shared = _shared_kv(h, kvw, fb, seq, fox_heads)
            k_aug, v_t, ccol = shared
            w_in = jnp.concatenate([_expand_heads(b_w_in[j][:, :fox_width], fox_heads),
                                    b_w_in[j][:, fox_width:]], axis=1).astype(BF16)
            mixed = _fox_layer(
                h, mem, k_aug, v_t, ccol, w_in, b_mem_kv[j].astype(BF16),
                b_w_out[j].astype(BF16), row(ln1_g[l]), row(ln1_b[l]), rwt, rb, dn_alpha, seq)
        h = _moe(mixed, moe_w1, moe_b1[l], moe_w2, moe_b2[l], l, row(ln2_g[l]), row(ln2_b[l]), dn_alpha)
    return h.reshape(n_batch, seq, d)
```

```python
import functools
import jax
import jax.numpy as jnp
from jax import lax
from jax.experimental import pallas as pl
from jax.experimental.pallas import tpu as pltpu
from jax.experimental.pallas import tpu_sc as plsc

HEAD_DIM = 64
MEM_HEADS = 4
MEM_WIDTH = MEM_HEADS * HEAD_DIM
POOL_WINDOWS = (2, 4, 8, 16)
TOP_K = 4
SWIGLU_ALPHA = 1.702
SWIGLU_LIMIT = 7.0
MOE_BLOCK = 512
LN_EPS = 1e-5
ATTN_SCALE = HEAD_DIM ** -0.5
LOG2_E = 1.4426950408889634

V7X_LANES = 128
V7X_VMEM_LIMIT_BYTES = 56 * 1024 * 1024
POOL_HISTORY = 32
SC_CHUNK = 64
SEQ_BLOCK = 512
Q_BLOCK = 512
SLOT_COLS = 8192

F32 = jnp.float32
BF16 = jnp.bfloat16
NEG_INF = float("-inf")


def _dot(a, b):
    return jnp.dot(a, b, preferred_element_type=F32)


def _dot_nt(a, b):
    return lax.dot_general(a, b, (((1,), (1,)), ((), ())), preferred_element_type=F32)


def _lane_iota(shape):
    return lax.broadcasted_iota(jnp.int32, shape, len(shape) - 1)


def _pair_attention(q2, keys, vals, bias=None):
    lane = _lane_iota(q2.shape)
    first = lane < HEAD_DIM
    outs = []
    for sel in (first, jnp.logical_not(first)):
        qh = jnp.where(sel, q2, 0.0).astype(BF16)
        s = _dot_nt(qh, keys)
        m = jnp.max(s, axis=-1, keepdims=True)
        p = jnp.exp(s - m)
        l = jnp.sum(p, axis=-1, keepdims=True)
        outs.append(_dot(p.astype(BF16), vals) / l)
    return jnp.where(first, outs[0], outs[1])


def _memory_attention(proj, kv_ref, cat_ref, out_off, q_off):
    for p in range(MEM_HEADS // 2):
        lo = p * V7X_LANES
        q2 = proj[:, q_off + lo:q_off + lo + V7X_LANES] * ATTN_SCALE
        keys = kv_ref[:, lo:lo + V7X_LANES]
        vals = kv_ref[:, MEM_WIDTH + lo:MEM_WIDTH + lo + V7X_LANES]
        cat_ref[:, out_off + lo:out_off + lo + V7X_LANES] = _pair_attention(q2, keys, vals).astype(BF16)


def _layer_norm(z, g, b):
    mu = jnp.mean(z, axis=-1, keepdims=True)
    zc = z - mu
    var = jnp.mean(zc * zc, axis=-1, keepdims=True)
    return zc * lax.rsqrt(var + LN_EPS) * g + b


def _pack_bf16_pairs(x):
    w = x.shape[1] // 2
    bits = lax.bitcast_convert_type(x.astype(BF16).astype(F32), jnp.uint32)
    return lax.shift_right_logical(bits[:, :w], jnp.uint32(16)) | (bits[:, w:] & jnp.uint32(0xFFFF0000))


def _unpack_bf16_pairs(p):
    low = lax.bitcast_convert_type(lax.shift_left(p, jnp.uint32(16)), F32)
    high = lax.bitcast_convert_type(p & jnp.uint32(0xFFFF0000), F32)
    return low, high


def _expert_ranks(idxs, tri_ref, rank_ref, counts_ref, base_ref):
    @pl.when(jnp.logical_and(pl.program_id(0) == 0, pl.program_id(1) == 0))
    def _():
        base_ref[...] = jnp.zeros(base_ref.shape, F32)

    n_exp = base_ref.shape[0]
    t = idxs[0].shape[1]
    rows = lax.broadcasted_iota(jnp.int32, (n_exp, t), 0)
    running = base_ref[:, 0:1]
    for k in range(TOP_K):
        onehot = rows == idxs[k]
        prefix = _dot(jnp.where(onehot, 1.0, 0.0).astype(BF16), tri_ref[...])
        rank = jnp.sum(jnp.where(onehot, prefix + running, 0.0), axis=0, keepdims=True)
        rank_ref[k:k + 1, :] = rank.astype(jnp.int32)
        running = running + jnp.sum(jnp.where(onehot, 1.0, 0.0), axis=1, keepdims=True)
    base_ref[...] = jnp.broadcast_to(running, base_ref.shape)
    counts_ref[...] = base_ref[...]


def _mix_epilogue(x, cat_ref, w_out_ref, g_ref, b_ref, rwt_ref, rb_ref, tri_ref, dn_alpha, h_out_ref, hp_ref, idx_ref,
                  rank_ref, counts_ref, gcol_ref, base_ref):
    mix = _dot(cat_ref[...], w_out_ref[...])
    h1 = _layer_norm(dn_alpha * x + mix, g_ref[...], b_ref[...])
    h_out_ref[...] = h1
    hp_ref[...] = _pack_bf16_pairs(h1)
    logits = lax.dot_general(rwt_ref[...], h1, (((1,), (1,)), ((), ())),
                             precision=lax.Precision.HIGHEST, preferred_element_type=F32) + rb_ref[...]
    n_exp, t = logits.shape
    rows = lax.broadcasted_iota(jnp.int32, (n_exp, t), 0)
    vals, idxs = [], []
    for _ in range(TOP_K):
        m = jnp.max(logits, axis=0, keepdims=True)
        ix = jnp.min(jnp.where(logits == m, rows, n_exp), axis=0, keepdims=True)
        vals.append(m)
        idxs.append(ix)
        logits = jnp.where(rows == ix, NEG_INF, logits)
    exps = [jnp.exp(v - vals[0]) for v in vals]
    denom = exps[0] + exps[1] + exps[2] + exps[3]
    for k in range(TOP_K):
        idx_ref[k:k + 1, :] = idxs[k]
    _expert_ranks(idxs, tri_ref, rank_ref, counts_ref, base_ref)
    lane_rows = lax.broadcasted_iota(jnp.int32, (V7X_LANES, t), 0)
    gates_t = jnp.zeros((V7X_LANES, t), F32)
    for k in range(TOP_K):
        gates_t = jnp.where(lane_rows == k, exps[k] / denom, gates_t)
    gcol_ref[...] = gates_t.T


def _pool_layer_kernel(dn_alpha, h_ref, mem_ref, w_in_ref, wbd_ref, pscale_ref, mem_kv_ref, w_out_ref, g_ref, b_ref,
                       rwt_ref, rb_ref, tri_ref, h_out_ref, hp_ref, idx_ref, rank_ref, counts_ref, gcol_ref,
                       ue_ref, w2_ref, w4_ref, w8_ref, kv_ref, cat_ref, base_ref):
    s = pl.program_id(1)
    ts = h_ref.shape[0]
    pool_width = wbd_ref.shape[0]
    hist = POOL_HISTORY

    @pl.when(s == 0)
    def _():
        ue_ref[0:hist, :] = jnp.zeros((hist, pool_width), F32)
        kv_ref[...] = _dot(mem_ref[0].astype(BF16), mem_kv_ref[...]).astype(BF16)

    x = h_ref[...]
    proj = _dot(x.astype(BF16), w_in_ref[...])
    u = proj[:, :pool_width]
    ue_ref[hist:, :] = u
    n = ts + hist
    w2_ref[8:n, :] = ue_ref[8:n, :] + ue_ref[7:n - 1, :]
    w4_ref[16:n, :] = w2_ref[16:n, :] + w2_ref[14:n - 2, :]
    w8_ref[24:n, :] = w4_ref[24:n, :] + w4_ref[20:n - 4, :]
    w16 = w8_ref[hist:n, :] + w8_ref[hist - 8:n - 8, :]
    group_dim = pool_width // len(POOL_WINDOWS)
    col = _lane_iota((ts, pool_width))
    wsum = jnp.where(col < group_dim, w2_ref[hist:n, :],
                     jnp.where(col < 2 * group_dim, w4_ref[hist:n, :],
                               jnp.where(col < 3 * group_dim, w8_ref[hist:n, :], w16)))
    window = jnp.where(col < group_dim, POOL_WINDOWS[0],
                       jnp.where(col < 2 * group_dim, POOL_WINDOWS[1],
                                 jnp.where(col < 3 * group_dim, POOL_WINDOWS[2], POOL_WINDOWS[3])))
    tpos = s * ts + lax.broadcasted_iota(jnp.int32, (ts, pool_width), 0) + 1
    cnt = jnp.minimum(tpos, window).astype(F32)
    d = wsum / cnt - u
    ue_ref[0:hist, :] = u[ts - hist:, :]
    pooled = _dot(d.astype(BF16), wbd_ref[...]) * pscale_ref[...]
    cat_ref[:, :pool_width] = pooled.astype(BF16)
    _memory_attention(proj, kv_ref, cat_ref, pool_width, pool_width)
    _mix_epilogue(x, cat_ref, w_out_ref, g_ref, b_ref, rwt_ref, rb_ref, tri_ref, dn_alpha, h_out_ref, hp_ref, idx_ref,
                  rank_ref, counts_ref, gcol_ref, base_ref)


def _bias_pieces(c):
    hi = c.astype(BF16).astype(F32)
    r = c - hi
    mid = r.astype(BF16).astype(F32)
    lo = (r - mid).astype(BF16).astype(F32)
    return hi, mid, lo


def _augment_head(x, c_col, is_query):
    lane = _lane_iota(x.shape)
    hi, mid, lo = _bias_pieces(jnp.broadcast_to(c_col, x.shape))
    b = HEAD_DIM
    if is_query:
        aug = jnp.where(lane == b, hi, jnp.where(lane == b + 1, mid, jnp.where(lane == b + 2, lo,
                        jnp.where(lane < b + 6, 1.0, 0.0))))
    else:
        aug = jnp.where(lane == b + 3, -hi, jnp.where(lane == b + 4, -mid, jnp.where(lane == b + 5, -lo,
                        jnp.where(lane < b + 3, 1.0, 0.0))))
    return jnp.where(lane < b, x, aug).astype(BF16)


def _shared_kv_kernel(h_ref, kvw_ref, fb_ref, tri_ref, k_ref, vt_ref, ccol_ref, carry_ref):
    s = pl.program_id(1)
    k_width = k_ref.shape[1]
    v_width = vt_ref.shape[1]

    @pl.when(s == 0)
    def _():
        carry_ref[...] = jnp.zeros(carry_ref.shape, F32)

    proj = _dot(h_ref[...].astype(BF16), kvw_ref[...])
    z = proj[:, k_width + v_width:] + fb_ref[...]
    logf = jnp.minimum(z, 0.0) - jnp.log(1.0 + jnp.exp(-jnp.abs(z)))
    hi, mid, lo = _bias_pieces(logf)
    tri = tri_ref[...]
    c = _dot(tri, hi.astype(BF16)) + _dot(tri, mid.astype(BF16)) + _dot(tri, lo.astype(BF16)) + carry_ref[0:1, :]
    ts = c.shape[0]
    carry_ref[0:1, :] = c[ts - 1:ts, :]
    ccol_ref[...] = c
    c_log2 = c * LOG2_E
    for h in range(k_width // V7X_LANES):
        cols = slice(h * V7X_LANES, (h + 1) * V7X_LANES)
        k_ref[:, cols] = _augment_head(proj[:, cols], c_log2[:, h:h + 1], False)
    vt_ref[0] = proj[:, k_width:k_width + v_width].T.astype(BF16)


def _fox_layer_kernel(dn_alpha, h_ref, mem_ref, k_ref, vt_ref, ccol_ref, w_in_ref, mem_kv_ref, w_out_ref,
                      g_ref, b_ref, rwt_ref, rb_ref, tri_ref, h_out_ref, hp_ref, idx_ref, rank_ref, counts_ref, gcol_ref,
                      kv_ref, cat_ref, qa_ref, m_ref, l_ref, acc_ref, base_ref):
    i = pl.program_id(1)
    tq = h_ref.shape[0]
    n_heads = k_ref.shape[2] // V7X_LANES
    fox_width = vt_ref.shape[1]

    @pl.when(i == 0)
    def _():
        kv_ref[...] = _dot(mem_ref[0].astype(BF16), mem_kv_ref[...]).astype(BF16)

    x = h_ref[...]
    proj = _dot(x.astype(BF16), w_in_ref[...])
    c_t = ccol_ref[...] * LOG2_E
    for h in range(n_heads):
        cols = slice(h * V7X_LANES, (h + 1) * V7X_LANES)
        qa_ref[h] = _augment_head(proj[:, cols] * (ATTN_SCALE * LOG2_E), c_t[:, h:h + 1], True)
    m_ref[...] = jnp.full(m_ref.shape, NEG_INF, F32)
    l_ref[...] = jnp.zeros(l_ref.shape, F32)
    acc_ref[...] = jnp.zeros(acc_ref.shape, F32)
    causal_t = lax.broadcasted_iota(jnp.int32, (tq, tq), 0) <= lax.broadcasted_iota(jnp.int32, (tq, tq), 1)

    ones_rows = (lax.broadcasted_iota(jnp.int32, (16, tq), 0) == 0).astype(BF16)

    def tile(j, masked):
        start = pl.multiple_of(j * tq, tq)

        def scores(h):
            kb = k_ref[0, pl.ds(start, tq), h * V7X_LANES:(h + 1) * V7X_LANES]
            return _dot_nt(kb, qa_ref[h])

        st_next = scores(0)
        for h in range(n_heads):
            st = st_next
            if h + 1 < n_heads:
                st_next = scores(h + 1)
            if masked:
                st = jnp.where(causal_t, st, NEG_INF)
            m_old = m_ref[h:h + 1, :]
            m_new = jnp.maximum(m_old, jnp.max(st, axis=0, keepdims=True))
            alpha = jnp.exp2(m_old - m_new)
            pt = jnp.exp2(st - m_new).astype(BF16)
            m_ref[h:h + 1, :] = m_new
            rows = slice(h * HEAD_DIM, (h + 1) * HEAD_DIM)
            vb = jnp.concatenate([vt_ref[0, rows, pl.ds(start, tq)], ones_rows], axis=0)
            pv = _dot(vb, pt)
            acc_ref[rows, :] = alpha * acc_ref[rows, :] + pv[:HEAD_DIM]
            l_ref[h:h + 1, :] = alpha * l_ref[h:h + 1, :] + pv[HEAD_DIM:HEAD_DIM + 1]

    tile(i, True)

    def off_diagonal(j, carry):
        tile(j, False)
        return carry

    lax.fori_loop(0, i, off_diagonal, 0)
    for h in range(n_heads):
        rows = slice(h * HEAD_DIM, (h + 1) * HEAD_DIM)
        acc_ref[rows, :] = acc_ref[rows, :] / l_ref[h:h + 1, :]
    cat_ref[:, :fox_width] = acc_ref[...].T.astype(BF16)
    _memory_attention(proj, kv_ref, cat_ref, fox_width, n_heads * V7X_LANES)
    _mix_epilogue(x, cat_ref, w_out_ref, g_ref, b_ref, rwt_ref, rb_ref, tri_ref, dn_alpha, h_out_ref, hp_ref, idx_ref,
                  rank_ref, counts_ref, gcol_ref, base_ref)


def _expert_kernel(layer, be_ref, bv_ref, bslot_ref, bnext_ref, nact_ref, x_ref, w1_hbm, b1_ref, w2_hbm, b2_ref, y_ref,
                   w1s_ref, w2s_ref, act_ref, w1f_ref, w2f_ref, sem_ref):
    i = pl.program_id(0)
    active = i < nact_ref[0]
    two_f = w1f_ref.shape[2]
    group = 2 * V7X_LANES
    n_groups = two_f // group

    def weight_copies(expert, slot):
        return (pltpu.make_async_copy(w1_hbm.at[layer, expert], w1f_ref.at[slot], sem_ref.at[0, slot]),
                pltpu.make_async_copy(w2_hbm.at[layer, expert], w2f_ref.at[slot], sem_ref.at[1, slot]))

    @pl.when(i == 0)
    def _():
        for cp in weight_copies(be_ref[0], bslot_ref[0]):
            cp.start()

    @pl.when(jnp.logical_and(active, jnp.logical_or(i == 0, be_ref[i] != be_ref[jnp.maximum(i - 1, 0)])))
    def _():
        slot = bslot_ref[i]
        for cp in weight_copies(be_ref[i], slot):
            cp.wait()

        @pl.when(bnext_ref[i] >= 0)
        def _():
            for cp in weight_copies(bnext_ref[i], 1 - slot):
                cp.start()

        src = lax.broadcasted_iota(jnp.int32, (group, group), 0)
        dst = lax.broadcasted_iota(jnp.int32, (group, group), 1)
        want = jnp.where(dst < V7X_LANES, 2 * dst, 2 * (dst - V7X_LANES) + 1)
        perm = (src == want).astype(BF16)
        for g in range(n_groups):
            cols = slice(g * group, (g + 1) * group)
            w1s_ref[:, cols] = _dot(w1f_ref[slot, :, cols].astype(BF16), perm).astype(BF16)
        w2s_ref[...] = w2f_ref[slot].astype(BF16)

    @pl.when(active)
    def _():
        row_id = lax.broadcasted_iota(jnp.int32, x_ref.shape, 0)
        x_low, x_high = _unpack_bf16_pairs(jnp.where(row_id < bv_ref[i], x_ref[...], jnp.uint32(0)))
        xb = jnp.concatenate([x_low.astype(BF16), x_high.astype(BF16)], axis=1)
        hdn = _dot(xb, w1s_ref[...]) + b1_ref[0]
        for g in range(n_groups):
            x_glu = jnp.minimum(hdn[:, g * group:g * group + V7X_LANES], SWIGLU_LIMIT)
            x_lin = jnp.clip(hdn[:, g * group + V7X_LANES:(g + 1) * group], -SWIGLU_LIMIT, SWIGLU_LIMIT)
            act = x_glu * jax.nn.sigmoid(SWIGLU_ALPHA * x_glu) * (x_lin + 1.0)
            act_ref[:, g * V7X_LANES:(g + 1) * V7X_LANES] = act.astype(BF16)
        y_ref[...] = _pack_bf16_pairs(_dot(act_ref[...], w2s_ref[...]) + b2_ref[0])

    @pl.when(jnp.logical_not(active))
    def _():
        y_ref[...] = jnp.zeros(y_ref.shape, jnp.uint32)


def _combine_kernel(dn_alpha, h_ref, y_ref, gcol_ref, g_ref, b_ref, o_ref):
    gates = gcol_ref[...]
    low, high = _unpack_bf16_pairs(y_ref[0])
    low, high = gates[:, 0:1] * low, gates[:, 0:1] * high
    for k in range(1, TOP_K):
        low_k, high_k = _unpack_bf16_pairs(y_ref[k])
        low = low + gates[:, k:k + 1] * low_k
        high = high + gates[:, k:k + 1] * high_k
    ffn = jnp.concatenate([low, high], axis=1)
    o_ref[...] = _layer_norm(dn_alpha * h_ref[...] + ffn, g_ref[...], b_ref[...])


def _sc_gather_rows(table, idx):
    n, = idx.shape
    _, d = table.shape
    info = plsc.get_sparse_core_info()
    n_workers = info.num_cores * info.num_subcores
    per_worker = n // n_workers
    n_chunks = per_worker // SC_CHUNK
    assert n % (n_workers * SC_CHUNK) == 0 and n_chunks % 2 == 0
    mesh = plsc.VectorSubcoreMesh(core_axis_name="core", subcore_axis_name="subcore")

    @functools.partial(
        pl.kernel,
        out_type=jax.ShapeDtypeStruct((n, d), table.dtype),
        mesh=mesh,
        scratch_types=[
            pltpu.VMEM((per_worker,), jnp.int32),
            pltpu.VMEM((2, SC_CHUNK, d), table.dtype),
            pltpu.SemaphoreType.DMA((2,)),
            pltpu.SemaphoreType.DMA((2,)),
        ],
    )
    def gather_kernel(table_hbm, idx_hbm, out_hbm, idx_v, rows_v, gather_sem, write_sem):
        worker = lax.axis_index("subcore") * info.num_cores + lax.axis_index("core")
        base = pl.multiple_of(worker * per_worker, SC_CHUNK)
        pltpu.sync_copy(idx_hbm.at[pl.ds(base, per_worker)], idx_v)

        def gather(c, buf):
            rows = idx_v.at[pl.ds(pl.multiple_of(c * SC_CHUNK, SC_CHUNK), SC_CHUNK)]
            return pltpu.make_async_copy(table_hbm.at[rows], rows_v.at[buf], gather_sem.at[buf])

        def write(c, buf):
            off = pl.multiple_of(base + c * SC_CHUNK, SC_CHUNK)
            return pltpu.make_async_copy(rows_v.at[buf], out_hbm.at[pl.ds(off, SC_CHUNK)], write_sem.at[buf])

        gather(0, 0).start()

        @pl.loop(0, n_chunks, step=2)
        def _(c0):
            for buf in range(2):
                c = c0 + buf
                gather(c, buf).wait()

                @pl.when(c >= 1)
                def _():
                    write(c - 1, 1 - buf).wait()

                @pl.when(c + 1 < n_chunks)
                def _():
                    gather(c + 1, 1 - buf).start()

                write(c, buf).start()

        write(n_chunks - 1, 1).wait()

    return gather_kernel(table, idx)


def _sc_scatter_rows(rows, pos, n_out):
    t, d = rows.shape
    top_k = pos.shape[0]
    pos_flat = pos.reshape(-1)
    info = plsc.get_sparse_core_info()
    n_workers = info.num_cores * info.num_subcores
    per_worker = t // n_workers
    n_chunks = per_worker // SC_CHUNK
    assert t % (n_workers * SC_CHUNK) == 0 and n_chunks % 2 == 0
    mesh = plsc.VectorSubcoreMesh(core_axis_name="core", subcore_axis_name="subcore")

    @functools.partial(
        pl.kernel,
        out_type=jax.ShapeDtypeStruct((n_out, d), rows.dtype),
        mesh=mesh,
        scratch_types=[pltpu.VMEM((2, SC_CHUNK, d), rows.dtype)]
        + [pltpu.VMEM((SC_CHUNK,), jnp.int32) for _ in range(2 * top_k)]
        + [pltpu.SemaphoreType.DMA((2,)), pltpu.SemaphoreType.DMA((2,))],
    )
    def scatter_kernel(rows_hbm, pos_hbm, out_hbm, rows_v, *rest):
        idx_bufs = rest[:2 * top_k]
        load_sem, write_sem = rest[2 * top_k:]
        worker = lax.axis_index("subcore") * info.num_cores + lax.axis_index("core")
        base = pl.multiple_of(worker * per_worker, SC_CHUNK)

        def loads(c, buf):
            off = pl.multiple_of(base + c * SC_CHUNK, SC_CHUNK)
            copies = [pltpu.make_async_copy(rows_hbm.at[pl.ds(off, SC_CHUNK)], rows_v.at[buf], load_sem.at[buf])]
            for k in range(top_k):
                src = pos_hbm.at[pl.ds(pl.multiple_of(k * t + off, SC_CHUNK), SC_CHUNK)]
                copies.append(pltpu.make_async_copy(src, idx_bufs[buf * top_k + k], load_sem.at[buf]))
            return copies

        def scatters(buf):
            return [pltpu.make_async_copy(rows_v.at[buf], out_hbm.at[idx_bufs[buf * top_k + k]], write_sem.at[buf])
                    for k in range(top_k)]

        for cp in loads(0, 0):
            cp.start()

        @pl.loop(0, n_chunks, step=2)
        def _(c0):
            for buf in range(2):
                c = c0 + buf
                for cp in loads(c, buf):
                    cp.wait()

                @pl.when(c >= 1)
                def _():
                    for cp in scatters(1 - buf):
                        cp.wait()

                @pl.when(c + 1 < n_chunks)
                def _():
                    for cp in loads(c + 1, 1 - buf):
                        cp.start()

                for cp in scatters(buf):
                    cp.start()

        for cp in scatters(1):
            cp.wait()

    return scatter_kernel(rows, pos_flat)


def _const_spec(shape):
    zeros = (0,) * len(shape)
    return pl.BlockSpec(shape, lambda *_: zeros, pipeline_mode=pl.Buffered(1))


def _tc_params(n_axes):
    return pltpu.CompilerParams(dimension_semantics=("arbitrary",) * n_axes,
                                vmem_limit_bytes=V7X_VMEM_LIMIT_BYTES)


def _mix_out_shapes(t, d, n_exp):
    return (jax.ShapeDtypeStruct((t, d), F32),
            jax.ShapeDtypeStruct((t, d // 2), jnp.uint32),
            jax.ShapeDtypeStruct((TOP_K, t), jnp.int32),
            jax.ShapeDtypeStruct((TOP_K, t), jnp.int32),
            jax.ShapeDtypeStruct((n_exp, V7X_LANES), F32),
            jax.ShapeDtypeStruct((t, V7X_LANES), F32))


def _mix_out_specs(rows, d, n_exp, row, tok_col):
    return (pl.BlockSpec((rows, d), row),
            pl.BlockSpec((rows, d // 2), row),
            pl.BlockSpec((TOP_K, rows), tok_col),
            pl.BlockSpec((TOP_K, rows), tok_col),
            pl.BlockSpec((n_exp, V7X_LANES), lambda *_: (0, 0)),
            pl.BlockSpec((rows, V7X_LANES), row))


def _strict_upper_tri(n):
    return (lax.broadcasted_iota(jnp.int32, (n, n), 0) < lax.broadcasted_iota(jnp.int32, (n, n), 1)).astype(BF16)


def _pool_layer(h, mem, w_in, wbd, pscale, mem_kv, w_out, ln_g, ln_b, rwt, rb, dn_alpha, seq):
    t, d = h.shape
    n_mem = mem.shape[1]
    pool_width = wbd.shape[0]
    ts = SEQ_BLOCK
    nsb = seq // ts
    row = lambda b, s: (b * nsb + s, 0)
    n_exp = rwt.shape[0]
    tri = _strict_upper_tri(ts)
    return pl.pallas_call(
        functools.partial(_pool_layer_kernel, dn_alpha),
        out_shape=_mix_out_shapes(t, d, n_exp),
        grid=(t // seq, nsb),
        in_specs=[
            pl.BlockSpec((ts, d), row),
            pl.BlockSpec((1, n_mem, d), lambda b, s: (b, 0, 0)),
            _const_spec(w_in.shape), _const_spec(wbd.shape), _const_spec(pscale.shape), _const_spec(mem_kv.shape),
            _const_spec(w_out.shape), _const_spec(ln_g.shape), _const_spec(ln_b.shape), _const_spec(rwt.shape),
            _const_spec(rb.shape), _const_spec(tri.shape),
        ],
        out_specs=_mix_out_specs(ts, d, n_exp, row, lambda b, s: (0, b * nsb + s)),
        scratch_shapes=[pltpu.VMEM((ts + POOL_HISTORY, pool_width), F32)] * 4
        + [pltpu.VMEM((n_mem, 2 * MEM_WIDTH), BF16), pltpu.VMEM((ts, d), BF16), pltpu.VMEM((n_exp, V7X_LANES), F32)],
        compiler_params=_tc_params(2),
        name="pool_layer",
    )(h, mem, w_in, wbd, pscale, mem_kv, w_out, ln_g, ln_b, rwt, rb, tri)


def _expand_heads(w, n_heads):
    d = w.shape[0]
    w3 = w.reshape(d, n_heads, HEAD_DIM)
    return jnp.pad(w3, ((0, 0), (0, 0), (0, V7X_LANES - HEAD_DIM))).reshape(d, n_heads * V7X_LANES)


def _shared_kv(h, kvw, fb, seq, n_heads):
    t, d = h.shape
    ts = SEQ_BLOCK
    nsb = seq // ts
    k_width = n_heads * V7X_LANES
    v_width = n_heads * HEAD_DIM
    tri = (lax.broadcasted_iota(jnp.int32, (ts, ts), 1) <= lax.broadcasted_iota(jnp.int32, (ts, ts), 0)).astype(BF16)
    row = lambda b, s: (b * nsb + s, 0)
    return pl.pallas_call(
        _shared_kv_kernel,
        out_shape=(jax.ShapeDtypeStruct((t, k_width), BF16),
                   jax.ShapeDtypeStruct((t // seq, v_width, seq), BF16),
                   jax.ShapeDtypeStruct((t, V7X_LANES), F32)),
        grid=(t // seq, nsb),
        in_specs=[pl.BlockSpec((ts, d), row), _const_spec(kvw.shape), _const_spec(fb.shape), _const_spec(tri.shape)],
        out_specs=(pl.BlockSpec((ts, k_width), row),
                   pl.BlockSpec((1, v_width, ts), lambda b, s: (b, 0, s)),
                   pl.BlockSpec((ts, V7X_LANES), row)),
        scratch_shapes=[pltpu.VMEM((8, V7X_LANES), F32)],
        compiler_params=_tc_params(2),
        name="shared_kv",
    )(h, kvw, fb, tri)


def _fox_layer(h, mem, k_aug, v_t, ccol, w_in, mem_kv, w_out, ln_g, ln_b, rwt, rb, dn_alpha, seq):
    t, d = h.shape
    n_mem = mem.shape[1]
    k_width = k_aug.shape[1]
    v_width = v_t.shape[1]
    n_heads = k_width // V7X_LANES
    tq = Q_BLOCK
    nqb = seq // tq
    n_batch = t // seq
    row = lambda b, i: (b * nqb + i, 0)
    per_batch = lambda b, i: (b, 0, 0)
    n_exp = rwt.shape[0]
    tri = _strict_upper_tri(tq)
    return pl.pallas_call(
        functools.partial(_fox_layer_kernel, dn_alpha),
        out_shape=_mix_out_shapes(t, d, n_exp),
        grid=(n_batch, nqb),
        in_specs=[
            pl.BlockSpec((tq, d), row),
            pl.BlockSpec((1, n_mem, d), per_batch),
            pl.BlockSpec((1, seq, k_width), per_batch),
            pl.BlockSpec((1, v_width, seq), per_batch),
            pl.BlockSpec((tq, V7X_LANES), row),
            _const_spec(w_in.shape), _const_spec(mem_kv.shape), _const_spec(w_out.shape), _const_spec(ln_g.shape),
            _const_spec(ln_b.shape), _const_spec(rwt.shape), _const_spec(rb.shape), _const_spec(tri.shape),
        ],
        out_specs=_mix_out_specs(tq, d, n_exp, row, lambda b, i: (0, b * nqb + i)),
        scratch_shapes=[pltpu.VMEM((n_mem, 2 * MEM_WIDTH), BF16), pltpu.VMEM((tq, d), BF16),
                        pltpu.VMEM((n_heads, tq, V7X_LANES), BF16),
                        pltpu.VMEM((16, tq), F32), pltpu.VMEM((16, tq), F32), pltpu.VMEM((v_width, tq), F32),
                        pltpu.VMEM((n_exp, V7X_LANES), F32)],
        compiler_params=_tc_params(2),
        name="fox_layer",
    )(h, mem, k_aug.reshape(n_batch, seq, k_width), v_t, ccol, w_in, mem_kv, w_out, ln_g, ln_b, rwt, rb, tri)


def _slot_kernel(pad_start_ref, idx_ref, rank_ref, pos_ref):
    pos = rank_ref[...]
    idx = idx_ref[...]
    for e in range(pad_start_ref.shape[0]):
        pos = pos + jnp.where(idx == e, pad_start_ref[e], 0)
    pos_ref[...] = pos


def _route(idx_t, rank_t, counts, n_experts):
    top_k, t = idx_t.shape
    n_assign = top_k * t
    padded = ((counts + MOE_BLOCK - 1) // MOE_BLOCK) * MOE_BLOCK
    pad_end = jnp.cumsum(padded)
    pad_start = pad_end - padded
    n_blocks = n_assign // MOE_BLOCK + n_experts
    block_start = jnp.arange(n_blocks, dtype=jnp.int32) * MOE_BLOCK
    block_expert = jnp.minimum(jnp.sum((pad_end[None, :] <= block_start[:, None]).astype(jnp.int32), axis=1),
                               n_experts - 1)
    block_valid = jnp.clip(counts[block_expert] - (block_start - pad_start[block_expert]), 0, MOE_BLOCK)
    n_active = (pad_end[-1:] // MOE_BLOCK).astype(jnp.int32)
    expert_ids = jnp.arange(n_experts, dtype=jnp.int32)
    in_use = counts > 0
    expert_slot = (jnp.cumsum(in_use.astype(jnp.int32)) - in_use.astype(jnp.int32)) % 2
    later = lax.cummin(jnp.where(in_use, expert_ids, n_experts), axis=0, reverse=True)
    next_in_use = jnp.concatenate([later[1:], jnp.full((1,), n_experts, jnp.int32)])
    next_in_use = jnp.where(next_in_use < n_experts, next_in_use, -1)
    block_slot = expert_slot[block_expert].astype(jnp.int32)
    block_next = next_in_use[block_expert].astype(jnp.int32)
    cols = min(SLOT_COLS, t)
    pos = pl.pallas_call(
        _slot_kernel,
        out_shape=jax.ShapeDtypeStruct((top_k, t), jnp.int32),
        grid_spec=pltpu.PrefetchScalarGridSpec(
            num_scalar_prefetch=1,
            grid=(t // cols,),
            in_specs=[pl.BlockSpec((top_k, cols), lambda i, ps: (0, i))] * 2,
            out_specs=pl.BlockSpec((top_k, cols), lambda i, ps: (0, i)),
        ),
        compiler_params=_tc_params(1),
        name="slots",
    )(pad_start.astype(jnp.int32), idx_t, rank_t)
    return pos, (block_expert, block_valid.astype(jnp.int32), block_slot, block_next, n_active)


def _experts(xs, block_expert, block_valid, block_slot, block_next, n_active, w1_all, b1, w2_all, b2, layer):
    n_slots, half_d = xs.shape
    _, n_exp, d, two_f = w1_all.shape
    d_ff = two_f // 2
    assert d == 2 * half_d
    block = lambda i, *_: (i, 0)
    per_expert = lambda i, be, *_: (be[i], 0, 0)
    grid_spec = pltpu.PrefetchScalarGridSpec(
        num_scalar_prefetch=5,
        grid=(n_slots // MOE_BLOCK,),
        in_specs=[
            pl.BlockSpec((MOE_BLOCK, half_d), block),
            pl.BlockSpec(memory_space=pl.ANY),
            pl.BlockSpec((1, 1, two_f), per_expert),
            pl.BlockSpec(memory_space=pl.ANY),
            pl.BlockSpec((1, 1, d), per_expert),
        ],
        out_specs=pl.BlockSpec((MOE_BLOCK, half_d), block),
        scratch_shapes=[pltpu.VMEM((d, two_f), BF16), pltpu.VMEM((d_ff, d), BF16), pltpu.VMEM((MOE_BLOCK, d_ff), BF16),
                        pltpu.VMEM((2, d, two_f), F32), pltpu.VMEM((2, d_ff, d), F32),
                        pltpu.SemaphoreType.DMA((2, 2))],
    )
    return pl.pallas_call(
        functools.partial(_expert_kernel, layer),
        out_shape=jax.ShapeDtypeStruct((n_slots, half_d), jnp.uint32),
        grid_spec=grid_spec,
        compiler_params=_tc_params(1),
        name="experts",
    )(block_expert, block_valid, block_slot, block_next, n_active, xs, w1_all, b1.reshape(n_exp, 1, two_f), w2_all,
      b2.reshape(n_exp, 1, d))


def _combine(h, yg, gcol, ln_g, ln_b, dn_alpha):
    t, d = h.shape
    ts = SEQ_BLOCK
    row = lambda i: (i, 0)
    return pl.pallas_call(
        functools.partial(_combine_kernel, dn_alpha),
        out_shape=jax.ShapeDtypeStruct((t, d), F32),
        grid=(t // ts,),
        in_specs=[pl.BlockSpec((ts, d), row), pl.BlockSpec((TOP_K, ts, d // 2), lambda i: (0, i, 0)),
                  pl.BlockSpec((ts, V7X_LANES), row), _const_spec(ln_g.shape), _const_spec(ln_b.shape)],
        out_specs=pl.BlockSpec((ts, d), row),
        compiler_params=_tc_params(1),
        name="combine",
    )(h, yg.reshape(TOP_K, t, d // 2), gcol, ln_g, ln_b)


def _moe(mixed, w1_all, b1, w2_all, b2, layer, ln_g, ln_b, dn_alpha):
    h1, h1_packed, idx_t, rank_t, counts, gcol = mixed
    n_exp = w1_all.shape[1]
    d_ff = w2_all.shape[2]
    pos, block_meta = _route(idx_t, rank_t, counts[:, 0].astype(jnp.int32), n_exp)
    b1p = jnp.swapaxes(b1.reshape(n_exp, d_ff // V7X_LANES, V7X_LANES, 2), 2, 3).reshape(n_exp, 2 * d_ff)
    xs = _sc_scatter_rows(h1_packed, pos, block_meta[0].shape[0] * MOE_BLOCK)
    y = _experts(xs, *block_meta, w1_all, b1p, w2_all, b2, layer)
    yg = _sc_gather_rows(y, pos.reshape(-1))
    return _combine(h1, yg, gcol, ln_g, ln_b, dn_alpha)


def kernel(x, mem, ln1_g, ln1_b, ln2_g, ln2_b, a_w_in, pool_w, pool_scale, a_mem_kv, a_w_out, kv_w, fgate_b, b_w_in, b_mem_kv, b_w_out, router_w, router_b, moe_w1, moe_b1, moe_w2, moe_b2):
    n_batch, seq, d = x.shape
    t = n_batch * seq
    depth = ln1_g.shape[0]
    n_a = a_w_in.shape[0]
    dn_alpha = float((2 * depth) ** 0.25)
    n_exp = router_w.shape[2]
    fox_heads = fgate_b.shape[0]
    fox_width = fox_heads * HEAD_DIM
    assert seq % SEQ_BLOCK == 0 and seq % Q_BLOCK == 0 and fox_heads <= 16

    h = x.reshape(t, d)
    row = lambda a: a.reshape(1, -1)
    shared = None
    for l in range(depth):
        rwt = router_w[l].T
        rb = router_b[l].reshape(n_exp, 1)
        if l < n_a:
            wbd = jax.scipy.linalg.block_diag(*[pool_w[l, g] for g in range(pool_w.shape[1])]).astype(BF16)
            mixed = _pool_layer(
                h, mem, a_w_in[l].astype(BF16), wbd, row(pool_scale[l]), a_mem_kv[l].astype(BF16),
                a_w_out[l].astype(BF16), row(ln1_g[l]), row(ln1_b[l]), rwt, rb, dn_alpha, seq)
        else:
            j = l - n_a
            if shared is None:
                pad = V7X_LANES - fox_heads
                kvw = jnp.concatenate([_expand_heads(kv_w[:, :fox_width], fox_heads),
                                       kv_w[:, fox_width:2 * fox_width],
                                       jnp.pad(kv_w[:, 2 * fox_width:], ((0, 0), (0, pad)))], axis=1).astype(BF16)
                fb = jnp.pad(fgate_b, (0, pad)).reshape(1, V7X_LANES)
                shared = _shared_kv(h, kvw, fb, seq, fox_heads)
            k_aug, v_t, ccol = shared
            w_in = jnp.concatenate([_expand_heads(b_w_in[j][:, :fox_width], fox_heads),
                                    b_w_in[j][:, fox_width:]], axis=1).astype(BF16)
            mixed = _fox_layer(
                h, mem, k_aug, v_t, ccol, w_in, b_mem_kv[j].astype(BF16),
                b_w_out[j].astype(BF16), row(ln1_g[l]), row(ln1_b[l]), rwt, rb, dn_alpha, seq)
        h = _moe(mixed, moe_w1, moe_b1[l], moe_w2, moe_b2[l], l, row(ln2_g[l]), row(ln2_b[l]), dn_alpha)
    return h.reshape(n_batch, seq, d)
```

```python
import functools
import jax
import jax.numpy as jnp
from jax import lax
from jax.experimental import pallas as pl
from jax.experimental.pallas import tpu as pltpu
from jax.experimental.pallas import tpu_sc as plsc

HEAD_DIM = 64
MEM_HEADS = 4
MEM_WIDTH = MEM_HEADS * HEAD_DIM
POOL_WINDOWS = (2, 4, 8, 16)
TOP_K = 4
SWIGLU_ALPHA = 1.702
SWIGLU_LIMIT = 7.0
MOE_BLOCK = 512
LN_EPS = 1e-5
ATTN_SCALE = HEAD_DIM ** -0.5
LOG2_E = 1.4426950408889634

V7X_LANES = 128
V7X_VMEM_LIMIT_BYTES = 56 * 1024 * 1024
POOL_HISTORY = 32
SC_CHUNK = 64
SEQ_BLOCK = 512
Q_BLOCK = 512
SLOT_COLS = 8192
N_STREAMS = 2

F32 = jnp.float32
BF16 = jnp.bfloat16
NEG_INF = float("-inf")


def _dot(a, b):
    return jnp.dot(a, b, preferred_element_type=F32)


def _dot_nt(a, b):
    return lax.dot_general(a, b, (((1,), (1,)), ((), ())), preferred_element_type=F32)


def _lane_iota(shape):
    return lax.broadcasted_iota(jnp.int32, shape, len(shape) - 1)


def _pair_attention(q2, keys, vals, bias=None):
    lane = _lane_iota(q2.shape)
    first = lane < HEAD_DIM
    outs = []
    for sel in (first, jnp.logical_not(first)):
        qh = jnp.where(sel, q2, 0.0).astype(BF16)
        s = _dot_nt(qh, keys)
        m = jnp.max(s, axis=-1, keepdims=True)
        p = jnp.exp(s - m)
        l = jnp.sum(p, axis=-1, keepdims=True)
        outs.append(_dot(p.astype(BF16), vals) / l)
    return jnp.where(first, outs[0], outs[1])


def _memory_attention(proj, kv_ref, cat_ref, out_off, q_off):
    for p in range(MEM_HEADS // 2):
        lo = p * V7X_LANES
        q2 = proj[:, q_off + lo:q_off + lo + V7X_LANES] * ATTN_SCALE
        keys = kv_ref[:, lo:lo + V7X_LANES]
        vals = kv_ref[:, MEM_WIDTH + lo:MEM_WIDTH + lo + V7X_LANES]
        cat_ref[:, out_off + lo:out_off + lo + V7X_LANES] = _pair_attention(q2, keys, vals).astype(BF16)


def _layer_norm(z, g, b):
    mu = jnp.mean(z, axis=-1, keepdims=True)
    zc = z - mu
    var = jnp.mean(zc * zc, axis=-1, keepdims=True)
    return zc * lax.rsqrt(var + LN_EPS) * g + b


def _pack_bf16_pairs(x):
    w = x.shape[1] // 2
    bits = lax.bitcast_convert_type(x.astype(BF16).astype(F32), jnp.uint32)
    return lax.shift_right_logical(bits[:, :w], jnp.uint32(16)) | (bits[:, w:] & jnp.uint32(0xFFFF0000))


def _unpack_bf16_pairs(p):
    low = lax.bitcast_convert_type(lax.shift_left(p, jnp.uint32(16)), F32)
    high = lax.bitcast_convert_type(p & jnp.uint32(0xFFFF0000), F32)
    return low, high


def _expert_ranks(idxs, tri_ref, rank_ref, counts_ref, base_ref):
    @pl.when(jnp.logical_and(pl.program_id(0) == 0, pl.program_id(1) == 0))
    def _():
        base_ref[...] = jnp.zeros(base_ref.shape, F32)

    n_exp = base_ref.shape[0]
    t = idxs[0].shape[1]
    rows = lax.broadcasted_iota(jnp.int32, (n_exp, t), 0)
    running = base_ref[:, 0:1]
    for k in range(TOP_K):
        onehot = rows == idxs[k]
        prefix = _dot(jnp.where(onehot, 1.0, 0.0).astype(BF16), tri_ref[...])
        rank = jnp.sum(jnp.where(onehot, prefix + running, 0.0), axis=0, keepdims=True)
        rank_ref[k:k + 1, :] = rank.astype(jnp.int32)
        running = running + jnp.sum(jnp.where(onehot, 1.0, 0.0), axis=1, keepdims=True)
    base_ref[...] = jnp.broadcast_to(running, base_ref.shape)
    counts_ref[...] = base_ref[...]


def _mix_epilogue(x, cat_ref, w_out_ref, g_ref, b_ref, rwt_ref, rb_ref, tri_ref, dn_alpha, h_out_ref, hp_ref, idx_ref,
                  rank_ref, counts_ref, gcol_ref, base_ref):
    mix = _dot(cat_ref[...], w_out_ref[...])
    h1 = _layer_norm(dn_alpha * x + mix, g_ref[...], b_ref[...])
    h_out_ref[...] = h1
    hp_ref[...] = _pack_bf16_pairs(h1)
    logits = lax.dot_general(rwt_ref[...], h1, (((1,), (1,)), ((), ())),
                             precision=lax.Precision.HIGHEST, preferred_element_type=F32) + rb_ref[...]
    n_exp, t = logits.shape
    rows = lax.broadcasted_iota(jnp.int32, (n_exp, t), 0)
    vals, idxs = [], []
    for _ in range(TOP_K):
        m = jnp.max(logits, axis=0, keepdims=True)
        ix = jnp.min(jnp.where(logits == m, rows, n_exp), axis=0, keepdims=True)
        vals.append(m)
        idxs.append(ix)
        logits = jnp.where(rows == ix, NEG_INF, logits)
    exps = [jnp.exp(v - vals[0]) for v in vals]
    denom = exps[0] + exps[1] + exps[2] + exps[3]
    for k in range(TOP_K):
        idx_ref[k:k + 1, :] = idxs[k]
    _expert_ranks(idxs, tri_ref, rank_ref, counts_ref, base_ref)
    lane_rows = lax.broadcasted_iota(jnp.int32, (V7X_LANES, t), 0)
    gates_t = jnp.zeros((V7X_LANES, t), F32)
    for k in range(TOP_K):
        gates_t = jnp.where(lane_rows == k, exps[k] / denom, gates_t)
    gcol_ref[...] = gates_t.T


def _pool_layer_kernel(dn_alpha, h_ref, mem_ref, w_in_ref, wbd_ref, pscale_ref, mem_kv_ref, w_out_ref, g_ref, b_ref,
                       rwt_ref, rb_ref, tri_ref, h_out_ref, hp_ref, idx_ref, rank_ref, counts_ref, gcol_ref,
                       ue_ref, w2_ref, w4_ref, w8_ref, kv_ref, cat_ref, base_ref):
    s = pl.program_id(1)
    ts = h_ref.shape[0]
    pool_width = wbd_ref.shape[0]
    hist = POOL_HISTORY

    @pl.when(s == 0)
    def _():
        ue_ref[0:hist, :] = jnp.zeros((hist, pool_width), F32)
        kv_ref[...] = _dot(mem_ref[0].astype(BF16), mem_kv_ref[...]).astype(BF16)

    x = h_ref[...]
    proj = _dot(x.astype(BF16), w_in_ref[...])
    u = proj[:, :pool_width]
    ue_ref[hist:, :] = u
    n = ts + hist
    w2_ref[8:n, :] = ue_ref[8:n, :] + ue_ref[7:n - 1, :]
    w4_ref[16:n, :] = w2_ref[16:n, :] + w2_ref[14:n - 2, :]
    w8_ref[24:n, :] = w4_ref[24:n, :] + w4_ref[20:n - 4, :]
    w16 = w8_ref[hist:n, :] + w8_ref[hist - 8:n - 8, :]
    group_dim = pool_width // len(POOL_WINDOWS)
    col = _lane_iota((ts, pool_width))
    wsum = jnp.where(col < group_dim, w2_ref[hist:n, :],
                     jnp.where(col < 2 * group_dim, w4_ref[hist:n, :],
                               jnp.where(col < 3 * group_dim, w8_ref[hist:n, :], w16)))
    window = jnp.where(col < group_dim, POOL_WINDOWS[0],
                       jnp.where(col < 2 * group_dim, POOL_WINDOWS[1],
                                 jnp.where(col < 3 * group_dim, POOL_WINDOWS[2], POOL_WINDOWS[3])))
    tpos = s * ts + lax.broadcasted_iota(jnp.int32, (ts, pool_width), 0) + 1
    cnt = jnp.minimum(tpos, window).astype(F32)
    d = wsum / cnt - u
    ue_ref[0:hist, :] = u[ts - hist:, :]
    pooled = _dot(d.astype(BF16), wbd_ref[...]) * pscale_ref[...]
    cat_ref[:, :pool_width] = pooled.astype(BF16)
    _memory_attention(proj, kv_ref, cat_ref, pool_width, pool_width)
    _mix_epilogue(x, cat_ref, w_out_ref, g_ref, b_ref, rwt_ref, rb_ref, tri_ref, dn_alpha, h_out_ref, hp_ref, idx_ref,
                  rank_ref, counts_ref, gcol_ref, base_ref)


def _bias_pieces(c):
    hi = c.astype(BF16).astype(F32)
    r = c - hi
    mid = r.astype(BF16).astype(F32)
    lo = (r - mid).astype(BF16).astype(F32)
    return hi, mid, lo


def _augment_head(x, c_col, is_query):
    lane = _lane_iota(x.shape)
    hi, mid, lo = _bias_pieces(jnp.broadcast_to(c_col, x.shape))
    b = HEAD_DIM
    if is_query:
        aug = jnp.where(lane == b, hi, jnp.where(lane == b + 1, mid, jnp.where(lane == b + 2, lo,
                        jnp.where(lane < b + 6, 1.0, 0.0))))
    else:
        aug = jnp.where(lane == b + 3, -hi, jnp.where(lane == b + 4, -mid, jnp.where(lane == b + 5, -lo,
                        jnp.where(lane < b + 3, 1.0, 0.0))))
    return jnp.where(lane < b, x, aug).astype(BF16)


def _shared_kv_kernel(h_ref, kvw_ref, fb_ref, tri_ref, k_ref, vt_ref, ccol_ref, carry_ref):
    s = pl.program_id(1)
    k_width = k_ref.shape[1]
    v_width = vt_ref.shape[1]

    @pl.when(s == 0)
    def _():
        carry_ref[...] = jnp.zeros(carry_ref.shape, F32)

    proj = _dot(h_ref[...].astype(BF16), kvw_ref[...])
    z = proj[:, k_width + v_width:] + fb_ref[...]
    logf = jnp.minimum(z, 0.0) - jnp.log(1.0 + jnp.exp(-jnp.abs(z)))
    hi, mid, lo = _bias_pieces(logf)
    tri = tri_ref[...]
    c = _dot(tri, hi.astype(BF16)) + _dot(tri, mid.astype(BF16)) + _dot(tri, lo.astype(BF16)) + carry_ref[0:1, :]
    ts = c.shape[0]
    carry_ref[0:1, :] = c[ts - 1:ts, :]
    ccol_ref[...] = c
    c_log2 = c * LOG2_E
    for h in range(k_width // V7X_LANES):
        cols = slice(h * V7X_LANES, (h + 1) * V7X_LANES)
        k_ref[:, cols] = _augment_head(proj[:, cols], c_log2[:, h:h + 1], False)
    vt_ref[0] = proj[:, k_width:k_width + v_width].T.astype(BF16)


def _fox_layer_kernel(dn_alpha, h_ref, mem_ref, k_ref, vt_ref, ccol_ref, w_in_ref, mem_kv_ref, w_out_ref,
                      g_ref, b_ref, rwt_ref, rb_ref, tri_ref, h_out_ref, hp_ref, idx_ref, rank_ref, counts_ref, gcol_ref,
                      kv_ref, cat_ref, qa_ref, m_ref, l_ref, acc_ref, base_ref):
    i = pl.program_id(1)
    tq = h_ref.shape[0]
    n_heads = k_ref.shape[2] // V7X_LANES
    fox_width = vt_ref.shape[1]

    @pl.when(i == 0)
    def _():
        kv_ref[...] = _dot(mem_ref[0].astype(BF16), mem_kv_ref[...]).astype(BF16)

    x = h_ref[...]
    proj = _dot(x.astype(BF16), w_in_ref[...])
    c_t = ccol_ref[...] * LOG2_E
    for h in range(n_heads):
        cols = slice(h * V7X_LANES, (h + 1) * V7X_LANES)
        qa_ref[h] = _augment_head(proj[:, cols] * (ATTN_SCALE * LOG2_E), c_t[:, h:h + 1], True)
    m_ref[...] = jnp.full(m_ref.shape, NEG_INF, F32)
    l_ref[...] = jnp.zeros(l_ref.shape, F32)
    acc_ref[...] = jnp.zeros(acc_ref.shape, F32)
    causal_t = lax.broadcasted_iota(jnp.int32, (tq, tq), 0) <= lax.broadcasted_iota(jnp.int32, (tq, tq), 1)

    ones_rows = (lax.broadcasted_iota(jnp.int32, (16, tq), 0) == 0).astype(BF16)

    def tile(j, masked):
        start = pl.multiple_of(j * tq, tq)

        def scores(h):
            kb = k_ref[0, pl.ds(start, tq), h * V7X_LANES:(h + 1) * V7X_LANES]
            return _dot_nt(kb, qa_ref[h])

        st_next = scores(0)
        for h in range(n_heads):
            st = st_next
            if h + 1 < n_heads:
                st_next = scores(h + 1)
            if masked:
                st = jnp.where(causal_t, st, NEG_INF)
            m_old = m_ref[h:h + 1, :]
            m_new = jnp.maximum(m_old, jnp.max(st, axis=0, keepdims=True))
            alpha = jnp.exp2(m_old - m_new)
            pt = jnp.exp2(st - m_new).astype(BF16)
            m_ref[h:h + 1, :] = m_new
            rows = slice(h * HEAD_DIM, (h + 1) * HEAD_DIM)
            vb = jnp.concatenate([vt_ref[0, rows, pl.ds(start, tq)], ones_rows], axis=0)
            pv = _dot(vb, pt)
            acc_ref[rows, :] = alpha * acc_ref[rows, :] + pv[:HEAD_DIM]
            l_ref[h:h + 1, :] = alpha * l_ref[h:h + 1, :] + pv[HEAD_DIM:HEAD_DIM + 1]

    tile(i, True)

    def off_diagonal(j, carry):
        tile(j, False)
        return carry

    lax.fori_loop(0, i, off_diagonal, 0)
    for h in range(n_heads):
        rows = slice(h * HEAD_DIM, (h + 1) * HEAD_DIM)
        acc_ref[rows, :] = acc_ref[rows, :] / l_ref[h:h + 1, :]
    cat_ref[:, :fox_width] = acc_ref[...].T.astype(BF16)
    _memory_attention(proj, kv_ref, cat_ref, fox_width, n_heads * V7X_LANES)
    _mix_epilogue(x, cat_ref, w_out_ref, g_ref, b_ref, rwt_ref, rb_ref, tri_ref, dn_alpha, h_out_ref, hp_ref, idx_ref,
                  rank_ref, counts_ref, gcol_ref, base_ref)


def _expert_kernel(layer, be_ref, bv_ref, bslot_ref, bnext_ref, nact_ref, x_ref, w1_hbm, b1_ref, w2_hbm, b2_ref, y_ref,
                   w1s_ref, w2s_ref, act_ref, w1f_ref, w2f_ref, sem_ref):
    i = pl.program_id(0)
    active = i < nact_ref[0]
    two_f = w1f_ref.shape[2]
    group = 2 * V7X_LANES
    n_groups = two_f // group

    def weight_copies(expert, slot):
        return (pltpu.make_async_copy(w1_hbm.at[layer, expert], w1f_ref.at[slot], sem_ref.at[0, slot]),
                pltpu.make_async_copy(w2_hbm.at[layer, expert], w2f_ref.at[slot], sem_ref.at[1, slot]))

    @pl.when(i == 0)
    def _():
        for cp in weight_copies(be_ref[0], bslot_ref[0]):
            cp.start()

    @pl.when(jnp.logical_and(active, jnp.logical_or(i == 0, be_ref[i] != be_ref[jnp.maximum(i - 1, 0)])))
    def _():
        slot = bslot_ref[i]
        for cp in weight_copies(be_ref[i], slot):
            cp.wait()

        @pl.when(bnext_ref[i] >= 0)
        def _():
            for cp in weight_copies(bnext_ref[i], 1 - slot):
                cp.start()

        src = lax.broadcasted_iota(jnp.int32, (group, group), 0)
        dst = lax.broadcasted_iota(jnp.int32, (group, group), 1)
        want = jnp.where(dst < V7X_LANES, 2 * dst, 2 * (dst - V7X_LANES) + 1)
        perm = (src == want).astype(BF16)
        for g in range(n_groups):
            cols = slice(g * group, (g + 1) * group)
            w1s_ref[:, cols] = _dot(w1f_ref[slot, :, cols].astype(BF16), perm).astype(BF16)
        w2s_ref[...] = w2f_ref[slot].astype(BF16)

    @pl.when(active)
    def _():
        row_id = lax.broadcasted_iota(jnp.int32, x_ref.shape, 0)
        x_low, x_high = _unpack_bf16_pairs(jnp.where(row_id < bv_ref[i], x_ref[...], jnp.uint32(0)))
        xb = jnp.concatenate([x_low.astype(BF16), x_high.astype(BF16)], axis=1)
        hdn = _dot(xb, w1s_ref[...]) + b1_ref[0]
        for g in range(n_groups):
            x_glu = jnp.minimum(hdn[:, g * group:g * group + V7X_LANES], SWIGLU_LIMIT)
            x_lin = jnp.clip(hdn[:, g * group + V7X_LANES:(g + 1) * group], -SWIGLU_LIMIT, SWIGLU_LIMIT)
            act = x_glu * jax.nn.sigmoid(SWIGLU_ALPHA * x_glu) * (x_lin + 1.0)
            act_ref[:, g * V7X_LANES:(g + 1) * V7X_LANES] = act.astype(BF16)
        y_ref[...] = _pack_bf16_pairs(_dot(act_ref[...], w2s_ref[...]) + b2_ref[0])

    @pl.when(jnp.logical_not(active))
    def _():
        y_ref[...] = jnp.zeros(y_ref.shape, jnp.uint32)


def _combine_kernel(dn_alpha, h_ref, y_ref, gcol_ref, g_ref, b_ref, o_ref):
    gates = gcol_ref[...]
    low, high = _unpack_bf16_pairs(y_ref[0])
    low, high = gates[:, 0:1] * low, gates[:, 0:1] * high
    for k in range(1, TOP_K):
        low_k, high_k = _unpack_bf16_pairs(y_ref[k])
        low = low + gates[:, k:k + 1] * low_k
        high = high + gates[:, k:k + 1] * high_k
    ffn = jnp.concatenate([low, high], axis=1)
    o_ref[...] = _layer_norm(dn_alpha * h_ref[...] + ffn, g_ref[...], b_ref[...])


def _sc_gather_rows(table, idx):
    n, = idx.shape
    _, d = table.shape
    info = plsc.get_sparse_core_info()
    n_workers = info.num_cores * info.num_subcores
    per_worker = n // n_workers
    n_chunks = per_worker // SC_CHUNK
    assert n % (n_workers * SC_CHUNK) == 0 and n_chunks % 2 == 0
    mesh = plsc.VectorSubcoreMesh(core_axis_name="core", subcore_axis_name="subcore")

    @functools.partial(
        pl.kernel,
        out_type=jax.ShapeDtypeStruct((n, d), table.dtype),
        mesh=mesh,
        scratch_types=[
            pltpu.VMEM((per_worker,), jnp.int32),
            pltpu.VMEM((2, SC_CHUNK, d), table.dtype),
            pltpu.SemaphoreType.DMA((2,)),
            pltpu.SemaphoreType.DMA((2,)),
        ],
    )
    def gather_kernel(table_hbm, idx_hbm, out_hbm, idx_v, rows_v, gather_sem, write_sem):
        worker = lax.axis_index("subcore") * info.num_cores + lax.axis_index("core")
        base = pl.multiple_of(worker * per_worker, SC_CHUNK)
        pltpu.sync_copy(idx_hbm.at[pl.ds(base, per_worker)], idx_v)

        def gather(c, buf):
            rows = idx_v.at[pl.ds(pl.multiple_of(c * SC_CHUNK, SC_CHUNK), SC_CHUNK)]
            return pltpu.make_async_copy(table_hbm.at[rows], rows_v.at[buf], gather_sem.at[buf])

        def write(c, buf):
            off = pl.multiple_of(base + c * SC_CHUNK, SC_CHUNK)
            return pltpu.make_async_copy(rows_v.at[buf], out_hbm.at[pl.ds(off, SC_CHUNK)], write_sem.at[buf])

        gather(0, 0).start()

        @pl.loop(0, n_chunks, step=2)
        def _(c0):
            for buf in range(2):
                c = c0 + buf
                gather(c, buf).wait()

                @pl.when(c >= 1)
                def _():
                    write(c - 1, 1 - buf).wait()

                @pl.when(c + 1 < n_chunks)
                def _():
                    gather(c + 1, 1 - buf).start()

                write(c, buf).start()

        write(n_chunks - 1, 1).wait()

    return gather_kernel(table, idx)


def _sc_scatter_rows(rows, pos, n_out):
    t, d = rows.shape
    top_k = pos.shape[0]
    pos_flat = pos.reshape(-1)
    info = plsc.get_sparse_core_info()
    n_workers = info.num_cores * info.num_subcores
    per_worker = t // n_workers
    n_chunks = per_worker // SC_CHUNK
    assert t % (n_workers * SC_CHUNK) == 0 and n_chunks % 2 == 0
    mesh = plsc.VectorSubcoreMesh(core_axis_name="core", subcore_axis_name="subcore")

    @functools.partial(
        pl.kernel,
        out_type=jax.ShapeDtypeStruct((n_out, d), rows.dtype),
        mesh=mesh,
        scratch_types=[pltpu.VMEM((2, SC_CHUNK, d), rows.dtype)]
        + [pltpu.VMEM((SC_CHUNK,), jnp.int32) for _ in range(2 * top_k)]
        + [pltpu.SemaphoreType.DMA((2,)), pltpu.SemaphoreType.DMA((2,))],
    )
    def scatter_kernel(rows_hbm, pos_hbm, out_hbm, rows_v, *rest):
        idx_bufs = rest[:2 * top_k]
        load_sem, write_sem = rest[2 * top_k:]
        worker = lax.axis_index("subcore") * info.num_cores + lax.axis_index("core")
        base = pl.multiple_of(worker * per_worker, SC_CHUNK)

        def loads(c, buf):
            off = pl.multiple_of(base + c * SC_CHUNK, SC_CHUNK)
            copies = [pltpu.make_async_copy(rows_hbm.at[pl.ds(off, SC_CHUNK)], rows_v.at[buf], load_sem.at[buf])]
            for k in range(top_k):
                src = pos_hbm.at[pl.ds(pl.multiple_of(k * t + off, SC_CHUNK), SC_CHUNK)]
                copies.append(pltpu.make_async_copy(src, idx_bufs[buf * top_k + k], load_sem.at[buf]))
            return copies

        def scatters(buf):
            return [pltpu.make_async_copy(rows_v.at[buf], out_hbm.at[idx_bufs[buf * top_k + k]], write_sem.at[buf])
                    for k in range(top_k)]

        for cp in loads(0, 0):
            cp.start()

        @pl.loop(0, n_chunks, step=2)
        def _(c0):
            for buf in range(2):
                c = c0 + buf
                for cp in loads(c, buf):
                    cp.wait()

                @pl.when(c >= 1)
                def _():
                    for cp in scatters(1 - buf):
                        cp.wait()

                @pl.when(c + 1 < n_chunks)
                def _():
                    for cp in loads(c + 1, 1 - buf):
                        cp.start()

                for cp in scatters(buf):
                    cp.start()

        for cp in scatters(1):
            cp.wait()

    return scatter_kernel(rows, pos_flat)


def _const_spec(shape):
    zeros = (0,) * len(shape)
    return pl.BlockSpec(shape, lambda *_: zeros, pipeline_mode=pl.Buffered(1))


def _tc_params(n_axes):
    return pltpu.CompilerParams(dimension_semantics=("arbitrary",) * n_axes,
                                vmem_limit_bytes=V7X_VMEM_LIMIT_BYTES)


def _mix_out_shapes(t, d, n_exp):
    return (jax.ShapeDtypeStruct((t, d), F32),
            jax.ShapeDtypeStruct((t, d // 2), jnp.uint32),
            jax.ShapeDtypeStruct((TOP_K, t), jnp.int32),
            jax.ShapeDtypeStruct((TOP_K, t), jnp.int32),
            jax.ShapeDtypeStruct((n_exp, V7X_LANES), F32),
            jax.ShapeDtypeStruct((t, V7X_LANES), F32))


def _mix_out_specs(rows, d, n_exp, row, tok_col):
    return (pl.BlockSpec((rows, d), row),
            pl.BlockSpec((rows, d // 2), row),
            pl.BlockSpec((TOP_K, rows), tok_col),
            pl.BlockSpec((TOP_K, rows), tok_col),
            pl.BlockSpec((n_exp, V7X_LANES), lambda *_: (0, 0)),
            pl.BlockSpec((rows, V7X_LANES), row))


def _strict_upper_tri(n):
    return (lax.broadcasted_iota(jnp.int32, (n, n), 0) < lax.broadcasted_iota(jnp.int32, (n, n), 1)).astype(BF16)


def _pool_layer(h, mem, w_in, wbd, pscale, mem_kv, w_out, ln_g, ln_b, rwt, rb, dn_alpha, seq):
    t, d = h.shape
    n_mem = mem.shape[1]
    pool_width = wbd.shape[0]
    ts = SEQ_BLOCK
    nsb = seq // ts
    row = lambda b, s: (b * nsb + s, 0)
    n_exp = rwt.shape[0]
    tri = _strict_upper_tri(ts)
    return pl.pallas_call(
        functools.partial(_pool_layer_kernel, dn_alpha),
        out_shape=_mix_out_shapes(t, d, n_exp),
        grid=(t // seq, nsb),
        in_specs=[
            pl.BlockSpec((ts, d), row),
            pl.BlockSpec((1, n_mem, d), lambda b, s: (b, 0, 0)),
            _const_spec(w_in.shape), _const_spec(wbd.shape), _const_spec(pscale.shape), _const_spec(mem_kv.shape),
            _const_spec(w_out.shape), _const_spec(ln_g.shape), _const_spec(ln_b.shape), _const_spec(rwt.shape),
            _const_spec(rb.shape), _const_spec(tri.shape),
        ],
        out_specs=_mix_out_specs(ts, d, n_exp, row, lambda b, s: (0, b * nsb + s)),
        scratch_shapes=[pltpu.VMEM((ts + POOL_HISTORY, pool_width), F32)] * 4
        + [pltpu.VMEM((n_mem, 2 * MEM_WIDTH), BF16), pltpu.VMEM((ts, d), BF16), pltpu.VMEM((n_exp, V7X_LANES), F32)],
        compiler_params=_tc_params(2),
        name="pool_layer",
    )(h, mem, w_in, wbd, pscale, mem_kv, w_out, ln_g, ln_b, rwt, rb, tri)


def _expand_heads(w, n_heads):
    d = w.shape[0]
    w3 = w.reshape(d, n_heads, HEAD_DIM)
    return jnp.pad(w3, ((0, 0), (0, 0), (0, V7X_LANES - HEAD_DIM))).reshape(d, n_heads * V7X_LANES)


def _shared_kv(h, kvw, fb, seq, n_heads):
    t, d = h.shape
    ts = SEQ_BLOCK
    nsb = seq // ts
    k_width = n_heads * V7X_LANES
    v_width = n_heads * HEAD_DIM
    tri = (lax.broadcasted_iota(jnp.int32, (ts, ts), 1) <= lax.broadcasted_iota(jnp.int32, (ts, ts), 0)).astype(BF16)
    row = lambda b, s: (b * nsb + s, 0)
    return pl.pallas_call(
        _shared_kv_kernel,
        out_shape=(jax.ShapeDtypeStruct((t, k_width), BF16),
                   jax.ShapeDtypeStruct((t // seq, v_width, seq), BF16),
                   jax.ShapeDtypeStruct((t, V7X_LANES), F32)),
        grid=(t // seq, nsb),
        in_specs=[pl.BlockSpec((ts, d), row), _const_spec(kvw.shape), _const_spec(fb.shape), _const_spec(tri.shape)],
        out_specs=(pl.BlockSpec((ts, k_width), row),
                   pl.BlockSpec((1, v_width, ts), lambda b, s: (b, 0, s)),
                   pl.BlockSpec((ts, V7X_LANES), row)),
        scratch_shapes=[pltpu.VMEM((8, V7X_LANES), F32)],
        compiler_params=_tc_params(2),
        name="shared_kv",
    )(h, kvw, fb, tri)


def _fox_layer(h, mem, k_aug, v_t, ccol, w_in, mem_kv, w_out, ln_g, ln_b, rwt, rb, dn_alpha, seq):
    t, d = h.shape
    n_mem = mem.shape[1]
    k_width = k_aug.shape[1]
    v_width = v_t.shape[1]
    n_heads = k_width // V7X_LANES
    tq = Q_BLOCK
    nqb = seq // tq
    n_batch = t // seq
    row = lambda b, i: (b * nqb + i, 0)
    per_batch = lambda b, i: (b, 0, 0)
    n_exp = rwt.shape[0]
    tri = _strict_upper_tri(tq)
    return pl.pallas_call(
        functools.partial(_fox_layer_kernel, dn_alpha),
        out_shape=_mix_out_shapes(t, d, n_exp),
        grid=(n_batch, nqb),
        in_specs=[
            pl.BlockSpec((tq, d), row),
            pl.BlockSpec((1, n_mem, d), per_batch),
            pl.BlockSpec((1, seq, k_width), per_batch),
            pl.BlockSpec((1, v_width, seq), per_batch),
            pl.BlockSpec((tq, V7X_LANES), row),
            _const_spec(w_in.shape), _const_spec(mem_kv.shape), _const_spec(w_out.shape), _const_spec(ln_g.shape),
            _const_spec(ln_b.shape), _const_spec(rwt.shape), _const_spec(rb.shape), _const_spec(tri.shape),
        ],
        out_specs=_mix_out_specs(tq, d, n_exp, row, lambda b, i: (0, b * nqb + i)),
        scratch_shapes=[pltpu.VMEM((n_mem, 2 * MEM_WIDTH), BF16), pltpu.VMEM((tq, d), BF16),
                        pltpu.VMEM((n_heads, tq, V7X_LANES), BF16),
                        pltpu.VMEM((16, tq), F32), pltpu.VMEM((16, tq), F32), pltpu.VMEM((v_width, tq), F32),
                        pltpu.VMEM((n_exp, V7X_LANES), F32)],
        compiler_params=_tc_params(2),
        name="fox_layer",
    )(h, mem, k_aug.reshape(n_batch, seq, k_width), v_t, ccol, w_in, mem_kv, w_out, ln_g, ln_b, rwt, rb, tri)


def _slot_kernel(pad_start_ref, idx_ref, rank_ref, pos_ref):
    pos = rank_ref[...]
    idx = idx_ref[...]
    for e in range(pad_start_ref.shape[0]):
        pos = pos + jnp.where(idx == e, pad_start_ref[e], 0)
    pos_ref[...] = pos


def _route(idx_t, rank_t, counts, n_experts):
    top_k, t = idx_t.shape
    n_assign = top_k * t
    padded = ((counts + MOE_BLOCK - 1) // MOE_BLOCK) * MOE_BLOCK
    pad_end = jnp.cumsum(padded)
    pad_start = pad_end - padded
    n_blocks = n_assign // MOE_BLOCK + n_experts
    block_start = jnp.arange(n_blocks, dtype=jnp.int32) * MOE_BLOCK
    block_expert = jnp.minimum(jnp.sum((pad_end[None, :] <= block_start[:, None]).astype(jnp.int32), axis=1),
                               n_experts - 1)
    block_valid = jnp.clip(counts[block_expert] - (block_start - pad_start[block_expert]), 0, MOE_BLOCK)
    n_active = (pad_end[-1:] // MOE_BLOCK).astype(jnp.int32)
    expert_ids = jnp.arange(n_experts, dtype=jnp.int32)
    in_use = counts > 0
    expert_slot = (jnp.cumsum(in_use.astype(jnp.int32)) - in_use.astype(jnp.int32)) % 2
    later = lax.cummin(jnp.where(in_use, expert_ids, n_experts), axis=0, reverse=True)
    next_in_use = jnp.concatenate([later[1:], jnp.full((1,), n_experts, jnp.int32)])
    next_in_use = jnp.where(next_in_use < n_experts, next_in_use, -1)
    block_slot = expert_slot[block_expert].astype(jnp.int32)
    block_next = next_in_use[block_expert].astype(jnp.int32)
    cols = min(SLOT_COLS, t)
    pos = pl.pallas_call(
        _slot_kernel,
        out_shape=jax.ShapeDtypeStruct((top_k, t), jnp.int32),
        grid_spec=pltpu.PrefetchScalarGridSpec(
            num_scalar_prefetch=1,
            grid=(t // cols,),
            in_specs=[pl.BlockSpec((top_k, cols), lambda i, ps: (0, i))] * 2,
            out_specs=pl.BlockSpec((top_k, cols), lambda i, ps: (0, i)),
        ),
        compiler_params=_tc_params(1),
        name="slots",
    )(pad_start.astype(jnp.int32), idx_t, rank_t)
    return pos, (block_expert, block_valid.astype(jnp.int32), block_slot, block_next, n_active)


def _experts(xs, block_expert, block_valid, block_slot, block_next, n_active, w1_all, b1, w2_all, b2, layer):
    n_slots, half_d = xs.shape
    _, n_exp, d, two_f = w1_all.shape
    d_ff = two_f // 2
    assert d == 2 * half_d
    block = lambda i, *_: (i, 0)
    per_expert = lambda i, be, *_: (be[i], 0, 0)
    grid_spec = pltpu.PrefetchScalarGridSpec(
        num_scalar_prefetch=5,
        grid=(n_slots // MOE_BLOCK,),
        in_specs=[
            pl.BlockSpec((MOE_BLOCK, half_d), block),
            pl.BlockSpec(memory_space=pl.ANY),
            pl.BlockSpec((1, 1, two_f), per_expert),
            pl.BlockSpec(memory_space=pl.ANY),
            pl.BlockSpec((1, 1, d), per_expert),
        ],
        out_specs=pl.BlockSpec((MOE_BLOCK, half_d), block),
        scratch_shapes=[pltpu.VMEM((d, two_f), BF16), pltpu.VMEM((d_ff, d), BF16), pltpu.VMEM((MOE_BLOCK, d_ff), BF16),
                        pltpu.VMEM((2, d, two_f), F32), pltpu.VMEM((2, d_ff, d), F32),
                        pltpu.SemaphoreType.DMA((2, 2))],
    )
    return pl.pallas_call(
        functools.partial(_expert_kernel, layer),
        out_shape=jax.ShapeDtypeStruct((n_slots, half_d), jnp.uint32),
        grid_spec=grid_spec,
        compiler_params=_tc_params(1),
        name="experts",
    )(block_expert, block_valid, block_slot, block_next, n_active, xs, w1_all, b1.reshape(n_exp, 1, two_f), w2_all,
      b2.reshape(n_exp, 1, d))


def _combine(h, yg, gcol, ln_g, ln_b, dn_alpha):
    t, d = h.shape
    ts = SEQ_BLOCK
    row = lambda i: (i, 0)
    return pl.pallas_call(
        functools.partial(_combine_kernel, dn_alpha),
        out_shape=jax.ShapeDtypeStruct((t, d), F32),
        grid=(t // ts,),
        in_specs=[pl.BlockSpec((ts, d), row), pl.BlockSpec((TOP_K, ts, d // 2), lambda i: (0, i, 0)),
                  pl.BlockSpec((ts, V7X_LANES), row), _const_spec(ln_g.shape), _const_spec(ln_b.shape)],
        out_specs=pl.BlockSpec((ts, d), row),
        compiler_params=_tc_params(1),
        name="combine",
    )(h, yg.reshape(TOP_K, t, d // 2), gcol, ln_g, ln_b)


def _moe(mixed, w1_all, b1, w2_all, b2, layer, ln_g, ln_b, dn_alpha):
    h1, h1_packed, idx_t, rank_t, counts, gcol = mixed
    n_exp = w1_all.shape[1]
    d_ff = w2_all.shape[2]
    pos, block_meta = _route(idx_t, rank_t, counts[:, 0].astype(jnp.int32), n_exp)
    b1p = jnp.swapaxes(b1.reshape(n_exp, d_ff // V7X_LANES, V7X_LANES, 2), 2, 3).reshape(n_exp, 2 * d_ff)
    xs = _sc_scatter_rows(h1_packed, pos, block_meta[0].shape[0] * MOE_BLOCK)
    y = _experts(xs, *block_meta, w1_all, b1p, w2_all, b2, layer)
    yg = _sc_gather_rows(y, pos.reshape(-1))
    return _combine(h1, yg, gcol, ln_g, ln_b, dn_alpha)


def kernel(x, mem, ln1_g, ln1_b, ln2_g, ln2_b, a_w_in, pool_w, pool_scale, a_mem_kv, a_w_out, kv_w, fgate_b, b_w_in, b_mem_kv, b_w_out, router_w, router_b, moe_w1, moe_b1, moe_w2, moe_b2):
    n_batch, seq, d = x.shape
    t = n_batch * seq
    depth = ln1_g.shape[0]
    n_a = a_w_in.shape[0]
    dn_alpha = float((2 * depth) ** 0.25)
    n_exp = router_w.shape[2]
    fox_heads = fgate_b.shape[0]
    fox_width = fox_heads * HEAD_DIM
    assert seq % SEQ_BLOCK == 0 and seq % Q_BLOCK == 0 and fox_heads <= 16

    assert n_batch % N_STREAMS == 0
    per_stream = n_batch // N_STREAMS
    hs = [x[s * per_stream:(s + 1) * per_stream].reshape(per_stream * seq, d) for s in range(N_STREAMS)]
    mems = [mem[s * per_stream:(s + 1) * per_stream] for s in range(N_STREAMS)]
    row = lambda a: a.reshape(1, -1)
    shared = [None] * N_STREAMS
    for l in range(depth):
        rwt = router_w[l].T
        rb = router_b[l].reshape(n_exp, 1)
        mixed = []
        for s in range(N_STREAMS):
            if l < n_a:
                wbd = jax.scipy.linalg.block_diag(*[pool_w[l, g] for g in range(pool_w.shape[1])]).astype(BF16)
                mixed.append(_pool_layer(
                    hs[s], mems[s], a_w_in[l].astype(BF16), wbd, row(pool_scale[l]), a_mem_kv[l].astype(BF16),
                    a_w_out[l].astype(BF16), row(ln1_g[l]), row(ln1_b[l]), rwt, rb, dn_alpha, seq))
            else:
                j = l - n_a
                if shared[s] is None:
                    pad = V7X_LANES - fox_heads
                    kvw = jnp.concatenate([_expand_heads(kv_w[:, :fox_width], fox_heads),
                                           kv_w[:, fox_width:2 * fox_width],
                                           jnp.pad(kv_w[:, 2 * fox_width:], ((0, 0), (0, pad)))], axis=1).astype(BF16)
                    fb = jnp.pad(fgate_b, (0, pad)).reshape(1, V7X_LANES)
                    shared[s] = _shared_kv(hs[s], kvw, fb, seq, fox_heads)
                k_aug, v_t, ccol = shared[s]
                w_in = jnp.concatenate([_expand_heads(b_w_in[j][:, :fox_width], fox_heads),
                                        b_w_in[j][:, fox_width:]], axis=1).astype(BF16)
                mixed.append(_fox_layer(
                    hs[s], mems[s], k_aug, v_t, ccol, w_in, b_mem_kv[j].astype(BF16),
                    b_w_out[j].astype(BF16), row(ln1_g[l]), row(ln1_b[l]), rwt, rb, dn_alpha, seq))
        hs = [_moe(mixed[s], moe_w1, moe_b1[l], moe_w2, moe_b2[l], l, row(ln2_g[l]), row(ln2_b[l]), dn_alpha)
              for s in range(N_STREAMS)]
    return jnp.concatenate(hs, axis=0).reshape(n_batch, seq, d)
```

```python
import functools
import jax
import jax.numpy as jnp
from jax import lax
from jax.experimental import pallas as pl
from jax.experimental.pallas import tpu as pltpu
from jax.experimental.pallas import tpu_sc as plsc

HEAD_DIM = 64
MEM_HEADS = 4
MEM_WIDTH = MEM_HEADS * HEAD_DIM
POOL_WINDOWS = (2, 4, 8, 16)
TOP_K = 4
SWIGLU_ALPHA = 1.702
SWIGLU_LIMIT = 7.0
MOE_BLOCK = 1024
EXPERT_ROWS = 512
LN_EPS = 1e-5
ATTN_SCALE = HEAD_DIM ** -0.5
LOG2_E = 1.4426950408889634

V7X_LANES = 128
V7X_VMEM_LIMIT_BYTES = 56 * 1024 * 1024
POOL_HISTORY = 32
SC_CHUNK = 64
SEQ_BLOCK = 512
Q_BLOCK = 512
SLOT_COLS = 8192

F32 = jnp.float32
BF16 = jnp.bfloat16
NEG_INF = float("-inf")


def _dot(a, b):
    return jnp.dot(a, b, preferred_element_type=F32)


def _dot_nt(a, b):
    return lax.dot_general(a, b, (((1,), (1,)), ((), ())), preferred_element_type=F32)


def _lane_iota(shape):
    return lax.broadcasted_iota(jnp.int32, shape, len(shape) - 1)


def _pair_attention(q2, keys, vals, bias=None):
    lane = _lane_iota(q2.shape)
    first = lane < HEAD_DIM
    outs = []
    for sel in (first, jnp.logical_not(first)):
        qh = jnp.where(sel, q2, 0.0).astype(BF16)
        s = _dot_nt(qh, keys)
        m = jnp.max(s, axis=-1, keepdims=True)
        p = jnp.exp(s - m)
        l = jnp.sum(p, axis=-1, keepdims=True)
        outs.append(_dot(p.astype(BF16), vals) / l)
    return jnp.where(first, outs[0], outs[1])


def _memory_attention(proj, kv_ref, cat_ref, out_off, q_off):
    for p in range(MEM_HEADS // 2):
        lo = p * V7X_LANES
        q2 = proj[:, q_off + lo:q_off + lo + V7X_LANES] * ATTN_SCALE
        keys = kv_ref[:, lo:lo + V7X_LANES]
        vals = kv_ref[:, MEM_WIDTH + lo:MEM_WIDTH + lo + V7X_LANES]
        cat_ref[:, out_off + lo:out_off + lo + V7X_LANES] = _pair_attention(q2, keys, vals).astype(BF16)


def _layer_norm(z, g, b):
    mu = jnp.mean(z, axis=-1, keepdims=True)
    zc = z - mu
    var = jnp.mean(zc * zc, axis=-1, keepdims=True)
    return zc * lax.rsqrt(var + LN_EPS) * g + b


def _pack_bf16_pairs(x):
    w = x.shape[1] // 2
    bits = lax.bitcast_convert_type(x.astype(BF16).astype(F32), jnp.uint32)
    return lax.shift_right_logical(bits[:, :w], jnp.uint32(16)) | (bits[:, w:] & jnp.uint32(0xFFFF0000))


def _unpack_bf16_pairs(p):
    low = lax.bitcast_convert_type(lax.shift_left(p, jnp.uint32(16)), F32)
    high = lax.bitcast_convert_type(p & jnp.uint32(0xFFFF0000), F32)
    return low, high


def _expert_ranks(idxs, tri_ref, rank_ref, counts_ref, base_ref):
    @pl.when(jnp.logical_and(pl.program_id(0) == 0, pl.program_id(1) == 0))
    def _():
        base_ref[...] = jnp.zeros(base_ref.shape, F32)

    n_exp = base_ref.shape[0]
    t = idxs[0].shape[1]
    rows = lax.broadcasted_iota(jnp.int32, (n_exp, t), 0)
    running = base_ref[:, 0:1]
    for k in range(TOP_K):
        onehot = rows == idxs[k]
        prefix = _dot(jnp.where(onehot, 1.0, 0.0).astype(BF16), tri_ref[...])
        rank = jnp.sum(jnp.where(onehot, prefix + running, 0.0), axis=0, keepdims=True)
        rank_ref[k:k + 1, :] = rank.astype(jnp.int32)
        running = running + jnp.sum(jnp.where(onehot, 1.0, 0.0), axis=1, keepdims=True)
    base_ref[...] = jnp.broadcast_to(running, base_ref.shape)
    counts_ref[...] = base_ref[...]


def _mix_epilogue(x, cat_ref, w_out_ref, g_ref, b_ref, rwt_ref, rb_ref, tri_ref, dn_alpha, h_out_ref, hp_ref, idx_ref,
                  rank_ref, counts_ref, gcol_ref, base_ref):
    mix = _dot(cat_ref[...], w_out_ref[...])
    h1 = _layer_norm(dn_alpha * x + mix, g_ref[...], b_ref[...])
    h_out_ref[...] = h1
    hp_ref[...] = _pack_bf16_pairs(h1)
    logits = lax.dot_general(rwt_ref[...], h1, (((1,), (1,)), ((), ())),
                             precision=lax.Precision.HIGHEST, preferred_element_type=F32) + rb_ref[...]
    n_exp, t = logits.shape
    rows = lax.broadcasted_iota(jnp.int32, (n_exp, t), 0)
    vals, idxs = [], []
    for _ in range(TOP_K):
        m = jnp.max(logits, axis=0, keepdims=True)
        ix = jnp.min(jnp.where(logits == m, rows, n_exp), axis=0, keepdims=True)
        vals.append(m)
        idxs.append(ix)
        logits = jnp.where(rows == ix, NEG_INF, logits)
    exps = [jnp.exp(v - vals[0]) for v in vals]
    denom = exps[0] + exps[1] + exps[2] + exps[3]
    for k in range(TOP_K):
        idx_ref[k:k + 1, :] = idxs[k]
    _expert_ranks(idxs, tri_ref, rank_ref, counts_ref, base_ref)
    lane_rows = lax.broadcasted_iota(jnp.int32, (V7X_LANES, t), 0)
    gates_t = jnp.zeros((V7X_LANES, t), F32)
    for k in range(TOP_K):
        gates_t = jnp.where(lane_rows == k, exps[k] / denom, gates_t)
    gcol_ref[...] = gates_t.T


def _pool_layer_kernel(dn_alpha, h_ref, mem_ref, w_in_ref, wbd_ref, pscale_ref, mem_kv_ref, w_out_ref, g_ref, b_ref,
                       rwt_ref, rb_ref, tri_ref, h_out_ref, hp_ref, idx_ref, rank_ref, counts_ref, gcol_ref,
                       ue_ref, w2_ref, w4_ref, w8_ref, kv_ref, cat_ref, base_ref):
    s = pl.program_id(1)
    ts = h_ref.shape[0]
    pool_width = wbd_ref.shape[0]
    hist = POOL_HISTORY

    @pl.when(s == 0)
    def _():
        ue_ref[0:hist, :] = jnp.zeros((hist, pool_width), F32)
        kv_ref[...] = _dot(mem_ref[0].astype(BF16), mem_kv_ref[...]).astype(BF16)

    x = h_ref[...]
    proj = _dot(x.astype(BF16), w_in_ref[...])
    u = proj[:, :pool_width]
    ue_ref[hist:, :] = u
    n = ts + hist
    w2_ref[8:n, :] = ue_ref[8:n, :] + ue_ref[7:n - 1, :]
    w4_ref[16:n, :] = w2_ref[16:n, :] + w2_ref[14:n - 2, :]
    w8_ref[24:n, :] = w4_ref[24:n, :] + w4_ref[20:n - 4, :]
    w16 = w8_ref[hist:n, :] + w8_ref[hist - 8:n - 8, :]
    group_dim = pool_width // len(POOL_WINDOWS)
    col = _lane_iota((ts, pool_width))
    wsum = jnp.where(col < group_dim, w2_ref[hist:n, :],
                     jnp.where(col < 2 * group_dim, w4_ref[hist:n, :],
                               jnp.where(col < 3 * group_dim, w8_ref[hist:n, :], w16)))
    window = jnp.where(col < group_dim, POOL_WINDOWS[0],
                       jnp.where(col < 2 * group_dim, POOL_WINDOWS[1],
                                 jnp.where(col < 3 * group_dim, POOL_WINDOWS[2], POOL_WINDOWS[3])))
    tpos = s * ts + lax.broadcasted_iota(jnp.int32, (ts, pool_width), 0) + 1
    cnt = jnp.minimum(tpos, window).astype(F32)
    d = wsum / cnt - u
    ue_ref[0:hist, :] = u[ts - hist:, :]
    pooled = _dot(d.astype(BF16), wbd_ref[...]) * pscale_ref[...]
    cat_ref[:, :pool_width] = pooled.astype(BF16)
    _memory_attention(proj, kv_ref, cat_ref, pool_width, pool_width)
    _mix_epilogue(x, cat_ref, w_out_ref, g_ref, b_ref, rwt_ref, rb_ref, tri_ref, dn_alpha, h_out_ref, hp_ref, idx_ref,
                  rank_ref, counts_ref, gcol_ref, base_ref)


def _bias_pieces(c):
    hi = c.astype(BF16).astype(F32)
    r = c - hi
    mid = r.astype(BF16).astype(F32)
    lo = (r - mid).astype(BF16).astype(F32)
    return hi, mid, lo


def _augment_head(x, c_col, is_query):
    lane = _lane_iota(x.shape)
    hi, mid, lo = _bias_pieces(jnp.broadcast_to(c_col, x.shape))
    b = HEAD_DIM
    if is_query:
        aug = jnp.where(lane == b, hi, jnp.where(lane == b + 1, mid, jnp.where(lane == b + 2, lo,
                        jnp.where(lane < b + 6, 1.0, 0.0))))
    else:
        aug = jnp.where(lane == b + 3, -hi, jnp.where(lane == b + 4, -mid, jnp.where(lane == b + 5, -lo,
                        jnp.where(lane < b + 3, 1.0, 0.0))))
    return jnp.where(lane < b, x, aug).astype(BF16)


def _shared_kv_kernel(h_ref, kvw_ref, fb_ref, tri_ref, k_ref, vt_ref, ccol_ref, carry_ref):
    s = pl.program_id(1)
    k_width = k_ref.shape[1]
    v_width = vt_ref.shape[1]

    @pl.when(s == 0)
    def _():
        carry_ref[...] = jnp.zeros(carry_ref.shape, F32)

    proj = _dot(h_ref[...].astype(BF16), kvw_ref[...])
    z = proj[:, k_width + v_width:] + fb_ref[...]
    logf = jnp.minimum(z, 0.0) - jnp.log(1.0 + jnp.exp(-jnp.abs(z)))
    hi, mid, lo = _bias_pieces(logf)
    tri = tri_ref[...]
    c = _dot(tri, hi.astype(BF16)) + _dot(tri, mid.astype(BF16)) + _dot(tri, lo.astype(BF16)) + carry_ref[0:1, :]
    ts = c.shape[0]
    carry_ref[0:1, :] = c[ts - 1:ts, :]
    ccol_ref[...] = c
    c_log2 = c * LOG2_E
    for h in range(k_width // V7X_LANES):
        cols = slice(h * V7X_LANES, (h + 1) * V7X_LANES)
        k_ref[:, cols] = _augment_head(proj[:, cols], c_log2[:, h:h + 1], False)
    vt_ref[0] = proj[:, k_width:k_width + v_width].T.astype(BF16)


def _fox_layer_kernel(dn_alpha, h_ref, mem_ref, k_ref, vt_ref, ccol_ref, w_in_ref, mem_kv_ref, w_out_ref,
                      g_ref, b_ref, rwt_ref, rb_ref, tri_ref, h_out_ref, hp_ref, idx_ref, rank_ref, counts_ref, gcol_ref,
                      kv_ref, cat_ref, qa_ref, m_ref, l_ref, acc_ref, base_ref):
    i = pl.program_id(1)
    tq = h_ref.shape[0]
    n_heads = k_ref.shape[2] // V7X_LANES
    fox_width = vt_ref.shape[1]

    @pl.when(i == 0)
    def _():
        kv_ref[...] = _dot(mem_ref[0].astype(BF16), mem_kv_ref[...]).astype(BF16)

    x = h_ref[...]
    proj = _dot(x.astype(BF16), w_in_ref[...])
    c_t = ccol_ref[...] * LOG2_E
    for h in range(n_heads):
        cols = slice(h * V7X_LANES, (h + 1) * V7X_LANES)
        qa_ref[h] = _augment_head(proj[:, cols] * (ATTN_SCALE * LOG2_E), c_t[:, h:h + 1], True)
    m_ref[...] = jnp.full(m_ref.shape, NEG_INF, F32)
    l_ref[...] = jnp.zeros(l_ref.shape, F32)
    acc_ref[...] = jnp.zeros(acc_ref.shape, F32)
    causal_t = lax.broadcasted_iota(jnp.int32, (tq, tq), 0) <= lax.broadcasted_iota(jnp.int32, (tq, tq), 1)

    ones_rows = (lax.broadcasted_iota(jnp.int32, (16, tq), 0) == 0).astype(BF16)

    def tile(j, masked):
        start = pl.multiple_of(j * tq, tq)

        def scores(h):
            kb = k_ref[0, pl.ds(start, tq), h * V7X_LANES:(h + 1) * V7X_LANES]
            return _dot_nt(kb, qa_ref[h])

        st_next = scores(0)
        for h in range(n_heads):
            st = st_next
            if h + 1 < n_heads:
                st_next = scores(h + 1)
            if masked:
                st = jnp.where(causal_t, st, NEG_INF)
            m_old = m_ref[h:h + 1, :]
            m_new = jnp.maximum(m_old, jnp.max(st, axis=0, keepdims=True))
            alpha = jnp.exp2(m_old - m_new)
            pt = jnp.exp2(st - m_new).astype(BF16)
            m_ref[h:h + 1, :] = m_new
            rows = slice(h * HEAD_DIM, (h + 1) * HEAD_DIM)
            vb = jnp.concatenate([vt_ref[0, rows, pl.ds(start, tq)], ones_rows], axis=0)
            pv = _dot(vb, pt)
            acc_ref[rows, :] = alpha * acc_ref[rows, :] + pv[:HEAD_DIM]
            l_ref[h:h + 1, :] = alpha * l_ref[h:h + 1, :] + pv[HEAD_DIM:HEAD_DIM + 1]

    tile(i, True)

    def off_diagonal(j, carry):
        tile(j, False)
        return carry

    lax.fori_loop(0, i, off_diagonal, 0)
    for h in range(n_heads):
        rows = slice(h * HEAD_DIM, (h + 1) * HEAD_DIM)
        acc_ref[rows, :] = acc_ref[rows, :] / l_ref[h:h + 1, :]
    cat_ref[:, :fox_width] = acc_ref[...].T.astype(BF16)
    _memory_attention(proj, kv_ref, cat_ref, fox_width, n_heads * V7X_LANES)
    _mix_epilogue(x, cat_ref, w_out_ref, g_ref, b_ref, rwt_ref, rb_ref, tri_ref, dn_alpha, h_out_ref, hp_ref, idx_ref,
                  rank_ref, counts_ref, gcol_ref, base_ref)


def _expert_kernel(layer, be_ref, bv_ref, bslot_ref, bnext_ref, nact_ref, x_ref, w1_hbm, b1_ref, w2_hbm, b2_ref, y_ref,
                   w1s_ref, w2s_ref, act_ref, w1f_ref, w2f_ref, sem_ref):
    i = pl.program_id(0)
    active = i < nact_ref[0]
    two_f = w1f_ref.shape[2]
    group = 2 * V7X_LANES
    n_groups = two_f // group

    def weight_copies(expert, slot):
        return (pltpu.make_async_copy(w1_hbm.at[layer, expert], w1f_ref.at[slot], sem_ref.at[0, slot]),
                pltpu.make_async_copy(w2_hbm.at[layer, expert], w2f_ref.at[slot], sem_ref.at[1, slot]))

    @pl.when(i == 0)
    def _():
        for cp in weight_copies(be_ref[0], bslot_ref[0]):
            cp.start()

    @pl.when(jnp.logical_and(active, jnp.logical_or(i == 0, be_ref[i] != be_ref[jnp.maximum(i - 1, 0)])))
    def _():
        slot = bslot_ref[i]
        for cp in weight_copies(be_ref[i], slot):
            cp.wait()

        @pl.when(bnext_ref[i] >= 0)
        def _():
            for cp in weight_copies(bnext_ref[i], 1 - slot):
                cp.start()

        src = lax.broadcasted_iota(jnp.int32, (group, group), 0)
        dst = lax.broadcasted_iota(jnp.int32, (group, group), 1)
        want = jnp.where(dst < V7X_LANES, 2 * dst, 2 * (dst - V7X_LANES) + 1)
        perm = (src == want).astype(BF16)
        for g in range(n_groups):
            cols = slice(g * group, (g + 1) * group)
            w1s_ref[:, cols] = _dot(w1f_ref[slot, :, cols].astype(BF16), perm).astype(BF16)
        w2s_ref[...] = w2f_ref[slot].astype(BF16)

    sub = act_ref.shape[0]

    def sub_block(s, carry):
        rows = pl.ds(pl.multiple_of(s * sub, sub), sub)
        n_valid = bv_ref[i] - s * sub

        @pl.when(jnp.logical_and(active, n_valid > 0))
        def _():
            row_id = lax.broadcasted_iota(jnp.int32, (sub, x_ref.shape[1]), 0)
            x_low, x_high = _unpack_bf16_pairs(jnp.where(row_id < n_valid, x_ref[rows, :], jnp.uint32(0)))
            xb = jnp.concatenate([x_low.astype(BF16), x_high.astype(BF16)], axis=1)
            hdn = _dot(xb, w1s_ref[...]) + b1_ref[0]
            for g in range(n_groups):
                x_glu = jnp.minimum(hdn[:, g * group:g * group + V7X_LANES], SWIGLU_LIMIT)
                x_lin = jnp.clip(hdn[:, g * group + V7X_LANES:(g + 1) * group], -SWIGLU_LIMIT, SWIGLU_LIMIT)
                act = x_glu * jax.nn.sigmoid(SWIGLU_ALPHA * x_glu) * (x_lin + 1.0)
                act_ref[:, g * V7X_LANES:(g + 1) * V7X_LANES] = act.astype(BF16)
            y_ref[rows, :] = _pack_bf16_pairs(_dot(act_ref[...], w2s_ref[...]) + b2_ref[0])

        @pl.when(jnp.logical_not(jnp.logical_and(active, n_valid > 0)))
        def _():
            y_ref[rows, :] = jnp.zeros((sub, y_ref.shape[1]), jnp.uint32)

        return carry

    lax.fori_loop(0, x_ref.shape[0] // sub, sub_block, 0)


def _combine_kernel(dn_alpha, h_ref, y_ref, gcol_ref, g_ref, b_ref, o_ref):
    gates = gcol_ref[...]
    low, high = _unpack_bf16_pairs(y_ref[0])
    low, high = gates[:, 0:1] * low, gates[:, 0:1] * high
    for k in range(1, TOP_K):
        low_k, high_k = _unpack_bf16_pairs(y_ref[k])
        low = low + gates[:, k:k + 1] * low_k
        high = high + gates[:, k:k + 1] * high_k
    ffn = jnp.concatenate([low, high], axis=1)
    o_ref[...] = _layer_norm(dn_alpha * h_ref[...] + ffn, g_ref[...], b_ref[...])


def _sc_gather_rows(table, idx):
    n, = idx.shape
    _, d = table.shape
    info = plsc.get_sparse_core_info()
    n_workers = info.num_cores * info.num_subcores
    per_worker = n // n_workers
    n_chunks = per_worker // SC_CHUNK
    assert n % (n_workers * SC_CHUNK) == 0 and n_chunks % 2 == 0
    mesh = plsc.VectorSubcoreMesh(core_axis_name="core", subcore_axis_name="subcore")

    @functools.partial(
        pl.kernel,
        out_type=jax.ShapeDtypeStruct((n, d), table.dtype),
        mesh=mesh,
        scratch_types=[
            pltpu.VMEM((per_worker,), jnp.int32),
            pltpu.VMEM((2, SC_CHUNK, d), table.dtype),
            pltpu.SemaphoreType.DMA((2,)),
            pltpu.SemaphoreType.DMA((2,)),
        ],
    )
    def gather_kernel(table_hbm, idx_hbm, out_hbm, idx_v, rows_v, gather_sem, write_sem):
        worker = lax.axis_index("subcore") * info.num_cores + lax.axis_index("core")
        base = pl.multiple_of(worker * per_worker, SC_CHUNK)
        pltpu.sync_copy(idx_hbm.at[pl.ds(base, per_worker)], idx_v)

        def gather(c, buf):
            rows = idx_v.at[pl.ds(pl.multiple_of(c * SC_CHUNK, SC_CHUNK), SC_CHUNK)]
            return pltpu.make_async_copy(table_hbm.at[rows], rows_v.at[buf], gather_sem.at[buf])

        def write(c, buf):
            off = pl.multiple_of(base + c * SC_CHUNK, SC_CHUNK)
            return pltpu.make_async_copy(rows_v.at[buf], out_hbm.at[pl.ds(off, SC_CHUNK)], write_sem.at[buf])

        gather(0, 0).start()

        @pl.loop(0, n_chunks, step=2)
        def _(c0):
            for buf in range(2):
                c = c0 + buf
                gather(c, buf).wait()

                @pl.when(c >= 1)
                def _():
                    write(c - 1, 1 - buf).wait()

                @pl.when(c + 1 < n_chunks)
                def _():
                    gather(c + 1, 1 - buf).start()

                write(c, buf).start()

        write(n_chunks - 1, 1).wait()

    return gather_kernel(table, idx)


def _sc_scatter_rows(rows, pos, n_out):
    t, d = rows.shape
    top_k = pos.shape[0]
    pos_flat = pos.reshape(-1)
    info = plsc.get_sparse_core_info()
    n_workers = info.num_cores * info.num_subcores
    per_worker = t // n_workers
    n_chunks = per_worker // SC_CHUNK
    assert t % (n_workers * SC_CHUNK) == 0 and n_chunks % 2 == 0
    mesh = plsc.VectorSubcoreMesh(core_axis_name="core", subcore_axis_name="subcore")

    @functools.partial(
        pl.kernel,
        out_type=jax.ShapeDtypeStruct((n_out, d), rows.dtype),
        mesh=mesh,
        scratch_types=[pltpu.VMEM((2, SC_CHUNK, d), rows.dtype)]
        + [pltpu.VMEM((SC_CHUNK,), jnp.int32) for _ in range(2 * top_k)]
        + [pltpu.SemaphoreType.DMA((2,)), pltpu.SemaphoreType.DMA((2,))],
    )
    def scatter_kernel(rows_hbm, pos_hbm, out_hbm, rows_v, *rest):
        idx_bufs = rest[:2 * top_k]
        load_sem, write_sem = rest[2 * top_k:]
        worker = lax.axis_index("subcore") * info.num_cores + lax.axis_index("core")
        base = pl.multiple_of(worker * per_worker, SC_CHUNK)

        def loads(c, buf):
            off = pl.multiple_of(base + c * SC_CHUNK, SC_CHUNK)
            copies = [pltpu.make_async_copy(rows_hbm.at[pl.ds(off, SC_CHUNK)], rows_v.at[buf], load_sem.at[buf])]
            for k in range(top_k):
                src = pos_hbm.at[pl.ds(pl.multiple_of(k * t + off, SC_CHUNK), SC_CHUNK)]
                copies.append(pltpu.make_async_copy(src, idx_bufs[buf * top_k + k], load_sem.at[buf]))
            return copies

        def scatters(buf):
            return [pltpu.make_async_copy(rows_v.at[buf], out_hbm.at[idx_bufs[buf * top_k + k]], write_sem.at[buf])
                    for k in range(top_k)]

        for cp in loads(0, 0):
            cp.start()

        @pl.loop(0, n_chunks, step=2)
        def _(c0):
            for buf in range(2):
                c = c0 + buf
                for cp in loads(c, buf):
                    cp.wait()

                @pl.when(c >= 1)
                def _():
                    for cp in scatters(1 - buf):
                        cp.wait()

                @pl.when(c + 1 < n_chunks)
                def _():
                    for cp in loads(c + 1, 1 - buf):
                        cp.start()

                for cp in scatters(buf):
                    cp.start()

        for cp in scatters(1):
            cp.wait()

    return scatter_kernel(rows, pos_flat)


def _const_spec(shape):
    zeros = (0,) * len(shape)
    return pl.BlockSpec(shape, lambda *_: zeros, pipeline_mode=pl.Buffered(1))


def _tc_params(n_axes):
    return pltpu.CompilerParams(dimension_semantics=("arbitrary",) * n_axes,
                                vmem_limit_bytes=V7X_VMEM_LIMIT_BYTES)


def _mix_out_shapes(t, d, n_exp):
    return (jax.ShapeDtypeStruct((t, d), F32),
            jax.ShapeDtypeStruct((t, d // 2), jnp.uint32),
            jax.ShapeDtypeStruct((TOP_K, t), jnp.int32),
            jax.ShapeDtypeStruct((TOP_K, t), jnp.int32),
            jax.ShapeDtypeStruct((n_exp, V7X_LANES), F32),
            jax.ShapeDtypeStruct((t, V7X_LANES), F32))


def _mix_out_specs(rows, d, n_exp, row, tok_col):
    return (pl.BlockSpec((rows, d), row),
            pl.BlockSpec((rows, d // 2), row),
            pl.BlockSpec((TOP_K, rows), tok_col),
            pl.BlockSpec((TOP_K, rows), tok_col),
            pl.BlockSpec((n_exp, V7X_LANES), lambda *_: (0, 0)),
            pl.BlockSpec((rows, V7X_LANES), row))


def _strict_upper_tri(n):
    return (lax.broadcasted_iota(jnp.int32, (n, n), 0) < lax.broadcasted_iota(jnp.int32, (n, n), 1)).astype(BF16)


def _pool_layer(h, mem, w_in, wbd, pscale, mem_kv, w_out, ln_g, ln_b, rwt, rb, dn_alpha, seq):
    t, d = h.shape
    n_mem = mem.shape[1]
    pool_width = wbd.shape[0]
    ts = SEQ_BLOCK
    nsb = seq // ts
    row = lambda b, s: (b * nsb + s, 0)
    n_exp = rwt.shape[0]
    tri = _strict_upper_tri(ts)
    return pl.pallas_call(
        functools.partial(_pool_layer_kernel, dn_alpha),
        out_shape=_mix_out_shapes(t, d, n_exp),
        grid=(t // seq, nsb),
        in_specs=[
            pl.BlockSpec((ts, d), row),
            pl.BlockSpec((1, n_mem, d), lambda b, s: (b, 0, 0)),
            _const_spec(w_in.shape), _const_spec(wbd.shape), _const_spec(pscale.shape), _const_spec(mem_kv.shape),
            _const_spec(w_out.shape), _const_spec(ln_g.shape), _const_spec(ln_b.shape), _const_spec(rwt.shape),
            _const_spec(rb.shape), _const_spec(tri.shape),
        ],
        out_specs=_mix_out_specs(ts, d, n_exp, row, lambda b, s: (0, b * nsb + s)),
        scratch_shapes=[pltpu.VMEM((ts + POOL_HISTORY, pool_width), F32)] * 4
        + [pltpu.VMEM((n_mem, 2 * MEM_WIDTH), BF16), pltpu.VMEM((ts, d), BF16), pltpu.VMEM((n_exp, V7X_LANES), F32)],
        compiler_params=_tc_params(2),
        name="pool_layer",
    )(h, mem, w_in, wbd, pscale, mem_kv, w_out, ln_g, ln_b, rwt, rb, tri)


def _expand_heads(w, n_heads):
    d = w.shape[0]
    w3 = w.reshape(d, n_heads, HEAD_DIM)
    return jnp.pad(w3, ((0, 0), (0, 0), (0, V7X_LANES - HEAD_DIM))).reshape(d, n_heads * V7X_LANES)


def _shared_kv(h, kvw, fb, seq, n_heads):
    t, d = h.shape
    ts = SEQ_BLOCK
    nsb = seq // ts
    k_width = n_heads * V7X_LANES
    v_width = n_heads * HEAD_DIM
    tri = (lax.broadcasted_iota(jnp.int32, (ts, ts), 1) <= lax.broadcasted_iota(jnp.int32, (ts, ts), 0)).astype(BF16)
    row = lambda b, s: (b * nsb + s, 0)
    return pl.pallas_call(
        _shared_kv_kernel,
        out_shape=(jax.ShapeDtypeStruct((t, k_width), BF16),
                   jax.ShapeDtypeStruct((t // seq, v_width, seq), BF16),
                   jax.ShapeDtypeStruct((t, V7X_LANES), F32)),
        grid=(t // seq, nsb),
        in_specs=[pl.BlockSpec((ts, d), row), _const_spec(kvw.shape), _const_spec(fb.shape), _const_spec(tri.shape)],
        out_specs=(pl.BlockSpec((ts, k_width), row),
                   pl.BlockSpec((1, v_width, ts), lambda b, s: (b, 0, s)),
                   pl.BlockSpec((ts, V7X_LANES), row)),
        scratch_shapes=[pltpu.VMEM((8, V7X_LANES), F32)],
        compiler_params=_tc_params(2),
        name="shared_kv",
    )(h, kvw, fb, tri)


def _fox_layer(h, mem, k_aug, v_t, ccol, w_in, mem_kv, w_out, ln_g, ln_b, rwt, rb, dn_alpha, seq):
    t, d = h.shape
    n_mem = mem.shape[1]
    k_width = k_aug.shape[1]
    v_width = v_t.shape[1]
    n_heads = k_width // V7X_LANES
    tq = Q_BLOCK
    nqb = seq // tq
    n_batch = t // seq
    row = lambda b, i: (b * nqb + i, 0)
    per_batch = lambda b, i: (b, 0, 0)
    n_exp = rwt.shape[0]
    tri = _strict_upper_tri(tq)
    return pl.pallas_call(
        functools.partial(_fox_layer_kernel, dn_alpha),
        out_shape=_mix_out_shapes(t, d, n_exp),
        grid=(n_batch, nqb),
        in_specs=[
            pl.BlockSpec((tq, d), row),
            pl.BlockSpec((1, n_mem, d), per_batch),
            pl.BlockSpec((1, seq, k_width), per_batch),
            pl.BlockSpec((1, v_width, seq), per_batch),
            pl.BlockSpec((tq, V7X_LANES), row),
            _const_spec(w_in.shape), _const_spec(mem_kv.shape), _const_spec(w_out.shape), _const_spec(ln_g.shape),
            _const_spec(ln_b.shape), _const_spec(rwt.shape), _const_spec(rb.shape), _const_spec(tri.shape),
        ],
        out_specs=_mix_out_specs(tq, d, n_exp, row, lambda b, i: (0, b * nqb + i)),
        scratch_shapes=[pltpu.VMEM((n_mem, 2 * MEM_WIDTH), BF16), pltpu.VMEM((tq, d), BF16),
                        pltpu.VMEM((n_heads, tq, V7X_LANES), BF16),
                        pltpu.VMEM((16, tq), F32), pltpu.VMEM((16, tq), F32), pltpu.VMEM((v_width, tq), F32),
                        pltpu.VMEM((n_exp, V7X_LANES), F32)],
        compiler_params=_tc_params(2),
        name="fox_layer",
    )(h, mem, k_aug.reshape(n_batch, seq, k_width), v_t, ccol, w_in, mem_kv, w_out, ln_g, ln_b, rwt, rb, tri)


def _slot_kernel(pad_start_ref, idx_ref, rank_ref, pos_ref):
    pos = rank_ref[...]
    idx = idx_ref[...]
    for e in range(pad_start_ref.shape[0]):
        pos = pos + jnp.where(idx == e, pad_start_ref[e], 0)
    pos_ref[...] = pos


def _route(idx_t, rank_t, counts, n_experts):
    top_k, t = idx_t.shape
    n_assign = top_k * t
    padded = ((counts + MOE_BLOCK - 1) // MOE_BLOCK) * MOE_BLOCK
    pad_end = jnp.cumsum(padded)
    pad_start = pad_end - padded
    n_blocks = n_assign // MOE_BLOCK + n_experts
    block_start = jnp.arange(n_blocks, dtype=jnp.int32) * MOE_BLOCK
    block_expert = jnp.minimum(jnp.sum((pad_end[None, :] <= block_start[:, None]).astype(jnp.int32), axis=1),
                               n_experts - 1)
    block_valid = jnp.clip(counts[block_expert] - (block_start - pad_start[block_expert]), 0, MOE_BLOCK)
    n_active = (pad_end[-1:] // MOE_BLOCK).astype(jnp.int32)
    expert_ids = jnp.arange(n_experts, dtype=jnp.int32)
    in_use = counts > 0
    expert_slot = (jnp.cumsum(in_use.astype(jnp.int32)) - in_use.astype(jnp.int32)) % 2
    later = lax.cummin(jnp.where(in_use, expert_ids, n_experts), axis=0, reverse=True)
    next_in_use = jnp.concatenate([later[1:], jnp.full((1,), n_experts, jnp.int32)])
    next_in_use = jnp.where(next_in_use < n_experts, next_in_use, -1)
    block_slot = expert_slot[block_expert].astype(jnp.int32)
    block_next = next_in_use[block_expert].astype(jnp.int32)
    cols = min(SLOT_COLS, t)
    pos = pl.pallas_call(
        _slot_kernel,
        out_shape=jax.ShapeDtypeStruct((top_k, t), jnp.int32),
        grid_spec=pltpu.PrefetchScalarGridSpec(
            num_scalar_prefetch=1,
            grid=(t // cols,),
            in_specs=[pl.BlockSpec((top_k, cols), lambda i, ps: (0, i))] * 2,
            out_specs=pl.BlockSpec((top_k, cols), lambda i, ps: (0, i)),
        ),
        compiler_params=_tc_params(1),
        name="slots",
    )(pad_start.astype(jnp.int32), idx_t, rank_t)
    return pos, (block_expert, block_valid.astype(jnp.int32), block_slot, block_next, n_active)


def _experts(xs, block_expert, block_valid, block_slot, block_next, n_active, w1_all, b1, w2_all, b2, layer):
    n_slots, half_d = xs.shape
    _, n_exp, d, two_f = w1_all.shape
    d_ff = two_f // 2
    assert d == 2 * half_d
    block = lambda i, *_: (i, 0)
    per_expert = lambda i, be, *_: (be[i], 0, 0)
    grid_spec = pltpu.PrefetchScalarGridSpec(
        num_scalar_prefetch=5,
        grid=(n_slots // MOE_BLOCK,),
        in_specs=[
            pl.BlockSpec((MOE_BLOCK, half_d), block),
            pl.BlockSpec(memory_space=pl.ANY),
            pl.BlockSpec((1, 1, two_f), per_expert),
            pl.BlockSpec(memory_space=pl.ANY),
            pl.BlockSpec((1, 1, d), per_expert),
        ],
        out_specs=pl.BlockSpec((MOE_BLOCK, half_d), block),
        scratch_shapes=[pltpu.VMEM((d, two_f), BF16), pltpu.VMEM((d_ff, d), BF16), pltpu.VMEM((EXPERT_ROWS, d_ff), BF16),
                        pltpu.VMEM((2, d, two_f), F32), pltpu.VMEM((2, d_ff, d), F32),
                        pltpu.SemaphoreType.DMA((2, 2))],
    )
    return pl.pallas_call(
        functools.partial(_expert_kernel, layer),
        out_shape=jax.ShapeDtypeStruct((n_slots, half_d), jnp.uint32),
        grid_spec=grid_spec,
        compiler_params=_tc_params(1),
        name="experts",
    )(block_expert, block_valid, block_slot, block_next, n_active, xs, w1_all, b1.reshape(n_exp, 1, two_f), w2_all,
      b2.reshape(n_exp, 1, d))


def _combine(h, yg, gcol, ln_g, ln_b, dn_alpha):
    t, d = h.shape
    ts = SEQ_BLOCK
    row = lambda i: (i, 0)
    return pl.pallas_call(
        functools.partial(_combine_kernel, dn_alpha),
        out_shape=jax.ShapeDtypeStruct((t, d), F32),
        grid=(t // ts,),
        in_specs=[pl.BlockSpec((ts, d), row), pl.BlockSpec((TOP_K, ts, d // 2), lambda i: (0, i, 0)),
                  pl.BlockSpec((ts, V7X_LANES), row), _const_spec(ln_g.shape), _const_spec(ln_b.shape)],
        out_specs=pl.BlockSpec((ts, d), row),
        compiler_params=_tc_params(1),
        name="combine",
    )(h, yg.reshape(TOP_K, t, d // 2), gcol, ln_g, ln_b)


def _moe(mixed, w1_all, b1, w2_all, b2, layer, ln_g, ln_b, dn_alpha):
    h1, h1_packed, idx_t, rank_t, counts, gcol = mixed
    n_exp = w1_all.shape[1]
    d_ff = w2_all.shape[2]
    pos, block_meta = _route(idx_t, rank_t, counts[:, 0].astype(jnp.int32), n_exp)
    b1p = jnp.swapaxes(b1.reshape(n_exp, d_ff // V7X_LANES, V7X_LANES, 2), 2, 3).reshape(n_exp, 2 * d_ff)
    xs = _sc_scatter_rows(h1_packed, pos, block_meta[0].shape[0] * MOE_BLOCK)
    y = _experts(xs, *block_meta, w1_all, b1p, w2_all, b2, layer)
    yg = _sc_gather_rows(y, pos.reshape(-1))
    return _combine(h1, yg, gcol, ln_g, ln_b, dn_alpha)


def kernel(x, mem, ln1_g, ln1_b, ln2_g, ln2_b, a_w_in, pool_w, pool_scale, a_mem_kv, a_w_out, kv_w, fgate_b, b_w_in, b_mem_kv, b_w_out, router_w, router_b, moe_w1, moe_b1, moe_w2, moe_b2):
    n_batch, seq, d = x.shape
    t = n_batch * seq
    depth = ln1_g.shape[0]
    n_a = a_w_in.shape[0]
    dn_alpha = float((2 * depth) ** 0.25)
    n_exp = router_w.shape[2]
    fox_heads = fgate_b.shape[0]
    fox_width = fox_heads * HEAD_DIM
    assert seq % SEQ_BLOCK == 0 and seq % Q_BLOCK == 0 and fox_heads <= 16

    h = x.reshape(t, d)
    row = lambda a: a.reshape(1, -1)
    shared = None
    for l in range(depth):
        rwt = router_w[l].T
        rb = router_b[l].reshape(n_exp, 1)
        if l < n_a:
            wbd = jax.scipy.linalg.block_diag(*[pool_w[l, g] for g in range(pool_w.shape[1])]).astype(BF16)
            mixed = _pool_layer(
                h, mem, a_w_in[l].astype(BF16), wbd, row(pool_scale[l]), a_mem_kv[l].astype(BF16),
                a_w_out[l].astype(BF16), row(ln1_g[l]), row(ln1_b[l]), rwt, rb, dn_alpha, seq)
        else:
            j = l - n_a
            if shared is None:
                pad = V7X_LANES - fox_heads
                kvw = jnp.concatenate([_expand_heads(kv_w[:, :fox_width], fox_heads),
                                       kv_w[:, fox_width:2 * fox_width],
                                       jnp.pad(kv_w[:, 2 * fox_width:], ((0, 0), (0, pad)))], axis=1).astype(BF16)
                fb = jnp.pad(fgate_b, (0, pad)).reshape(1, V7X_LANES)
                shared = _shared_kv(h, kvw, fb, seq, fox_heads)
            k_aug, v_t, ccol = shared
            w_in = jnp.concatenate([_expand_heads(b_w_in[j][:, :fox_width], fox_heads),
                                    b_w_in[j][:, fox_width:]], axis=1).astype(BF16)
            mixed = _fox_layer(
                h, mem, k_aug, v_t, ccol, w_in, b_mem_kv[j].astype(BF16),
                b_w_out[j].astype(BF16), row(ln1_g[l]), row(ln1_b[l]), rwt, rb, dn_alpha, seq)
        h = _moe(mixed, moe_w1, moe_b1[l], moe_w2, moe_b2[l], l, row(ln2_g[l]), row(ln2_b[l]), dn_alpha)
    return h.reshape(n_batch, seq, d)
```

```python
import functools
import jax
import jax.numpy as jnp
from jax import lax
from jax.experimental import pallas as pl
from jax.experimental.pallas import tpu as pltpu
from jax.experimental.pallas import tpu_sc as plsc

HEAD_DIM = 64
MEM_HEADS = 4
MEM_WIDTH = MEM_HEADS * HEAD_DIM
POOL_WINDOWS = (2, 4, 8, 16)
TOP_K = 4
SWIGLU_ALPHA = 1.702
SWIGLU_LIMIT = 7.0
MOE_BLOCK = 512
LN_EPS = 1e-5
ATTN_SCALE = HEAD_DIM ** -0.5
LOG2_E = 1.4426950408889634

V7X_LANES = 128
V7X_VMEM_LIMIT_BYTES = 56 * 1024 * 1024
POOL_HISTORY = 32
SC_CHUNK = 64
SEQ_BLOCK = 512
Q_BLOCK = 512
SLOT_COLS = 8192

F32 = jnp.float32
BF16 = jnp.bfloat16
NEG_INF = float("-inf")


def _dot(a, b):
    return jnp.dot(a, b, preferred_element_type=F32)


def _dot_nt(a, b):
    return lax.dot_general(a, b, (((1,), (1,)), ((), ())), preferred_element_type=F32)


def _lane_iota(shape):
    return lax.broadcasted_iota(jnp.int32, shape, len(shape) - 1)


def _pair_attention(q2, keys, vals, bias=None):
    lane = _lane_iota(q2.shape)
    first = lane < HEAD_DIM
    outs = []
    for sel in (first, jnp.logical_not(first)):
        qh = jnp.where(sel, q2, 0.0).astype(BF16)
        s = _dot_nt(qh, keys)
        m = jnp.max(s, axis=-1, keepdims=True)
        p = jnp.exp(s - m)
        l = jnp.sum(p, axis=-1, keepdims=True)
        outs.append(_dot(p.astype(BF16), vals) / l)
    return jnp.where(first, outs[0], outs[1])


def _memory_attention(proj, kv_ref, cat_ref, out_off, q_off):
    for p in range(MEM_HEADS // 2):
        lo = p * V7X_LANES
        q2 = proj[:, q_off + lo:q_off + lo + V7X_LANES] * ATTN_SCALE
        keys = kv_ref[:, lo:lo + V7X_LANES]
        vals = kv_ref[:, MEM_WIDTH + lo:MEM_WIDTH + lo + V7X_LANES]
        cat_ref[:, out_off + lo:out_off + lo + V7X_LANES] = _pair_attention(q2, keys, vals).astype(BF16)


def _layer_norm(z, g, b):
    mu = jnp.mean(z, axis=-1, keepdims=True)
    zc = z - mu
    var = jnp.mean(zc * zc, axis=-1, keepdims=True)
    return zc * lax.rsqrt(var + LN_EPS) * g + b


def _pack_bf16_pairs(x):
    w = x.shape[1] // 2
    bits = lax.bitcast_convert_type(x.astype(BF16).astype(F32), jnp.uint32)
    return lax.shift_right_logical(bits[:, :w], jnp.uint32(16)) | (bits[:, w:] & jnp.uint32(0xFFFF0000))


def _unpack_bf16_pairs(p):
    low = lax.bitcast_convert_type(lax.shift_left(p, jnp.uint32(16)), F32)
    high = lax.bitcast_convert_type(p & jnp.uint32(0xFFFF0000), F32)
    return low, high


def _expert_ranks(idxs, tri_ref, rank_ref, counts_ref, base_ref):
    @pl.when(jnp.logical_and(pl.program_id(0) == 0, pl.program_id(1) == 0))
    def _():
        base_ref[...] = jnp.zeros(base_ref.shape, F32)

    n_exp = base_ref.shape[0]
    t = idxs[0].shape[1]
    rows = lax.broadcasted_iota(jnp.int32, (n_exp, t), 0)
    running = base_ref[:, 0:1]
    for k in range(TOP_K):
        onehot = rows == idxs[k]
        prefix = _dot(jnp.where(onehot, 1.0, 0.0).astype(BF16), tri_ref[...])
        rank = jnp.sum(jnp.where(onehot, prefix + running, 0.0), axis=0, keepdims=True)
        rank_ref[k:k + 1, :] = rank.astype(jnp.int32)
        running = running + jnp.sum(jnp.where(onehot, 1.0, 0.0), axis=1, keepdims=True)
    base_ref[...] = jnp.broadcast_to(running, base_ref.shape)
    counts_ref[...] = base_ref[...]


def _mix_epilogue(x, cat_ref, w_out_ref, g_ref, b_ref, rwt_ref, rb_ref, tri_ref, dn_alpha, h_out_ref, hp_ref, idx_ref,
                  rank_ref, counts_ref, gcol_ref, base_ref):
    mix = _dot(cat_ref[...], w_out_ref[...])
    h1 = _layer_norm(dn_alpha * x + mix, g_ref[...], b_ref[...])
    h_out_ref[...] = h1
    hp_ref[...] = _pack_bf16_pairs(h1)
    logits = lax.dot_general(rwt_ref[...], h1, (((1,), (1,)), ((), ())),
                             precision=lax.Precision.HIGHEST, preferred_element_type=F32) + rb_ref[...]
    n_exp, t = logits.shape
    rows = lax.broadcasted_iota(jnp.int32, (n_exp, t), 0)
    vals, idxs = [], []
    for _ in range(TOP_K):
        m = jnp.max(logits, axis=0, keepdims=True)
        ix = jnp.min(jnp.where(logits == m, rows, n_exp), axis=0, keepdims=True)
        vals.append(m)
        idxs.append(ix)
        logits = jnp.where(rows == ix, NEG_INF, logits)
    exps = [jnp.exp(v - vals[0]) for v in vals]
    denom = exps[0] + exps[1] + exps[2] + exps[3]
    for k in range(TOP_K):
        idx_ref[k:k + 1, :] = idxs[k]
    _expert_ranks(idxs, tri_ref, rank_ref, counts_ref, base_ref)
    lane_rows = lax.broadcasted_iota(jnp.int32, (V7X_LANES, t), 0)
    gates_t = jnp.zeros((V7X_LANES, t), F32)
    for k in range(TOP_K):
        gates_t = jnp.where(lane_rows == k, exps[k] / denom, gates_t)
    gcol_ref[...] = gates_t.T


def _pool_layer_kernel(dn_alpha, h_ref, mem_ref, w_in_ref, wbd_ref, pscale_ref, mem_kv_ref, w_out_ref, g_ref, b_ref,
                       rwt_ref, rb_ref, tri_ref, h_out_ref, hp_ref, idx_ref, rank_ref, counts_ref, gcol_ref,
                       ue_ref, w2_ref, w4_ref, w8_ref, kv_ref, cat_ref, base_ref):
    s = pl.program_id(1)
    ts = h_ref.shape[0]
    pool_width = wbd_ref.shape[0]
    hist = POOL_HISTORY

    @pl.when(s == 0)
    def _():
        ue_ref[0:hist, :] = jnp.zeros((hist, pool_width), F32)
        kv_ref[...] = _dot(mem_ref[0].astype(BF16), mem_kv_ref[...]).astype(BF16)

    x = h_ref[...]
    proj = _dot(x.astype(BF16), w_in_ref[...])
    u = proj[:, :pool_width]
    ue_ref[hist:, :] = u
    n = ts + hist
    w2_ref[8:n, :] = ue_ref[8:n, :] + ue_ref[7:n - 1, :]
    w4_ref[16:n, :] = w2_ref[16:n, :] + w2_ref[14:n - 2, :]
    w8_ref[24:n, :] = w4_ref[24:n, :] + w4_ref[20:n - 4, :]
    w16 = w8_ref[hist:n, :] + w8_ref[hist - 8:n - 8, :]
    group_dim = pool_width // len(POOL_WINDOWS)
    col = _lane_iota((ts, pool_width))
    wsum = jnp.where(col < group_dim, w2_ref[hist:n, :],
                     jnp.where(col < 2 * group_dim, w4_ref[hist:n, :],
                               jnp.where(col < 3 * group_dim, w8_ref[hist:n, :], w16)))
    window = jnp.where(col < group_dim, POOL_WINDOWS[0],
                       jnp.where(col < 2 * group_dim, POOL_WINDOWS[1],
                                 jnp.where(col < 3 * group_dim, POOL_WINDOWS[2], POOL_WINDOWS[3])))
    tpos = s * ts + lax.broadcasted_iota(jnp.int32, (ts, pool_width), 0) + 1
    cnt = jnp.minimum(tpos, window).astype(F32)
    d = wsum / cnt - u
    ue_ref[0:hist, :] = u[ts - hist:, :]
    pooled = _dot(d.astype(BF16), wbd_ref[...]) * pscale_ref[...]
    cat_ref[:, :pool_width] = pooled.astype(BF16)
    _memory_attention(proj, kv_ref, cat_ref, pool_width, pool_width)
    _mix_epilogue(x, cat_ref, w_out_ref, g_ref, b_ref, rwt_ref, rb_ref, tri_ref, dn_alpha, h_out_ref, hp_ref, idx_ref,
                  rank_ref, counts_ref, gcol_ref, base_ref)


def _bias_pieces(c):
    hi = c.astype(BF16).astype(F32)
    r = c - hi
    mid = r.astype(BF16).astype(F32)
    lo = (r - mid).astype(BF16).astype(F32)
    return hi, mid, lo


def _augment_head(x, c_col, is_query):
    lane = _lane_iota(x.shape)
    hi, mid, lo = _bias_pieces(jnp.broadcast_to(c_col, x.shape))
    b = HEAD_DIM
    if is_query:
        aug = jnp.where(lane == b, hi, jnp.where(lane == b + 1, mid, jnp.where(lane == b + 2, lo,
                        jnp.where(lane < b + 6, 1.0, 0.0))))
    else:
        aug = jnp.where(lane == b + 3, -hi, jnp.where(lane == b + 4, -mid, jnp.where(lane == b + 5, -lo,
                        jnp.where(lane < b + 3, 1.0, 0.0))))
    return jnp.where(lane < b, x, aug).astype(BF16)


def _shared_kv_kernel(h_ref, kvw_ref, fb_ref, tri_ref, k_ref, vt_ref, ccol_ref, carry_ref):
    s = pl.program_id(1)
    k_width = k_ref.shape[1]
    v_width = vt_ref.shape[1]

    @pl.when(s == 0)
    def _():
        carry_ref[...] = jnp.zeros(carry_ref.shape, F32)

    proj = _dot(h_ref[...].astype(BF16), kvw_ref[...])
    z = proj[:, k_width + v_width:] + fb_ref[...]
    logf = jnp.minimum(z, 0.0) - jnp.log(1.0 + jnp.exp(-jnp.abs(z)))
    hi, mid, lo = _bias_pieces(logf)
    tri = tri_ref[...]
    c = _dot(tri, hi.astype(BF16)) + _dot(tri, mid.astype(BF16)) + _dot(tri, lo.astype(BF16)) + carry_ref[0:1, :]
    ts = c.shape[0]
    carry_ref[0:1, :] = c[ts - 1:ts, :]
    ccol_ref[...] = c
    c_log2 = c * LOG2_E
    for h in range(k_width // V7X_LANES):
        cols = slice(h * V7X_LANES, (h + 1) * V7X_LANES)
        k_ref[:, cols] = _augment_head(proj[:, cols], c_log2[:, h:h + 1], False)
    vt_ref[0] = proj[:, k_width:k_width + v_width].T.astype(BF16)


def _fox_layer_kernel(dn_alpha, h_ref, mem_ref, k_ref, vt_ref, ccol_ref, w_in_ref, mem_kv_ref, w_out_ref,
                      g_ref, b_ref, rwt_ref, rb_ref, tri_ref, h_out_ref, hp_ref, idx_ref, rank_ref, counts_ref, gcol_ref,
                      kv_ref, cat_ref, qa_ref, m_ref, l_ref, acc_ref, base_ref):
    i = pl.program_id(1)
    tq = h_ref.shape[0]
    n_heads = k_ref.shape[2] // V7X_LANES
    fox_width = vt_ref.shape[1]

    @pl.when(i == 0)
    def _():
        kv_ref[...] = _dot(mem_ref[0].astype(BF16), mem_kv_ref[...]).astype(BF16)

    x = h_ref[...]
    proj = _dot(x.astype(BF16), w_in_ref[...])
    c_t = ccol_ref[...] * LOG2_E
    for h in range(n_heads):
        cols = slice(h * V7X_LANES, (h + 1) * V7X_LANES)
        qa_ref[h] = _augment_head(proj[:, cols] * (ATTN_SCALE * LOG2_E), c_t[:, h:h + 1], True)
    m_ref[...] = jnp.full(m_ref.shape, NEG_INF, F32)
    l_ref[...] = jnp.zeros(l_ref.shape, F32)
    acc_ref[...] = jnp.zeros(acc_ref.shape, F32)
    causal_t = lax.broadcasted_iota(jnp.int32, (tq, tq), 0) <= lax.broadcasted_iota(jnp.int32, (tq, tq), 1)

    ones_rows = (lax.broadcasted_iota(jnp.int32, (16, tq), 0) == 0).astype(BF16)

    def tile(j, masked):
        start = pl.multiple_of(j * tq, tq)

        def scores(h):
            kb = k_ref[0, pl.ds(start, tq), h * V7X_LANES:(h + 1) * V7X_LANES]
            return _dot_nt(kb, qa_ref[h])

        st_next = scores(0)
        for h in range(n_heads):
            st = st_next
            if h + 1 < n_heads:
                st_next = scores(h + 1)
            if masked:
                st = jnp.where(causal_t, st, NEG_INF)
            m_old = m_ref[h:h + 1, :]
            m_new = jnp.maximum(m_old, jnp.max(st, axis=0, keepdims=True))
            alpha = jnp.exp2(m_old - m_new)
            pt = jnp.exp2(st - m_new).astype(BF16)
            m_ref[h:h + 1, :] = m_new
            rows = slice(h * HEAD_DIM, (h + 1) * HEAD_DIM)
            vb = jnp.concatenate([vt_ref[0, rows, pl.ds(start, tq)], ones_rows], axis=0)
            pv = _dot(vb, pt)
            acc_ref[rows, :] = alpha * acc_ref[rows, :] + pv[:HEAD_DIM]
            l_ref[h:h + 1, :] = alpha * l_ref[h:h + 1, :] + pv[HEAD_DIM:HEAD_DIM + 1]

    tile(i, True)

    def off_diagonal(j, carry):
        tile(j, False)
        return carry

    lax.fori_loop(0, i, off_diagonal, 0)
    for h in range(n_heads):
        rows = slice(h * HEAD_DIM, (h + 1) * HEAD_DIM)
        acc_ref[rows, :] = acc_ref[rows, :] / l_ref[h:h + 1, :]
    cat_ref[:, :fox_width] = acc_ref[...].T.astype(BF16)
    _memory_attention(proj, kv_ref, cat_ref, fox_width, n_heads * V7X_LANES)
    _mix_epilogue(x, cat_ref, w_out_ref, g_ref, b_ref, rwt_ref, rb_ref, tri_ref, dn_alpha, h_out_ref, hp_ref, idx_ref,
                  rank_ref, counts_ref, gcol_ref, base_ref)


SWIGLU_GROUP = 2 * V7X_LANES


def _expert_weight_kernel(w1_ref, w2_ref, w1s_ref, w2s_ref):
    group = SWIGLU_GROUP
    src = lax.broadcasted_iota(jnp.int32, (group, group), 0)
    dst = lax.broadcasted_iota(jnp.int32, (group, group), 1)
    want = jnp.where(dst < V7X_LANES, 2 * dst, 2 * (dst - V7X_LANES) + 1)
    perm = (src == want).astype(BF16)
    for g in range(w1_ref.shape[2] // group):
        cols = slice(g * group, (g + 1) * group)
        w1s_ref[0, :, cols] = _dot(w1_ref[0, :, cols].astype(BF16), perm).astype(BF16)
    w2s_ref[0] = w2_ref[0].astype(BF16)


def _expert_kernel(be_ref, bv_ref, nact_ref, x_ref, w1s_ref, b1_ref, w2s_ref, b2_ref, y_ref, act_ref):
    i = pl.program_id(0)
    active = i < nact_ref[0]
    group = SWIGLU_GROUP
    n_groups = w1s_ref.shape[2] // group

    @pl.when(active)
    def _():
        row_id = lax.broadcasted_iota(jnp.int32, x_ref.shape, 0)
        x_low, x_high = _unpack_bf16_pairs(jnp.where(row_id < bv_ref[i], x_ref[...], jnp.uint32(0)))
        xb = jnp.concatenate([x_low.astype(BF16), x_high.astype(BF16)], axis=1)
        hdn = _dot(xb, w1s_ref[0]) + b1_ref[0]
        for g in range(n_groups):
            x_glu = jnp.minimum(hdn[:, g * group:g * group + V7X_LANES], SWIGLU_LIMIT)
            x_lin = jnp.clip(hdn[:, g * group + V7X_LANES:(g + 1) * group], -SWIGLU_LIMIT, SWIGLU_LIMIT)
            act = x_glu * jax.nn.sigmoid(SWIGLU_ALPHA * x_glu) * (x_lin + 1.0)
            act_ref[:, g * V7X_LANES:(g + 1) * V7X_LANES] = act.astype(BF16)
        y_ref[...] = _pack_bf16_pairs(_dot(act_ref[...], w2s_ref[0]) + b2_ref[0])

    @pl.when(jnp.logical_not(active))
    def _():
        y_ref[...] = jnp.zeros(y_ref.shape, jnp.uint32)


def _combine_kernel(dn_alpha, h_ref, y_ref, gcol_ref, g_ref, b_ref, o_ref):
    gates = gcol_ref[...]
    low, high = _unpack_bf16_pairs(y_ref[0])
    low, high = gates[:, 0:1] * low, gates[:, 0:1] * high
    for k in range(1, TOP_K):
        low_k, high_k = _unpack_bf16_pairs(y_ref[k])
        low = low + gates[:, k:k + 1] * low_k
        high = high + gates[:, k:k + 1] * high_k
    ffn = jnp.concatenate([low, high], axis=1)
    o_ref[...] = _layer_norm(dn_alpha * h_ref[...] + ffn, g_ref[...], b_ref[...])


def _sc_gather_rows(table, idx):
    n, = idx.shape
    _, d = table.shape
    info = plsc.get_sparse_core_info()
    n_workers = info.num_cores * info.num_subcores
    per_worker = n // n_workers
    n_chunks = per_worker // SC_CHUNK
    assert n % (n_workers * SC_CHUNK) == 0 and n_chunks % 2 == 0
    mesh = plsc.VectorSubcoreMesh(core_axis_name="core", subcore_axis_name="subcore")

    @functools.partial(
        pl.kernel,
        out_type=jax.ShapeDtypeStruct((n, d), table.dtype),
        mesh=mesh,
        scratch_types=[
            pltpu.VMEM((per_worker,), jnp.int32),
            pltpu.VMEM((2, SC_CHUNK, d), table.dtype),
            pltpu.SemaphoreType.DMA((2,)),
            pltpu.SemaphoreType.DMA((2,)),
        ],
    )
    def gather_kernel(table_hbm, idx_hbm, out_hbm, idx_v, rows_v, gather_sem, write_sem):
        worker = lax.axis_index("subcore") * info.num_cores + lax.axis_index("core")
        base = pl.multiple_of(worker * per_worker, SC_CHUNK)
        pltpu.sync_copy(idx_hbm.at[pl.ds(base, per_worker)], idx_v)

        def gather(c, buf):
            rows = idx_v.at[pl.ds(pl.multiple_of(c * SC_CHUNK, SC_CHUNK), SC_CHUNK)]
            return pltpu.make_async_copy(table_hbm.at[rows], rows_v.at[buf], gather_sem.at[buf])

        def write(c, buf):
            off = pl.multiple_of(base + c * SC_CHUNK, SC_CHUNK)
            return pltpu.make_async_copy(rows_v.at[buf], out_hbm.at[pl.ds(off, SC_CHUNK)], write_sem.at[buf])

        gather(0, 0).start()

        @pl.loop(0, n_chunks, step=2)
        def _(c0):
            for buf in range(2):
                c = c0 + buf
                gather(c, buf).wait()

                @pl.when(c >= 1)
                def _():
                    write(c - 1, 1 - buf).wait()

                @pl.when(c + 1 < n_chunks)
                def _():
                    gather(c + 1, 1 - buf).start()

                write(c, buf).start()

        write(n_chunks - 1, 1).wait()

    return gather_kernel(table, idx)


def _sc_scatter_rows(rows, pos, n_out):
    t, d = rows.shape
    top_k = pos.shape[0]
    pos_flat = pos.reshape(-1)
    info = plsc.get_sparse_core_info()
    n_workers = info.num_cores * info.num_subcores
    per_worker = t // n_workers
    n_chunks = per_worker // SC_CHUNK
    assert t % (n_workers * SC_CHUNK) == 0 and n_chunks % 2 == 0
    mesh = plsc.VectorSubcoreMesh(core_axis_name="core", subcore_axis_name="subcore")

    @functools.partial(
        pl.kernel,
        out_type=jax.ShapeDtypeStruct((n_out, d), rows.dtype),
        mesh=mesh,
        scratch_types=[pltpu.VMEM((2, SC_CHUNK, d), rows.dtype)]
        + [pltpu.VMEM((SC_CHUNK,), jnp.int32) for _ in range(2 * top_k)]
        + [pltpu.SemaphoreType.DMA((2,)), pltpu.SemaphoreType.DMA((2,))],
    )
    def scatter_kernel(rows_hbm, pos_hbm, out_hbm, rows_v, *rest):
        idx_bufs = rest[:2 * top_k]
        load_sem, write_sem = rest[2 * top_k:]
        worker = lax.axis_index("subcore") * info.num_cores + lax.axis_index("core")
        base = pl.multiple_of(worker * per_worker, SC_CHUNK)

        def loads(c, buf):
            off = pl.multiple_of(base + c * SC_CHUNK, SC_CHUNK)
            copies = [pltpu.make_async_copy(rows_hbm.at[pl.ds(off, SC_CHUNK)], rows_v.at[buf], load_sem.at[buf])]
            for k in range(top_k):
                src = pos_hbm.at[pl.ds(pl.multiple_of(k * t + off, SC_CHUNK), SC_CHUNK)]
                copies.append(pltpu.make_async_copy(src, idx_bufs[buf * top_k + k], load_sem.at[buf]))
            return copies

        def scatters(buf):
            return [pltpu.make_async_copy(rows_v.at[buf], out_hbm.at[idx_bufs[buf * top_k + k]], write_sem.at[buf])
                    for k in range(top_k)]

        for cp in loads(0, 0):
            cp.start()

        @pl.loop(0, n_chunks, step=2)
        def _(c0):
            for buf in range(2):
                c = c0 + buf
                for cp in loads(c, buf):
                    cp.wait()

                @pl.when(c >= 1)
                def _():
                    for cp in scatters(1 - buf):
                        cp.wait()

                @pl.when(c + 1 < n_chunks)
                def _():
                    for cp in loads(c + 1, 1 - buf):
                        cp.start()

                for cp in scatters(buf):
                    cp.start()

        for cp in scatters(1):
            cp.wait()

    return scatter_kernel(rows, pos_flat)


def _const_spec(shape):
    zeros = (0,) * len(shape)
    return pl.BlockSpec(shape, lambda *_: zeros, pipeline_mode=pl.Buffered(1))


def _tc_params(n_axes):
    return pltpu.CompilerParams(dimension_semantics=("arbitrary",) * n_axes,
                                vmem_limit_bytes=V7X_VMEM_LIMIT_BYTES)


def _mix_out_shapes(t, d, n_exp):
    return (jax.ShapeDtypeStruct((t, d), F32),
            jax.ShapeDtypeStruct((t, d // 2), jnp.uint32),
            jax.ShapeDtypeStruct((TOP_K, t), jnp.int32),
            jax.ShapeDtypeStruct((TOP_K, t), jnp.int32),
            jax.ShapeDtypeStruct((n_exp, V7X_LANES), F32),
            jax.ShapeDtypeStruct((t, V7X_LANES), F32))


def _mix_out_specs(rows, d, n_exp, row, tok_col):
    return (pl.BlockSpec((rows, d), row),
            pl.BlockSpec((rows, d // 2), row),
            pl.BlockSpec((TOP_K, rows), tok_col),
            pl.BlockSpec((TOP_K, rows), tok_col),
            pl.BlockSpec((n_exp, V7X_LANES), lambda *_: (0, 0)),
            pl.BlockSpec((rows, V7X_LANES), row))


def _strict_upper_tri(n):
    return (lax.broadcasted_iota(jnp.int32, (n, n), 0) < lax.broadcasted_iota(jnp.int32, (n, n), 1)).astype(BF16)


def _pool_layer(h, mem, w_in, wbd, pscale, mem_kv, w_out, ln_g, ln_b, rwt, rb, dn_alpha, seq):
    t, d = h.shape
    n_mem = mem.shape[1]
    pool_width = wbd.shape[0]
    ts = SEQ_BLOCK
    nsb = seq // ts
    row = lambda b, s: (b * nsb + s, 0)
    n_exp = rwt.shape[0]
    tri = _strict_upper_tri(ts)
    return pl.pallas_call(
        functools.partial(_pool_layer_kernel, dn_alpha),
        out_shape=_mix_out_shapes(t, d, n_exp),
        grid=(t // seq, nsb),
        in_specs=[
            pl.BlockSpec((ts, d), row),
            pl.BlockSpec((1, n_mem, d), lambda b, s: (b, 0, 0)),
            _const_spec(w_in.shape), _const_spec(wbd.shape), _const_spec(pscale.shape), _const_spec(mem_kv.shape),
            _const_spec(w_out.shape), _const_spec(ln_g.shape), _const_spec(ln_b.shape), _const_spec(rwt.shape),
            _const_spec(rb.shape), _const_spec(tri.shape),
        ],
        out_specs=_mix_out_specs(ts, d, n_exp, row, lambda b, s: (0, b * nsb + s)),
        scratch_shapes=[pltpu.VMEM((ts + POOL_HISTORY, pool_width), F32)] * 4
        + [pltpu.VMEM((n_mem, 2 * MEM_WIDTH), BF16), pltpu.VMEM((ts, d), BF16), pltpu.VMEM((n_exp, V7X_LANES), F32)],
        compiler_params=_tc_params(2),
        name="pool_layer",
    )(h, mem, w_in, wbd, pscale, mem_kv, w_out, ln_g, ln_b, rwt, rb, tri)


def _expand_heads(w, n_heads):
    d = w.shape[0]
    w3 = w.reshape(d, n_heads, HEAD_DIM)
    return jnp.pad(w3, ((0, 0), (0, 0), (0, V7X_LANES - HEAD_DIM))).reshape(d, n_heads * V7X_LANES)


def _shared_kv(h, kvw, fb, seq, n_heads):
    t, d = h.shape
    ts = SEQ_BLOCK
    nsb = seq // ts
    k_width = n_heads * V7X_LANES
    v_width = n_heads * HEAD_DIM
    tri = (lax.broadcasted_iota(jnp.int32, (ts, ts), 1) <= lax.broadcasted_iota(jnp.int32, (ts, ts), 0)).astype(BF16)
    row = lambda b, s: (b * nsb + s, 0)
    return pl.pallas_call(
        _shared_kv_kernel,
        out_shape=(jax.ShapeDtypeStruct((t, k_width), BF16),
                   jax.ShapeDtypeStruct((t // seq, v_width, seq), BF16),
                   jax.ShapeDtypeStruct((t, V7X_LANES), F32)),
        grid=(t // seq, nsb),
        in_specs=[pl.BlockSpec((ts, d), row), _const_spec(kvw.shape), _const_spec(fb.shape), _const_spec(tri.shape)],
        out_specs=(pl.BlockSpec((ts, k_width), row),
                   pl.BlockSpec((1, v_width, ts), lambda b, s: (b, 0, s)),
                   pl.BlockSpec((ts, V7X_LANES), row)),
        scratch_shapes=[pltpu.VMEM((8, V7X_LANES), F32)],
        compiler_params=_tc_params(2),
        name="shared_kv",
    )(h, kvw, fb, tri)


def _fox_layer(h, mem, k_aug, v_t, ccol, w_in, mem_kv, w_out, ln_g, ln_b, rwt, rb, dn_alpha, seq):
    t, d = h.shape
    n_mem = mem.shape[1]
    k_width = k_aug.shape[1]
    v_width = v_t.shape[1]
    n_heads = k_width // V7X_LANES
    tq = Q_BLOCK
    nqb = seq // tq
    n_batch = t // seq
    row = lambda b, i: (b * nqb + i, 0)
    per_batch = lambda b, i: (b, 0, 0)
    n_exp = rwt.shape[0]
    tri = _strict_upper_tri(tq)
    return pl.pallas_call(
        functools.partial(_fox_layer_kernel, dn_alpha),
        out_shape=_mix_out_shapes(t, d, n_exp),
        grid=(n_batch, nqb),
        in_specs=[
            pl.BlockSpec((tq, d), row),
            pl.BlockSpec((1, n_mem, d), per_batch),
            pl.BlockSpec((1, seq, k_width), per_batch),
            pl.BlockSpec((1, v_width, seq), per_batch),
            pl.BlockSpec((tq, V7X_LANES), row),
            _const_spec(w_in.shape), _const_spec(mem_kv.shape), _const_spec(w_out.shape), _const_spec(ln_g.shape),
            _const_spec(ln_b.shape), _const_spec(rwt.shape), _const_spec(rb.shape), _const_spec(tri.shape),
        ],
        out_specs=_mix_out_specs(tq, d, n_exp, row, lambda b, i: (0, b * nqb + i)),
        scratch_shapes=[pltpu.VMEM((n_mem, 2 * MEM_WIDTH), BF16), pltpu.VMEM((tq, d), BF16),
                        pltpu.VMEM((n_heads, tq, V7X_LANES), BF16),
                        pltpu.VMEM((16, tq), F32), pltpu.VMEM((16, tq), F32), pltpu.VMEM((v_width, tq), F32),
                        pltpu.VMEM((n_exp, V7X_LANES), F32)],
        compiler_params=_tc_params(2),
        name="fox_layer",
    )(h, mem, k_aug.reshape(n_batch, seq, k_width), v_t, ccol, w_in, mem_kv, w_out, ln_g, ln_b, rwt, rb, tri)


def _slot_kernel(pad_start_ref, idx_ref, rank_ref, pos_ref):
    pos = rank_ref[...]
    idx = idx_ref[...]
    for e in range(pad_start_ref.shape[0]):
        pos = pos + jnp.where(idx == e, pad_start_ref[e], 0)
    pos_ref[...] = pos


def _route(idx_t, rank_t, counts, n_experts):
    top_k, t = idx_t.shape
    n_assign = top_k * t
    padded = ((counts + MOE_BLOCK - 1) // MOE_BLOCK) * MOE_BLOCK
    pad_end = jnp.cumsum(padded)
    pad_start = pad_end - padded
    n_blocks = n_assign // MOE_BLOCK + n_experts
    block_start = jnp.arange(n_blocks, dtype=jnp.int32) * MOE_BLOCK
    block_expert = jnp.minimum(jnp.sum((pad_end[None, :] <= block_start[:, None]).astype(jnp.int32), axis=1),
                               n_experts - 1)
    block_valid = jnp.clip(counts[block_expert] - (block_start - pad_start[block_expert]), 0, MOE_BLOCK)
    n_active = (pad_end[-1:] // MOE_BLOCK).astype(jnp.int32)
    cols = min(SLOT_COLS, t)
    pos = pl.pallas_call(
        _slot_kernel,
        out_shape=jax.ShapeDtypeStruct((top_k, t), jnp.int32),
        grid_spec=pltpu.PrefetchScalarGridSpec(
            num_scalar_prefetch=1,
            grid=(t // cols,),
            in_specs=[pl.BlockSpec((top_k, cols), lambda i, ps: (0, i))] * 2,
            out_specs=pl.BlockSpec((top_k, cols), lambda i, ps: (0, i)),
        ),
        compiler_params=_tc_params(1),
        name="slots",
    )(pad_start.astype(jnp.int32), idx_t, rank_t)
    return pos, (block_expert, block_valid.astype(jnp.int32), n_active)


def _expert_weights(w1_all, w2_all, layer):
    _, n_exp, d, two_f = w1_all.shape
    d_ff = two_f // 2
    return pl.pallas_call(
        _expert_weight_kernel,
        out_shape=(jax.ShapeDtypeStruct((n_exp, d, two_f), BF16), jax.ShapeDtypeStruct((n_exp, d_ff, d), BF16)),
        grid=(n_exp,),
        in_specs=[pl.BlockSpec((None, 1, d, two_f), lambda e: (layer, e, 0, 0)),
                  pl.BlockSpec((None, 1, d_ff, d), lambda e: (layer, e, 0, 0))],
        out_specs=(pl.BlockSpec((1, d, two_f), lambda e: (e, 0, 0)), pl.BlockSpec((1, d_ff, d), lambda e: (e, 0, 0))),
        compiler_params=_tc_params(1),
        name="expert_weights",
    )(w1_all, w2_all)


def _experts(xs, block_expert, block_valid, n_active, w1s, b1, w2s, b2):
    n_slots, half_d = xs.shape
    n_exp, d, two_f = w1s.shape
    d_ff = two_f // 2
    assert d == 2 * half_d
    block = lambda i, *_: (i, 0)
    per_expert = lambda i, be, *_: (be[i], 0, 0)
    grid_spec = pltpu.PrefetchScalarGridSpec(
        num_scalar_prefetch=3,
        grid=(n_slots // MOE_BLOCK,),
        in_specs=[
            pl.BlockSpec((MOE_BLOCK, half_d), block),
            pl.BlockSpec((1, d, two_f), per_expert),
            pl.BlockSpec((1, 1, two_f), per_expert),
            pl.BlockSpec((1, d_ff, d), per_expert),
            pl.BlockSpec((1, 1, d), per_expert),
        ],
        out_specs=pl.BlockSpec((MOE_BLOCK, half_d), block),
        scratch_shapes=[pltpu.VMEM((MOE_BLOCK, d_ff), BF16)],
    )
    return pl.pallas_call(
        _expert_kernel,
        out_shape=jax.ShapeDtypeStruct((n_slots, half_d), jnp.uint32),
        grid_spec=grid_spec,
        compiler_params=_tc_params(1),
        name="experts",
    )(block_expert, block_valid, n_active, xs, w1s, b1.reshape(n_exp, 1, two_f), w2s, b2.reshape(n_exp, 1, d))


def _combine(h, yg, gcol, ln_g, ln_b, dn_alpha):
    t, d = h.shape
    ts = SEQ_BLOCK
    row = lambda i: (i, 0)
    return pl.pallas_call(
        functools.partial(_combine_kernel, dn_alpha),
        out_shape=jax.ShapeDtypeStruct((t, d), F32),
        grid=(t // ts,),
        in_specs=[pl.BlockSpec((ts, d), row), pl.BlockSpec((TOP_K, ts, d // 2), lambda i: (0, i, 0)),
                  pl.BlockSpec((ts, V7X_LANES), row), _const_spec(ln_g.shape), _const_spec(ln_b.shape)],
        out_specs=pl.BlockSpec((ts, d), row),
        compiler_params=_tc_params(1),
        name="combine",
    )(h, yg.reshape(TOP_K, t, d // 2), gcol, ln_g, ln_b)


def _moe(mixed, w1_all, b1, w2_all, b2, layer, ln_g, ln_b, dn_alpha):
    h1, h1_packed, idx_t, rank_t, counts, gcol = mixed
    n_exp = w1_all.shape[1]
    d_ff = w2_all.shape[2]
    pos, block_meta = _route(idx_t, rank_t, counts[:, 0].astype(jnp.int32), n_exp)
    b1p = jnp.swapaxes(b1.reshape(n_exp, d_ff // V7X_LANES, V7X_LANES, 2), 2, 3).reshape(n_exp, 2 * d_ff)
    w1s, w2s = _expert_weights(w1_all, w2_all, layer)
    xs = _sc_scatter_rows(h1_packed, pos, block_meta[0].shape[0] * MOE_BLOCK)
    y = _experts(xs, *block_meta, w1s, b1p, w2s, b2)
    yg = _sc_gather_rows(y, pos.reshape(-1))
    return _combine(h1, yg, gcol, ln_g, ln_b, dn_alpha)


def kernel(x, mem, ln1_g, ln1_b, ln2_g, ln2_b, a_w_in, pool_w, pool_scale, a_mem_kv, a_w_out, kv_w, fgate_b, b_w_in, b_mem_kv, b_w_out, router_w, router_b, moe_w1, moe_b1, moe_w2, moe_b2):
    n_batch, seq, d = x.shape
    t = n_batch * seq
    depth = ln1_g.shape[0]
    n_a = a_w_in.shape[0]
    dn_alpha = float((2 * depth) ** 0.25)
    n_exp = router_w.shape[2]
    fox_heads = fgate_b.shape[0]
    fox_width = fox_heads * HEAD_DIM
    assert seq % SEQ_BLOCK == 0 and seq % Q_BLOCK == 0 and fox_heads <= 16

    h = x.reshape(t, d)
    row = lambda a: a.reshape(1, -1)
    shared = None
    for l in range(depth):
        rwt = router_w[l].T
        rb = router_b[l].reshape(n_exp, 1)
        if l < n_a:
            wbd = jax.scipy.linalg.block_diag(*[pool_w[l, g] for g in range(pool_w.shape[1])]).astype(BF16)
            mixed = _pool_layer(
                h, mem, a_w_in[l].astype(BF16), wbd, row(pool_scale[l]), a_mem_kv[l].astype(BF16),
                a_w_out[l].astype(BF16), row(ln1_g[l]), row(ln1_b[l]), rwt, rb, dn_alpha, seq)
        else:
            j = l - n_a
            if shared is None:
                pad = V7X_LANES - fox_heads
                kvw = jnp.concatenate([_expand_heads(kv_w[:, :fox_width], fox_heads),
                                       kv_w[:, fox_width:2 * fox_width],
                                       jnp.pad(kv_w[:, 2 * fox_width:], ((0, 0), (0, pad)))], axis=1).astype(BF16)
                fb = jnp.pad(fgate_b, (0, pad)).reshape(1, V7X_LANES)
                shared = _shared_kv(h, kvw, fb, seq, fox_heads)
            k_aug, v_t, ccol = shared
            w_in = jnp.concatenate([_expand_heads(b_w_in[j][:, :fox_width], fox_heads),
                                    b_w_in[j][:, fox_width:]], axis=1).astype(BF16)
            mixed = _fox_layer(
                h, mem, k_aug, v_t, ccol, w_in, b_mem_kv[j].astype(BF16),
                b_w_out[j].astype(BF16), row(ln1_g[l]), row(ln1_b[l]), rwt, rb, dn_alpha, seq)
        h = _moe(mixed, moe_w1, moe_b1[l], moe_w2, moe_b2[l], l, row(ln2_g[l]), row(ln2_b[l]), dn_alpha)
    return h.reshape(n_batch, seq, d)
```

```python
import functools
import numpy as np
import jax
import jax.numpy as jnp
from jax import lax
from jax.experimental import pallas as pl
from jax.experimental.pallas import tpu as pltpu
from jax.experimental.pallas import tpu_sc as plsc

HEAD_DIM = 64
MEM_HEADS = 4
MEM_WIDTH = MEM_HEADS * HEAD_DIM
POOL_WINDOWS = (2, 4, 8, 16)
TOP_K = 4
SWIGLU_ALPHA = 1.702
SWIGLU_LIMIT = 7.0
MOE_BLOCK = 512
LN_EPS = 1e-5
ATTN_SCALE = HEAD_DIM ** -0.5
LOG2_E = 1.4426950408889634

V7X_LANES = 128
V7X_VMEM_LIMIT_BYTES = 56 * 1024 * 1024
POOL_HISTORY = 32
SC_CHUNK = 64
SEQ_BLOCK = 512
Q_BLOCK = 512
SLOT_COLS = 8192

F32 = jnp.float32
BF16 = jnp.bfloat16
NEG_INF = float("-inf")


def _dot(a, b):
    return jnp.dot(a, b, preferred_element_type=F32)


def _dot_nt(a, b):
    return lax.dot_general(a, b, (((1,), (1,)), ((), ())), preferred_element_type=F32)


def _lane_iota(shape):
    return lax.broadcasted_iota(jnp.int32, shape, len(shape) - 1)


def _pair_attention(q2, keys, vals, bias=None):
    lane = _lane_iota(q2.shape)
    first = lane < HEAD_DIM
    outs = []
    for sel in (first, jnp.logical_not(first)):
        qh = jnp.where(sel, q2, 0.0).astype(BF16)
        s = _dot_nt(qh, keys)
        m = jnp.max(s, axis=-1, keepdims=True)
        p = jnp.exp(s - m)
        l = jnp.sum(p, axis=-1, keepdims=True)
        outs.append(_dot(p.astype(BF16), vals) / l)
    return jnp.where(first, outs[0], outs[1])


def _memory_attention(proj, kv_ref, cat_ref, out_off, q_off):
    for p in range(MEM_HEADS // 2):
        lo = p * V7X_LANES
        q2 = proj[:, q_off + lo:q_off + lo + V7X_LANES] * ATTN_SCALE
        keys = kv_ref[:, lo:lo + V7X_LANES]
        vals = kv_ref[:, MEM_WIDTH + lo:MEM_WIDTH + lo + V7X_LANES]
        cat_ref[:, out_off + lo:out_off + lo + V7X_LANES] = _pair_attention(q2, keys, vals).astype(BF16)


def _layer_norm(z, g, b):
    mu = jnp.mean(z, axis=-1, keepdims=True)
    zc = z - mu
    var = jnp.mean(zc * zc, axis=-1, keepdims=True)
    return zc * lax.rsqrt(var + LN_EPS) * g + b


def _pack_bf16_pairs(x):
    w = x.shape[1] // 2
    bits = lax.bitcast_convert_type(x.astype(BF16).astype(F32), jnp.uint32)
    return lax.shift_right_logical(bits[:, :w], jnp.uint32(16)) | (bits[:, w:] & jnp.uint32(0xFFFF0000))


def _unpack_bf16_pairs(p):
    low = lax.bitcast_convert_type(lax.shift_left(p, jnp.uint32(16)), F32)
    high = lax.bitcast_convert_type(p & jnp.uint32(0xFFFF0000), F32)
    return low, high


def _expert_ranks(idxs, tri_ref, rank_ref, counts_ref, base_ref):
    @pl.when(jnp.logical_and(pl.program_id(0) == 0, pl.program_id(1) == 0))
    def _():
        base_ref[...] = jnp.zeros(base_ref.shape, F32)

    n_exp = base_ref.shape[0]
    t = idxs[0].shape[1]
    rows = lax.broadcasted_iota(jnp.int32, (n_exp, t), 0)
    running = base_ref[:, 0:1]
    for k in range(TOP_K):
        onehot = rows == idxs[k]
        prefix = _dot(jnp.where(onehot, 1.0, 0.0).astype(BF16), tri_ref[...])
        rank = jnp.sum(jnp.where(onehot, prefix + running, 0.0), axis=0, keepdims=True)
        rank_ref[k:k + 1, :] = rank.astype(jnp.int32)
        running = running + jnp.sum(jnp.where(onehot, 1.0, 0.0), axis=1, keepdims=True)
    base_ref[...] = jnp.broadcast_to(running, base_ref.shape)
    counts_ref[...] = base_ref[...]


def _mix_epilogue(x, cat_ref, w_out_ref, g_ref, b_ref, rwt_ref, rb_ref, tri_ref, dn_alpha, h_out_ref, hp_ref, idx_ref,
                  rank_ref, counts_ref, gcol_ref, base_ref):
    mix = _dot(cat_ref[...], w_out_ref[...])
    h1 = _layer_norm(dn_alpha * x + mix, g_ref[...], b_ref[...])
    h_out_ref[...] = h1
    hp_ref[...] = _pack_bf16_pairs(h1)
    logits = lax.dot_general(rwt_ref[...], h1, (((1,), (1,)), ((), ())),
                             precision=lax.Precision.HIGHEST, preferred_element_type=F32) + rb_ref[...]
    n_exp, t = logits.shape
    rows = lax.broadcasted_iota(jnp.int32, (n_exp, t), 0)
    vals, idxs = [], []
    for _ in range(TOP_K):
        m = jnp.max(logits, axis=0, keepdims=True)
        ix = jnp.min(jnp.where(logits == m, rows, n_exp), axis=0, keepdims=True)
        vals.append(m)
        idxs.append(ix)
        logits = jnp.where(rows == ix, NEG_INF, logits)
    exps = [jnp.exp(v - vals[0]) for v in vals]
    denom = exps[0] + exps[1] + exps[2] + exps[3]
    for k in range(TOP_K):
        idx_ref[k:k + 1, :] = idxs[k]
    _expert_ranks(idxs, tri_ref, rank_ref, counts_ref, base_ref)
    lane_rows = lax.broadcasted_iota(jnp.int32, (V7X_LANES, t), 0)
    gates_t = jnp.zeros((V7X_LANES, t), F32)
    for k in range(TOP_K):
        gates_t = jnp.where(lane_rows == k, exps[k] / denom, gates_t)
    gcol_ref[...] = gates_t.T


def _pool_layer_kernel(dn_alpha, h_ref, mem_ref, w_in_ref, wbd_ref, pscale_ref, mem_kv_ref, w_out_ref, g_ref, b_ref,
                       rwt_ref, rb_ref, tri_ref, h_out_ref, hp_ref, idx_ref, rank_ref, counts_ref, gcol_ref,
                       ue_ref, w2_ref, w4_ref, w8_ref, kv_ref, cat_ref, base_ref):
    s = pl.program_id(1)
    ts = h_ref.shape[0]
    pool_width = wbd_ref.shape[0]
    hist = POOL_HISTORY

    @pl.when(s == 0)
    def _():
        ue_ref[0:hist, :] = jnp.zeros((hist, pool_width), F32)
        kv_ref[...] = _dot(mem_ref[0].astype(BF16), mem_kv_ref[...]).astype(BF16)

    x = h_ref[...]
    proj = _dot(x.astype(BF16), w_in_ref[...])
    u = proj[:, :pool_width]
    ue_ref[hist:, :] = u
    n = ts + hist
    w2_ref[8:n, :] = ue_ref[8:n, :] + ue_ref[7:n - 1, :]
    w4_ref[16:n, :] = w2_ref[16:n, :] + w2_ref[14:n - 2, :]
    w8_ref[24:n, :] = w4_ref[24:n, :] + w4_ref[20:n - 4, :]
    w16 = w8_ref[hist:n, :] + w8_ref[hist - 8:n - 8, :]
    group_dim = pool_width // len(POOL_WINDOWS)
    col = _lane_iota((ts, pool_width))
    wsum = jnp.where(col < group_dim, w2_ref[hist:n, :],
                     jnp.where(col < 2 * group_dim, w4_ref[hist:n, :],
                               jnp.where(col < 3 * group_dim, w8_ref[hist:n, :], w16)))
    window = jnp.where(col < group_dim, POOL_WINDOWS[0],
                       jnp.where(col < 2 * group_dim, POOL_WINDOWS[1],
                                 jnp.where(col < 3 * group_dim, POOL_WINDOWS[2], POOL_WINDOWS[3])))
    tpos = s * ts + lax.broadcasted_iota(jnp.int32, (ts, pool_width), 0) + 1
    cnt = jnp.minimum(tpos, window).astype(F32)
    d = wsum / cnt - u
    ue_ref[0:hist, :] = u[ts - hist:, :]
    pooled = _dot(d.astype(BF16), wbd_ref[...]) * pscale_ref[...]
    cat_ref[:, :pool_width] = pooled.astype(BF16)
    _memory_attention(proj, kv_ref, cat_ref, pool_width, pool_width)
    _mix_epilogue(x, cat_ref, w_out_ref, g_ref, b_ref, rwt_ref, rb_ref, tri_ref, dn_alpha, h_out_ref, hp_ref, idx_ref,
                  rank_ref, counts_ref, gcol_ref, base_ref)


def _bias_pieces(c):
    hi = c.astype(BF16).astype(F32)
    r = c - hi
    mid = r.astype(BF16).astype(F32)
    lo = (r - mid).astype(BF16).astype(F32)
    return hi, mid, lo


def _bias_lane_pieces(c3, n_heads):
    hi, mid, lo = _bias_pieces(c3)
    lane = _lane_iota(c3.shape)
    out = jnp.where(lane < n_heads, hi, jnp.where(lane < 2 * n_heads, mid, jnp.where(lane < 3 * n_heads, lo,
                    jnp.where(lane == 3 * n_heads, 1.0, 0.0))))
    return out.astype(BF16)


def _key_bias_selector(n_heads):
    sel = np.zeros((V7X_LANES, n_heads * V7X_LANES), np.float32)
    for h in range(n_heads):
        spare = h * V7X_LANES + HEAD_DIM
        for p in range(3):
            sel[3 * n_heads, spare + p] = 1.0
            sel[p * n_heads + h, spare + 3 + p] = -1.0
    return jnp.asarray(sel, BF16)


def _shared_kv_kernel(h_ref, kvw_ref, fb_ref, tri_ref, sel_ref, k_ref, vt_ref, cq_ref, carry_ref):
    s = pl.program_id(1)
    k_width = k_ref.shape[1]
    v_width = vt_ref.shape[1]
    n_heads = k_width // V7X_LANES

    @pl.when(s == 0)
    def _():
        carry_ref[...] = jnp.zeros(carry_ref.shape, F32)

    proj = _dot(h_ref[...].astype(BF16), kvw_ref[...])
    z = proj[:, k_width + v_width:] + fb_ref[...]
    logf = jnp.minimum(z, 0.0) - jnp.log(1.0 + jnp.exp(-jnp.abs(z)))
    hi, mid, lo = _bias_pieces(logf)
    tri = tri_ref[...]
    c = _dot(tri, hi.astype(BF16)) + _dot(tri, mid.astype(BF16)) + _dot(tri, lo.astype(BF16)) + carry_ref[0:1, :]
    ts = c.shape[0]
    carry_ref[0:1, :] = c[ts - 1:ts, :]
    pieces = _bias_lane_pieces(c * LOG2_E, n_heads)
    cq_ref[...] = pieces
    k_ref[...] = (proj[:, :k_width] + _dot(pieces, sel_ref[...])).astype(BF16)
    vt_ref[0] = proj[:, k_width:k_width + v_width].T.astype(BF16)


def _fox_layer_kernel(dn_alpha, h_ref, mem_ref, k_ref, vt_ref, cq_ref, w_in_ref, mem_kv_ref, w_out_ref,
                      g_ref, b_ref, rwt_ref, rb_ref, tri_ref, h_out_ref, hp_ref, idx_ref, rank_ref, counts_ref, gcol_ref,
                      kv_ref, cat_ref, qa_ref, m_ref, l_ref, acc_ref, base_ref):
    i = pl.program_id(1)
    tq = h_ref.shape[0]
    n_heads = k_ref.shape[2] // V7X_LANES
    fox_width = vt_ref.shape[1]

    @pl.when(i == 0)
    def _():
        kv_ref[...] = _dot(mem_ref[0].astype(BF16), mem_kv_ref[...]).astype(BF16)

    x = h_ref[...]
    proj = _dot(x.astype(BF16), w_in_ref[...])
    cq = cq_ref[...].astype(F32)
    lane = _lane_iota((tq, V7X_LANES))
    for h in range(n_heads):
        cols = slice(h * V7X_LANES, (h + 1) * V7X_LANES)
        piece = [cq[:, p * n_heads + h:p * n_heads + h + 1] for p in range(3)]
        spare = jnp.where(lane == HEAD_DIM, piece[0], jnp.where(lane == HEAD_DIM + 1, piece[1],
                          jnp.where(lane == HEAD_DIM + 2, piece[2], jnp.where(lane < HEAD_DIM + 6, 1.0, 0.0))))
        qa_ref[h] = jnp.where(lane < HEAD_DIM, proj[:, cols] * (ATTN_SCALE * LOG2_E), spare).astype(BF16)
    m_ref[...] = jnp.full(m_ref.shape, NEG_INF, F32)
    l_ref[...] = jnp.zeros(l_ref.shape, F32)
    acc_ref[...] = jnp.zeros(acc_ref.shape, F32)
    causal_t = lax.broadcasted_iota(jnp.int32, (tq, tq), 0) <= lax.broadcasted_iota(jnp.int32, (tq, tq), 1)

    ones_rows = (lax.broadcasted_iota(jnp.int32, (16, tq), 0) == 0).astype(BF16)

    def tile(j, masked):
        start = pl.multiple_of(j * tq, tq)

        def scores(h):
            kb = k_ref[0, pl.ds(start, tq), h * V7X_LANES:(h + 1) * V7X_LANES]
            return _dot_nt(kb, qa_ref[h])

        st_next = scores(0)
        for h in range(n_heads):
            st = st_next
            if h + 1 < n_heads:
                st_next = scores(h + 1)
            if masked:
                st = jnp.where(causal_t, st, NEG_INF)
            m_old = m_ref[h:h + 1, :]
            m_new = jnp.maximum(m_old, jnp.max(st, axis=0, keepdims=True))
            alpha = jnp.exp2(m_old - m_new)
            pt = jnp.exp2(st - m_new).astype(BF16)
            m_ref[h:h + 1, :] = m_new
            rows = slice(h * HEAD_DIM, (h + 1) * HEAD_DIM)
            vb = jnp.concatenate([vt_ref[0, rows, pl.ds(start, tq)], ones_rows], axis=0)
            pv = _dot(vb, pt)
            acc_ref[rows, :] = alpha * acc_ref[rows, :] + pv[:HEAD_DIM]
            l_ref[h:h + 1, :] = alpha * l_ref[h:h + 1, :] + pv[HEAD_DIM:HEAD_DIM + 1]

    tile(i, True)

    def off_diagonal(j, carry):
        tile(j, False)
        return carry

    lax.fori_loop(0, i, off_diagonal, 0)
    for h in range(n_heads):
        rows = slice(h * HEAD_DIM, (h + 1) * HEAD_DIM)
        acc_ref[rows, :] = acc_ref[rows, :] / l_ref[h:h + 1, :]
    cat_ref[:, :fox_width] = acc_ref[...].T.astype(BF16)
    _memory_attention(proj, kv_ref, cat_ref, fox_width, n_heads * V7X_LANES)
    _mix_epilogue(x, cat_ref, w_out_ref, g_ref, b_ref, rwt_ref, rb_ref, tri_ref, dn_alpha, h_out_ref, hp_ref, idx_ref,
                  rank_ref, counts_ref, gcol_ref, base_ref)


def _expert_kernel(layer, be_ref, bv_ref, bslot_ref, bnext_ref, nact_ref, x_ref, w1_hbm, b1_ref, w2_hbm, b2_ref, y_ref,
                   w1s_ref, w2s_ref, act_ref, w1f_ref, w2f_ref, sem_ref):
    i = pl.program_id(0)
    active = i < nact_ref[0]
    two_f = w1f_ref.shape[2]
    group = 2 * V7X_LANES
    n_groups = two_f // group

    def weight_copies(expert, slot):
        return (pltpu.make_async_copy(w1_hbm.at[layer, expert], w1f_ref.at[slot], sem_ref.at[0, slot]),
                pltpu.make_async_copy(w2_hbm.at[layer, expert], w2f_ref.at[slot], sem_ref.at[1, slot]))

    @pl.when(i == 0)
    def _():
        for cp in weight_copies(be_ref[0], bslot_ref[0]):
            cp.start()

    @pl.when(jnp.logical_and(active, jnp.logical_or(i == 0, be_ref[i] != be_ref[jnp.maximum(i - 1, 0)])))
    def _():
        slot = bslot_ref[i]
        for cp in weight_copies(be_ref[i], slot):
            cp.wait()

        @pl.when(bnext_ref[i] >= 0)
        def _():
            for cp in weight_copies(bnext_ref[i], 1 - slot):
                cp.start()

        src = lax.broadcasted_iota(jnp.int32, (group, group), 0)
        dst = lax.broadcasted_iota(jnp.int32, (group, group), 1)
        want = jnp.where(dst < V7X_LANES, 2 * dst, 2 * (dst - V7X_LANES) + 1)
        perm = (src == want).astype(BF16)
        for g in range(n_groups):
            cols = slice(g * group, (g + 1) * group)
            w1s_ref[:, cols] = _dot(w1f_ref[slot, :, cols].astype(BF16), perm).astype(BF16)
        w2s_ref[...] = w2f_ref[slot].astype(BF16)

    @pl.when(active)
    def _():
        row_id = lax.broadcasted_iota(jnp.int32, x_ref.shape, 0)
        x_low, x_high = _unpack_bf16_pairs(jnp.where(row_id < bv_ref[i], x_ref[...], jnp.uint32(0)))
        xb = jnp.concatenate([x_low.astype(BF16), x_high.astype(BF16)], axis=1)
        hdn = _dot(xb, w1s_ref[...]) + b1_ref[0]
        for g in range(n_groups):
            x_glu = jnp.minimum(hdn[:, g * group:g * group + V7X_LANES], SWIGLU_LIMIT)
            x_lin = jnp.clip(hdn[:, g * group + V7X_LANES:(g + 1) * group], -SWIGLU_LIMIT, SWIGLU_LIMIT)
            act = x_glu * jax.nn.sigmoid(SWIGLU_ALPHA * x_glu) * (x_lin + 1.0)
            act_ref[:, g * V7X_LANES:(g + 1) * V7X_LANES] = act.astype(BF16)
        y_ref[...] = _pack_bf16_pairs(_dot(act_ref[...], w2s_ref[...]) + b2_ref[0])

    @pl.when(jnp.logical_not(active))
    def _():
        y_ref[...] = jnp.zeros(y_ref.shape, jnp.uint32)


def _combine_kernel(dn_alpha, h_ref, y_ref, gcol_ref, g_ref, b_ref, o_ref):
    gates = gcol_ref[...]
    low, high = _unpack_bf16_pairs(y_ref[0])
    low, high = gates[:, 0:1] * low, gates[:, 0:1] * high
    for k in range(1, TOP_K):
        low_k, high_k = _unpack_bf16_pairs(y_ref[k])
        low = low + gates[:, k:k + 1] * low_k
        high = high + gates[:, k:k + 1] * high_k
    ffn = jnp.concatenate([low, high], axis=1)
    o_ref[...] = _layer_norm(dn_alpha * h_ref[...] + ffn, g_ref[...], b_ref[...])


def _sc_gather_rows(table, idx):
    n, = idx.shape
    _, d = table.shape
    info = plsc.get_sparse_core_info()
    n_workers = info.num_cores * info.num_subcores
    per_worker = n // n_workers
    n_chunks = per_worker // SC_CHUNK
    assert n % (n_workers * SC_CHUNK) == 0 and n_chunks % 2 == 0
    mesh = plsc.VectorSubcoreMesh(core_axis_name="core", subcore_axis_name="subcore")

    @functools.partial(
        pl.kernel,
        out_type=jax.ShapeDtypeStruct((n, d), table.dtype),
        mesh=mesh,
        scratch_types=[
            pltpu.VMEM((per_worker,), jnp.int32),
            pltpu.VMEM((2, SC_CHUNK, d), table.dtype),
            pltpu.SemaphoreType.DMA((2,)),
            pltpu.SemaphoreType.DMA((2,)),
        ],
    )
    def gather_kernel(table_hbm, idx_hbm, out_hbm, idx_v, rows_v, gather_sem, write_sem):
        worker = lax.axis_index("subcore") * info.num_cores + lax.axis_index("core")
        base = pl.multiple_of(worker * per_worker, SC_CHUNK)
        pltpu.sync_copy(idx_hbm.at[pl.ds(base, per_worker)], idx_v)

        def gather(c, buf):
            rows = idx_v.at[pl.ds(pl.multiple_of(c * SC_CHUNK, SC_CHUNK), SC_CHUNK)]
            return pltpu.make_async_copy(table_hbm.at[rows], rows_v.at[buf], gather_sem.at[buf])

        def write(c, buf):
            off = pl.multiple_of(base + c * SC_CHUNK, SC_CHUNK)
            return pltpu.make_async_copy(rows_v.at[buf], out_hbm.at[pl.ds(off, SC_CHUNK)], write_sem.at[buf])

        gather(0, 0).start()

        @pl.loop(0, n_chunks, step=2)
        def _(c0):
            for buf in range(2):
                c = c0 + buf
                gather(c, buf).wait()

                @pl.when(c >= 1)
                def _():
                    write(c - 1, 1 - buf).wait()

                @pl.when(c + 1 < n_chunks)
                def _():
                    gather(c + 1, 1 - buf).start()

                write(c, buf).start()

        write(n_chunks - 1, 1).wait()

    return gather_kernel(table, idx)


def _sc_scatter_rows(rows, pos, n_out):
    t, d = rows.shape
    top_k = pos.shape[0]
    pos_flat = pos.reshape(-1)
    info = plsc.get_sparse_core_info()
    n_workers = info.num_cores * info.num_subcores
    per_worker = t // n_workers
    n_chunks = per_worker // SC_CHUNK
    assert t % (n_workers * SC_CHUNK) == 0 and n_chunks % 2 == 0
    mesh = plsc.VectorSubcoreMesh(core_axis_name="core", subcore_axis_name="subcore")

    @functools.partial(
        pl.kernel,
        out_type=jax.ShapeDtypeStruct((n_out, d), rows.dtype),
        mesh=mesh,
        scratch_types=[pltpu.VMEM((2, SC_CHUNK, d), rows.dtype)]
        + [pltpu.VMEM((SC_CHUNK,), jnp.int32) for _ in range(2 * top_k)]
        + [pltpu.SemaphoreType.DMA((2,)), pltpu.SemaphoreType.DMA((2,))],
    )
    def scatter_kernel(rows_hbm, pos_hbm, out_hbm, rows_v, *rest):
        idx_bufs = rest[:2 * top_k]
        load_sem, write_sem = rest[2 * top_k:]
        worker = lax.axis_index("subcore") * info.num_cores + lax.axis_index("core")
        base = pl.multiple_of(worker * per_worker, SC_CHUNK)

        def loads(c, buf):
            off = pl.multiple_of(base + c * SC_CHUNK, SC_CHUNK)
            copies = [pltpu.make_async_copy(rows_hbm.at[pl.ds(off, SC_CHUNK)], rows_v.at[buf], load_sem.at[buf])]
            for k in range(top_k):
                src = pos_hbm.at[pl.ds(pl.multiple_of(k * t + off, SC_CHUNK), SC_CHUNK)]
                copies.append(pltpu.make_async_copy(src, idx_bufs[buf * top_k + k], load_sem.at[buf]))
            return copies

        def scatters(buf):
            return [pltpu.make_async_copy(rows_v.at[buf], out_hbm.at[idx_bufs[buf * top_k + k]], write_sem.at[buf])
                    for k in range(top_k)]

        for cp in loads(0, 0):
            cp.start()

        @pl.loop(0, n_chunks, step=2)
        def _(c0):
            for buf in range(2):
                c = c0 + buf
                for cp in loads(c, buf):
                    cp.wait()

                @pl.when(c >= 1)
                def _():
                    for cp in scatters(1 - buf):
                        cp.wait()

                @pl.when(c + 1 < n_chunks)
                def _():
                    for cp in loads(c + 1, 1 - buf):
                        cp.start()

                for cp in scatters(buf):
                    cp.start()

        for cp in scatters(1):
            cp.wait()

    return scatter_kernel(rows, pos_flat)


def _const_spec(shape):
    zeros = (0,) * len(shape)
    return pl.BlockSpec(shape, lambda *_: zeros, pipeline_mode=pl.Buffered(1))


def _tc_params(n_axes):
    return pltpu.CompilerParams(dimension_semantics=("arbitrary",) * n_axes,
                                vmem_limit_bytes=V7X_VMEM_LIMIT_BYTES)


def _mix_out_shapes(t, d, n_exp):
    return (jax.ShapeDtypeStruct((t, d), F32),
            jax.ShapeDtypeStruct((t, d // 2), jnp.uint32),
            jax.ShapeDtypeStruct((TOP_K, t), jnp.int32),
            jax.ShapeDtypeStruct((TOP_K, t), jnp.int32),
            jax.ShapeDtypeStruct((n_exp, V7X_LANES), F32),
            jax.ShapeDtypeStruct((t, V7X_LANES), F32))


def _mix_out_specs(rows, d, n_exp, row, tok_col):
    return (pl.BlockSpec((rows, d), row),
            pl.BlockSpec((rows, d // 2), row),
            pl.BlockSpec((TOP_K, rows), tok_col),
            pl.BlockSpec((TOP_K, rows), tok_col),
            pl.BlockSpec((n_exp, V7X_LANES), lambda *_: (0, 0)),
            pl.BlockSpec((rows, V7X_LANES), row))


def _strict_upper_tri(n):
    return (lax.broadcasted_iota(jnp.int32, (n, n), 0) < lax.broadcasted_iota(jnp.int32, (n, n), 1)).astype(BF16)


def _pool_layer(h, mem, w_in, wbd, pscale, mem_kv, w_out, ln_g, ln_b, rwt, rb, dn_alpha, seq):
    t, d = h.shape
    n_mem = mem.shape[1]
    pool_width = wbd.shape[0]
    ts = SEQ_BLOCK
    nsb = seq // ts
    row = lambda b, s: (b * nsb + s, 0)
    n_exp = rwt.shape[0]
    tri = _strict_upper_tri(ts)
    return pl.pallas_call(
        functools.partial(_pool_layer_kernel, dn_alpha),
        out_shape=_mix_out_shapes(t, d, n_exp),
        grid=(t // seq, nsb),
        in_specs=[
            pl.BlockSpec((ts, d), row),
            pl.BlockSpec((1, n_mem, d), lambda b, s: (b, 0, 0)),
            _const_spec(w_in.shape), _const_spec(wbd.shape), _const_spec(pscale.shape), _const_spec(mem_kv.shape),
            _const_spec(w_out.shape), _const_spec(ln_g.shape), _const_spec(ln_b.shape), _const_spec(rwt.shape),
            _const_spec(rb.shape), _const_spec(tri.shape),
        ],
        out_specs=_mix_out_specs(ts, d, n_exp, row, lambda b, s: (0, b * nsb + s)),
        scratch_shapes=[pltpu.VMEM((ts + POOL_HISTORY, pool_width), F32)] * 4
        + [pltpu.VMEM((n_mem, 2 * MEM_WIDTH), BF16), pltpu.VMEM((ts, d), BF16), pltpu.VMEM((n_exp, V7X_LANES), F32)],
        compiler_params=_tc_params(2),
        name="pool_layer",
    )(h, mem, w_in, wbd, pscale, mem_kv, w_out, ln_g, ln_b, rwt, rb, tri)


def _expand_heads(w, n_heads):
    d = w.shape[0]
    w3 = w.reshape(d, n_heads, HEAD_DIM)
    return jnp.pad(w3, ((0, 0), (0, 0), (0, V7X_LANES - HEAD_DIM))).reshape(d, n_heads * V7X_LANES)


def _shared_kv(h, kvw, fb, sel_k, seq, n_heads):
    t, d = h.shape
    ts = SEQ_BLOCK
    nsb = seq // ts
    k_width = n_heads * V7X_LANES
    v_width = n_heads * HEAD_DIM
    tri = (lax.broadcasted_iota(jnp.int32, (ts, ts), 1) <= lax.broadcasted_iota(jnp.int32, (ts, ts), 0)).astype(BF16)
    row = lambda b, s: (b * nsb + s, 0)
    return pl.pallas_call(
        _shared_kv_kernel,
        out_shape=(jax.ShapeDtypeStruct((t, k_width), BF16),
                   jax.ShapeDtypeStruct((t // seq, v_width, seq), BF16),
                   jax.ShapeDtypeStruct((t, V7X_LANES), BF16)),
        grid=(t // seq, nsb),
        in_specs=[pl.BlockSpec((ts, d), row), _const_spec(kvw.shape), _const_spec(fb.shape), _const_spec(tri.shape),
                  _const_spec(sel_k.shape)],
        out_specs=(pl.BlockSpec((ts, k_width), row),
                   pl.BlockSpec((1, v_width, ts), lambda b, s: (b, 0, s)),
                   pl.BlockSpec((ts, V7X_LANES), row)),
        scratch_shapes=[pltpu.VMEM((8, V7X_LANES), F32)],
        compiler_params=_tc_params(2),
        name="shared_kv",
    )(h, kvw, fb, tri, sel_k)


def _fox_layer(h, mem, k_aug, v_t, cq, w_in, mem_kv, w_out, ln_g, ln_b, rwt, rb, dn_alpha, seq):
    t, d = h.shape
    n_mem = mem.shape[1]
    k_width = k_aug.shape[1]
    v_width = v_t.shape[1]
    n_heads = k_width // V7X_LANES
    tq = Q_BLOCK
    nqb = seq // tq
    n_batch = t // seq
    row = lambda b, i: (b * nqb + i, 0)
    per_batch = lambda b, i: (b, 0, 0)
    n_exp = rwt.shape[0]
    tri = _strict_upper_tri(tq)
    return pl.pallas_call(
        functools.partial(_fox_layer_kernel, dn_alpha),
        out_shape=_mix_out_shapes(t, d, n_exp),
        grid=(n_batch, nqb),
        in_specs=[
            pl.BlockSpec((tq, d), row),
            pl.BlockSpec((1, n_mem, d), per_batch),
            pl.BlockSpec((1, seq, k_width), per_batch),
            pl.BlockSpec((1, v_width, seq), per_batch),
            pl.BlockSpec((tq, V7X_LANES), row),
            _const_spec(w_in.shape), _const_spec(mem_kv.shape), _const_spec(w_out.shape), _const_spec(ln_g.shape),
            _const_spec(ln_b.shape), _const_spec(rwt.shape), _const_spec(rb.shape), _const_spec(tri.shape),
        ],
        out_specs=_mix_out_specs(tq, d, n_exp, row, lambda b, i: (0, b * nqb + i)),
        scratch_shapes=[pltpu.VMEM((n_mem, 2 * MEM_WIDTH), BF16), pltpu.VMEM((tq, d), BF16),
                        pltpu.VMEM((n_heads, tq, V7X_LANES), BF16),
                        pltpu.VMEM((16, tq), F32), pltpu.VMEM((16, tq), F32), pltpu.VMEM((v_width, tq), F32),
                        pltpu.VMEM((n_exp, V7X_LANES), F32)],
        compiler_params=_tc_params(2),
        name="fox_layer",
    )(h, mem, k_aug.reshape(n_batch, seq, k_width), v_t, cq, w_in, mem_kv, w_out, ln_g, ln_b, rwt, rb, tri)


def _slot_kernel(pad_start_ref, idx_ref, rank_ref, pos_ref):
    pos = rank_ref[...]
    idx = idx_ref[...]
    for e in range(pad_start_ref.shape[0]):
        pos = pos + jnp.where(idx == e, pad_start_ref[e], 0)
    pos_ref[...] = pos


def _route(idx_t, rank_t, counts, n_experts):
    top_k, t = idx_t.shape
    n_assign = top_k * t
    padded = ((counts + MOE_BLOCK - 1) // MOE_BLOCK) * MOE_BLOCK
    pad_end = jnp.cumsum(padded)
    pad_start = pad_end - padded
    n_blocks = n_assign // MOE_BLOCK + n_experts
    block_start = jnp.arange(n_blocks, dtype=jnp.int32) * MOE_BLOCK
    block_expert = jnp.minimum(jnp.sum((pad_end[None, :] <= block_start[:, None]).astype(jnp.int32), axis=1),
                               n_experts - 1)
    block_valid = jnp.clip(counts[block_expert] - (block_start - pad_start[block_expert]), 0, MOE_BLOCK)
    n_active = (pad_end[-1:] // MOE_BLOCK).astype(jnp.int32)
    expert_ids = jnp.arange(n_experts, dtype=jnp.int32)
    in_use = counts > 0
    expert_slot = (jnp.cumsum(in_use.astype(jnp.int32)) - in_use.astype(jnp.int32)) % 2
    later = lax.cummin(jnp.where(in_use, expert_ids, n_experts), axis=0, reverse=True)
    next_in_use = jnp.concatenate([later[1:], jnp.full((1,), n_experts, jnp.int32)])
    next_in_use = jnp.where(next_in_use < n_experts, next_in_use, -1)
    block_slot = expert_slot[block_expert].astype(jnp.int32)
    block_next = next_in_use[block_expert].astype(jnp.int32)
    cols = min(SLOT_COLS, t)
    pos = pl.pallas_call(
        _slot_kernel,
        out_shape=jax.ShapeDtypeStruct((top_k, t), jnp.int32),
        grid_spec=pltpu.PrefetchScalarGridSpec(
            num_scalar_prefetch=1,
            grid=(t // cols,),
            in_specs=[pl.BlockSpec((top_k, cols), lambda i, ps: (0, i))] * 2,
            out_specs=pl.BlockSpec((top_k, cols), lambda i, ps: (0, i)),
        ),
        compiler_params=_tc_params(1),
        name="slots",
    )(pad_start.astype(jnp.int32), idx_t, rank_t)
    return pos, (block_expert, block_valid.astype(jnp.int32), block_slot, block_next, n_active)


def _experts(xs, block_expert, block_valid, block_slot, block_next, n_active, w1_all, b1, w2_all, b2, layer):
    n_slots, half_d = xs.shape
    _, n_exp, d, two_f = w1_all.shape
    d_ff = two_f // 2
    assert d == 2 * half_d
    block = lambda i, *_: (i, 0)
    per_expert = lambda i, be, *_: (be[i], 0, 0)
    grid_spec = pltpu.PrefetchScalarGridSpec(
        num_scalar_prefetch=5,
        grid=(n_slots // MOE_BLOCK,),
        in_specs=[
            pl.BlockSpec((MOE_BLOCK, half_d), block),
            pl.BlockSpec(memory_space=pl.ANY),
            pl.BlockSpec((1, 1, two_f), per_expert),
            pl.BlockSpec(memory_space=pl.ANY),
            pl.BlockSpec((1, 1, d), per_expert),
        ],
        out_specs=pl.BlockSpec((MOE_BLOCK, half_d), block),
        scratch_shapes=[pltpu.VMEM((d, two_f), BF16), pltpu.VMEM((d_ff, d), BF16), pltpu.VMEM((MOE_BLOCK, d_ff), BF16),
                        pltpu.VMEM((2, d, two_f), F32), pltpu.VMEM((2, d_ff, d), F32),
                        pltpu.SemaphoreType.DMA((2, 2))],
    )
    return pl.pallas_call(
        functools.partial(_expert_kernel, layer),
        out_shape=jax.ShapeDtypeStruct((n_slots, half_d), jnp.uint32),
        grid_spec=grid_spec,
        compiler_params=_tc_params(1),
        name="experts",
    )(block_expert, block_valid, block_slot, block_next, n_active, xs, w1_all, b1.reshape(n_exp, 1, two_f), w2_all,
      b2.reshape(n_exp, 1, d))


def _combine(h, yg, gcol, ln_g, ln_b, dn_alpha):
    t, d = h.shape
    ts = SEQ_BLOCK
    row = lambda i: (i, 0)
    return pl.pallas_call(
        functools.partial(_combine_kernel, dn_alpha),
        out_shape=jax.ShapeDtypeStruct((t, d), F32),
        grid=(t // ts,),
        in_specs=[pl.BlockSpec((ts, d), row), pl.BlockSpec((TOP_K, ts, d // 2), lambda i: (0, i, 0)),
                  pl.BlockSpec((ts, V7X_LANES), row), _const_spec(ln_g.shape), _const_spec(ln_b.shape)],
        out_specs=pl.BlockSpec((ts, d), row),
        compiler_params=_tc_params(1),
        name="combine",
    )(h, yg.reshape(TOP_K, t, d // 2), gcol, ln_g, ln_b)


def _moe(mixed, w1_all, b1, w2_all, b2, layer, ln_g, ln_b, dn_alpha):
    h1, h1_packed, idx_t, rank_t, counts, gcol = mixed
    n_exp = w1_all.shape[1]
    d_ff = w2_all.shape[2]
    pos, block_meta = _route(idx_t, rank_t, counts[:, 0].astype(jnp.int32), n_exp)
    b1p = jnp.swapaxes(b1.reshape(n_exp, d_ff // V7X_LANES, V7X_LANES, 2), 2, 3).reshape(n_exp, 2 * d_ff)
    xs = _sc_scatter_rows(h1_packed, pos, block_meta[0].shape[0] * MOE_BLOCK)
    y = _experts(xs, *block_meta, w1_all, b1p, w2_all, b2, layer)
    yg = _sc_gather_rows(y, pos.reshape(-1))
    return _combine(h1, yg, gcol, ln_g, ln_b, dn_alpha)


def kernel(x, mem, ln1_g, ln1_b, ln2_g, ln2_b, a_w_in, pool_w, pool_scale, a_mem_kv, a_w_out, kv_w, fgate_b, b_w_in, b_mem_kv, b_w_out, router_w, router_b, moe_w1, moe_b1, moe_w2, moe_b2):
    n_batch, seq, d = x.shape
    t = n_batch * seq
    depth = ln1_g.shape[0]
    n_a = a_w_in.shape[0]
    dn_alpha = float((2 * depth) ** 0.25)
    n_exp = router_w.shape[2]
    fox_heads = fgate_b.shape[0]
    fox_width = fox_heads * HEAD_DIM
    assert seq % SEQ_BLOCK == 0 and seq % Q_BLOCK == 0 and fox_heads <= 16

    h = x.reshape(t, d)
    row = lambda a: a.reshape(1, -1)
    shared = None
    for l in range(depth):
        rwt = router_w[l].T
        rb = router_b[l].reshape(n_exp, 1)
        if l < n_a:
            wbd = jax.scipy.linalg.block_diag(*[pool_w[l, g] for g in range(pool_w.shape[1])]).astype(BF16)
            mixed = _pool_layer(
                h, mem, a_w_in[l].astype(BF16), wbd, row(pool_scale[l]), a_mem_kv[l].astype(BF16),
                a_w_out[l].astype(BF16), row(ln1_g[l]), row(ln1_b[l]), rwt, rb, dn_alpha, seq)
        else:
            j = l - n_a
            if shared is None:
                pad = V7X_LANES - 3 * fox_heads
                kvw = jnp.concatenate([_expand_heads(kv_w[:, :fox_width], fox_heads),
                                       kv_w[:, fox_width:2 * fox_width],
                                       jnp.pad(jnp.tile(kv_w[:, 2 * fox_width:], (1, 3)), ((0, 0), (0, pad)))],
                                      axis=1).astype(BF16)
                fb = jnp.pad(jnp.tile(fgate_b, 3), (0, pad)).reshape(1, V7X_LANES)
                sel_k = _key_bias_selector(fox_heads)
                shared = _shared_kv(h, kvw, fb, sel_k, seq, fox_heads)
            k_aug, v_t, cq = shared
            w_in = jnp.concatenate([_expand_heads(b_w_in[j][:, :fox_width], fox_heads),
                                    b_w_in[j][:, fox_width:]], axis=1).astype(BF16)
            mixed = _fox_layer(
                h, mem, k_aug, v_t, cq, w_in, b_mem_kv[j].astype(BF16),
                b_w_out[j].astype(BF16), row(ln1_g[l]), row(ln1_b[l]), rwt, rb, dn_alpha, seq)
        h = _moe(mixed, moe_w1, moe_b1[l], moe_w2, moe_b2[l], l, row(ln2_g[l]), row(ln2_b[l]), dn_alpha)
    return h.reshape(n_batch, seq, d)
```

```python
import functools
import numpy as np
import jax
import jax.numpy as jnp
from jax import lax
from jax.experimental import pallas as pl
from jax.experimental.pallas import tpu as pltpu
from jax.experimental.pallas import tpu_sc as plsc

HEAD_DIM = 64
MEM_HEADS = 4
MEM_WIDTH = MEM_HEADS * HEAD_DIM
POOL_WINDOWS = (2, 4, 8, 16)
TOP_K = 4
SWIGLU_ALPHA = 1.702
SWIGLU_LIMIT = 7.0
MOE_BLOCK = 512
LN_EPS = 1e-5
ATTN_SCALE = HEAD_DIM ** -0.5
LOG2_E = 1.4426950408889634

V7X_LANES = 128
V7X_VMEM_LIMIT_BYTES = 56 * 1024 * 1024
POOL_HISTORY = 32
SC_CHUNK = 64
SEQ_BLOCK = 512
Q_BLOCK = 512
SLOT_COLS = 8192

F32 = jnp.float32
BF16 = jnp.bfloat16
NEG_INF = float("-inf")


def _dot(a, b):
    return jnp.dot(a, b, preferred_element_type=F32)


def _dot_nt(a, b):
    return lax.dot_general(a, b, (((1,), (1,)), ((), ())), preferred_element_type=F32)


def _lane_iota(shape):
    return lax.broadcasted_iota(jnp.int32, shape, len(shape) - 1)


def _pair_attention(q2, keys, vals, bias=None):
    lane = _lane_iota(q2.shape)
    first = lane < HEAD_DIM
    outs = []
    for sel in (first, jnp.logical_not(first)):
        qh = jnp.where(sel, q2, 0.0).astype(BF16)
        s = _dot_nt(qh, keys)
        m = jnp.max(s, axis=-1, keepdims=True)
        p = jnp.exp(s - m)
        l = jnp.sum(p, axis=-1, keepdims=True)
        outs.append(_dot(p.astype(BF16), vals) / l)
    return jnp.where(first, outs[0], outs[1])


def _memory_attention(proj, kv_ref, cat_ref, out_off, q_off):
    for p in range(MEM_HEADS // 2):
        lo = p * V7X_LANES
        q2 = proj[:, q_off + lo:q_off + lo + V7X_LANES] * ATTN_SCALE
        keys = kv_ref[:, lo:lo + V7X_LANES]
        vals = kv_ref[:, MEM_WIDTH + lo:MEM_WIDTH + lo + V7X_LANES]
        cat_ref[:, out_off + lo:out_off + lo + V7X_LANES] = _pair_attention(q2, keys, vals).astype(BF16)


def _layer_norm(z, g, b):
    mu = jnp.mean(z, axis=-1, keepdims=True)
    zc = z - mu
    var = jnp.mean(zc * zc, axis=-1, keepdims=True)
    return zc * lax.rsqrt(var + LN_EPS) * g + b


def _pack_bf16_pairs(x):
    w = x.shape[1] // 2
    bits = lax.bitcast_convert_type(x.astype(BF16).astype(F32), jnp.uint32)
    return lax.shift_right_logical(bits[:, :w], jnp.uint32(16)) | (bits[:, w:] & jnp.uint32(0xFFFF0000))


def _unpack_bf16_pairs(p):
    low = lax.bitcast_convert_type(lax.shift_left(p, jnp.uint32(16)), F32)
    high = lax.bitcast_convert_type(p & jnp.uint32(0xFFFF0000), F32)
    return low, high


def _expert_ranks(idxs, tri_ref, rank_ref, counts_ref, base_ref):
    @pl.when(jnp.logical_and(pl.program_id(0) == 0, pl.program_id(1) == 0))
    def _():
        base_ref[...] = jnp.zeros(base_ref.shape, F32)

    n_exp = base_ref.shape[0]
    t = idxs[0].shape[1]
    rows = lax.broadcasted_iota(jnp.int32, (n_exp, t), 0)
    running = base_ref[:, 0:1]
    for k in range(TOP_K):
        onehot = rows == idxs[k]
        prefix = _dot(jnp.where(onehot, 1.0, 0.0).astype(BF16), tri_ref[...])
        rank = jnp.sum(jnp.where(onehot, prefix + running, 0.0), axis=0, keepdims=True)
        rank_ref[k:k + 1, :] = rank.astype(jnp.int32)
        running = running + jnp.sum(jnp.where(onehot, 1.0, 0.0), axis=1, keepdims=True)
    base_ref[...] = jnp.broadcast_to(running, base_ref.shape)
    counts_ref[...] = base_ref[...]


def _mix_epilogue(x, cat_ref, w_out_ref, g_ref, b_ref, rwt_ref, rb_ref, tri_ref, dn_alpha, h_out_ref, hp_ref, idx_ref,
                  rank_ref, counts_ref, gcol_ref, base_ref):
    mix = _dot(cat_ref[...], w_out_ref[...])
    h1 = _layer_norm(dn_alpha * x + mix, g_ref[...], b_ref[...])
    h_hi = h1.astype(BF16)
    h_lo = (h1 - h_hi.astype(F32)).astype(BF16)
    rw = rwt_ref[...]
    n_exp = rw.shape[0]
    rw_hi = rw.astype(BF16)
    rw_lo = (rw - rw_hi.astype(F32)).astype(BF16)
    both = _dot_nt(jnp.concatenate([rw_hi, rw_lo], axis=0), h_hi)
    logits_all = both[:n_exp] + both[n_exp:] + _dot_nt(rw_hi, h_lo) + rb_ref[...]
    h_out_ref[...] = h1
    hp_ref[...] = _pack_bf16_pairs(h1)
    t = logits_all.shape[1]
    half = t // 2
    idxs_halves, gates_halves = [], []
    for part in range(2):
        logits = logits_all[:, part * half:(part + 1) * half]
        rows = lax.broadcasted_iota(jnp.int32, (n_exp, half), 0)
        vals, idxs = [], []
        for _ in range(TOP_K):
            m = jnp.max(logits, axis=0, keepdims=True)
            ix = jnp.min(jnp.where(logits == m, rows, n_exp), axis=0, keepdims=True)
            vals.append(m)
            idxs.append(ix)
            logits = jnp.where(rows == ix, NEG_INF, logits)
        exps = [jnp.exp(v - vals[0]) for v in vals]
        denom = exps[0] + exps[1] + exps[2] + exps[3]
        idxs_halves.append(idxs)
        gates_halves.append([e / denom for e in exps])
    idxs = [jnp.concatenate([idxs_halves[0][k], idxs_halves[1][k]], axis=1) for k in range(TOP_K)]
    gates = [jnp.concatenate([gates_halves[0][k], gates_halves[1][k]], axis=1) for k in range(TOP_K)]
    for k in range(TOP_K):
        idx_ref[k:k + 1, :] = idxs[k]
    _expert_ranks(idxs, tri_ref, rank_ref, counts_ref, base_ref)
    lane_rows = lax.broadcasted_iota(jnp.int32, (V7X_LANES, t), 0)
    gates_t = jnp.zeros((V7X_LANES, t), F32)
    for k in range(TOP_K):
        gates_t = jnp.where(lane_rows == k, gates[k], gates_t)
    gcol_ref[...] = gates_t.T


def _pool_layer_kernel(dn_alpha, h_ref, mem_ref, w_in_ref, wbd_ref, pscale_ref, mem_kv_ref, w_out_ref, g_ref, b_ref,
                       rwt_ref, rb_ref, tri_ref, h_out_ref, hp_ref, idx_ref, rank_ref, counts_ref, gcol_ref,
                       ue_ref, w2_ref, w4_ref, w8_ref, kv_ref, cat_ref, base_ref):
    s = pl.program_id(1)
    ts = h_ref.shape[0]
    pool_width = wbd_ref.shape[0]
    hist = POOL_HISTORY

    @pl.when(s == 0)
    def _():
        ue_ref[0:hist, :] = jnp.zeros((hist, pool_width), F32)
        kv_ref[...] = _dot(mem_ref[0].astype(BF16), mem_kv_ref[...]).astype(BF16)

    x = h_ref[...]
    proj = _dot(x.astype(BF16), w_in_ref[...])
    u = proj[:, :pool_width]
    ue_ref[hist:, :] = u
    n = ts + hist
    w2_ref[8:n, :] = ue_ref[8:n, :] + ue_ref[7:n - 1, :]
    w4_ref[16:n, :] = w2_ref[16:n, :] + w2_ref[14:n - 2, :]
    w8_ref[24:n, :] = w4_ref[24:n, :] + w4_ref[20:n - 4, :]
    w16 = w8_ref[hist:n, :] + w8_ref[hist - 8:n - 8, :]
    group_dim = pool_width // len(POOL_WINDOWS)
    col = _lane_iota((ts, pool_width))
    wsum = jnp.where(col < group_dim, w2_ref[hist:n, :],
                     jnp.where(col < 2 * group_dim, w4_ref[hist:n, :],
                               jnp.where(col < 3 * group_dim, w8_ref[hist:n, :], w16)))
    window = jnp.where(col < group_dim, POOL_WINDOWS[0],
                       jnp.where(col < 2 * group_dim, POOL_WINDOWS[1],
                                 jnp.where(col < 3 * group_dim, POOL_WINDOWS[2], POOL_WINDOWS[3])))
    tpos = s * ts + lax.broadcasted_iota(jnp.int32, (ts, pool_width), 0) + 1
    cnt = jnp.minimum(tpos, window).astype(F32)
    d = wsum / cnt - u
    ue_ref[0:hist, :] = u[ts - hist:, :]
    pooled = _dot(d.astype(BF16), wbd_ref[...]) * pscale_ref[...]
    cat_ref[:, :pool_width] = pooled.astype(BF16)
    _memory_attention(proj, kv_ref, cat_ref, pool_width, pool_width)
    _mix_epilogue(x, cat_ref, w_out_ref, g_ref, b_ref, rwt_ref, rb_ref, tri_ref, dn_alpha, h_out_ref, hp_ref, idx_ref,
                  rank_ref, counts_ref, gcol_ref, base_ref)


def _bias_pieces(c):
    hi = c.astype(BF16).astype(F32)
    r = c - hi
    mid = r.astype(BF16).astype(F32)
    lo = (r - mid).astype(BF16).astype(F32)
    return hi, mid, lo


def _bias_lane_pieces(c3, n_heads):
    hi, mid, lo = _bias_pieces(c3)
    lane = _lane_iota(c3.shape)
    out = jnp.where(lane < n_heads, hi, jnp.where(lane < 2 * n_heads, mid, jnp.where(lane < 3 * n_heads, lo,
                    jnp.where(lane == 3 * n_heads, 1.0, 0.0))))
    return out.astype(BF16)


def _key_bias_selector(n_heads):
    sel = np.zeros((V7X_LANES, n_heads * V7X_LANES), np.float32)
    for h in range(n_heads):
        spare = h * V7X_LANES + HEAD_DIM
        for p in range(3):
            sel[3 * n_heads, spare + p] = 1.0
            sel[p * n_heads + h, spare + 3 + p] = -1.0
    return jnp.asarray(sel, BF16)


def _shared_kv_kernel(h_ref, kvw_ref, fb_ref, tri_ref, sel_ref, k_ref, vt_ref, cq_ref, carry_ref):
    s = pl.program_id(1)
    k_width = k_ref.shape[1]
    v_width = vt_ref.shape[1]
    n_heads = k_width // V7X_LANES

    @pl.when(s == 0)
    def _():
        carry_ref[...] = jnp.zeros(carry_ref.shape, F32)

    proj = _dot(h_ref[...].astype(BF16), kvw_ref[...])
    z = proj[:, k_width + v_width:] + fb_ref[...]
    logf = jnp.minimum(z, 0.0) - jnp.log(1.0 + jnp.exp(-jnp.abs(z)))
    hi, mid, lo = _bias_pieces(logf)
    tri = tri_ref[...]
    c = _dot(tri, hi.astype(BF16)) + _dot(tri, mid.astype(BF16)) + _dot(tri, lo.astype(BF16)) + carry_ref[0:1, :]
    ts = c.shape[0]
    carry_ref[0:1, :] = c[ts - 1:ts, :]
    pieces = _bias_lane_pieces(c * LOG2_E, n_heads)
    cq_ref[...] = pieces
    k_ref[...] = (proj[:, :k_width] + _dot(pieces, sel_ref[...])).astype(BF16)
    vt_ref[0] = proj[:, k_width:k_width + v_width].T.astype(BF16)


def _fox_layer_kernel(dn_alpha, h_ref, mem_ref, k_ref, vt_ref, cq_ref, w_in_ref, mem_kv_ref, w_out_ref,
                      g_ref, b_ref, rwt_ref, rb_ref, tri_ref, h_out_ref, hp_ref, idx_ref, rank_ref, counts_ref, gcol_ref,
                      kv_ref, cat_ref, qa_ref, m_ref, l_ref, acc_ref, base_ref):
    i = pl.program_id(1)
    tq = h_ref.shape[0]
    n_heads = k_ref.shape[2] // V7X_LANES
    fox_width = vt_ref.shape[1]

    @pl.when(i == 0)
    def _():
        kv_ref[...] = _dot(mem_ref[0].astype(BF16), mem_kv_ref[...]).astype(BF16)

    x = h_ref[...]
    proj = _dot(x.astype(BF16), w_in_ref[...])
    cq = cq_ref[...].astype(F32)
    lane = _lane_iota((tq, V7X_LANES))
    for h in range(n_heads):
        cols = slice(h * V7X_LANES, (h + 1) * V7X_LANES)
        piece = [cq[:, p * n_heads + h:p * n_heads + h + 1] for p in range(3)]
        spare = jnp.where(lane == HEAD_DIM, piece[0], jnp.where(lane == HEAD_DIM + 1, piece[1],
                          jnp.where(lane == HEAD_DIM + 2, piece[2], jnp.where(lane < HEAD_DIM + 6, 1.0, 0.0))))
        qa_ref[h] = jnp.where(lane < HEAD_DIM, proj[:, cols] * (ATTN_SCALE * LOG2_E), spare).astype(BF16)
    m_ref[...] = jnp.full(m_ref.shape, NEG_INF, F32)
    l_ref[...] = jnp.zeros(l_ref.shape, F32)
    acc_ref[...] = jnp.zeros(acc_ref.shape, F32)
    causal_t = lax.broadcasted_iota(jnp.int32, (tq, tq), 0) <= lax.broadcasted_iota(jnp.int32, (tq, tq), 1)

    ones_rows = (lax.broadcasted_iota(jnp.int32, (16, tq), 0) == 0).astype(BF16)

    def tile(j, masked):
        start = pl.multiple_of(j * tq, tq)

        def scores(h):
            kb = k_ref[0, pl.ds(start, tq), h * V7X_LANES:(h + 1) * V7X_LANES]
            return _dot_nt(kb, qa_ref[h])

        st_next = scores(0)
        for h in range(n_heads):
            st = st_next
            if h + 1 < n_heads:
                st_next = scores(h + 1)
            if masked:
                st = jnp.where(causal_t, st, NEG_INF)
            m_old = m_ref[h:h + 1, :]
            m_new = jnp.maximum(m_old, jnp.max(st, axis=0, keepdims=True))
            alpha = jnp.exp2(m_old - m_new)
            pt = jnp.exp2(st - m_new).astype(BF16)
            m_ref[h:h + 1, :] = m_new
            rows = slice(h * HEAD_DIM, (h + 1) * HEAD_DIM)
            vb = jnp.concatenate([vt_ref[0, rows, pl.ds(start, tq)], ones_rows], axis=0)
            pv = _dot(vb, pt)
            acc_ref[rows, :] = alpha * acc_ref[rows, :] + pv[:HEAD_DIM]
            l_ref[h:h + 1, :] = alpha * l_ref[h:h + 1, :] + pv[HEAD_DIM:HEAD_DIM + 1]

    tile(i, True)

    def off_diagonal(j, carry):
        tile(j, False)
        return carry

    lax.fori_loop(0, i, off_diagonal, 0)
    for h in range(n_heads):
        rows = slice(h * HEAD_DIM, (h + 1) * HEAD_DIM)
        acc_ref[rows, :] = acc_ref[rows, :] / l_ref[h:h + 1, :]
    cat_ref[:, :fox_width] = acc_ref[...].T.astype(BF16)
    _memory_attention(proj, kv_ref, cat_ref, fox_width, n_heads * V7X_LANES)
    _mix_epilogue(x, cat_ref, w_out_ref, g_ref, b_ref, rwt_ref, rb_ref, tri_ref, dn_alpha, h_out_ref, hp_ref, idx_ref,
                  rank_ref, counts_ref, gcol_ref, base_ref)


def _expert_kernel(layer, be_ref, bv_ref, bslot_ref, bnext_ref, nact_ref, x_ref, w1_hbm, b1_ref, w2_hbm, b2_ref, y_ref,
                   w1s_ref, w2s_ref, act_ref, w1f_ref, w2f_ref, sem_ref):
    i = pl.program_id(0)
    active = i < nact_ref[0]
    two_f = w1f_ref.shape[2]
    group = 2 * V7X_LANES
    n_groups = two_f // group

    def weight_copies(expert, slot):
        return (pltpu.make_async_copy(w1_hbm.at[layer, expert], w1f_ref.at[slot], sem_ref.at[0, slot]),
                pltpu.make_async_copy(w2_hbm.at[layer, expert], w2f_ref.at[slot], sem_ref.at[1, slot]))

    @pl.when(i == 0)
    def _():
        for cp in weight_copies(be_ref[0], bslot_ref[0]):
            cp.start()

    @pl.when(jnp.logical_and(active, jnp.logical_or(i == 0, be_ref[i] != be_ref[jnp.maximum(i - 1, 0)])))
    def _():
        slot = bslot_ref[i]
        for cp in weight_copies(be_ref[i], slot):
            cp.wait()

        @pl.when(bnext_ref[i] >= 0)
        def _():
            for cp in weight_copies(bnext_ref[i], 1 - slot):
                cp.start()

        src = lax.broadcasted_iota(jnp.int32, (group, group), 0)
        dst = lax.broadcasted_iota(jnp.int32, (group, group), 1)
        want = jnp.where(dst < V7X_LANES, 2 * dst, 2 * (dst - V7X_LANES) + 1)
        perm = (src == want).astype(BF16)
        for g in range(n_groups):
            cols = slice(g * group, (g + 1) * group)
            w1s_ref[:, cols] = _dot(w1f_ref[slot, :, cols].astype(BF16), perm).astype(BF16)
        w2s_ref[...] = w2f_ref[slot].astype(BF16)

    @pl.when(active)
    def _():
        row_id = lax.broadcasted_iota(jnp.int32, x_ref.shape, 0)
        x_low, x_high = _unpack_bf16_pairs(jnp.where(row_id < bv_ref[i], x_ref[...], jnp.uint32(0)))
        xb = jnp.concatenate([x_low.astype(BF16), x_high.astype(BF16)], axis=1)
        hdn = _dot(xb, w1s_ref[...]) + b1_ref[0]
        for g in range(n_groups):
            x_glu = jnp.minimum(hdn[:, g * group:g * group + V7X_LANES], SWIGLU_LIMIT)
            x_lin = jnp.clip(hdn[:, g * group + V7X_LANES:(g + 1) * group], -SWIGLU_LIMIT, SWIGLU_LIMIT)
            act = x_glu * jax.nn.sigmoid(SWIGLU_ALPHA * x_glu) * (x_lin + 1.0)
            act_ref[:, g * V7X_LANES:(g + 1) * V7X_LANES] = act.astype(BF16)
        y_ref[...] = _pack_bf16_pairs(_dot(act_ref[...], w2s_ref[...]) + b2_ref[0])

    @pl.when(jnp.logical_not(active))
    def _():
        y_ref[...] = jnp.zeros(y_ref.shape, jnp.uint32)


def _combine_kernel(dn_alpha, h_ref, y_ref, gcol_ref, g_ref, b_ref, o_ref):
    gates = gcol_ref[...]
    low, high = _unpack_bf16_pairs(y_ref[0])
    low, high = gates[:, 0:1] * low, gates[:, 0:1] * high
    for k in range(1, TOP_K):
        low_k, high_k = _unpack_bf16_pairs(y_ref[k])
        low = low + gates[:, k:k + 1] * low_k
        high = high + gates[:, k:k + 1] * high_k
    ffn = jnp.concatenate([low, high], axis=1)
    o_ref[...] = _layer_norm(dn_alpha * h_ref[...] + ffn, g_ref[...], b_ref[...])


def _sc_gather_rows(table, idx):
    n, = idx.shape
    _, d = table.shape
    info = plsc.get_sparse_core_info()
    n_workers = info.num_cores * info.num_subcores
    per_worker = n // n_workers
    n_chunks = per_worker // SC_CHUNK
    assert n % (n_workers * SC_CHUNK) == 0 and n_chunks % 2 == 0
    mesh = plsc.VectorSubcoreMesh(core_axis_name="core", subcore_axis_name="subcore")

    @functools.partial(
        pl.kernel,
        out_type=jax.ShapeDtypeStruct((n, d), table.dtype),
        mesh=mesh,
        scratch_types=[
            pltpu.VMEM((per_worker,), jnp.int32),
            pltpu.VMEM((2, SC_CHUNK, d), table.dtype),
            pltpu.SemaphoreType.DMA((2,)),
            pltpu.SemaphoreType.DMA((2,)),
        ],
    )
    def gather_kernel(table_hbm, idx_hbm, out_hbm, idx_v, rows_v, gather_sem, write_sem):
        worker = lax.axis_index("subcore") * info.num_cores + lax.axis_index("core")
        base = pl.multiple_of(worker * per_worker, SC_CHUNK)
        pltpu.sync_copy(idx_hbm.at[pl.ds(base, per_worker)], idx_v)

        def gather(c, buf):
            rows = idx_v.at[pl.ds(pl.multiple_of(c * SC_CHUNK, SC_CHUNK), SC_CHUNK)]
            return pltpu.make_async_copy(table_hbm.at[rows], rows_v.at[buf], gather_sem.at[buf])

        def write(c, buf):
            off = pl.multiple_of(base + c * SC_CHUNK, SC_CHUNK)
            return pltpu.make_async_copy(rows_v.at[buf], out_hbm.at[pl.ds(off, SC_CHUNK)], write_sem.at[buf])

        gather(0, 0).start()

        @pl.loop(0, n_chunks, step=2)
        def _(c0):
            for buf in range(2):
                c = c0 + buf
                gather(c, buf).wait()

                @pl.when(c >= 1)
                def _():
                    write(c - 1, 1 - buf).wait()

                @pl.when(c + 1 < n_chunks)
                def _():
                    gather(c + 1, 1 - buf).start()

                write(c, buf).start()

        write(n_chunks - 1, 1).wait()

    return gather_kernel(table, idx)


def _sc_scatter_rows(rows, pos, n_out):
    t, d = rows.shape
    top_k = pos.shape[0]
    pos_flat = pos.reshape(-1)
    info = plsc.get_sparse_core_info()
    n_workers = info.num_cores * info.num_subcores
    per_worker = t // n_workers
    n_chunks = per_worker // SC_CHUNK
    assert t % (n_workers * SC_CHUNK) == 0 and n_chunks % 2 == 0
    mesh = plsc.VectorSubcoreMesh(core_axis_name="core", subcore_axis_name="subcore")

    @functools.partial(
        pl.kernel,
        out_type=jax.ShapeDtypeStruct((n_out, d), rows.dtype),
        mesh=mesh,
        scratch_types=[pltpu.VMEM((2, SC_CHUNK, d), rows.dtype)]
        + [pltpu.VMEM((SC_CHUNK,), jnp.int32) for _ in range(2 * top_k)]
        + [pltpu.SemaphoreType.DMA((2,)), pltpu.SemaphoreType.DMA((2,))],
    )
    def scatter_kernel(rows_hbm, pos_hbm, out_hbm, rows_v, *rest):
        idx_bufs = rest[:2 * top_k]
        load_sem, write_sem = rest[2 * top_k:]
        worker = lax.axis_index("subcore") * info.num_cores + lax.axis_index("core")
        base = pl.multiple_of(worker * per_worker, SC_CHUNK)

        def loads(c, buf):
            off = pl.multiple_of(base + c * SC_CHUNK, SC_CHUNK)
            copies = [pltpu.make_async_copy(rows_hbm.at[pl.ds(off, SC_CHUNK)], rows_v.at[buf], load_sem.at[buf])]
            for k in range(top_k):
                src = pos_hbm.at[pl.ds(pl.multiple_of(k * t + off, SC_CHUNK), SC_CHUNK)]
                copies.append(pltpu.make_async_copy(src, idx_bufs[buf * top_k + k], load_sem.at[buf]))
            return copies

        def scatters(buf):
            return [pltpu.make_async_copy(rows_v.at[buf], out_hbm.at[idx_bufs[buf * top_k + k]], write_sem.at[buf])
                    for k in range(top_k)]

        for cp in loads(0, 0):
            cp.start()

        @pl.loop(0, n_chunks, step=2)
        def _(c0):
            for buf in range(2):
                c = c0 + buf
                for cp in loads(c, buf):
                    cp.wait()

                @pl.when(c >= 1)
                def _():
                    for cp in scatters(1 - buf):
                        cp.wait()

                @pl.when(c + 1 < n_chunks)
                def _():
                    for cp in loads(c + 1, 1 - buf):
                        cp.start()

                for cp in scatters(buf):
                    cp.start()

        for cp in scatters(1):
            cp.wait()

    return scatter_kernel(rows, pos_flat)


def _const_spec(shape):
    zeros = (0,) * len(shape)
    return pl.BlockSpec(shape, lambda *_: zeros, pipeline_mode=pl.Buffered(1))


def _tc_params(n_axes):
    return pltpu.CompilerParams(dimension_semantics=("arbitrary",) * n_axes,
                                vmem_limit_bytes=V7X_VMEM_LIMIT_BYTES)


def _mix_out_shapes(t, d, n_exp):
    return (jax.ShapeDtypeStruct((t, d), F32),
            jax.ShapeDtypeStruct((t, d // 2), jnp.uint32),
            jax.ShapeDtypeStruct((TOP_K, t), jnp.int32),
            jax.ShapeDtypeStruct((TOP_K, t), jnp.int32),
            jax.ShapeDtypeStruct((n_exp, V7X_LANES), F32),
            jax.ShapeDtypeStruct((t, V7X_LANES), F32))


def _mix_out_specs(rows, d, n_exp, row, tok_col):
    return (pl.BlockSpec((rows, d), row),
            pl.BlockSpec((rows, d // 2), row),
            pl.BlockSpec((TOP_K, rows), tok_col),
            pl.BlockSpec((TOP_K, rows), tok_col),
            pl.BlockSpec((n_exp, V7X_LANES), lambda *_: (0, 0)),
            pl.BlockSpec((rows, V7X_LANES), row))


def _strict_upper_tri(n):
    return (lax.broadcasted_iota(jnp.int32, (n, n), 0) < lax.broadcasted_iota(jnp.int32, (n, n), 1)).astype(BF16)


def _pool_layer(h, mem, w_in, wbd, pscale, mem_kv, w_out, ln_g, ln_b, rwt, rb, dn_alpha, seq):
    t, d = h.shape
    n_mem = mem.shape[1]
    pool_width = wbd.shape[0]
    ts = SEQ_BLOCK
    nsb = seq // ts
    row = lambda b, s: (b * nsb + s, 0)
    n_exp = rwt.shape[0]
    tri = _strict_upper_tri(ts)
    return pl.pallas_call(
        functools.partial(_pool_layer_kernel, dn_alpha),
        out_shape=_mix_out_shapes(t, d, n_exp),
        grid=(t // seq, nsb),
        in_specs=[
            pl.BlockSpec((ts, d), row),
            pl.BlockSpec((1, n_mem, d), lambda b, s: (b, 0, 0)),
            _const_spec(w_in.shape), _const_spec(wbd.shape), _const_spec(pscale.shape), _const_spec(mem_kv.shape),
            _const_spec(w_out.shape), _const_spec(ln_g.shape), _const_spec(ln_b.shape), _const_spec(rwt.shape),
            _const_spec(rb.shape), _const_spec(tri.shape),
        ],
        out_specs=_mix_out_specs(ts, d, n_exp, row, lambda b, s: (0, b * nsb + s)),
        scratch_shapes=[pltpu.VMEM((ts + POOL_HISTORY, pool_width), F32)] * 4
        + [pltpu.VMEM((n_mem, 2 * MEM_WIDTH), BF16), pltpu.VMEM((ts, d), BF16), pltpu.VMEM((n_exp, V7X_LANES), F32)],
        compiler_params=_tc_params(2),
        name="pool_layer",
    )(h, mem, w_in, wbd, pscale, mem_kv, w_out, ln_g, ln_b, rwt, rb, tri)


def _expand_heads(w, n_heads):
    d = w.shape[0]
    w3 = w.reshape(d, n_heads, HEAD_DIM)
    return jnp.pad(w3, ((0, 0), (0, 0), (0, V7X_LANES - HEAD_DIM))).reshape(d, n_heads * V7X_LANES)


def _shared_kv(h, kvw, fb, sel_k, seq, n_heads):
    t, d = h.shape
    ts = SEQ_BLOCK
    nsb = seq // ts
    k_width = n_heads * V7X_LANES
    v_width = n_heads * HEAD_DIM
    tri = (lax.broadcasted_iota(jnp.int32, (ts, ts), 1) <= lax.broadcasted_iota(jnp.int32, (ts, ts), 0)).astype(BF16)
    row = lambda b, s: (b * nsb + s, 0)
    return pl.pallas_call(
        _shared_kv_kernel,
        out_shape=(jax.ShapeDtypeStruct((t, k_width), BF16),
                   jax.ShapeDtypeStruct((t // seq, v_width, seq), BF16),
                   jax.ShapeDtypeStruct((t, V7X_LANES), BF16)),
        grid=(t // seq, nsb),
        in_specs=[pl.BlockSpec((ts, d), row), _const_spec(kvw.shape), _const_spec(fb.shape), _const_spec(tri.shape),
                  _const_spec(sel_k.shape)],
        out_specs=(pl.BlockSpec((ts, k_width), row),
                   pl.BlockSpec((1, v_width, ts), lambda b, s: (b, 0, s)),
                   pl.BlockSpec((ts, V7X_LANES), row)),
        scratch_shapes=[pltpu.VMEM((8, V7X_LANES), F32)],
        compiler_params=_tc_params(2),
        name="shared_kv",
    )(h, kvw, fb, tri, sel_k)


def _fox_layer(h, mem, k_aug, v_t, cq, w_in, mem_kv, w_out, ln_g, ln_b, rwt, rb, dn_alpha, seq):
    t, d = h.shape
    n_mem = mem.shape[1]
    k_width = k_aug.shape[1]
    v_width = v_t.shape[1]
    n_heads = k_width // V7X_LANES
    tq = Q_BLOCK
    nqb = seq // tq
    n_batch = t // seq
    row = lambda b, i: (b * nqb + i, 0)
    per_batch = lambda b, i: (b, 0, 0)
    n_exp = rwt.shape[0]
    tri = _strict_upper_tri(tq)
    return pl.pallas_call(
        functools.partial(_fox_layer_kernel, dn_alpha),
        out_shape=_mix_out_shapes(t, d, n_exp),
        grid=(n_batch, nqb),
        in_specs=[
            pl.BlockSpec((tq, d), row),
            pl.BlockSpec((1, n_mem, d), per_batch),
            pl.BlockSpec((1, seq, k_width), per_batch),
            pl.BlockSpec((1, v_width, seq), per_batch),
            pl.BlockSpec((tq, V7X_LANES), row),
            _const_spec(w_in.shape), _const_spec(mem_kv.shape), _const_spec(w_out.shape), _const_spec(ln_g.shape),
            _const_spec(ln_b.shape), _const_spec(rwt.shape), _const_spec(rb.shape), _const_spec(tri.shape),
        ],
        out_specs=_mix_out_specs(tq, d, n_exp, row, lambda b, i: (0, b * nqb + i)),
        scratch_shapes=[pltpu.VMEM((n_mem, 2 * MEM_WIDTH), BF16), pltpu.VMEM((tq, d), BF16),
                        pltpu.VMEM((n_heads, tq, V7X_LANES), BF16),
                        pltpu.VMEM((16, tq), F32), pltpu.VMEM((16, tq), F32), pltpu.VMEM((v_width, tq), F32),
                        pltpu.VMEM((n_exp, V7X_LANES), F32)],
        compiler_params=_tc_params(2),
        name="fox_layer",
    )(h, mem, k_aug.reshape(n_batch, seq, k_width), v_t, cq, w_in, mem_kv, w_out, ln_g, ln_b, rwt, rb, tri)


def _slot_kernel(pad_start_ref, idx_ref, rank_ref, pos_ref):
    pos = rank_ref[...]
    idx = idx_ref[...]
    for e in range(pad_start_ref.shape[0]):
        pos = pos + jnp.where(idx == e, pad_start_ref[e], 0)
    pos_ref[...] = pos


def _route(idx_t, rank_t, counts, n_experts):
    top_k, t = idx_t.shape
    n_assign = top_k * t
    padded = ((counts + MOE_BLOCK - 1) // MOE_BLOCK) * MOE_BLOCK
    pad_end = jnp.cumsum(padded)
    pad_start = pad_end - padded
    n_blocks = n_assign // MOE_BLOCK + n_experts
    block_start = jnp.arange(n_blocks, dtype=jnp.int32) * MOE_BLOCK
    block_expert = jnp.minimum(jnp.sum((pad_end[None, :] <= block_start[:, None]).astype(jnp.int32), axis=1),
                               n_experts - 1)
    block_valid = jnp.clip(counts[block_expert] - (block_start - pad_start[block_expert]), 0, MOE_BLOCK)
    n_active = (pad_end[-1:] // MOE_BLOCK).astype(jnp.int32)
    expert_ids = jnp.arange(n_experts, dtype=jnp.int32)
    in_use = counts > 0
    expert_slot = (jnp.cumsum(in_use.astype(jnp.int32)) - in_use.astype(jnp.int32)) % 2
    later = lax.cummin(jnp.where(in_use, expert_ids, n_experts), axis=0, reverse=True)
    next_in_use = jnp.concatenate([later[1:], jnp.full((1,), n_experts, jnp.int32)])
    next_in_use = jnp.where(next_in_use < n_experts, next_in_use, -1)
    block_slot = expert_slot[block_expert].astype(jnp.int32)
    block_next = next_in_use[block_expert].astype(jnp.int32)
    cols = min(SLOT_COLS, t)
    pos = pl.pallas_call(
        _slot_kernel,
        out_shape=jax.ShapeDtypeStruct((top_k, t), jnp.int32),
        grid_spec=pltpu.PrefetchScalarGridSpec(
            num_scalar_prefetch=1,
            grid=(t // cols,),
            in_specs=[pl.BlockSpec((top_k, cols), lambda i, ps: (0, i))] * 2,
            out_specs=pl.BlockSpec((top_k, cols), lambda i, ps: (0, i)),
        ),
        compiler_params=_tc_params(1),
        name="slots",
    )(pad_start.astype(jnp.int32), idx_t, rank_t)
    return pos, (block_expert, block_valid.astype(jnp.int32), block_slot, block_next, n_active)


def _experts(xs, block_expert, block_valid, block_slot, block_next, n_active, w1_all, b1, w2_all, b2, layer):
    n_slots, half_d = xs.shape
    _, n_exp, d, two_f = w1_all.shape
    d_ff = two_f // 2
    assert d == 2 * half_d
    block = lambda i, *_: (i, 0)
    per_expert = lambda i, be, *_: (be[i], 0, 0)
    grid_spec = pltpu.PrefetchScalarGridSpec(
        num_scalar_prefetch=5,
        grid=(n_slots // MOE_BLOCK,),
        in_specs=[
            pl.BlockSpec((MOE_BLOCK, half_d), block),
            pl.BlockSpec(memory_space=pl.ANY),
            pl.BlockSpec((1, 1, two_f), per_expert),
            pl.BlockSpec(memory_space=pl.ANY),
            pl.BlockSpec((1, 1, d), per_expert),
        ],
        out_specs=pl.BlockSpec((MOE_BLOCK, half_d), block),
        scratch_shapes=[pltpu.VMEM((d, two_f), BF16), pltpu.VMEM((d_ff, d), BF16), pltpu.VMEM((MOE_BLOCK, d_ff), BF16),
                        pltpu.VMEM((2, d, two_f), F32), pltpu.VMEM((2, d_ff, d), F32),
                        pltpu.SemaphoreType.DMA((2, 2))],
    )
    return pl.pallas_call(
        functools.partial(_expert_kernel, layer),
        out_shape=jax.ShapeDtypeStruct((n_slots, half_d), jnp.uint32),
        grid_spec=grid_spec,
        compiler_params=_tc_params(1),
        name="experts",
    )(block_expert, block_valid, block_slot, block_next, n_active, xs, w1_all, b1.reshape(n_exp, 1, two_f), w2_all,
      b2.reshape(n_exp, 1, d))


def _combine(h, yg, gcol, ln_g, ln_b, dn_alpha):
    t, d = h.shape
    ts = SEQ_BLOCK
    row = lambda i: (i, 0)
    return pl.pallas_call(
        functools.partial(_combine_kernel, dn_alpha),
        out_shape=jax.ShapeDtypeStruct((t, d), F32),
        grid=(t // ts,),
        in_specs=[pl.BlockSpec((ts, d), row), pl.BlockSpec((TOP_K, ts, d // 2), lambda i: (0, i, 0)),
                  pl.BlockSpec((ts, V7X_LANES), row), _const_spec(ln_g.shape), _const_spec(ln_b.shape)],
        out_specs=pl.BlockSpec((ts, d), row),
        compiler_params=_tc_params(1),
        name="combine",
    )(h, yg.reshape(TOP_K, t, d // 2), gcol, ln_g, ln_b)


def _moe(mixed, w1_all, b1, w2_all, b2, layer, ln_g, ln_b, dn_alpha):
    h1, h1_packed, idx_t, rank_t, counts, gcol = mixed
    n_exp = w1_all.shape[1]
    d_ff = w2_all.shape[2]
    pos, block_meta = _route(idx_t, rank_t, counts[:, 0].astype(jnp.int32), n_exp)
    b1p = jnp.swapaxes(b1.reshape(n_exp, d_ff // V7X_LANES, V7X_LANES, 2), 2, 3).reshape(n_exp, 2 * d_ff)
    xs = _sc_scatter_rows(h1_packed, pos, block_meta[0].shape[0] * MOE_BLOCK)
    y = _experts(xs, *block_meta, w1_all, b1p, w2_all, b2, layer)
    yg = _sc_gather_rows(y, pos.reshape(-1))
    return _combine(h1, yg, gcol, ln_g, ln_b, dn_alpha)


def kernel(x, mem, ln1_g, ln1_b, ln2_g, ln2_b, a_w_in, pool_w, pool_scale, a_mem_kv, a_w_out, kv_w, fgate_b, b_w_in, b_mem_kv, b_w_out, router_w, router_b, moe_w1, moe_b1, moe_w2, moe_b2):
    n_batch, seq, d = x.shape
    t = n_batch * seq
    depth = ln1_g.shape[0]
    n_a = a_w_in.shape[0]
    dn_alpha = float((2 * depth) ** 0.25)
    n_exp = router_w.shape[2]
    fox_heads = fgate_b.shape[0]
    fox_width = fox_heads * HEAD_DIM
    assert seq % SEQ_BLOCK == 0 and seq % Q_BLOCK == 0 and fox_heads <= 16

    h = x.reshape(t, d)
    row = lambda a: a.reshape(1, -1)
    shared = None
    for l in range(depth):
        rwt = router_w[l].T
        rb = router_b[l].reshape(n_exp, 1)
        if l < n_a:
            wbd = jax.scipy.linalg.block_diag(*[pool_w[l, g] for g in range(pool_w.shape[1])]).astype(BF16)
            mixed = _pool_layer(
                h, mem, a_w_in[l].astype(BF16), wbd, row(pool_scale[l]), a_mem_kv[l].astype(BF16),
                a_w_out[l].astype(BF16), row(ln1_g[l]), row(ln1_b[l]), rwt, rb, dn_alpha, seq)
        else:
            j = l - n_a
            if shared is None:
                pad = V7X_LANES - 3 * fox_heads
                kvw = jnp.concatenate([_expand_heads(kv_w[:, :fox_width], fox_heads),
                                       kv_w[:, fox_width:2 * fox_width],
                                       jnp.pad(jnp.tile(kv_w[:, 2 * fox_width:], (1, 3)), ((0, 0), (0, pad)))],
                                      axis=1).astype(BF16)
                fb = jnp.pad(jnp.tile(fgate_b, 3), (0, pad)).reshape(1, V7X_LANES)
                sel_k = _key_bias_selector(fox_heads)
                shared = _shared_kv(h, kvw, fb, sel_k, seq, fox_heads)
            k_aug, v_t, cq = shared
            w_in = jnp.concatenate([_expand_heads(b_w_in[j][:, :fox_width], fox_heads),
                                    b_w_in[j][:, fox_width:]], axis=1).astype(BF16)
            mixed = _fox_layer(
                h, mem, k_aug, v_t, cq, w_in, b_mem_kv[j].astype(BF16),
                b_w_out[j].astype(BF16), row(ln1_g[l]), row(ln1_b[l]), rwt, rb, dn_alpha, seq)
        h = _moe(mixed, moe_w1, moe_b1[l], moe_w2, moe_b2[l], l, row(ln2_g[l]), row(ln2_b[l]), dn_alpha)
    return h.reshape(n_batch, seq, d)
```

```python
import functools
import numpy as np
import jax
import jax.numpy as jnp
from jax import lax
from jax.experimental import pallas as pl
from jax.experimental.pallas import tpu as pltpu
from jax.experimental.pallas import tpu_sc as plsc

HEAD_DIM = 64
MEM_HEADS = 4
MEM_WIDTH = MEM_HEADS * HEAD_DIM
POOL_WINDOWS = (2, 4, 8, 16)
TOP_K = 4
SWIGLU_ALPHA = 1.702
SWIGLU_LIMIT = 7.0
MOE_BLOCK = 512
LN_EPS = 1e-5
ATTN_SCALE = HEAD_DIM ** -0.5
LOG2_E = 1.4426950408889634

V7X_LANES = 128
V7X_SUBLANES = 8
V7X_BF16_SUBLANES = 16
V7X_VMEM_LIMIT_BYTES = 56 * 1024 * 1024
POOL_HISTORY = 32
SC_CHUNK = 64
SEQ_BLOCK = 512
Q_BLOCK = 512
SLOT_COLS = 8192

F32 = jnp.float32
BF16 = jnp.bfloat16
NEG_INF = float("-inf")


def _dot(a, b):
    return jnp.dot(a, b, preferred_element_type=F32)


def _dot_nt(a, b):
    return lax.dot_general(a, b, (((1,), (1,)), ((), ())), preferred_element_type=F32)


def _lane_iota(shape):
    return lax.broadcasted_iota(jnp.int32, shape, len(shape) - 1)


def _pair_attention(q2, keys, vals):
    lane = _lane_iota(q2.shape)
    first = lane < HEAD_DIM
    outs = []
    for sel in (first, jnp.logical_not(first)):
        qh = jnp.where(sel, q2, 0.0).astype(BF16)
        s = _dot_nt(qh, keys)
        m = jnp.max(s, axis=-1, keepdims=True)
        p = jnp.exp(s - m)
        l = jnp.sum(p, axis=-1, keepdims=True)
        outs.append(_dot(p.astype(BF16), vals) / l)
    return jnp.where(first, outs[0], outs[1])


def _memory_attention(proj, kv_ref, cat_ref, out_off, q_off):
    for p in range(MEM_HEADS // 2):
        lo = p * V7X_LANES
        q2 = proj[:, q_off + lo:q_off + lo + V7X_LANES] * ATTN_SCALE
        keys = kv_ref[:, lo:lo + V7X_LANES]
        vals = kv_ref[:, MEM_WIDTH + lo:MEM_WIDTH + lo + V7X_LANES]
        cat_ref[:, out_off + lo:out_off + lo + V7X_LANES] = _pair_attention(q2, keys, vals).astype(BF16)


def _layer_norm(z, g, b):
    mu = jnp.mean(z, axis=-1, keepdims=True)
    zc = z - mu
    var = jnp.mean(zc * zc, axis=-1, keepdims=True)
    return zc * lax.rsqrt(var + LN_EPS) * g + b


def _pack_bf16_pairs(x):
    w = x.shape[1] // 2
    bits = lax.bitcast_convert_type(x.astype(BF16).astype(F32), jnp.uint32)
    return lax.shift_right_logical(bits[:, :w], jnp.uint32(16)) | (bits[:, w:] & jnp.uint32(0xFFFF0000))


def _unpack_bf16_pairs(p):
    low = lax.bitcast_convert_type(lax.shift_left(p, jnp.uint32(16)), F32)
    high = lax.bitcast_convert_type(p & jnp.uint32(0xFFFF0000), F32)
    return low, high


def _expert_ranks(idxs, tri_ref, rank_ref, counts_ref, base_ref):
    @pl.when(jnp.logical_and(pl.program_id(0) == 0, pl.program_id(1) == 0))
    def _():
        base_ref[...] = jnp.zeros(base_ref.shape, F32)

    n_exp = base_ref.shape[0]
    t = idxs[0].shape[1]
    rows = lax.broadcasted_iota(jnp.int32, (n_exp, t), 0)
    running = base_ref[:, 0:1]
    for k in range(TOP_K):
        onehot = rows == idxs[k]
        prefix = _dot(jnp.where(onehot, 1.0, 0.0).astype(BF16), tri_ref[...])
        rank = jnp.sum(jnp.where(onehot, prefix + running, 0.0), axis=0, keepdims=True)
        rank_ref[k:k + 1, :] = rank.astype(jnp.int32)
        running = running + jnp.sum(jnp.where(onehot, 1.0, 0.0), axis=1, keepdims=True)
    base_ref[...] = jnp.broadcast_to(running, base_ref.shape)
    counts_ref[...] = base_ref[...]


def _mix_epilogue(x, cat_ref, w_out_ref, g_ref, b_ref, rwt_ref, rb_ref, tri_ref, dn_alpha, h_out_ref, hp_ref, idx_ref,
                  rank_ref, counts_ref, gcol_ref, base_ref):
    mix = _dot(cat_ref[...], w_out_ref[...])
    h1 = _layer_norm(dn_alpha * x + mix, g_ref[...], b_ref[...])
    h_hi = h1.astype(BF16)
    h_lo = (h1 - h_hi.astype(F32)).astype(BF16)
    rw = rwt_ref[...]
    n_exp = rw.shape[0]
    rw_hi = rw.astype(BF16)
    rw_lo = (rw - rw_hi.astype(F32)).astype(BF16)
    both = _dot_nt(jnp.concatenate([rw_hi, rw_lo], axis=0), h_hi)
    logits_all = both[:n_exp] + both[n_exp:] + _dot_nt(rw_hi, h_lo) + rb_ref[...]
    h_out_ref[...] = h1
    hp_ref[...] = _pack_bf16_pairs(h1)
    t = logits_all.shape[1]
    half = t // 2
    idxs_halves, gates_halves = [], []
    for part in range(2):
        logits = logits_all[:, part * half:(part + 1) * half]
        rows = lax.broadcasted_iota(jnp.int32, (n_exp, half), 0)
        vals, idxs = [], []
        for _ in range(TOP_K):
            m = jnp.max(logits, axis=0, keepdims=True)
            ix = jnp.min(jnp.where(logits == m, rows, n_exp), axis=0, keepdims=True)
            vals.append(m)
            idxs.append(ix)
            logits = jnp.where(rows == ix, NEG_INF, logits)
        exps = [jnp.exp(v - vals[0]) for v in vals]
        denom = exps[0] + exps[1] + exps[2] + exps[3]
        idxs_halves.append(idxs)
        gates_halves.append([e / denom for e in exps])
    idxs = [jnp.concatenate([idxs_halves[0][k], idxs_halves[1][k]], axis=1) for k in range(TOP_K)]
    gates = [jnp.concatenate([gates_halves[0][k], gates_halves[1][k]], axis=1) for k in range(TOP_K)]
    for k in range(TOP_K):
        idx_ref[k:k + 1, :] = idxs[k]
    _expert_ranks(idxs, tri_ref, rank_ref, counts_ref, base_ref)
    lane_rows = lax.broadcasted_iota(jnp.int32, (V7X_LANES, t), 0)
    gates_t = jnp.zeros((V7X_LANES, t), F32)
    for k in range(TOP_K):
        gates_t = jnp.where(lane_rows == k, gates[k], gates_t)
    gcol_ref[...] = gates_t.T


def _pool_layer_kernel(dn_alpha, h_ref, mem_ref, w_in_ref, wbd_ref, pscale_ref, mem_kv_ref, w_out_ref, g_ref, b_ref,
                       rwt_ref, rb_ref, tri_ref, h_out_ref, hp_ref, idx_ref, rank_ref, counts_ref, gcol_ref,
                       ue_ref, w2_ref, w4_ref, w8_ref, kv_ref, cat_ref, base_ref):
    s = pl.program_id(1)
    ts = h_ref.shape[0]
    pool_width = wbd_ref.shape[0]
    hist = POOL_HISTORY

    @pl.when(s == 0)
    def _():
        ue_ref[0:hist, :] = jnp.zeros((hist, pool_width), F32)
        kv_ref[...] = _dot(mem_ref[0].astype(BF16), mem_kv_ref[...]).astype(BF16)

    x = h_ref[...]
    proj = _dot(x.astype(BF16), w_in_ref[...])
    u = proj[:, :pool_width]
    ue_ref[hist:, :] = u
    n = ts + hist
    w2_ref[8:n, :] = ue_ref[8:n, :] + ue_ref[7:n - 1, :]
    w4_ref[16:n, :] = w2_ref[16:n, :] + w2_ref[14:n - 2, :]
    w8_ref[24:n, :] = w4_ref[24:n, :] + w4_ref[20:n - 4, :]
    w16 = w8_ref[hist:n, :] + w8_ref[hist - 8:n - 8, :]
    group_dim = pool_width // len(POOL_WINDOWS)
    col = _lane_iota((ts, pool_width))
    wsum = jnp.where(col < group_dim, w2_ref[hist:n, :],
                     jnp.where(col < 2 * group_dim, w4_ref[hist:n, :],
                               jnp.where(col < 3 * group_dim, w8_ref[hist:n, :], w16)))
    window = jnp.where(col < group_dim, POOL_WINDOWS[0],
                       jnp.where(col < 2 * group_dim, POOL_WINDOWS[1],
                                 jnp.where(col < 3 * group_dim, POOL_WINDOWS[2], POOL_WINDOWS[3])))
    tpos = s * ts + lax.broadcasted_iota(jnp.int32, (ts, pool_width), 0) + 1
    cnt = jnp.minimum(tpos, window).astype(F32)
    d = wsum / cnt - u
    ue_ref[0:hist, :] = u[ts - hist:, :]
    pooled = _dot(d.astype(BF16), wbd_ref[...]) * pscale_ref[...]
    cat_ref[:, :pool_width] = pooled.astype(BF16)
    _memory_attention(proj, kv_ref, cat_ref, pool_width, pool_width)
    _mix_epilogue(x, cat_ref, w_out_ref, g_ref, b_ref, rwt_ref, rb_ref, tri_ref, dn_alpha, h_out_ref, hp_ref, idx_ref,
                  rank_ref, counts_ref, gcol_ref, base_ref)


def _bias_pieces(c):
    hi = c.astype(BF16).astype(F32)
    r = c - hi
    mid = r.astype(BF16).astype(F32)
    lo = (r - mid).astype(BF16).astype(F32)
    return hi, mid, lo


def _bias_lane_pieces(c3, n_heads):
    hi, mid, lo = _bias_pieces(c3)
    lane = _lane_iota(c3.shape)
    out = jnp.where(lane < n_heads, hi, jnp.where(lane < 2 * n_heads, mid, jnp.where(lane < 3 * n_heads, lo,
                    jnp.where(lane == 3 * n_heads, 1.0, 0.0))))
    return out.astype(BF16)


def _key_bias_selector(n_heads):
    sel = np.zeros((V7X_LANES, n_heads * V7X_LANES), np.float32)
    for h in range(n_heads):
        spare = h * V7X_LANES + HEAD_DIM
        for p in range(3):
            sel[3 * n_heads, spare + p] = 1.0
            sel[p * n_heads + h, spare + 3 + p] = -1.0
    return jnp.asarray(sel, BF16)


def _shared_kv_block(h, kvw_ref, fb_ref, tri_ref, sel_ref, k_ref, vt_ref, cq_ref, carry_ref):
    s = pl.program_id(1)
    k_width = k_ref.shape[1]
    v_width = vt_ref.shape[1]
    n_heads = k_width // V7X_LANES

    @pl.when(s == 0)
    def _():
        carry_ref[...] = jnp.zeros(carry_ref.shape, F32)

    proj = _dot(h.astype(BF16), kvw_ref[...])
    z = proj[:, k_width + v_width:] + fb_ref[...]
    logf = jnp.minimum(z, 0.0) - jnp.log(1.0 + jnp.exp(-jnp.abs(z)))
    hi, mid, lo = _bias_pieces(logf)
    tri = tri_ref[...]
    c = _dot(tri, hi.astype(BF16)) + _dot(tri, mid.astype(BF16)) + _dot(tri, lo.astype(BF16)) + carry_ref[0:1, :]
    ts = c.shape[0]
    carry_ref[0:1, :] = c[ts - 1:ts, :]
    pieces = _bias_lane_pieces(c * LOG2_E, n_heads)
    cq_ref[...] = pieces
    k_ref[...] = (proj[:, :k_width] + _dot(pieces, sel_ref[...])).astype(BF16)
    vt_ref[0] = proj[:, k_width:k_width + v_width].T.astype(BF16)


def _fox_layer_kernel(dn_alpha, h_ref, mem_ref, k_ref, vt_ref, cq_ref, w_in_ref, mem_kv_ref, w_out_ref,
                      g_ref, b_ref, rwt_ref, rb_ref, tri_ref, h_out_ref, hp_ref, idx_ref, rank_ref, counts_ref, gcol_ref,
                      kv_ref, cat_ref, qa_ref, m_ref, l_ref, acc_ref, base_ref):
    i = pl.program_id(1)
    tq = h_ref.shape[0]
    n_heads = k_ref.shape[2] // V7X_LANES
    fox_width = vt_ref.shape[1]

    @pl.when(i == 0)
    def _():
        kv_ref[...] = _dot(mem_ref[0].astype(BF16), mem_kv_ref[...]).astype(BF16)

    x = h_ref[...]
    proj = _dot(x.astype(BF16), w_in_ref[...])
    cq = cq_ref[...].astype(F32)
    lane = _lane_iota((tq, V7X_LANES))
    for h in range(n_heads):
        cols = slice(h * V7X_LANES, (h + 1) * V7X_LANES)
        piece = [cq[:, p * n_heads + h:p * n_heads + h + 1] for p in range(3)]
        spare = jnp.where(lane == HEAD_DIM, piece[0], jnp.where(lane == HEAD_DIM + 1, piece[1],
                          jnp.where(lane == HEAD_DIM + 2, piece[2], jnp.where(lane < HEAD_DIM + 6, 1.0, 0.0))))
        qa_ref[h] = jnp.where(lane < HEAD_DIM, proj[:, cols] * (ATTN_SCALE * LOG2_E), spare).astype(BF16)
    m_ref[...] = jnp.full(m_ref.shape, NEG_INF, F32)
    l_ref[...] = jnp.zeros(l_ref.shape, F32)
    acc_ref[...] = jnp.zeros(acc_ref.shape, F32)
    causal_t = lax.broadcasted_iota(jnp.int32, (tq, tq), 0) <= lax.broadcasted_iota(jnp.int32, (tq, tq), 1)

    ones_rows = (lax.broadcasted_iota(jnp.int32, (V7X_BF16_SUBLANES, tq), 0) == 0).astype(BF16)

    def tile(j, masked):
        start = pl.multiple_of(j * tq, tq)

        def scores(h):
            kb = k_ref[0, pl.ds(start, tq), h * V7X_LANES:(h + 1) * V7X_LANES]
            return _dot_nt(kb, qa_ref[h])

        st_next = scores(0)
        for h in range(n_heads):
            st = st_next
            if h + 1 < n_heads:
                st_next = scores(h + 1)
            if masked:
                st = jnp.where(causal_t, st, NEG_INF)
            m_old = m_ref[h:h + 1, :]
            m_new = jnp.maximum(m_old, jnp.max(st, axis=0, keepdims=True))
            alpha = jnp.exp2(m_old - m_new)
            pt = jnp.exp2(st - m_new).astype(BF16)
            m_ref[h:h + 1, :] = m_new
            rows = slice(h * HEAD_DIM, (h + 1) * HEAD_DIM)
            vb = jnp.concatenate([vt_ref[0, rows, pl.ds(start, tq)], ones_rows], axis=0)
            pv = _dot(vb, pt)
            acc_ref[rows, :] = alpha * acc_ref[rows, :] + pv[:HEAD_DIM]
            l_ref[h:h + 1, :] = alpha * l_ref[h:h + 1, :] + pv[HEAD_DIM:HEAD_DIM + 1]

    tile(i, True)

    def off_diagonal(j, carry):
        tile(j, False)
        return carry

    lax.fori_loop(0, i, off_diagonal, 0)
    for h in range(n_heads):
        rows = slice(h * HEAD_DIM, (h + 1) * HEAD_DIM)
        acc_ref[rows, :] = acc_ref[rows, :] / l_ref[h:h + 1, :]
    cat_ref[:, :fox_width] = acc_ref[...].T.astype(BF16)
    _memory_attention(proj, kv_ref, cat_ref, fox_width, n_heads * V7X_LANES)
    _mix_epilogue(x, cat_ref, w_out_ref, g_ref, b_ref, rwt_ref, rb_ref, tri_ref, dn_alpha, h_out_ref, hp_ref, idx_ref,
                  rank_ref, counts_ref, gcol_ref, base_ref)


def _expert_kernel(layer, be_ref, bv_ref, bslot_ref, bnext_ref, nact_ref, x_ref, w1_hbm, b1_ref, w2_hbm, b2_ref, y_ref,
                   w1s_ref, w2s_ref, act_ref, w1f_ref, w2f_ref, sem_ref):
    i = pl.program_id(0)
    active = i < nact_ref[0]
    two_f = w1f_ref.shape[2]
    group = 2 * V7X_LANES
    n_groups = two_f // group

    def weight_copies(expert, slot):
        return (pltpu.make_async_copy(w1_hbm.at[layer, expert], w1f_ref.at[slot], sem_ref.at[0, slot]),
                pltpu.make_async_copy(w2_hbm.at[layer, expert], w2f_ref.at[slot], sem_ref.at[1, slot]))

    @pl.when(i == 0)
    def _():
        for cp in weight_copies(be_ref[0], bslot_ref[0]):
            cp.start()

    @pl.when(jnp.logical_and(active, jnp.logical_or(i == 0, be_ref[i] != be_ref[jnp.maximum(i - 1, 0)])))
    def _():
        slot = bslot_ref[i]
        for cp in weight_copies(be_ref[i], slot):
            cp.wait()

        @pl.when(bnext_ref[i] >= 0)
        def _():
            for cp in weight_copies(bnext_ref[i], 1 - slot):
                cp.start()

        src = lax.broadcasted_iota(jnp.int32, (group, group), 0)
        dst = lax.broadcasted_iota(jnp.int32, (group, group), 1)
        want = jnp.where(dst < V7X_LANES, 2 * dst, 2 * (dst - V7X_LANES) + 1)
        perm = (src == want).astype(BF16)
        for g in range(n_groups):
            cols = slice(g * group, (g + 1) * group)
            w1s_ref[:, cols] = _dot(w1f_ref[slot, :, cols].astype(BF16), perm).astype(BF16)
        w2s_ref[...] = w2f_ref[slot].astype(BF16)

    @pl.when(active)
    def _():
        row_id = lax.broadcasted_iota(jnp.int32, x_ref.shape, 0)
        x_low, x_high = _unpack_bf16_pairs(jnp.where(row_id < bv_ref[i], x_ref[...], jnp.uint32(0)))
        xb = jnp.concatenate([x_low.astype(BF16), x_high.astype(BF16)], axis=1)
        hdn = _dot(xb, w1s_ref[...]) + b1_ref[0]
        for g in range(n_groups):
            x_glu = jnp.minimum(hdn[:, g * group:g * group + V7X_LANES], SWIGLU_LIMIT)
            x_lin = jnp.clip(hdn[:, g * group + V7X_LANES:(g + 1) * group], -SWIGLU_LIMIT, SWIGLU_LIMIT)
            act = x_glu * jax.nn.sigmoid(SWIGLU_ALPHA * x_glu) * (x_lin + 1.0)
            act_ref[:, g * V7X_LANES:(g + 1) * V7X_LANES] = act.astype(BF16)
        y_ref[...] = _pack_bf16_pairs(_dot(act_ref[...], w2s_ref[...]) + b2_ref[0])

    @pl.when(jnp.logical_not(active))
    def _():
        y_ref[...] = jnp.zeros(y_ref.shape, jnp.uint32)


def _combine_rows(dn_alpha, h_ref, y_ref, gcol_ref, g_ref, b_ref):
    gates = gcol_ref[...]
    low, high = _unpack_bf16_pairs(y_ref[0])
    low, high = gates[:, 0:1] * low, gates[:, 0:1] * high
    for k in range(1, TOP_K):
        low_k, high_k = _unpack_bf16_pairs(y_ref[k])
        low = low + gates[:, k:k + 1] * low_k
        high = high + gates[:, k:k + 1] * high_k
    ffn = jnp.concatenate([low, high], axis=1)
    return _layer_norm(dn_alpha * h_ref[...] + ffn, g_ref[...], b_ref[...])


def _combine_kernel(dn_alpha, h_ref, y_ref, gcol_ref, g_ref, b_ref, o_ref):
    o_ref[...] = _combine_rows(dn_alpha, h_ref, y_ref, gcol_ref, g_ref, b_ref)


def _combine_kv_kernel(dn_alpha, h_ref, y_ref, gcol_ref, g_ref, b_ref, kvw_ref, fb_ref, tri_ref, sel_ref,
                       o_ref, k_ref, vt_ref, cq_ref, carry_ref):
    h2 = _combine_rows(dn_alpha, h_ref, y_ref, gcol_ref, g_ref, b_ref)
    o_ref[...] = h2
    _shared_kv_block(h2, kvw_ref, fb_ref, tri_ref, sel_ref, k_ref, vt_ref, cq_ref, carry_ref)


def _sc_gather_rows(table, idx):
    n, = idx.shape
    _, d = table.shape
    info = plsc.get_sparse_core_info()
    n_workers = info.num_cores * info.num_subcores
    per_worker = n // n_workers
    n_chunks = per_worker // SC_CHUNK
    assert n % (n_workers * SC_CHUNK) == 0 and n_chunks % 2 == 0
    mesh = plsc.VectorSubcoreMesh(core_axis_name="core", subcore_axis_name="subcore")

    @functools.partial(
        pl.kernel,
        out_type=jax.ShapeDtypeStruct((n, d), table.dtype),
        mesh=mesh,
        scratch_types=[
            pltpu.VMEM((per_worker,), jnp.int32),
            pltpu.VMEM((2, SC_CHUNK, d), table.dtype),
            pltpu.SemaphoreType.DMA((2,)),
            pltpu.SemaphoreType.DMA((2,)),
        ],
    )
    def gather_kernel(table_hbm, idx_hbm, out_hbm, idx_v, rows_v, gather_sem, write_sem):
        worker = lax.axis_index("subcore") * info.num_cores + lax.axis_index("core")
        base = pl.multiple_of(worker * per_worker, SC_CHUNK)
        pltpu.sync_copy(idx_hbm.at[pl.ds(base, per_worker)], idx_v)

        def gather(c, buf):
            rows = idx_v.at[pl.ds(pl.multiple_of(c * SC_CHUNK, SC_CHUNK), SC_CHUNK)]
            return pltpu.make_async_copy(table_hbm.at[rows], rows_v.at[buf], gather_sem.at[buf])

        def write(c, buf):
            off = pl.multiple_of(base + c * SC_CHUNK, SC_CHUNK)
            return pltpu.make_async_copy(rows_v.at[buf], out_hbm.at[pl.ds(off, SC_CHUNK)], write_sem.at[buf])

        gather(0, 0).start()

        @pl.loop(0, n_chunks, step=2)
        def _(c0):
            for buf in range(2):
                c = c0 + buf
                gather(c, buf).wait()

                @pl.when(c >= 1)
                def _():
                    write(c - 1, 1 - buf).wait()

                @pl.when(c + 1 < n_chunks)
                def _():
                    gather(c + 1, 1 - buf).start()

                write(c, buf).start()

        write(n_chunks - 1, 1).wait()

    return gather_kernel(table, idx)


def _sc_scatter_rows(rows, pos, n_out):
    t, d = rows.shape
    top_k = pos.shape[0]
    pos_flat = pos.reshape(-1)
    info = plsc.get_sparse_core_info()
    n_workers = info.num_cores * info.num_subcores
    per_worker = t // n_workers
    n_chunks = per_worker // SC_CHUNK
    assert t % (n_workers * SC_CHUNK) == 0 and n_chunks % 2 == 0
    mesh = plsc.VectorSubcoreMesh(core_axis_name="core", subcore_axis_name="subcore")

    @functools.partial(
        pl.kernel,
        out_type=jax.ShapeDtypeStruct((n_out, d), rows.dtype),
        mesh=mesh,
        scratch_types=[pltpu.VMEM((2, SC_CHUNK, d), rows.dtype)]
        + [pltpu.VMEM((SC_CHUNK,), jnp.int32) for _ in range(2 * top_k)]
        + [pltpu.SemaphoreType.DMA((2,)), pltpu.SemaphoreType.DMA((2,))],
    )
    def scatter_kernel(rows_hbm, pos_hbm, out_hbm, rows_v, *rest):
        idx_bufs = rest[:2 * top_k]
        load_sem, write_sem = rest[2 * top_k:]
        worker = lax.axis_index("subcore") * info.num_cores + lax.axis_index("core")
        base = pl.multiple_of(worker * per_worker, SC_CHUNK)

        def loads(c, buf):
            off = pl.multiple_of(base + c * SC_CHUNK, SC_CHUNK)
            copies = [pltpu.make_async_copy(rows_hbm.at[pl.ds(off, SC_CHUNK)], rows_v.at[buf], load_sem.at[buf])]
            for k in range(top_k):
                src = pos_hbm.at[pl.ds(pl.multiple_of(k * t + off, SC_CHUNK), SC_CHUNK)]
                copies.append(pltpu.make_async_copy(src, idx_bufs[buf * top_k + k], load_sem.at[buf]))
            return copies

        def scatters(buf):
            return [pltpu.make_async_copy(rows_v.at[buf], out_hbm.at[idx_bufs[buf * top_k + k]], write_sem.at[buf])
                    for k in range(top_k)]

        for cp in loads(0, 0):
            cp.start()

        @pl.loop(0, n_chunks, step=2)
        def _(c0):
            for buf in range(2):
                c = c0 + buf
                for cp in loads(c, buf):
                    cp.wait()

                @pl.when(c >= 1)
                def _():
                    for cp in scatters(1 - buf):
                        cp.wait()

                @pl.when(c + 1 < n_chunks)
                def _():
                    for cp in loads(c + 1, 1 - buf):
                        cp.start()

                for cp in scatters(buf):
                    cp.start()

        for cp in scatters(1):
            cp.wait()

    return scatter_kernel(rows, pos_flat)


def _const_spec(shape):
    zeros = (0,) * len(shape)
    return pl.BlockSpec(shape, lambda *_: zeros, pipeline_mode=pl.Buffered(1))


def _tc_params(n_axes):
    return pltpu.CompilerParams(dimension_semantics=("arbitrary",) * n_axes,
                                vmem_limit_bytes=V7X_VMEM_LIMIT_BYTES)


def _mix_out_shapes(t, d, n_exp):
    return (jax.ShapeDtypeStruct((t, d), F32),
            jax.ShapeDtypeStruct((t, d // 2), jnp.uint32),
            jax.ShapeDtypeStruct((TOP_K, t), jnp.int32),
            jax.ShapeDtypeStruct((TOP_K, t), jnp.int32),
            jax.ShapeDtypeStruct((n_exp, V7X_LANES), F32),
            jax.ShapeDtypeStruct((t, V7X_LANES), F32))


def _mix_out_specs(rows, d, n_exp, row, tok_col):
    return (pl.BlockSpec((rows, d), row),
            pl.BlockSpec((rows, d // 2), row),
            pl.BlockSpec((TOP_K, rows), tok_col),
            pl.BlockSpec((TOP_K, rows), tok_col),
            pl.BlockSpec((n_exp, V7X_LANES), lambda *_: (0, 0)),
            pl.BlockSpec((rows, V7X_LANES), row))


def _strict_upper_tri(n):
    return (lax.broadcasted_iota(jnp.int32, (n, n), 0) < lax.broadcasted_iota(jnp.int32, (n, n), 1)).astype(BF16)


def _pool_layer(h, mem, w_in, wbd, pscale, mem_kv, w_out, ln_g, ln_b, rwt, rb, dn_alpha, seq):
    t, d = h.shape
    n_mem = mem.shape[1]
    pool_width = wbd.shape[0]
    ts = SEQ_BLOCK
    nsb = seq // ts
    row = lambda b, s: (b * nsb + s, 0)
    n_exp = rwt.shape[0]
    tri = _strict_upper_tri(ts)
    return pl.pallas_call(
        functools.partial(_pool_layer_kernel, dn_alpha),
        out_shape=_mix_out_shapes(t, d, n_exp),
        grid=(t // seq, nsb),
        in_specs=[
            pl.BlockSpec((ts, d), row),
            pl.BlockSpec((1, n_mem, d), lambda b, s: (b, 0, 0)),
            _const_spec(w_in.shape), _const_spec(wbd.shape), _const_spec(pscale.shape), _const_spec(mem_kv.shape),
            _const_spec(w_out.shape), _const_spec(ln_g.shape), _const_spec(ln_b.shape), _const_spec(rwt.shape),
            _const_spec(rb.shape), _const_spec(tri.shape),
        ],
        out_specs=_mix_out_specs(ts, d, n_exp, row, lambda b, s: (0, b * nsb + s)),
        scratch_shapes=[pltpu.VMEM((ts + POOL_HISTORY, pool_width), F32)] * 4
        + [pltpu.VMEM((n_mem, 2 * MEM_WIDTH), BF16), pltpu.VMEM((ts, d), BF16), pltpu.VMEM((n_exp, V7X_LANES), F32)],
        compiler_params=_tc_params(2),
        name="pool_layer",
    )(h, mem, w_in, wbd, pscale, mem_kv, w_out, ln_g, ln_b, rwt, rb, tri)


def _expand_heads(w, n_heads):
    d = w.shape[0]
    w3 = w.reshape(d, n_heads, HEAD_DIM)
    return jnp.pad(w3, ((0, 0), (0, 0), (0, V7X_LANES - HEAD_DIM))).reshape(d, n_heads * V7X_LANES)


def _fox_layer(h, mem, k_aug, v_t, cq, w_in, mem_kv, w_out, ln_g, ln_b, rwt, rb, dn_alpha, seq):
    t, d = h.shape
    n_mem = mem.shape[1]
    k_width = k_aug.shape[1]
    v_width = v_t.shape[1]
    n_heads = k_width // V7X_LANES
    stat_rows = -(-n_heads // V7X_SUBLANES) * V7X_SUBLANES
    tq = Q_BLOCK
    nqb = seq // tq
    n_batch = t // seq
    row = lambda b, i: (b * nqb + i, 0)
    per_batch = lambda b, i: (b, 0, 0)
    n_exp = rwt.shape[0]
    tri = _strict_upper_tri(tq)
    return pl.pallas_call(
        functools.partial(_fox_layer_kernel, dn_alpha),
        out_shape=_mix_out_shapes(t, d, n_exp),
        grid=(n_batch, nqb),
        in_specs=[
            pl.BlockSpec((tq, d), row),
            pl.BlockSpec((1, n_mem, d), per_batch),
            pl.BlockSpec((1, seq, k_width), per_batch),
            pl.BlockSpec((1, v_width, seq), per_batch),
            pl.BlockSpec((tq, V7X_LANES), row),
            _const_spec(w_in.shape), _const_spec(mem_kv.shape), _const_spec(w_out.shape), _const_spec(ln_g.shape),
            _const_spec(ln_b.shape), _const_spec(rwt.shape), _const_spec(rb.shape), _const_spec(tri.shape),
        ],
        out_specs=_mix_out_specs(tq, d, n_exp, row, lambda b, i: (0, b * nqb + i)),
        scratch_shapes=[pltpu.VMEM((n_mem, 2 * MEM_WIDTH), BF16), pltpu.VMEM((tq, d), BF16),
                        pltpu.VMEM((n_heads, tq, V7X_LANES), BF16),
                        pltpu.VMEM((stat_rows, tq), F32), pltpu.VMEM((stat_rows, tq), F32),
                        pltpu.VMEM((v_width, tq), F32),
                        pltpu.VMEM((n_exp, V7X_LANES), F32)],
        compiler_params=_tc_params(2),
        name="fox_layer",
    )(h, mem, k_aug.reshape(n_batch, seq, k_width), v_t, cq, w_in, mem_kv, w_out, ln_g, ln_b, rwt, rb, tri)


def _slot_kernel(pad_start_ref, idx_ref, rank_ref, pos_ref):
    pos = rank_ref[...]
    idx = idx_ref[...]
    for e in range(pad_start_ref.shape[0]):
        pos = pos + jnp.where(idx == e, pad_start_ref[e], 0)
    pos_ref[...] = pos


def _route(idx_t, rank_t, counts, n_experts):
    top_k, t = idx_t.shape
    n_assign = top_k * t
    padded = ((counts + MOE_BLOCK - 1) // MOE_BLOCK) * MOE_BLOCK
    pad_end = jnp.cumsum(padded)
    pad_start = pad_end - padded
    n_blocks = n_assign // MOE_BLOCK + n_experts
    block_start = jnp.arange(n_blocks, dtype=jnp.int32) * MOE_BLOCK
    block_expert = jnp.minimum(jnp.sum((pad_end[None, :] <= block_start[:, None]).astype(jnp.int32), axis=1),
                               n_experts - 1)
    block_valid = jnp.clip(counts[block_expert] - (block_start - pad_start[block_expert]), 0, MOE_BLOCK)
    n_active = (pad_end[-1:] // MOE_BLOCK).astype(jnp.int32)
    expert_ids = jnp.arange(n_experts, dtype=jnp.int32)
    in_use = counts > 0
    expert_slot = (jnp.cumsum(in_use.astype(jnp.int32)) - in_use.astype(jnp.int32)) % 2
    later = lax.cummin(jnp.where(in_use, expert_ids, n_experts), axis=0, reverse=True)
    next_in_use = jnp.concatenate([later[1:], jnp.full((1,), n_experts, jnp.int32)])
    next_in_use = jnp.where(next_in_use < n_experts, next_in_use, -1)
    block_slot = expert_slot[block_expert].astype(jnp.int32)
    block_next = next_in_use[block_expert].astype(jnp.int32)
    cols = min(SLOT_COLS, t)
    pos = pl.pallas_call(
        _slot_kernel,
        out_shape=jax.ShapeDtypeStruct((top_k, t), jnp.int32),
        grid_spec=pltpu.PrefetchScalarGridSpec(
            num_scalar_prefetch=1,
            grid=(t // cols,),
            in_specs=[pl.BlockSpec((top_k, cols), lambda i, ps: (0, i))] * 2,
            out_specs=pl.BlockSpec((top_k, cols), lambda i, ps: (0, i)),
        ),
        compiler_params=_tc_params(1),
        name="slots",
    )(pad_start.astype(jnp.int32), idx_t, rank_t)
    return pos, (block_expert, block_valid.astype(jnp.int32), block_slot, block_next, n_active)


def _experts(xs, block_expert, block_valid, block_slot, block_next, n_active, w1_all, b1, w2_all, b2, layer):
    n_slots, half_d = xs.shape
    _, n_exp, d, two_f = w1_all.shape
    d_ff = two_f // 2
    assert d == 2 * half_d
    block = lambda i, *_: (i, 0)
    per_expert = lambda i, be, *_: (be[i], 0, 0)
    grid_spec = pltpu.PrefetchScalarGridSpec(
        num_scalar_prefetch=5,
        grid=(n_slots // MOE_BLOCK,),
        in_specs=[
            pl.BlockSpec((MOE_BLOCK, half_d), block),
            pl.BlockSpec(memory_space=pl.ANY),
            pl.BlockSpec((1, 1, two_f), per_expert),
            pl.BlockSpec(memory_space=pl.ANY),
            pl.BlockSpec((1, 1, d), per_expert),
        ],
        out_specs=pl.BlockSpec((MOE_BLOCK, half_d), block),
        scratch_shapes=[pltpu.VMEM((d, two_f), BF16), pltpu.VMEM((d_ff, d), BF16), pltpu.VMEM((MOE_BLOCK, d_ff), BF16),
                        pltpu.VMEM((2, d, two_f), F32), pltpu.VMEM((2, d_ff, d), F32),
                        pltpu.SemaphoreType.DMA((2, 2))],
    )
    return pl.pallas_call(
        functools.partial(_expert_kernel, layer),
        out_shape=jax.ShapeDtypeStruct((n_slots, half_d), jnp.uint32),
        grid_spec=grid_spec,
        compiler_params=_tc_params(1),
        name="experts",
    )(block_expert, block_valid, block_slot, block_next, n_active, xs, w1_all, b1.reshape(n_exp, 1, two_f), w2_all,
      b2.reshape(n_exp, 1, d))


def _combine(h, yg, gcol, ln_g, ln_b, dn_alpha, seq, shared_kv_weights=None):
    t, d = h.shape
    ts = SEQ_BLOCK
    nsb = seq // ts
    row = lambda b, s: (b * nsb + s, 0)
    in_specs = [pl.BlockSpec((ts, d), row), pl.BlockSpec((TOP_K, ts, d // 2), lambda b, s: (0, b * nsb + s, 0)),
                pl.BlockSpec((ts, V7X_LANES), row), _const_spec(ln_g.shape), _const_spec(ln_b.shape)]
    args = (h, yg.reshape(TOP_K, t, d // 2), gcol, ln_g, ln_b)
    if shared_kv_weights is None:
        return pl.pallas_call(
            functools.partial(_combine_kernel, dn_alpha),
            out_shape=jax.ShapeDtypeStruct((t, d), F32),
            grid=(t // seq, nsb),
            in_specs=in_specs,
            out_specs=pl.BlockSpec((ts, d), row),
            compiler_params=_tc_params(2),
            name="combine",
        )(*args), None
    kvw, fb, sel_k, n_heads = shared_kv_weights
    k_width = n_heads * V7X_LANES
    v_width = n_heads * HEAD_DIM
    tri = (lax.broadcasted_iota(jnp.int32, (ts, ts), 1) <= lax.broadcasted_iota(jnp.int32, (ts, ts), 0)).astype(BF16)
    out = pl.pallas_call(
        functools.partial(_combine_kv_kernel, dn_alpha),
        out_shape=(jax.ShapeDtypeStruct((t, d), F32),
                   jax.ShapeDtypeStruct((t, k_width), BF16),
                   jax.ShapeDtypeStruct((t // seq, v_width, seq), BF16),
                   jax.ShapeDtypeStruct((t, V7X_LANES), BF16)),
        grid=(t // seq, nsb),
        in_specs=in_specs + [_const_spec(kvw.shape), _const_spec(fb.shape), _const_spec(tri.shape),
                             _const_spec(sel_k.shape)],
        out_specs=(pl.BlockSpec((ts, d), row),
                   pl.BlockSpec((ts, k_width), row),
                   pl.BlockSpec((1, v_width, ts), lambda b, s: (b, 0, s)),
                   pl.BlockSpec((ts, V7X_LANES), row)),
        scratch_shapes=[pltpu.VMEM((V7X_SUBLANES, V7X_LANES), F32)],
        compiler_params=_tc_params(2),
        name="combine_shared_kv",
    )(*args, kvw, fb, tri, sel_k)
    return out[0], out[1:]


def _moe(mixed, w1_all, b1, w2_all, b2, layer, ln_g, ln_b, dn_alpha, seq, shared_kv_weights):
    h1, h1_packed, idx_t, rank_t, counts, gcol = mixed
    n_exp = w1_all.shape[1]
    d_ff = w2_all.shape[2]
    pos, block_meta = _route(idx_t, rank_t, counts[:, 0].astype(jnp.int32), n_exp)
    b1p = jnp.swapaxes(b1.reshape(n_exp, d_ff // V7X_LANES, V7X_LANES, 2), 2, 3).reshape(n_exp, 2 * d_ff)
    xs = _sc_scatter_rows(h1_packed, pos, block_meta[0].shape[0] * MOE_BLOCK)
    y = _experts(xs, *block_meta, w1_all, b1p, w2_all, b2, layer)
    yg = _sc_gather_rows(y, pos.reshape(-1))
    return _combine(h1, yg, gcol, ln_g, ln_b, dn_alpha, seq, shared_kv_weights)


def kernel(x, mem, ln1_g, ln1_b, ln2_g, ln2_b, a_w_in, pool_w, pool_scale, a_mem_kv, a_w_out, kv_w, fgate_b, b_w_in, b_mem_kv, b_w_out, router_w, router_b, moe_w1, moe_b1, moe_w2, moe_b2):
    n_batch, seq, d = x.shape
    t = n_batch * seq
    depth = ln1_g.shape[0]
    n_a = a_w_in.shape[0]
    dn_alpha = float((2 * depth) ** 0.25)
    n_exp = router_w.shape[2]
    fox_heads = fgate_b.shape[0]
    fox_width = fox_heads * HEAD_DIM
    assert seq % SEQ_BLOCK == 0 and seq % Q_BLOCK == 0
    assert 3 * fox_heads < V7X_LANES

    assert 1 <= n_a < depth
    pad = V7X_LANES - 3 * fox_heads
    kvw = jnp.concatenate([_expand_heads(kv_w[:, :fox_width], fox_heads),
                           kv_w[:, fox_width:2 * fox_width],
                           jnp.pad(jnp.tile(kv_w[:, 2 * fox_width:], (1, 3)), ((0, 0), (0, pad)))],
                          axis=1).astype(BF16)
    fb = jnp.pad(jnp.tile(fgate_b, 3), (0, pad)).reshape(1, V7X_LANES)
    shared_kv_weights = (kvw, fb, _key_bias_selector(fox_heads), fox_heads)

    h = x.reshape(t, d)
    row = lambda a: a.reshape(1, -1)
    shared = None
    for l in range(depth):
        rwt = router_w[l].T
        rb = router_b[l].reshape(n_exp, 1)
        if l < n_a:
            wbd = jax.scipy.linalg.block_diag(*[pool_w[l, g] for g in range(pool_w.shape[1])]).astype(BF16)
            mixed = _pool_layer(
                h, mem, a_w_in[l].astype(BF16), wbd, row(pool_scale[l]), a_mem_kv[l].astype(BF16),
                a_w_out[l].astype(BF16), row(ln1_g[l]), row(ln1_b[l]), rwt, rb, dn_alpha, seq)
        else:
            j = l - n_a
            k_aug, v_t, cq = shared
            w_in = jnp.concatenate([_expand_heads(b_w_in[j][:, :fox_width], fox_heads),
                                    b_w_in[j][:, fox_width:]], axis=1).astype(BF16)
            mixed = _fox_layer(
                h, mem, k_aug, v_t, cq, w_in, b_mem_kv[j].astype(BF16),
                b_w_out[j].astype(BF16), row(ln1_g[l]), row(ln1_b[l]), rwt, rb, dn_alpha, seq)
        h, new_shared = _moe(mixed, moe_w1, moe_b1[l], moe_w2, moe_b2[l], l, row(ln2_g[l]), row(ln2_b[l]), dn_alpha,
                             seq, shared_kv_weights if l == n_a - 1 else None)
        if new_shared is not None:
            shared = new_shared
    return h.reshape(n_batch, seq, d)
```

```python
import functools
import numpy as np
import jax
import jax.numpy as jnp
from jax import lax
from jax.experimental import pallas as pl
from jax.experimental.pallas import tpu as pltpu
from jax.experimental.pallas import tpu_sc as plsc

HEAD_DIM = 64
MEM_HEADS = 4
MEM_WIDTH = MEM_HEADS * HEAD_DIM
POOL_WINDOWS = (2, 4, 8, 16)
TOP_K = 4
SWIGLU_ALPHA = 1.702
SWIGLU_LIMIT = 7.0
MOE_BLOCK = 512
LN_EPS = 1e-5
ATTN_SCALE = HEAD_DIM ** -0.5
LOG2_E = 1.4426950408889634

V7X_LANES = 128
V7X_SUBLANES = 8
V7X_BF16_SUBLANES = 16
V7X_VMEM_LIMIT_BYTES = 56 * 1024 * 1024
POOL_HISTORY = 32
SC_CHUNK = 64
SC_GATHER_CHUNK = 32
SC_GATHER_DEPTH = 4
SEQ_BLOCK = 512
Q_BLOCK = 512
SLOT_COLS = 8192

F32 = jnp.float32
BF16 = jnp.bfloat16
NEG_INF = float("-inf")


def _dot(a, b):
    return jnp.dot(a, b, preferred_element_type=F32)


def _dot_nt(a, b):
    return lax.dot_general(a, b, (((1,), (1,)), ((), ())), preferred_element_type=F32)


def _lane_iota(shape):
    return lax.broadcasted_iota(jnp.int32, shape, len(shape) - 1)


def _pair_attention(q2, keys, vals):
    lane = _lane_iota(q2.shape)
    first = lane < HEAD_DIM
    outs = []
    for sel in (first, jnp.logical_not(first)):
        qh = jnp.where(sel, q2, 0.0).astype(BF16)
        s = _dot_nt(qh, keys)
        m = jnp.max(s, axis=-1, keepdims=True)
        p = jnp.exp(s - m)
        l = jnp.sum(p, axis=-1, keepdims=True)
        outs.append(_dot(p.astype(BF16), vals) / l)
    return jnp.where(first, outs[0], outs[1])


def _memory_attention(proj, kv_ref, cat_ref, out_off, q_off):
    for p in range(MEM_HEADS // 2):
        lo = p * V7X_LANES
        q2 = proj[:, q_off + lo:q_off + lo + V7X_LANES] * ATTN_SCALE
        keys = kv_ref[:, lo:lo + V7X_LANES]
        vals = kv_ref[:, MEM_WIDTH + lo:MEM_WIDTH + lo + V7X_LANES]
        cat_ref[:, out_off + lo:out_off + lo + V7X_LANES] = _pair_attention(q2, keys, vals).astype(BF16)


def _layer_norm(z, g, b):
    mu = jnp.mean(z, axis=-1, keepdims=True)
    zc = z - mu
    var = jnp.mean(zc * zc, axis=-1, keepdims=True)
    return zc * lax.rsqrt(var + LN_EPS) * g + b


def _pack_bf16_pairs(x):
    w = x.shape[1] // 2
    bits = lax.bitcast_convert_type(x.astype(BF16).astype(F32), jnp.uint32)
    return lax.shift_right_logical(bits[:, :w], jnp.uint32(16)) | (bits[:, w:] & jnp.uint32(0xFFFF0000))


def _unpack_bf16_pairs(p):
    low = lax.bitcast_convert_type(lax.shift_left(p, jnp.uint32(16)), F32)
    high = lax.bitcast_convert_type(p & jnp.uint32(0xFFFF0000), F32)
    return low, high


def _expert_ranks(idxs, tri_ref, rank_ref, counts_ref, base_ref):
    @pl.when(jnp.logical_and(pl.program_id(0) == 0, pl.program_id(1) == 0))
    def _():
        base_ref[...] = jnp.zeros(base_ref.shape, F32)

    n_exp = base_ref.shape[0]
    t = idxs[0].shape[1]
    rows = lax.broadcasted_iota(jnp.int32, (n_exp, t), 0)
    running = base_ref[:, 0:1]
    for k in range(TOP_K):
        onehot = rows == idxs[k]
        prefix = _dot(jnp.where(onehot, 1.0, 0.0).astype(BF16), tri_ref[...])
        rank = jnp.sum(jnp.where(onehot, prefix + running, 0.0), axis=0, keepdims=True)
        rank_ref[k:k + 1, :] = rank.astype(jnp.int32)
        running = running + jnp.sum(jnp.where(onehot, 1.0, 0.0), axis=1, keepdims=True)
    base_ref[...] = jnp.broadcast_to(running, base_ref.shape)
    counts_ref[...] = base_ref[...]


def _mix_epilogue(x, cat_ref, w_out_ref, g_ref, b_ref, rwt_ref, rb_ref, tri_ref, dn_alpha, h_out_ref, hp_ref, idx_ref,
                  rank_ref, counts_ref, gcol_ref, base_ref):
    mix = _dot(cat_ref[...], w_out_ref[...])
    h1 = _layer_norm(dn_alpha * x + mix, g_ref[...], b_ref[...])
    h_hi = h1.astype(BF16)
    h_lo = (h1 - h_hi.astype(F32)).astype(BF16)
    rw = rwt_ref[...]
    n_exp = rw.shape[0]
    rw_hi = rw.astype(BF16)
    rw_lo = (rw - rw_hi.astype(F32)).astype(BF16)
    both = _dot_nt(jnp.concatenate([rw_hi, rw_lo], axis=0), h_hi)
    logits_all = both[:n_exp] + both[n_exp:] + _dot_nt(rw_hi, h_lo) + rb_ref[...]
    h_out_ref[...] = h1
    hp_ref[...] = _pack_bf16_pairs(h1)
    t = logits_all.shape[1]
    half = t // 2
    idxs_halves, gates_halves = [], []
    for part in range(2):
        logits = logits_all[:, part * half:(part + 1) * half]
        rows = lax.broadcasted_iota(jnp.int32, (n_exp, half), 0)
        vals, idxs = [], []
        for _ in range(TOP_K):
            m = jnp.max(logits, axis=0, keepdims=True)
            ix = jnp.min(jnp.where(logits == m, rows, n_exp), axis=0, keepdims=True)
            vals.append(m)
            idxs.append(ix)
            logits = jnp.where(rows == ix, NEG_INF, logits)
        exps = [jnp.exp(v - vals[0]) for v in vals]
        denom = exps[0] + exps[1] + exps[2] + exps[3]
        idxs_halves.append(idxs)
        gates_halves.append([e / denom for e in exps])
    idxs = [jnp.concatenate([idxs_halves[0][k], idxs_halves[1][k]], axis=1) for k in range(TOP_K)]
    gates = [jnp.concatenate([gates_halves[0][k], gates_halves[1][k]], axis=1) for k in range(TOP_K)]
    for k in range(TOP_K):
        idx_ref[k:k + 1, :] = idxs[k]
    _expert_ranks(idxs, tri_ref, rank_ref, counts_ref, base_ref)
    lane_rows = lax.broadcasted_iota(jnp.int32, (V7X_LANES, t), 0)
    gates_t = jnp.zeros((V7X_LANES, t), F32)
    for k in range(TOP_K):
        gates_t = jnp.where(lane_rows == k, gates[k], gates_t)
    gcol_ref[...] = gates_t.T


def _pool_layer_kernel(dn_alpha, h_ref, mem_ref, w_in_ref, wbd_ref, pscale_ref, mem_kv_ref, w_out_ref, g_ref, b_ref,
                       rwt_ref, rb_ref, tri_ref, h_out_ref, hp_ref, idx_ref, rank_ref, counts_ref, gcol_ref,
                       ue_ref, w2_ref, w4_ref, w8_ref, kv_ref, cat_ref, base_ref):
    s = pl.program_id(1)
    ts = h_ref.shape[0]
    pool_width = wbd_ref.shape[0]
    hist = POOL_HISTORY

    @pl.when(s == 0)
    def _():
        ue_ref[0:hist, :] = jnp.zeros((hist, pool_width), F32)
        kv_ref[...] = _dot(mem_ref[0].astype(BF16), mem_kv_ref[...]).astype(BF16)

    x = h_ref[...]
    proj = _dot(x.astype(BF16), w_in_ref[...])
    u = proj[:, :pool_width]
    ue_ref[hist:, :] = u
    n = ts + hist
    w2_ref[8:n, :] = ue_ref[8:n, :] + ue_ref[7:n - 1, :]
    w4_ref[16:n, :] = w2_ref[16:n, :] + w2_ref[14:n - 2, :]
    w8_ref[24:n, :] = w4_ref[24:n, :] + w4_ref[20:n - 4, :]
    w16 = w8_ref[hist:n, :] + w8_ref[hist - 8:n - 8, :]
    group_dim = pool_width // len(POOL_WINDOWS)
    col = _lane_iota((ts, pool_width))
    wsum = jnp.where(col < group_dim, w2_ref[hist:n, :],
                     jnp.where(col < 2 * group_dim, w4_ref[hist:n, :],
                               jnp.where(col < 3 * group_dim, w8_ref[hist:n, :], w16)))
    window = jnp.where(col < group_dim, POOL_WINDOWS[0],
                       jnp.where(col < 2 * group_dim, POOL_WINDOWS[1],
                                 jnp.where(col < 3 * group_dim, POOL_WINDOWS[2], POOL_WINDOWS[3])))
    tpos = s * ts + lax.broadcasted_iota(jnp.int32, (ts, pool_width), 0) + 1
    cnt = jnp.minimum(tpos, window).astype(F32)
    d = wsum / cnt - u
    ue_ref[0:hist, :] = u[ts - hist:, :]
    pooled = _dot(d.astype(BF16), wbd_ref[...]) * pscale_ref[...]
    cat_ref[:, :pool_width] = pooled.astype(BF16)
    _memory_attention(proj, kv_ref, cat_ref, pool_width, pool_width)
    _mix_epilogue(x, cat_ref, w_out_ref, g_ref, b_ref, rwt_ref, rb_ref, tri_ref, dn_alpha, h_out_ref, hp_ref, idx_ref,
                  rank_ref, counts_ref, gcol_ref, base_ref)


def _bias_pieces(c):
    hi = c.astype(BF16).astype(F32)
    r = c - hi
    mid = r.astype(BF16).astype(F32)
    lo = (r - mid).astype(BF16).astype(F32)
    return hi, mid, lo


def _bias_lane_pieces(c3, n_heads):
    hi, mid, lo = _bias_pieces(c3)
    lane = _lane_iota(c3.shape)
    out = jnp.where(lane < n_heads, hi, jnp.where(lane < 2 * n_heads, mid, jnp.where(lane < 3 * n_heads, lo,
                    jnp.where(lane == 3 * n_heads, 1.0, 0.0))))
    return out.astype(BF16)


def _key_bias_selector(n_heads):
    sel = np.zeros((V7X_LANES, n_heads * V7X_LANES), np.float32)
    for h in range(n_heads):
        spare = h * V7X_LANES + HEAD_DIM
        for p in range(3):
            sel[3 * n_heads, spare + p] = 1.0
            sel[p * n_heads + h, spare + 3 + p] = -1.0
    return jnp.asarray(sel, BF16)


def _shared_kv_block(h, kvw_ref, fb_ref, tri_ref, sel_ref, k_ref, vt_ref, cq_ref, carry_ref):
    s = pl.program_id(1)
    k_width = k_ref.shape[1]
    v_width = vt_ref.shape[1]
    n_heads = k_width // V7X_LANES

    @pl.when(s == 0)
    def _():
        carry_ref[...] = jnp.zeros(carry_ref.shape, F32)

    proj = _dot(h.astype(BF16), kvw_ref[...])
    z = proj[:, k_width + v_width:] + fb_ref[...]
    logf = jnp.minimum(z, 0.0) - jnp.log(1.0 + jnp.exp(-jnp.abs(z)))
    hi, mid, lo = _bias_pieces(logf)
    tri = tri_ref[...]
    c = _dot(tri, hi.astype(BF16)) + _dot(tri, mid.astype(BF16)) + _dot(tri, lo.astype(BF16)) + carry_ref[0:1, :]
    ts = c.shape[0]
    carry_ref[0:1, :] = c[ts - 1:ts, :]
    pieces = _bias_lane_pieces(c * LOG2_E, n_heads)
    cq_ref[...] = pieces
    k_ref[...] = (proj[:, :k_width] + _dot(pieces, sel_ref[...])).astype(BF16)
    vt_ref[0] = proj[:, k_width:k_width + v_width].T.astype(BF16)


def _fox_layer_kernel(dn_alpha, h_ref, mem_ref, k_ref, vt_ref, cq_ref, w_in_ref, mem_kv_ref, w_out_ref,
                      g_ref, b_ref, rwt_ref, rb_ref, tri_ref, h_out_ref, hp_ref, idx_ref, rank_ref, counts_ref, gcol_ref,
                      kv_ref, cat_ref, qa_ref, m_ref, l_ref, acc_ref, base_ref):
    i = pl.program_id(1)
    tq = h_ref.shape[0]
    n_heads = k_ref.shape[2] // V7X_LANES
    fox_width = vt_ref.shape[1]

    @pl.when(i == 0)
    def _():
        kv_ref[...] = _dot(mem_ref[0].astype(BF16), mem_kv_ref[...]).astype(BF16)

    x = h_ref[...]
    proj = _dot(x.astype(BF16), w_in_ref[...])
    cq = cq_ref[...].astype(F32)
    lane = _lane_iota((tq, V7X_LANES))
    for h in range(n_heads):
        cols = slice(h * V7X_LANES, (h + 1) * V7X_LANES)
        piece = [cq[:, p * n_heads + h:p * n_heads + h + 1] for p in range(3)]
        spare = jnp.where(lane == HEAD_DIM, piece[0], jnp.where(lane == HEAD_DIM + 1, piece[1],
                          jnp.where(lane == HEAD_DIM + 2, piece[2], jnp.where(lane < HEAD_DIM + 6, 1.0, 0.0))))
        qa_ref[h] = jnp.where(lane < HEAD_DIM, proj[:, cols] * (ATTN_SCALE * LOG2_E), spare).astype(BF16)
    m_ref[...] = jnp.full(m_ref.shape, NEG_INF, F32)
    l_ref[...] = jnp.zeros(l_ref.shape, F32)
    acc_ref[...] = jnp.zeros(acc_ref.shape, F32)
    causal_t = lax.broadcasted_iota(jnp.int32, (tq, tq), 0) <= lax.broadcasted_iota(jnp.int32, (tq, tq), 1)

    ones_rows = (lax.broadcasted_iota(jnp.int32, (V7X_BF16_SUBLANES, tq), 0) == 0).astype(BF16)

    def tile(j, masked):
        start = pl.multiple_of(j * tq, tq)

        def scores(h):
            kb = k_ref[0, pl.ds(start, tq), h * V7X_LANES:(h + 1) * V7X_LANES]
            return _dot_nt(kb, qa_ref[h])

        st_next = scores(0)
        for h in range(n_heads):
            st = st_next
            if h + 1 < n_heads:
                st_next = scores(h + 1)
            if masked:
                st = jnp.where(causal_t, st, NEG_INF)
            m_old = m_ref[h:h + 1, :]
            m_new = jnp.maximum(m_old, jnp.max(st, axis=0, keepdims=True))
            alpha = jnp.exp2(m_old - m_new)
            pt = jnp.exp2(st - m_new).astype(BF16)
            m_ref[h:h + 1, :] = m_new
            rows = slice(h * HEAD_DIM, (h + 1) * HEAD_DIM)
            vb = jnp.concatenate([vt_ref[0, rows, pl.ds(start, tq)], ones_rows], axis=0)
            pv = _dot(vb, pt)
            acc_ref[rows, :] = alpha * acc_ref[rows, :] + pv[:HEAD_DIM]
            l_ref[h:h + 1, :] = alpha * l_ref[h:h + 1, :] + pv[HEAD_DIM:HEAD_DIM + 1]

    tile(i, True)

    def off_diagonal(j, carry):
        tile(j, False)
        return carry

    lax.fori_loop(0, i, off_diagonal, 0)
    for h in range(n_heads):
        rows = slice(h * HEAD_DIM, (h + 1) * HEAD_DIM)
        acc_ref[rows, :] = acc_ref[rows, :] / l_ref[h:h + 1, :]
    cat_ref[:, :fox_width] = acc_ref[...].T.astype(BF16)
    _memory_attention(proj, kv_ref, cat_ref, fox_width, n_heads * V7X_LANES)
    _mix_epilogue(x, cat_ref, w_out_ref, g_ref, b_ref, rwt_ref, rb_ref, tri_ref, dn_alpha, h_out_ref, hp_ref, idx_ref,
                  rank_ref, counts_ref, gcol_ref, base_ref)


def _expert_kernel(layer, be_ref, bv_ref, bslot_ref, bnext_ref, nact_ref, x_ref, w1_hbm, b1_ref, w2_hbm, b2_ref, y_ref,
                   w1s_ref, w2s_ref, act_ref, w1f_ref, w2f_ref, sem_ref):
    i = pl.program_id(0)
    active = i < nact_ref[0]
    two_f = w1f_ref.shape[2]
    group = 2 * V7X_LANES
    n_groups = two_f // group

    def weight_copies(expert, slot):
        return (pltpu.make_async_copy(w1_hbm.at[layer, expert], w1f_ref.at[slot], sem_ref.at[0, slot]),
                pltpu.make_async_copy(w2_hbm.at[layer, expert], w2f_ref.at[slot], sem_ref.at[1, slot]))

    @pl.when(i == 0)
    def _():
        for cp in weight_copies(be_ref[0], bslot_ref[0]):
            cp.start()

    @pl.when(jnp.logical_and(active, jnp.logical_or(i == 0, be_ref[i] != be_ref[jnp.maximum(i - 1, 0)])))
    def _():
        slot = bslot_ref[i]
        for cp in weight_copies(be_ref[i], slot):
            cp.wait()

        @pl.when(bnext_ref[i] >= 0)
        def _():
            for cp in weight_copies(bnext_ref[i], 1 - slot):
                cp.start()

        src = lax.broadcasted_iota(jnp.int32, (group, group), 0)
        dst = lax.broadcasted_iota(jnp.int32, (group, group), 1)
        want = jnp.where(dst < V7X_LANES, 2 * dst, 2 * (dst - V7X_LANES) + 1)
        perm = (src == want).astype(BF16)
        for g in range(n_groups):
            cols = slice(g * group, (g + 1) * group)
            w1s_ref[:, cols] = _dot(w1f_ref[slot, :, cols].astype(BF16), perm).astype(BF16)
        w2s_ref[...] = w2f_ref[slot].astype(BF16)

    @pl.when(active)
    def _():
        row_id = lax.broadcasted_iota(jnp.int32, x_ref.shape, 0)
        x_low, x_high = _unpack_bf16_pairs(jnp.where(row_id < bv_ref[i], x_ref[...], jnp.uint32(0)))
        xb = jnp.concatenate([x_low.astype(BF16), x_high.astype(BF16)], axis=1)
        hdn = _dot(xb, w1s_ref[...]) + b1_ref[0]
        for g in range(n_groups):
            x_glu = jnp.minimum(hdn[:, g * group:g * group + V7X_LANES], SWIGLU_LIMIT)
            x_lin = jnp.clip(hdn[:, g * group + V7X_LANES:(g + 1) * group], -SWIGLU_LIMIT, SWIGLU_LIMIT)
            act = x_glu * jax.nn.sigmoid(SWIGLU_ALPHA * x_glu) * (x_lin + 1.0)
            act_ref[:, g * V7X_LANES:(g + 1) * V7X_LANES] = act.astype(BF16)
        y_ref[...] = _pack_bf16_pairs(_dot(act_ref[...], w2s_ref[...]) + b2_ref[0])

    @pl.when(jnp.logical_not(active))
    def _():
        y_ref[...] = jnp.zeros(y_ref.shape, jnp.uint32)


def _combine_rows(dn_alpha, h_ref, y_ref, gcol_ref, g_ref, b_ref):
    gates = gcol_ref[...]
    low, high = _unpack_bf16_pairs(y_ref[0])
    low, high = gates[:, 0:1] * low, gates[:, 0:1] * high
    for k in range(1, TOP_K):
        low_k, high_k = _unpack_bf16_pairs(y_ref[k])
        low = low + gates[:, k:k + 1] * low_k
        high = high + gates[:, k:k + 1] * high_k
    ffn = jnp.concatenate([low, high], axis=1)
    return _layer_norm(dn_alpha * h_ref[...] + ffn, g_ref[...], b_ref[...])


def _combine_kernel(dn_alpha, h_ref, y_ref, gcol_ref, g_ref, b_ref, o_ref):
    o_ref[...] = _combine_rows(dn_alpha, h_ref, y_ref, gcol_ref, g_ref, b_ref)


def _combine_kv_kernel(dn_alpha, h_ref, y_ref, gcol_ref, g_ref, b_ref, kvw_ref, fb_ref, tri_ref, sel_ref,
                       o_ref, k_ref, vt_ref, cq_ref, carry_ref):
    h2 = _combine_rows(dn_alpha, h_ref, y_ref, gcol_ref, g_ref, b_ref)
    o_ref[...] = h2
    _shared_kv_block(h2, kvw_ref, fb_ref, tri_ref, sel_ref, k_ref, vt_ref, cq_ref, carry_ref)


def _sc_gather_rows(table, idx):
    n, = idx.shape
    _, d = table.shape
    info = plsc.get_sparse_core_info()
    n_workers = info.num_cores * info.num_subcores
    per_worker = n // n_workers
    chunk, depth = SC_GATHER_CHUNK, SC_GATHER_DEPTH
    n_chunks = per_worker // chunk
    assert n % (n_workers * chunk) == 0 and n_chunks % depth == 0
    mesh = plsc.VectorSubcoreMesh(core_axis_name="core", subcore_axis_name="subcore")

    @functools.partial(
        pl.kernel,
        out_type=jax.ShapeDtypeStruct((n, d), table.dtype),
        mesh=mesh,
        scratch_types=[
            pltpu.VMEM((per_worker,), jnp.int32),
            pltpu.VMEM((depth, chunk, d), table.dtype),
            pltpu.SemaphoreType.DMA((depth,)),
            pltpu.SemaphoreType.DMA((depth,)),
        ],
    )
    def gather_kernel(table_hbm, idx_hbm, out_hbm, idx_v, rows_v, gather_sem, write_sem):
        worker = lax.axis_index("subcore") * info.num_cores + lax.axis_index("core")
        base = pl.multiple_of(worker * per_worker, chunk)
        pltpu.sync_copy(idx_hbm.at[pl.ds(base, per_worker)], idx_v)

        def gather(c, buf):
            rows = idx_v.at[pl.ds(pl.multiple_of(c * chunk, chunk), chunk)]
            return pltpu.make_async_copy(table_hbm.at[rows], rows_v.at[buf], gather_sem.at[buf])

        def write(c, buf):
            off = pl.multiple_of(base + c * chunk, chunk)
            return pltpu.make_async_copy(rows_v.at[buf], out_hbm.at[pl.ds(off, chunk)], write_sem.at[buf])

        for c in range(depth - 1):
            gather(c, c).start()

        @pl.loop(0, n_chunks, step=depth)
        def _(c0):
            for buf in range(depth):
                c = c0 + buf
                prev = (buf - 1) % depth
                gather(c, buf).wait()
                write(c, buf).start()

                @pl.when(c >= 1)
                def _():
                    write(c - 1, prev).wait()

                @pl.when(c + depth - 1 < n_chunks)
                def _():
                    gather(c + depth - 1, prev).start()

        write(n_chunks - 1, depth - 1).wait()

    return gather_kernel(table, idx)


def _sc_scatter_rows(rows, pos, n_out):
    t, d = rows.shape
    top_k = pos.shape[0]
    pos_flat = pos.reshape(-1)
    info = plsc.get_sparse_core_info()
    n_workers = info.num_cores * info.num_subcores
    per_worker = t // n_workers
    n_chunks = per_worker // SC_CHUNK
    assert t % (n_workers * SC_CHUNK) == 0 and n_chunks % 2 == 0
    mesh = plsc.VectorSubcoreMesh(core_axis_name="core", subcore_axis_name="subcore")

    @functools.partial(
        pl.kernel,
        out_type=jax.ShapeDtypeStruct((n_out, d), rows.dtype),
        mesh=mesh,
        scratch_types=[pltpu.VMEM((2, SC_CHUNK, d), rows.dtype)]
        + [pltpu.VMEM((SC_CHUNK,), jnp.int32) for _ in range(2 * top_k)]
        + [pltpu.SemaphoreType.DMA((2,)), pltpu.SemaphoreType.DMA((2,))],
    )
    def scatter_kernel(rows_hbm, pos_hbm, out_hbm, rows_v, *rest):
        idx_bufs = rest[:2 * top_k]
        load_sem, write_sem = rest[2 * top_k:]
        worker = lax.axis_index("subcore") * info.num_cores + lax.axis_index("core")
        base = pl.multiple_of(worker * per_worker, SC_CHUNK)

        def loads(c, buf):
            off = pl.multiple_of(base + c * SC_CHUNK, SC_CHUNK)
            copies = [pltpu.make_async_copy(rows_hbm.at[pl.ds(off, SC_CHUNK)], rows_v.at[buf], load_sem.at[buf])]
            for k in range(top_k):
                src = pos_hbm.at[pl.ds(pl.multiple_of(k * t + off, SC_CHUNK), SC_CHUNK)]
                copies.append(pltpu.make_async_copy(src, idx_bufs[buf * top_k + k], load_sem.at[buf]))
            return copies

        def scatters(buf):
            return [pltpu.make_async_copy(rows_v.at[buf], out_hbm.at[idx_bufs[buf * top_k + k]], write_sem.at[buf])
                    for k in range(top_k)]

        for cp in loads(0, 0):
            cp.start()

        @pl.loop(0, n_chunks, step=2)
        def _(c0):
            for buf in range(2):
                c = c0 + buf
                for cp in loads(c, buf):
                    cp.wait()

                @pl.when(c >= 1)
                def _():
                    for cp in scatters(1 - buf):
                        cp.wait()

                @pl.when(c + 1 < n_chunks)
                def _():
                    for cp in loads(c + 1, 1 - buf):
                        cp.start()

                for cp in scatters(buf):
                    cp.start()

        for cp in scatters(1):
            cp.wait()

    return scatter_kernel(rows, pos_flat)


def _const_spec(shape):
    zeros = (0,) * len(shape)
    return pl.BlockSpec(shape, lambda *_: zeros, pipeline_mode=pl.Buffered(1))


def _tc_params(n_axes):
    return pltpu.CompilerParams(dimension_semantics=("arbitrary",) * n_axes,
                                vmem_limit_bytes=V7X_VMEM_LIMIT_BYTES)


def _mix_out_shapes(t, d, n_exp):
    return (jax.ShapeDtypeStruct((t, d), F32),
            jax.ShapeDtypeStruct((t, d // 2), jnp.uint32),
            jax.ShapeDtypeStruct((TOP_K, t), jnp.int32),
            jax.ShapeDtypeStruct((TOP_K, t), jnp.int32),
            jax.ShapeDtypeStruct((n_exp, V7X_LANES), F32),
            jax.ShapeDtypeStruct((t, V7X_LANES), F32))


def _mix_out_specs(rows, d, n_exp, row, tok_col):
    return (pl.BlockSpec((rows, d), row),
            pl.BlockSpec((rows, d // 2), row),
            pl.BlockSpec((TOP_K, rows), tok_col),
            pl.BlockSpec((TOP_K, rows), tok_col),
            pl.BlockSpec((n_exp, V7X_LANES), lambda *_: (0, 0)),
            pl.BlockSpec((rows, V7X_LANES), row))


def _strict_upper_tri(n):
    return (lax.broadcasted_iota(jnp.int32, (n, n), 0) < lax.broadcasted_iota(jnp.int32, (n, n), 1)).astype(BF16)


def _pool_layer(h, mem, w_in, wbd, pscale, mem_kv, w_out, ln_g, ln_b, rwt, rb, dn_alpha, seq):
    t, d = h.shape
    n_mem = mem.shape[1]
    pool_width = wbd.shape[0]
    ts = SEQ_BLOCK
    nsb = seq // ts
    row = lambda b, s: (b * nsb + s, 0)
    n_exp = rwt.shape[0]
    tri = _strict_upper_tri(ts)
    return pl.pallas_call(
        functools.partial(_pool_layer_kernel, dn_alpha),
        out_shape=_mix_out_shapes(t, d, n_exp),
        grid=(t // seq, nsb),
        in_specs=[
            pl.BlockSpec((ts, d), row),
            pl.BlockSpec((1, n_mem, d), lambda b, s: (b, 0, 0)),
            _const_spec(w_in.shape), _const_spec(wbd.shape), _const_spec(pscale.shape), _const_spec(mem_kv.shape),
            _const_spec(w_out.shape), _const_spec(ln_g.shape), _const_spec(ln_b.shape), _const_spec(rwt.shape),
            _const_spec(rb.shape), _const_spec(tri.shape),
        ],
        out_specs=_mix_out_specs(ts, d, n_exp, row, lambda b, s: (0, b * nsb + s)),
        scratch_shapes=[pltpu.VMEM((ts + POOL_HISTORY, pool_width), F32)] * 4
        + [pltpu.VMEM((n_mem, 2 * MEM_WIDTH), BF16), pltpu.VMEM((ts, d), BF16), pltpu.VMEM((n_exp, V7X_LANES), F32)],
        compiler_params=_tc_params(2),
        name="pool_layer",
    )(h, mem, w_in, wbd, pscale, mem_kv, w_out, ln_g, ln_b, rwt, rb, tri)


def _expand_heads(w, n_heads):
    d = w.shape[0]
    w3 = w.reshape(d, n_heads, HEAD_DIM)
    return jnp.pad(w3, ((0, 0), (0, 0), (0, V7X_LANES - HEAD_DIM))).reshape(d, n_heads * V7X_LANES)


def _fox_layer(h, mem, k_aug, v_t, cq, w_in, mem_kv, w_out, ln_g, ln_b, rwt, rb, dn_alpha, seq):
    t, d = h.shape
    n_mem = mem.shape[1]
    k_width = k_aug.shape[1]
    v_width = v_t.shape[1]
    n_heads = k_width // V7X_LANES
    stat_rows = -(-n_heads // V7X_SUBLANES) * V7X_SUBLANES
    tq = Q_BLOCK
    nqb = seq // tq
    n_batch = t // seq
    row = lambda b, i: (b * nqb + i, 0)
    per_batch = lambda b, i: (b, 0, 0)
    n_exp = rwt.shape[0]
    tri = _strict_upper_tri(tq)
    return pl.pallas_call(
        functools.partial(_fox_layer_kernel, dn_alpha),
        out_shape=_mix_out_shapes(t, d, n_exp),
        grid=(n_batch, nqb),
        in_specs=[
            pl.BlockSpec((tq, d), row),
            pl.BlockSpec((1, n_mem, d), per_batch),
            pl.BlockSpec((1, seq, k_width), per_batch),
            pl.BlockSpec((1, v_width, seq), per_batch),
            pl.BlockSpec((tq, V7X_LANES), row),
            _const_spec(w_in.shape), _const_spec(mem_kv.shape), _const_spec(w_out.shape), _const_spec(ln_g.shape),
            _const_spec(ln_b.shape), _const_spec(rwt.shape), _const_spec(rb.shape), _const_spec(tri.shape),
        ],
        out_specs=_mix_out_specs(tq, d, n_exp, row, lambda b, i: (0, b * nqb + i)),
        scratch_shapes=[pltpu.VMEM((n_mem, 2 * MEM_WIDTH), BF16), pltpu.VMEM((tq, d), BF16),
                        pltpu.VMEM((n_heads, tq, V7X_LANES), BF16),
                        pltpu.VMEM((stat_rows, tq), F32), pltpu.VMEM((stat_rows, tq), F32),
                        pltpu.VMEM((v_width, tq), F32),
                        pltpu.VMEM((n_exp, V7X_LANES), F32)],
        compiler_params=_tc_params(2),
        name="fox_layer",
    )(h, mem, k_aug.reshape(n_batch, seq, k_width), v_t, cq, w_in, mem_kv, w_out, ln_g, ln_b, rwt, rb, tri)


def _slot_kernel(pad_start_ref, idx_ref, rank_ref, pos_ref):
    pos = rank_ref[...]
    idx = idx_ref[...]
    for e in range(pad_start_ref.shape[0]):
        pos = pos + jnp.where(idx == e, pad_start_ref[e], 0)
    pos_ref[...] = pos


def _route(idx_t, rank_t, counts, n_experts):
    top_k, t = idx_t.shape
    n_assign = top_k * t
    padded = ((counts + MOE_BLOCK - 1) // MOE_BLOCK) * MOE_BLOCK
    pad_end = jnp.cumsum(padded)
    pad_start = pad_end - padded
    n_blocks = n_assign // MOE_BLOCK + n_experts
    block_start = jnp.arange(n_blocks, dtype=jnp.int32) * MOE_BLOCK
    block_expert = jnp.minimum(jnp.sum((pad_end[None, :] <= block_start[:, None]).astype(jnp.int32), axis=1),
                               n_experts - 1)
    block_valid = jnp.clip(counts[block_expert] - (block_start - pad_start[block_expert]), 0, MOE_BLOCK)
    n_active = (pad_end[-1:] // MOE_BLOCK).astype(jnp.int32)
    expert_ids = jnp.arange(n_experts, dtype=jnp.int32)
    in_use = counts > 0
    expert_slot = (jnp.cumsum(in_use.astype(jnp.int32)) - in_use.astype(jnp.int32)) % 2
    later = lax.cummin(jnp.where(in_use, expert_ids, n_experts), axis=0, reverse=True)
    next_in_use = jnp.concatenate([later[1:], jnp.full((1,), n_experts, jnp.int32)])
    next_in_use = jnp.where(next_in_use < n_experts, next_in_use, -1)
    block_slot = expert_slot[block_expert].astype(jnp.int32)
    block_next = next_in_use[block_expert].astype(jnp.int32)
    cols = min(SLOT_COLS, t)
    pos = pl.pallas_call(
        _slot_kernel,
        out_shape=jax.ShapeDtypeStruct((top_k, t), jnp.int32),
        grid_spec=pltpu.PrefetchScalarGridSpec(
            num_scalar_prefetch=1,
            grid=(t // cols,),
            in_specs=[pl.BlockSpec((top_k, cols), lambda i, ps: (0, i))] * 2,
            out_specs=pl.BlockSpec((top_k, cols), lambda i, ps: (0, i)),
        ),
        compiler_params=_tc_params(1),
        name="slots",
    )(pad_start.astype(jnp.int32), idx_t, rank_t)
    return pos, (block_expert, block_valid.astype(jnp.int32), block_slot, block_next, n_active)


def _experts(xs, block_expert, block_valid, block_slot, block_next, n_active, w1_all, b1, w2_all, b2, layer):
    n_slots, half_d = xs.shape
    _, n_exp, d, two_f = w1_all.shape
    d_ff = two_f // 2
    assert d == 2 * half_d
    block = lambda i, *_: (i, 0)
    per_expert = lambda i, be, *_: (be[i], 0, 0)
    grid_spec = pltpu.PrefetchScalarGridSpec(
        num_scalar_prefetch=5,
        grid=(n_slots // MOE_BLOCK,),
        in_specs=[
            pl.BlockSpec((MOE_BLOCK, half_d), block),
            pl.BlockSpec(memory_space=pl.ANY),
            pl.BlockSpec((1, 1, two_f), per_expert),
            pl.BlockSpec(memory_space=pl.ANY),
            pl.BlockSpec((1, 1, d), per_expert),
        ],
        out_specs=pl.BlockSpec((MOE_BLOCK, half_d), block),
        scratch_shapes=[pltpu.VMEM((d, two_f), BF16), pltpu.VMEM((d_ff, d), BF16), pltpu.VMEM((MOE_BLOCK, d_ff), BF16),
                        pltpu.VMEM((2, d, two_f), F32), pltpu.VMEM((2, d_ff, d), F32),
                        pltpu.SemaphoreType.DMA((2, 2))],
    )
    return pl.pallas_call(
        functools.partial(_expert_kernel, layer),
        out_shape=jax.ShapeDtypeStruct((n_slots, half_d), jnp.uint32),
        grid_spec=grid_spec,
        compiler_params=_tc_params(1),
        name="experts",
    )(block_expert, block_valid, block_slot, block_next, n_active, xs, w1_all, b1.reshape(n_exp, 1, two_f), w2_all,
      b2.reshape(n_exp, 1, d))


def _combine(h, yg, gcol, ln_g, ln_b, dn_alpha, seq, shared_kv_weights=None):
    t, d = h.shape
    ts = SEQ_BLOCK
    nsb = seq // ts
    row = lambda b, s: (b * nsb + s, 0)
    in_specs = [pl.BlockSpec((ts, d), row), pl.BlockSpec((TOP_K, ts, d // 2), lambda b, s: (0, b * nsb + s, 0)),
                pl.BlockSpec((ts, V7X_LANES), row), _const_spec(ln_g.shape), _const_spec(ln_b.shape)]
    args = (h, yg.reshape(TOP_K, t, d // 2), gcol, ln_g, ln_b)
    if shared_kv_weights is None:
        return pl.pallas_call(
            functools.partial(_combine_kernel, dn_alpha),
            out_shape=jax.ShapeDtypeStruct((t, d), F32),
            grid=(t // seq, nsb),
            in_specs=in_specs,
            out_specs=pl.BlockSpec((ts, d), row),
            compiler_params=_tc_params(2),
            name="combine",
        )(*args), None
    kvw, fb, sel_k, n_heads = shared_kv_weights
    k_width = n_heads * V7X_LANES
    v_width = n_heads * HEAD_DIM
    tri = (lax.broadcasted_iota(jnp.int32, (ts, ts), 1) <= lax.broadcasted_iota(jnp.int32, (ts, ts), 0)).astype(BF16)
    out = pl.pallas_call(
        functools.partial(_combine_kv_kernel, dn_alpha),
        out_shape=(jax.ShapeDtypeStruct((t, d), F32),
                   jax.ShapeDtypeStruct((t, k_width), BF16),
                   jax.ShapeDtypeStruct((t // seq, v_width, seq), BF16),
                   jax.ShapeDtypeStruct((t, V7X_LANES), BF16)),
        grid=(t // seq, nsb),
        in_specs=in_specs + [_const_spec(kvw.shape), _const_spec(fb.shape), _const_spec(tri.shape),
                             _const_spec(sel_k.shape)],
        out_specs=(pl.BlockSpec((ts, d), row),
                   pl.BlockSpec((ts, k_width), row),
                   pl.BlockSpec((1, v_width, ts), lambda b, s: (b, 0, s)),
                   pl.BlockSpec((ts, V7X_LANES), row)),
        scratch_shapes=[pltpu.VMEM((V7X_SUBLANES, V7X_LANES), F32)],
        compiler_params=_tc_params(2),
        name="combine_shared_kv",
    )(*args, kvw, fb, tri, sel_k)
    return out[0], out[1:]


def _moe(mixed, w1_all, b1, w2_all, b2, layer, ln_g, ln_b, dn_alpha, seq, shared_kv_weights):
    h1, h1_packed, idx_t, rank_t, counts, gcol = mixed
    n_exp = w1_all.shape[1]
    d_ff = w2_all.shape[2]
    pos, block_meta = _route(idx_t, rank_t, counts[:, 0].astype(jnp.int32), n_exp)
    b1p = jnp.swapaxes(b1.reshape(n_exp, d_ff // V7X_LANES, V7X_LANES, 2), 2, 3).reshape(n_exp, 2 * d_ff)
    xs = _sc_scatter_rows(h1_packed, pos, block_meta[0].shape[0] * MOE_BLOCK)
    y = _experts(xs, *block_meta, w1_all, b1p, w2_all, b2, layer)
    yg = _sc_gather_rows(y, pos.reshape(-1))
    return _combine(h1, yg, gcol, ln_g, ln_b, dn_alpha, seq, shared_kv_weights)


def kernel(x, mem, ln1_g, ln1_b, ln2_g, ln2_b, a_w_in, pool_w, pool_scale, a_mem_kv, a_w_out, kv_w, fgate_b, b_w_in, b_mem_kv, b_w_out, router_w, router_b, moe_w1, moe_b1, moe_w2, moe_b2):
    n_batch, seq, d = x.shape
    t = n_batch * seq
    depth = ln1_g.shape[0]
    n_a = a_w_in.shape[0]
    dn_alpha = float((2 * depth) ** 0.25)
    n_exp = router_w.shape[2]
    fox_heads = fgate_b.shape[0]
    fox_width = fox_heads * HEAD_DIM
    assert seq % SEQ_BLOCK == 0 and seq % Q_BLOCK == 0
    assert 3 * fox_heads < V7X_LANES

    assert 1 <= n_a < depth
    pad = V7X_LANES - 3 * fox_heads
    kvw = jnp.concatenate([_expand_heads(kv_w[:, :fox_width], fox_heads),
                           kv_w[:, fox_width:2 * fox_width],
                           jnp.pad(jnp.tile(kv_w[:, 2 * fox_width:], (1, 3)), ((0, 0), (0, pad)))],
                          axis=1).astype(BF16)
    fb = jnp.pad(jnp.tile(fgate_b, 3), (0, pad)).reshape(1, V7X_LANES)
    shared_kv_weights = (kvw, fb, _key_bias_selector(fox_heads), fox_heads)

    h = x.reshape(t, d)
    row = lambda a: a.reshape(1, -1)
    shared = None
    for l in range(depth):
        rwt = router_w[l].T
        rb = router_b[l].reshape(n_exp, 1)
        if l < n_a:
            wbd = jax.scipy.linalg.block_diag(*[pool_w[l, g] for g in range(pool_w.shape[1])]).astype(BF16)
            mixed = _pool_layer(
                h, mem, a_w_in[l].astype(BF16), wbd, row(pool_scale[l]), a_mem_kv[l].astype(BF16),
                a_w_out[l].astype(BF16), row(ln1_g[l]), row(ln1_b[l]), rwt, rb, dn_alpha, seq)
        else:
            j = l - n_a
            k_aug, v_t, cq = shared
            w_in = jnp.concatenate([_expand_heads(b_w_in[j][:, :fox_width], fox_heads),
                                    b_w_in[j][:, fox_width:]], axis=1).astype(BF16)
            mixed = _fox_layer(
                h, mem, k_aug, v_t, cq, w_in, b_mem_kv[j].astype(BF16),
                b_w_out[j].astype(BF16), row(ln1_g[l]), row(ln1_b[l]), rwt, rb, dn_alpha, seq)
        h, new_shared = _moe(mixed, moe_w1, moe_b1[l], moe_w2, moe_b2[l], l, row(ln2_g[l]), row(ln2_b[l]), dn_alpha,
                             seq, shared_kv_weights if l == n_a - 1 else None)
        if new_shared is not None:
            shared = new_shared
    return h.reshape(n_batch, seq, d)
```

```python
import functools
import numpy as np
import jax
import jax.numpy as jnp
from jax import lax
from jax.experimental import pallas as pl
from jax.experimental.pallas import tpu as pltpu
from jax.experimental.pallas import tpu_sc as plsc

HEAD_DIM = 64
MEM_HEADS = 4
MEM_WIDTH = MEM_HEADS * HEAD_DIM
POOL_WINDOWS = (2, 4, 8, 16)
TOP_K = 4
SWIGLU_ALPHA = 1.702
SWIGLU_LIMIT = 7.0
MOE_BLOCK = 512
LN_EPS = 1e-5
ATTN_SCALE = HEAD_DIM ** -0.5
LOG2_E = 1.4426950408889634

V7X_LANES = 128
V7X_SUBLANES = 8
V7X_BF16_SUBLANES = 16
V7X_VMEM_LIMIT_BYTES = 56 * 1024 * 1024
POOL_HISTORY = 32
SC_CHUNK = 64
SEQ_BLOCK = 512
Q_BLOCK = 512
SLOT_COLS = 8192

F32 = jnp.float32
BF16 = jnp.bfloat16
NEG_INF = float("-inf")


def _dot(a, b):
    return jnp.dot(a, b, preferred_element_type=F32)


def _dot_nt(a, b):
    return lax.dot_general(a, b, (((1,), (1,)), ((), ())), preferred_element_type=F32)


def _lane_iota(shape):
    return lax.broadcasted_iota(jnp.int32, shape, len(shape) - 1)


def _pair_attention(q2, keys, vals):
    lane = _lane_iota(q2.shape)
    first = lane < HEAD_DIM
    outs = []
    for sel in (first, jnp.logical_not(first)):
        qh = jnp.where(sel, q2, 0.0).astype(BF16)
        s = _dot_nt(qh, keys)
        m = jnp.max(s, axis=-1, keepdims=True)
        p = jnp.exp(s - m)
        l = jnp.sum(p, axis=-1, keepdims=True)
        outs.append(_dot(p.astype(BF16), vals) / l)
    return jnp.where(first, outs[0], outs[1])


def _memory_attention(proj, kv_ref, cat_ref, out_off, q_off):
    for p in range(MEM_HEADS // 2):
        lo = p * V7X_LANES
        q2 = proj[:, q_off + lo:q_off + lo + V7X_LANES] * ATTN_SCALE
        keys = kv_ref[:, lo:lo + V7X_LANES]
        vals = kv_ref[:, MEM_WIDTH + lo:MEM_WIDTH + lo + V7X_LANES]
        cat_ref[:, out_off + lo:out_off + lo + V7X_LANES] = _pair_attention(q2, keys, vals).astype(BF16)


def _layer_norm(z, g, b):
    mu = jnp.mean(z, axis=-1, keepdims=True)
    zc = z - mu
    var = jnp.mean(zc * zc, axis=-1, keepdims=True)
    return zc * lax.rsqrt(var + LN_EPS) * g + b


def _pack_bf16_pairs(x):
    w = x.shape[1] // 2
    bits = lax.bitcast_convert_type(x.astype(BF16).astype(F32), jnp.uint32)
    return lax.shift_right_logical(bits[:, :w], jnp.uint32(16)) | (bits[:, w:] & jnp.uint32(0xFFFF0000))


def _unpack_bf16_pairs(p):
    low = lax.bitcast_convert_type(lax.shift_left(p, jnp.uint32(16)), F32)
    high = lax.bitcast_convert_type(p & jnp.uint32(0xFFFF0000), F32)
    return low, high


def _expert_ranks(idxs, tri_ref, rank_ref, counts_ref, base_ref):
    @pl.when(jnp.logical_and(pl.program_id(0) == 0, pl.program_id(1) == 0))
    def _():
        base_ref[...] = jnp.zeros(base_ref.shape, F32)

    n_exp = base_ref.shape[0]
    t = idxs[0].shape[1]
    rows = lax.broadcasted_iota(jnp.int32, (n_exp, t), 0)
    running = base_ref[:, 0:1]
    for k in range(TOP_K):
        onehot = rows == idxs[k]
        prefix = _dot(jnp.where(onehot, 1.0, 0.0).astype(BF16), tri_ref[...])
        rank = jnp.sum(jnp.where(onehot, prefix + running, 0.0), axis=0, keepdims=True)
        rank_ref[k:k + 1, :] = rank.astype(jnp.int32)
        running = running + jnp.sum(jnp.where(onehot, 1.0, 0.0), axis=1, keepdims=True)
    base_ref[...] = jnp.broadcast_to(running, base_ref.shape)
    counts_ref[...] = base_ref[...]


def _mix_epilogue(x, cat_ref, w_out_ref, g_ref, b_ref, rwt_ref, rb_ref, tri_ref, dn_alpha, h_out_ref, hp_ref, idx_ref,
                  rank_ref, counts_ref, gcol_ref, base_ref):
    mix = _dot(cat_ref[...], w_out_ref[...])
    h1 = _layer_norm(dn_alpha * x + mix, g_ref[...], b_ref[...])
    h_hi = h1.astype(BF16)
    h_lo = (h1 - h_hi.astype(F32)).astype(BF16)
    rw = rwt_ref[...]
    n_exp = rw.shape[0]
    rw_hi = rw.astype(BF16)
    rw_lo = (rw - rw_hi.astype(F32)).astype(BF16)
    both = _dot_nt(jnp.concatenate([rw_hi, rw_lo], axis=0), h_hi)
    logits_all = both[:n_exp] + both[n_exp:] + _dot_nt(rw_hi, h_lo) + rb_ref[...]
    h_out_ref[...] = h1
    hp_ref[...] = _pack_bf16_pairs(h1)
    t = logits_all.shape[1]
    half = t // 2
    idxs_halves, gates_halves = [], []
    for part in range(2):
        logits = logits_all[:, part * half:(part + 1) * half]
        rows = lax.broadcasted_iota(jnp.int32, (n_exp, half), 0)
        vals, idxs = [], []
        for _ in range(TOP_K):
            m = jnp.max(logits, axis=0, keepdims=True)
            ix = jnp.min(jnp.where(logits == m, rows, n_exp), axis=0, keepdims=True)
            vals.append(m)
            idxs.append(ix)
            logits = jnp.where(rows == ix, NEG_INF, logits)
        exps = [jnp.exp(v - vals[0]) for v in vals]
        denom = exps[0] + exps[1] + exps[2] + exps[3]
        idxs_halves.append(idxs)
        gates_halves.append([e / denom for e in exps])
    idxs = [jnp.concatenate([idxs_halves[0][k], idxs_halves[1][k]], axis=1) for k in range(TOP_K)]
    gates = [jnp.concatenate([gates_halves[0][k], gates_halves[1][k]], axis=1) for k in range(TOP_K)]
    for k in range(TOP_K):
        idx_ref[k:k + 1, :] = idxs[k]
    _expert_ranks(idxs, tri_ref, rank_ref, counts_ref, base_ref)
    lane_rows = lax.broadcasted_iota(jnp.int32, (V7X_LANES, t), 0)
    gates_t = jnp.zeros((V7X_LANES, t), F32)
    for k in range(TOP_K):
        gates_t = jnp.where(lane_rows == k, gates[k], gates_t)
    gcol_ref[...] = gates_t.T


def _pool_layer_kernel(dn_alpha, h_ref, mem_ref, w_in_ref, wbd_ref, pscale_ref, mem_kv_ref, w_out_ref, g_ref, b_ref,
                       rwt_ref, rb_ref, tri_ref, h_out_ref, hp_ref, idx_ref, rank_ref, counts_ref, gcol_ref,
                       ue_ref, w2_ref, w4_ref, w8_ref, kv_ref, cat_ref, base_ref):
    s = pl.program_id(1)
    ts = h_ref.shape[0]
    pool_width = wbd_ref.shape[0]
    hist = POOL_HISTORY

    @pl.when(s == 0)
    def _():
        ue_ref[0:hist, :] = jnp.zeros((hist, pool_width), F32)
        kv_ref[...] = _dot(mem_ref[0].astype(BF16), mem_kv_ref[...]).astype(BF16)

    x = h_ref[...]
    proj = _dot(x.astype(BF16), w_in_ref[...])
    u = proj[:, :pool_width]
    ue_ref[hist:, :] = u
    n = ts + hist
    w2_ref[8:n, :] = ue_ref[8:n, :] + ue_ref[7:n - 1, :]
    w4_ref[16:n, :] = w2_ref[16:n, :] + w2_ref[14:n - 2, :]
    w8_ref[24:n, :] = w4_ref[24:n, :] + w4_ref[20:n - 4, :]
    w16 = w8_ref[hist:n, :] + w8_ref[hist - 8:n - 8, :]
    group_dim = pool_width // len(POOL_WINDOWS)
    col = _lane_iota((ts, pool_width))
    wsum = jnp.where(col < group_dim, w2_ref[hist:n, :],
                     jnp.where(col < 2 * group_dim, w4_ref[hist:n, :],
                               jnp.where(col < 3 * group_dim, w8_ref[hist:n, :], w16)))
    window = jnp.where(col < group_dim, POOL_WINDOWS[0],
                       jnp.where(col < 2 * group_dim, POOL_WINDOWS[1],
                                 jnp.where(col < 3 * group_dim, POOL_WINDOWS[2], POOL_WINDOWS[3])))
    tpos = s * ts + lax.broadcasted_iota(jnp.int32, (ts, pool_width), 0) + 1
    cnt = jnp.minimum(tpos, window).astype(F32)
    d = wsum / cnt - u
    ue_ref[0:hist, :] = u[ts - hist:, :]
    pooled = _dot(d.astype(BF16), wbd_ref[...]) * pscale_ref[...]
    cat_ref[:, :pool_width] = pooled.astype(BF16)
    _memory_attention(proj, kv_ref, cat_ref, pool_width, pool_width)
    _mix_epilogue(x, cat_ref, w_out_ref, g_ref, b_ref, rwt_ref, rb_ref, tri_ref, dn_alpha, h_out_ref, hp_ref, idx_ref,
                  rank_ref, counts_ref, gcol_ref, base_ref)


def _bias_pieces(c):
    hi = c.astype(BF16).astype(F32)
    r = c - hi
    mid = r.astype(BF16).astype(F32)
    lo = (r - mid).astype(BF16).astype(F32)
    return hi, mid, lo


def _bias_lane_pieces(c3, n_heads):
    hi, mid, lo = _bias_pieces(c3)
    lane = _lane_iota(c3.shape)
    out = jnp.where(lane < n_heads, hi, jnp.where(lane < 2 * n_heads, mid, jnp.where(lane < 3 * n_heads, lo,
                    jnp.where(lane == 3 * n_heads, 1.0, 0.0))))
    return out.astype(BF16)


def _key_bias_selector(n_heads):
    sel = np.zeros((V7X_LANES, n_heads * V7X_LANES), np.float32)
    for h in range(n_heads):
        spare = h * V7X_LANES + HEAD_DIM
        for p in range(3):
            sel[3 * n_heads, spare + p] = 1.0
            sel[p * n_heads + h, spare + 3 + p] = -1.0
    return jnp.asarray(sel, BF16)


def _shared_kv_block(h, kvw_ref, fb_ref, tri_ref, sel_ref, k_ref, vt_ref, cq_ref, carry_ref):
    s = pl.program_id(1)
    v_width = vt_ref.shape[1]
    n_heads = k_ref.shape[1] // V7X_LANES

    @pl.when(s == 0)
    def _():
        carry_ref[...] = jnp.zeros(carry_ref.shape, F32)

    proj = _dot(h.astype(BF16), kvw_ref[...])
    z = proj[:, 2 * v_width:] + fb_ref[...]
    logf = jnp.minimum(z, 0.0) - jnp.log(1.0 + jnp.exp(-jnp.abs(z)))
    hi, mid, lo = _bias_pieces(logf)
    tri = tri_ref[...]
    c = _dot(tri, hi.astype(BF16)) + _dot(tri, mid.astype(BF16)) + _dot(tri, lo.astype(BF16)) + carry_ref[0:1, :]
    ts = c.shape[0]
    carry_ref[0:1, :] = c[ts - 1:ts, :]
    pieces = _bias_lane_pieces(c * LOG2_E, n_heads)
    cq_ref[...] = pieces
    bias_lanes = _dot(pieces, sel_ref[...])
    lane = _lane_iota((ts, V7X_LANES))
    for hd in range(n_heads):
        pair = proj[:, (hd // 2) * V7X_LANES:(hd // 2 + 1) * V7X_LANES]
        k_h = pair if hd % 2 == 0 else pltpu.roll(pair, HEAD_DIM, axis=1)
        cols = slice(hd * V7X_LANES, (hd + 1) * V7X_LANES)
        k_ref[:, cols] = jnp.where(lane < HEAD_DIM, k_h, bias_lanes[:, cols]).astype(BF16)
    vt_ref[0] = proj[:, v_width:2 * v_width].T.astype(BF16)


def _fox_layer_kernel(dn_alpha, h_ref, mem_ref, k_ref, vt_ref, cq_ref, w_in_ref, mem_kv_ref, w_out_ref,
                      g_ref, b_ref, rwt_ref, rb_ref, tri_ref, h_out_ref, hp_ref, idx_ref, rank_ref, counts_ref, gcol_ref,
                      kv_ref, cat_ref, qa_ref, m_ref, l_ref, acc_ref, base_ref):
    i = pl.program_id(1)
    tq = h_ref.shape[0]
    n_heads = k_ref.shape[2] // V7X_LANES
    fox_width = vt_ref.shape[1]

    @pl.when(i == 0)
    def _():
        kv_ref[...] = _dot(mem_ref[0].astype(BF16), mem_kv_ref[...]).astype(BF16)

    x = h_ref[...]
    proj = _dot(x.astype(BF16), w_in_ref[...])
    cq = cq_ref[...].astype(F32)
    lane = _lane_iota((tq, V7X_LANES))
    for h in range(n_heads):
        pair = proj[:, (h // 2) * V7X_LANES:(h // 2 + 1) * V7X_LANES] * (ATTN_SCALE * LOG2_E)
        q_h = pair if h % 2 == 0 else pltpu.roll(pair, HEAD_DIM, axis=1)
        piece = [cq[:, p * n_heads + h:p * n_heads + h + 1] for p in range(3)]
        spare = jnp.where(lane == HEAD_DIM, piece[0], jnp.where(lane == HEAD_DIM + 1, piece[1],
                          jnp.where(lane == HEAD_DIM + 2, piece[2], jnp.where(lane < HEAD_DIM + 6, 1.0, 0.0))))
        qa_ref[h] = jnp.where(lane < HEAD_DIM, q_h, spare).astype(BF16)
    m_ref[...] = jnp.full(m_ref.shape, NEG_INF, F32)
    l_ref[...] = jnp.zeros(l_ref.shape, F32)
    acc_ref[...] = jnp.zeros(acc_ref.shape, F32)
    causal_t = lax.broadcasted_iota(jnp.int32, (tq, tq), 0) <= lax.broadcasted_iota(jnp.int32, (tq, tq), 1)

    ones_rows = (lax.broadcasted_iota(jnp.int32, (V7X_BF16_SUBLANES, tq), 0) == 0).astype(BF16)

    def tile(j, masked):
        start = pl.multiple_of(j * tq, tq)

        def scores(h):
            kb = k_ref[0, pl.ds(start, tq), h * V7X_LANES:(h + 1) * V7X_LANES]
            return _dot_nt(kb, qa_ref[h])

        st_next = scores(0)
        for h in range(n_heads):
            st = st_next
            if h + 1 < n_heads:
                st_next = scores(h + 1)
            if masked:
                st = jnp.where(causal_t, st, NEG_INF)
            m_old = m_ref[h:h + 1, :]
            m_new = jnp.maximum(m_old, jnp.max(st, axis=0, keepdims=True))
            alpha = jnp.exp2(m_old - m_new)
            pt = jnp.exp2(st - m_new).astype(BF16)
            m_ref[h:h + 1, :] = m_new
            rows = slice(h * HEAD_DIM, (h + 1) * HEAD_DIM)
            vb = jnp.concatenate([vt_ref[0, rows, pl.ds(start, tq)], ones_rows], axis=0)
            pv = _dot(vb, pt)
            acc_ref[rows, :] = alpha * acc_ref[rows, :] + pv[:HEAD_DIM]
            l_ref[h:h + 1, :] = alpha * l_ref[h:h + 1, :] + pv[HEAD_DIM:HEAD_DIM + 1]

    tile(i, True)

    def off_diagonal(j, carry):
        tile(j, False)
        return carry

    lax.fori_loop(0, i, off_diagonal, 0)
    for h in range(n_heads):
        rows = slice(h * HEAD_DIM, (h + 1) * HEAD_DIM)
        acc_ref[rows, :] = acc_ref[rows, :] / l_ref[h:h + 1, :]
    cat_ref[:, :fox_width] = acc_ref[...].T.astype(BF16)
    _memory_attention(proj, kv_ref, cat_ref, fox_width, fox_width)
    _mix_epilogue(x, cat_ref, w_out_ref, g_ref, b_ref, rwt_ref, rb_ref, tri_ref, dn_alpha, h_out_ref, hp_ref, idx_ref,
                  rank_ref, counts_ref, gcol_ref, base_ref)


def _expert_kernel(layer, be_ref, bv_ref, bslot_ref, bnext_ref, nact_ref, x_ref, w1_hbm, b1_ref, w2_hbm, b2_ref, y_ref,
                   w1s_ref, w2s_ref, act_ref, w1f_ref, w2f_ref, sem_ref):
    i = pl.program_id(0)
    active = i < nact_ref[0]
    two_f = w1f_ref.shape[2]
    group = 2 * V7X_LANES
    n_groups = two_f // group

    def weight_copies(expert, slot):
        return (pltpu.make_async_copy(w1_hbm.at[layer, expert], w1f_ref.at[slot], sem_ref.at[0, slot]),
                pltpu.make_async_copy(w2_hbm.at[layer, expert], w2f_ref.at[slot], sem_ref.at[1, slot]))

    @pl.when(i == 0)
    def _():
        for cp in weight_copies(be_ref[0], bslot_ref[0]):
            cp.start()

    @pl.when(jnp.logical_and(active, jnp.logical_or(i == 0, be_ref[i] != be_ref[jnp.maximum(i - 1, 0)])))
    def _():
        slot = bslot_ref[i]
        for cp in weight_copies(be_ref[i], slot):
            cp.wait()

        @pl.when(bnext_ref[i] >= 0)
        def _():
            for cp in weight_copies(bnext_ref[i], 1 - slot):
                cp.start()

        src = lax.broadcasted_iota(jnp.int32, (group, group), 0)
        dst = lax.broadcasted_iota(jnp.int32, (group, group), 1)
        want = jnp.where(dst < V7X_LANES, 2 * dst, 2 * (dst - V7X_LANES) + 1)
        perm = (src == want).astype(BF16)
        for g in range(n_groups):
            cols = slice(g * group, (g + 1) * group)
            w1s_ref[:, cols] = _dot(w1f_ref[slot, :, cols].astype(BF16), perm).astype(BF16)
        w2s_ref[...] = w2f_ref[slot].astype(BF16)

    @pl.when(active)
    def _():
        row_id = lax.broadcasted_iota(jnp.int32, x_ref.shape, 0)
        x_low, x_high = _unpack_bf16_pairs(jnp.where(row_id < bv_ref[i], x_ref[...], jnp.uint32(0)))
        xb = jnp.concatenate([x_low.astype(BF16), x_high.astype(BF16)], axis=1)
        hdn = _dot(xb, w1s_ref[...]) + b1_ref[0]
        for g in range(n_groups):
            x_glu = jnp.minimum(hdn[:, g * group:g * group + V7X_LANES], SWIGLU_LIMIT)
            x_lin = jnp.clip(hdn[:, g * group + V7X_LANES:(g + 1) * group], -SWIGLU_LIMIT, SWIGLU_LIMIT)
            act = x_glu * jax.nn.sigmoid(SWIGLU_ALPHA * x_glu) * (x_lin + 1.0)
            act_ref[:, g * V7X_LANES:(g + 1) * V7X_LANES] = act.astype(BF16)
        y_ref[...] = _pack_bf16_pairs(_dot(act_ref[...], w2s_ref[...]) + b2_ref[0])

    @pl.when(jnp.logical_not(active))
    def _():
        y_ref[...] = jnp.zeros(y_ref.shape, jnp.uint32)


def _combine_rows(dn_alpha, h_ref, y_ref, gcol_ref, g_ref, b_ref):
    gates = gcol_ref[...]
    low, high = _unpack_bf16_pairs(y_ref[0])
    low, high = gates[:, 0:1] * low, gates[:, 0:1] * high
    for k in range(1, TOP_K):
        low_k, high_k = _unpack_bf16_pairs(y_ref[k])
        low = low + gates[:, k:k + 1] * low_k
        high = high + gates[:, k:k + 1] * high_k
    ffn = jnp.concatenate([low, high], axis=1)
    return _layer_norm(dn_alpha * h_ref[...] + ffn, g_ref[...], b_ref[...])


def _combine_kernel(dn_alpha, h_ref, y_ref, gcol_ref, g_ref, b_ref, o_ref):
    o_ref[...] = _combine_rows(dn_alpha, h_ref, y_ref, gcol_ref, g_ref, b_ref)


def _combine_kv_kernel(dn_alpha, h_ref, y_ref, gcol_ref, g_ref, b_ref, kvw_ref, fb_ref, tri_ref, sel_ref,
                       o_ref, k_ref, vt_ref, cq_ref, carry_ref):
    h2 = _combine_rows(dn_alpha, h_ref, y_ref, gcol_ref, g_ref, b_ref)
    o_ref[...] = h2
    _shared_kv_block(h2, kvw_ref, fb_ref, tri_ref, sel_ref, k_ref, vt_ref, cq_ref, carry_ref)


def _sc_gather_rows(table, idx):
    n, = idx.shape
    _, d = table.shape
    info = plsc.get_sparse_core_info()
    n_workers = info.num_cores * info.num_subcores
    per_worker = n // n_workers
    n_chunks = per_worker // SC_CHUNK
    assert n % (n_workers * SC_CHUNK) == 0 and n_chunks % 2 == 0
    mesh = plsc.VectorSubcoreMesh(core_axis_name="core", subcore_axis_name="subcore")

    @functools.partial(
        pl.kernel,
        out_type=jax.ShapeDtypeStruct((n, d), table.dtype),
        mesh=mesh,
        scratch_types=[
            pltpu.VMEM((per_worker,), jnp.int32),
            pltpu.VMEM((2, SC_CHUNK, d), table.dtype),
            pltpu.SemaphoreType.DMA((2,)),
            pltpu.SemaphoreType.DMA((2,)),
        ],
    )
    def gather_kernel(table_hbm, idx_hbm, out_hbm, idx_v, rows_v, gather_sem, write_sem):
        worker = lax.axis_index("subcore") * info.num_cores + lax.axis_index("core")
        base = pl.multiple_of(worker * per_worker, SC_CHUNK)
        pltpu.sync_copy(idx_hbm.at[pl.ds(base, per_worker)], idx_v)

        def gather(c, buf):
            rows = idx_v.at[pl.ds(pl.multiple_of(c * SC_CHUNK, SC_CHUNK), SC_CHUNK)]
            return pltpu.make_async_copy(table_hbm.at[rows], rows_v.at[buf], gather_sem.at[buf])

        def write(c, buf):
            off = pl.multiple_of(base + c * SC_CHUNK, SC_CHUNK)
            return pltpu.make_async_copy(rows_v.at[buf], out_hbm.at[pl.ds(off, SC_CHUNK)], write_sem.at[buf])

        gather(0, 0).start()

        @pl.loop(0, n_chunks, step=2)
        def _(c0):
            for buf in range(2):
                c = c0 + buf
                gather(c, buf).wait()

                @pl.when(c >= 1)
                def _():
                    write(c - 1, 1 - buf).wait()

                @pl.when(c + 1 < n_chunks)
                def _():
                    gather(c + 1, 1 - buf).start()

                write(c, buf).start()

        write(n_chunks - 1, 1).wait()

    return gather_kernel(table, idx)


def _sc_scatter_rows(rows, pos, n_out):
    t, d = rows.shape
    top_k = pos.shape[0]
    pos_flat = pos.reshape(-1)
    info = plsc.get_sparse_core_info()
    n_workers = info.num_cores * info.num_subcores
    per_worker = t // n_workers
    n_chunks = per_worker // SC_CHUNK
    assert t % (n_workers * SC_CHUNK) == 0 and n_chunks % 2 == 0
    mesh = plsc.VectorSubcoreMesh(core_axis_name="core", subcore_axis_name="subcore")

    @functools.partial(
        pl.kernel,
        out_type=jax.ShapeDtypeStruct((n_out, d), rows.dtype),
        mesh=mesh,
        scratch_types=[pltpu.VMEM((2, SC_CHUNK, d), rows.dtype)]
        + [pltpu.VMEM((SC_CHUNK,), jnp.int32) for _ in range(2 * top_k)]
        + [pltpu.SemaphoreType.DMA((2,)), pltpu.SemaphoreType.DMA((2,))],
    )
    def scatter_kernel(rows_hbm, pos_hbm, out_hbm, rows_v, *rest):
        idx_bufs = rest[:2 * top_k]
        load_sem, write_sem = rest[2 * top_k:]
        worker = lax.axis_index("subcore") * info.num_cores + lax.axis_index("core")
        base = pl.multiple_of(worker * per_worker, SC_CHUNK)

        def loads(c, buf):
            off = pl.multiple_of(base + c * SC_CHUNK, SC_CHUNK)
            copies = [pltpu.make_async_copy(rows_hbm.at[pl.ds(off, SC_CHUNK)], rows_v.at[buf], load_sem.at[buf])]
            for k in range(top_k):
                src = pos_hbm.at[pl.ds(pl.multiple_of(k * t + off, SC_CHUNK), SC_CHUNK)]
                copies.append(pltpu.make_async_copy(src, idx_bufs[buf * top_k + k], load_sem.at[buf]))
            return copies

        def scatters(buf):
            return [pltpu.make_async_copy(rows_v.at[buf], out_hbm.at[idx_bufs[buf * top_k + k]], write_sem.at[buf])
                    for k in range(top_k)]

        for cp in loads(0, 0):
            cp.start()

        @pl.loop(0, n_chunks, step=2)
        def _(c0):
            for buf in range(2):
                c = c0 + buf
                for cp in loads(c, buf):
                    cp.wait()

                @pl.when(c >= 1)
                def _():
                    for cp in scatters(1 - buf):
                        cp.wait()

                @pl.when(c + 1 < n_chunks)
                def _():
                    for cp in loads(c + 1, 1 - buf):
                        cp.start()

                for cp in scatters(buf):
                    cp.start()

        for cp in scatters(1):
            cp.wait()

    return scatter_kernel(rows, pos_flat)


def _const_spec(shape):
    zeros = (0,) * len(shape)
    return pl.BlockSpec(shape, lambda *_: zeros, pipeline_mode=pl.Buffered(1))


def _tc_params(n_axes):
    return pltpu.CompilerParams(dimension_semantics=("arbitrary",) * n_axes,
                                vmem_limit_bytes=V7X_VMEM_LIMIT_BYTES)


def _mix_out_shapes(t, d, n_exp):
    return (jax.ShapeDtypeStruct((t, d), F32),
            jax.ShapeDtypeStruct((t, d // 2), jnp.uint32),
            jax.ShapeDtypeStruct((TOP_K, t), jnp.int32),
            jax.ShapeDtypeStruct((TOP_K, t), jnp.int32),
            jax.ShapeDtypeStruct((n_exp, V7X_LANES), F32),
            jax.ShapeDtypeStruct((t, V7X_LANES), F32))


def _mix_out_specs(rows, d, n_exp, row, tok_col):
    return (pl.BlockSpec((rows, d), row),
            pl.BlockSpec((rows, d // 2), row),
            pl.BlockSpec((TOP_K, rows), tok_col),
            pl.BlockSpec((TOP_K, rows), tok_col),
            pl.BlockSpec((n_exp, V7X_LANES), lambda *_: (0, 0)),
            pl.BlockSpec((rows, V7X_LANES), row))


def _strict_upper_tri(n):
    return (lax.broadcasted_iota(jnp.int32, (n, n), 0) < lax.broadcasted_iota(jnp.int32, (n, n), 1)).astype(BF16)


def _pool_layer(h, mem, w_in, wbd, pscale, mem_kv, w_out, ln_g, ln_b, rwt, rb, dn_alpha, seq):
    t, d = h.shape
    n_mem = mem.shape[1]
    pool_width = wbd.shape[0]
    ts = SEQ_BLOCK
    nsb = seq // ts
    row = lambda b, s: (b * nsb + s, 0)
    n_exp = rwt.shape[0]
    tri = _strict_upper_tri(ts)
    return pl.pallas_call(
        functools.partial(_pool_layer_kernel, dn_alpha),
        out_shape=_mix_out_shapes(t, d, n_exp),
        grid=(t // seq, nsb),
        in_specs=[
            pl.BlockSpec((ts, d), row),
            pl.BlockSpec((1, n_mem, d), lambda b, s: (b, 0, 0)),
            _const_spec(w_in.shape), _const_spec(wbd.shape), _const_spec(pscale.shape), _const_spec(mem_kv.shape),
            _const_spec(w_out.shape), _const_spec(ln_g.shape), _const_spec(ln_b.shape), _const_spec(rwt.shape),
            _const_spec(rb.shape), _const_spec(tri.shape),
        ],
        out_specs=_mix_out_specs(ts, d, n_exp, row, lambda b, s: (0, b * nsb + s)),
        scratch_shapes=[pltpu.VMEM((ts + POOL_HISTORY, pool_width), F32)] * 4
        + [pltpu.VMEM((n_mem, 2 * MEM_WIDTH), BF16), pltpu.VMEM((ts, d), BF16), pltpu.VMEM((n_exp, V7X_LANES), F32)],
        compiler_params=_tc_params(2),
        name="pool_layer",
    )(h, mem, w_in, wbd, pscale, mem_kv, w_out, ln_g, ln_b, rwt, rb, tri)


def _fox_layer(h, mem, k_aug, v_t, cq, w_in, mem_kv, w_out, ln_g, ln_b, rwt, rb, dn_alpha, seq):
    t, d = h.shape
    n_mem = mem.shape[1]
    k_width = k_aug.shape[1]
    v_width = v_t.shape[1]
    n_heads = k_width // V7X_LANES
    stat_rows = -(-n_heads // V7X_SUBLANES) * V7X_SUBLANES
    tq = Q_BLOCK
    nqb = seq // tq
    n_batch = t // seq
    row = lambda b, i: (b * nqb + i, 0)
    per_batch = lambda b, i: (b, 0, 0)
    n_exp = rwt.shape[0]
    tri = _strict_upper_tri(tq)
    return pl.pallas_call(
        functools.partial(_fox_layer_kernel, dn_alpha),
        out_shape=_mix_out_shapes(t, d, n_exp),
        grid=(n_batch, nqb),
        in_specs=[
            pl.BlockSpec((tq, d), row),
            pl.BlockSpec((1, n_mem, d), per_batch),
            pl.BlockSpec((1, seq, k_width), per_batch),
            pl.BlockSpec((1, v_width, seq), per_batch),
            pl.BlockSpec((tq, V7X_LANES), row),
            _const_spec(w_in.shape), _const_spec(mem_kv.shape), _const_spec(w_out.shape), _const_spec(ln_g.shape),
            _const_spec(ln_b.shape), _const_spec(rwt.shape), _const_spec(rb.shape), _const_spec(tri.shape),
        ],
        out_specs=_mix_out_specs(tq, d, n_exp, row, lambda b, i: (0, b * nqb + i)),
        scratch_shapes=[pltpu.VMEM((n_mem, 2 * MEM_WIDTH), BF16), pltpu.VMEM((tq, d), BF16),
                        pltpu.VMEM((n_heads, tq, V7X_LANES), BF16),
                        pltpu.VMEM((stat_rows, tq), F32), pltpu.VMEM((stat_rows, tq), F32),
                        pltpu.VMEM((v_width, tq), F32),
                        pltpu.VMEM((n_exp, V7X_LANES), F32)],
        compiler_params=_tc_params(2),
        name="fox_layer",
    )(h, mem, k_aug.reshape(n_batch, seq, k_width), v_t, cq, w_in, mem_kv, w_out, ln_g, ln_b, rwt, rb, tri)


def _slot_kernel(pad_start_ref, idx_ref, rank_ref, pos_ref):
    pos = rank_ref[...]
    idx = idx_ref[...]
    for e in range(pad_start_ref.shape[0]):
        pos = pos + jnp.where(idx == e, pad_start_ref[e], 0)
    pos_ref[...] = pos


def _route(idx_t, rank_t, counts, n_experts):
    top_k, t = idx_t.shape
    n_assign = top_k * t
    padded = ((counts + MOE_BLOCK - 1) // MOE_BLOCK) * MOE_BLOCK
    pad_end = jnp.cumsum(padded)
    pad_start = pad_end - padded
    n_blocks = n_assign // MOE_BLOCK + n_experts
    block_start = jnp.arange(n_blocks, dtype=jnp.int32) * MOE_BLOCK
    block_expert = jnp.minimum(jnp.sum((pad_end[None, :] <= block_start[:, None]).astype(jnp.int32), axis=1),
                               n_experts - 1)
    block_valid = jnp.clip(counts[block_expert] - (block_start - pad_start[block_expert]), 0, MOE_BLOCK)
    n_active = (pad_end[-1:] // MOE_BLOCK).astype(jnp.int32)
    expert_ids = jnp.arange(n_experts, dtype=jnp.int32)
    in_use = counts > 0
    expert_slot = (jnp.cumsum(in_use.astype(jnp.int32)) - in_use.astype(jnp.int32)) % 2
    later = lax.cummin(jnp.where(in_use, expert_ids, n_experts), axis=0, reverse=True)
    next_in_use = jnp.concatenate([later[1:], jnp.full((1,), n_experts, jnp.int32)])
    next_in_use = jnp.where(next_in_use < n_experts, next_in_use, -1)
    block_slot = expert_slot[block_expert].astype(jnp.int32)
    block_next = next_in_use[block_expert].astype(jnp.int32)
    cols = min(SLOT_COLS, t)
    pos = pl.pallas_call(
        _slot_kernel,
        out_shape=jax.ShapeDtypeStruct((top_k, t), jnp.int32),
        grid_spec=pltpu.PrefetchScalarGridSpec(
            num_scalar_prefetch=1,
            grid=(t // cols,),
            in_specs=[pl.BlockSpec((top_k, cols), lambda i, ps: (0, i))] * 2,
            out_specs=pl.BlockSpec((top_k, cols), lambda i, ps: (0, i)),
        ),
        compiler_params=_tc_params(1),
        name="slots",
    )(pad_start.astype(jnp.int32), idx_t, rank_t)
    return pos, (block_expert, block_valid.astype(jnp.int32), block_slot, block_next, n_active)


def _experts(xs, block_expert, block_valid, block_slot, block_next, n_active, w1_all, b1, w2_all, b2, layer):
    n_slots, half_d = xs.shape
    _, n_exp, d, two_f = w1_all.shape
    d_ff = two_f // 2
    assert d == 2 * half_d
    block = lambda i, *_: (i, 0)
    per_expert = lambda i, be, *_: (be[i], 0, 0)
    grid_spec = pltpu.PrefetchScalarGridSpec(
        num_scalar_prefetch=5,
        grid=(n_slots // MOE_BLOCK,),
        in_specs=[
            pl.BlockSpec((MOE_BLOCK, half_d), block),
            pl.BlockSpec(memory_space=pl.ANY),
            pl.BlockSpec((1, 1, two_f), per_expert),
            pl.BlockSpec(memory_space=pl.ANY),
            pl.BlockSpec((1, 1, d), per_expert),
        ],
        out_specs=pl.BlockSpec((MOE_BLOCK, half_d), block),
        scratch_shapes=[pltpu.VMEM((d, two_f), BF16), pltpu.VMEM((d_ff, d), BF16), pltpu.VMEM((MOE_BLOCK, d_ff), BF16),
                        pltpu.VMEM((2, d, two_f), F32), pltpu.VMEM((2, d_ff, d), F32),
                        pltpu.SemaphoreType.DMA((2, 2))],
    )
    return pl.pallas_call(
        functools.partial(_expert_kernel, layer),
        out_shape=jax.ShapeDtypeStruct((n_slots, half_d), jnp.uint32),
        grid_spec=grid_spec,
        compiler_params=_tc_params(1),
        name="experts",
    )(block_expert, block_valid, block_slot, block_next, n_active, xs, w1_all, b1.reshape(n_exp, 1, two_f), w2_all,
      b2.reshape(n_exp, 1, d))


def _combine(h, yg, gcol, ln_g, ln_b, dn_alpha, seq, shared_kv_weights=None):
    t, d = h.shape
    ts = SEQ_BLOCK
    nsb = seq // ts
    row = lambda b, s: (b * nsb + s, 0)
    in_specs = [pl.BlockSpec((ts, d), row), pl.BlockSpec((TOP_K, ts, d // 2), lambda b, s: (0, b * nsb + s, 0)),
                pl.BlockSpec((ts, V7X_LANES), row), _const_spec(ln_g.shape), _const_spec(ln_b.shape)]
    args = (h, yg.reshape(TOP_K, t, d // 2), gcol, ln_g, ln_b)
    if shared_kv_weights is None:
        return pl.pallas_call(
            functools.partial(_combine_kernel, dn_alpha),
            out_shape=jax.ShapeDtypeStruct((t, d), F32),
            grid=(t // seq, nsb),
            in_specs=in_specs,
            out_specs=pl.BlockSpec((ts, d), row),
            compiler_params=_tc_params(2),
            name="combine",
        )(*args), None
    kvw, fb, sel_k, n_heads = shared_kv_weights
    k_width = n_heads * V7X_LANES
    v_width = n_heads * HEAD_DIM
    tri = (lax.broadcasted_iota(jnp.int32, (ts, ts), 1) <= lax.broadcasted_iota(jnp.int32, (ts, ts), 0)).astype(BF16)
    out = pl.pallas_call(
        functools.partial(_combine_kv_kernel, dn_alpha),
        out_shape=(jax.ShapeDtypeStruct((t, d), F32),
                   jax.ShapeDtypeStruct((t, k_width), BF16),
                   jax.ShapeDtypeStruct((t // seq, v_width, seq), BF16),
                   jax.ShapeDtypeStruct((t, V7X_LANES), BF16)),
        grid=(t // seq, nsb),
        in_specs=in_specs + [_const_spec(kvw.shape), _const_spec(fb.shape), _const_spec(tri.shape),
                             _const_spec(sel_k.shape)],
        out_specs=(pl.BlockSpec((ts, d), row),
                   pl.BlockSpec((ts, k_width), row),
                   pl.BlockSpec((1, v_width, ts), lambda b, s: (b, 0, s)),
                   pl.BlockSpec((ts, V7X_LANES), row)),
        scratch_shapes=[pltpu.VMEM((V7X_SUBLANES, V7X_LANES), F32)],
        compiler_params=_tc_params(2),
        name="combine_shared_kv",
    )(*args, kvw, fb, tri, sel_k)
    return out[0], out[1:]


def _moe(mixed, w1_all, b1, w2_all, b2, layer, ln_g, ln_b, dn_alpha, seq, shared_kv_weights):
    h1, h1_packed, idx_t, rank_t, counts, gcol = mixed
    n_exp = w1_all.shape[1]
    d_ff = w2_all.shape[2]
    pos, block_meta = _route(idx_t, rank_t, counts[:, 0].astype(jnp.int32), n_exp)
    b1p = jnp.swapaxes(b1.reshape(n_exp, d_ff // V7X_LANES, V7X_LANES, 2), 2, 3).reshape(n_exp, 2 * d_ff)
    xs = _sc_scatter_rows(h1_packed, pos, block_meta[0].shape[0] * MOE_BLOCK)
    y = _experts(xs, *block_meta, w1_all, b1p, w2_all, b2, layer)
    yg = _sc_gather_rows(y, pos.reshape(-1))
    return _combine(h1, yg, gcol, ln_g, ln_b, dn_alpha, seq, shared_kv_weights)


def kernel(x, mem, ln1_g, ln1_b, ln2_g, ln2_b, a_w_in, pool_w, pool_scale, a_mem_kv, a_w_out, kv_w, fgate_b, b_w_in, b_mem_kv, b_w_out, router_w, router_b, moe_w1, moe_b1, moe_w2, moe_b2):
    n_batch, seq, d = x.shape
    t = n_batch * seq
    depth = ln1_g.shape[0]
    n_a = a_w_in.shape[0]
    dn_alpha = float((2 * depth) ** 0.25)
    n_exp = router_w.shape[2]
    fox_heads = fgate_b.shape[0]
    fox_width = fox_heads * HEAD_DIM
    assert seq % SEQ_BLOCK == 0 and seq % Q_BLOCK == 0
    assert 3 * fox_heads < V7X_LANES

    assert 1 <= n_a < depth
    pad = V7X_LANES - 3 * fox_heads
    kvw = jnp.concatenate([kv_w[:, :2 * fox_width],
                           jnp.pad(jnp.tile(kv_w[:, 2 * fox_width:], (1, 3)), ((0, 0), (0, pad)))],
                          axis=1).astype(BF16)
    fb = jnp.pad(jnp.tile(fgate_b, 3), (0, pad)).reshape(1, V7X_LANES)
    shared_kv_weights = (kvw, fb, _key_bias_selector(fox_heads), fox_heads)

    h = x.reshape(t, d)
    row = lambda a: a.reshape(1, -1)
    shared = None
    for l in range(depth):
        rwt = router_w[l].T
        rb = router_b[l].reshape(n_exp, 1)
        if l < n_a:
            wbd = jax.scipy.linalg.block_diag(*[pool_w[l, g] for g in range(pool_w.shape[1])]).astype(BF16)
            mixed = _pool_layer(
                h, mem, a_w_in[l].astype(BF16), wbd, row(pool_scale[l]), a_mem_kv[l].astype(BF16),
                a_w_out[l].astype(BF16), row(ln1_g[l]), row(ln1_b[l]), rwt, rb, dn_alpha, seq)
        else:
            j = l - n_a
            k_aug, v_t, cq = shared
            mixed = _fox_layer(
                h, mem, k_aug, v_t, cq, b_w_in[j].astype(BF16), b_mem_kv[j].astype(BF16),
                b_w_out[j].astype(BF16), row(ln1_g[l]), row(ln1_b[l]), rwt, rb, dn_alpha, seq)
        h, new_shared = _moe(mixed, moe_w1, moe_b1[l], moe_w2, moe_b2[l], l, row(ln2_g[l]), row(ln2_b[l]), dn_alpha,
                             seq, shared_kv_weights if l == n_a - 1 else None)
        if new_shared is not None:
            shared = new_shared
    return h.reshape(n_batch, seq, d)
```

```python
import functools
import numpy as np
import jax
import jax.numpy as jnp
from jax import lax
from jax.experimental import pallas as pl
from jax.experimental.pallas import tpu as pltpu
from jax.experimental.pallas import tpu_sc as plsc

HEAD_DIM = 64
MEM_HEADS = 4
MEM_WIDTH = MEM_HEADS * HEAD_DIM
POOL_WINDOWS = (2, 4, 8, 16)
TOP_K = 4
SWIGLU_ALPHA = 1.702
SWIGLU_LIMIT = 7.0
MOE_BLOCK = 1024
LN_EPS = 1e-5
ATTN_SCALE = HEAD_DIM ** -0.5
LOG2_E = 1.4426950408889634

V7X_LANES = 128
V7X_SUBLANES = 8
V7X_BF16_SUBLANES = 16
V7X_VMEM_LIMIT_BYTES = 56 * 1024 * 1024
POOL_HISTORY = 32
SC_CHUNK = 64
SEQ_BLOCK = 512
Q_BLOCK = 512
SLOT_COLS = 8192

F32 = jnp.float32
BF16 = jnp.bfloat16
NEG_INF = float("-inf")


def _dot(a, b):
    return jnp.dot(a, b, preferred_element_type=F32)


def _dot_nt(a, b):
    return lax.dot_general(a, b, (((1,), (1,)), ((), ())), preferred_element_type=F32)


def _lane_iota(shape):
    return lax.broadcasted_iota(jnp.int32, shape, len(shape) - 1)


def _pair_attention(q2, keys, vals):
    lane = _lane_iota(q2.shape)
    first = lane < HEAD_DIM
    outs = []
    for sel in (first, jnp.logical_not(first)):
        qh = jnp.where(sel, q2, 0.0).astype(BF16)
        s = _dot_nt(qh, keys)
        m = jnp.max(s, axis=-1, keepdims=True)
        p = jnp.exp(s - m)
        l = jnp.sum(p, axis=-1, keepdims=True)
        outs.append(_dot(p.astype(BF16), vals) / l)
    return jnp.where(first, outs[0], outs[1])


def _memory_attention(proj, kv_ref, cat_ref, out_off, q_off):
    for p in range(MEM_HEADS // 2):
        lo = p * V7X_LANES
        q2 = proj[:, q_off + lo:q_off + lo + V7X_LANES] * ATTN_SCALE
        keys = kv_ref[:, lo:lo + V7X_LANES]
        vals = kv_ref[:, MEM_WIDTH + lo:MEM_WIDTH + lo + V7X_LANES]
        cat_ref[:, out_off + lo:out_off + lo + V7X_LANES] = _pair_attention(q2, keys, vals).astype(BF16)


def _layer_norm(z, g, b):
    mu = jnp.mean(z, axis=-1, keepdims=True)
    zc = z - mu
    var = jnp.mean(zc * zc, axis=-1, keepdims=True)
    return zc * lax.rsqrt(var + LN_EPS) * g + b


def _pack_bf16_pairs(x):
    w = x.shape[1] // 2
    bits = lax.bitcast_convert_type(x.astype(BF16).astype(F32), jnp.uint32)
    return lax.shift_right_logical(bits[:, :w], jnp.uint32(16)) | (bits[:, w:] & jnp.uint32(0xFFFF0000))


def _unpack_bf16_pairs(p):
    low = lax.bitcast_convert_type(lax.shift_left(p, jnp.uint32(16)), F32)
    high = lax.bitcast_convert_type(p & jnp.uint32(0xFFFF0000), F32)
    return low, high


def _expert_ranks(idxs, tri_ref, rank_ref, counts_ref, base_ref):
    @pl.when(jnp.logical_and(pl.program_id(0) == 0, pl.program_id(1) == 0))
    def _():
        base_ref[...] = jnp.zeros(base_ref.shape, F32)

    n_exp = base_ref.shape[0]
    t = idxs[0].shape[1]
    rows = lax.broadcasted_iota(jnp.int32, (n_exp, t), 0)
    running = base_ref[:, 0:1]
    for k in range(TOP_K):
        onehot = rows == idxs[k]
        prefix = _dot(jnp.where(onehot, 1.0, 0.0).astype(BF16), tri_ref[...])
        rank = jnp.sum(jnp.where(onehot, prefix + running, 0.0), axis=0, keepdims=True)
        rank_ref[k:k + 1, :] = rank.astype(jnp.int32)
        running = running + jnp.sum(jnp.where(onehot, 1.0, 0.0), axis=1, keepdims=True)
    base_ref[...] = jnp.broadcast_to(running, base_ref.shape)
    counts_ref[...] = base_ref[...]


def _mix_epilogue(x, cat_ref, w_out_ref, g_ref, b_ref, rwt_ref, rb_ref, tri_ref, dn_alpha, h_out_ref, hp_ref, idx_ref,
                  rank_ref, counts_ref, gcol_ref, base_ref):
    mix = _dot(cat_ref[...], w_out_ref[...])
    h1 = _layer_norm(dn_alpha * x + mix, g_ref[...], b_ref[...])
    h_hi = h1.astype(BF16)
    h_lo = (h1 - h_hi.astype(F32)).astype(BF16)
    rw = rwt_ref[...]
    n_exp = rw.shape[0]
    rw_hi = rw.astype(BF16)
    rw_lo = (rw - rw_hi.astype(F32)).astype(BF16)
    both = _dot_nt(jnp.concatenate([rw_hi, rw_lo], axis=0), h_hi)
    logits_all = both[:n_exp] + both[n_exp:] + _dot_nt(rw_hi, h_lo) + rb_ref[...]
    h_out_ref[...] = h1
    hp_ref[...] = _pack_bf16_pairs(h1)
    t = logits_all.shape[1]
    half = t // 2
    idxs_halves, gates_halves = [], []
    for part in range(2):
        logits = logits_all[:, part * half:(part + 1) * half]
        rows = lax.broadcasted_iota(jnp.int32, (n_exp, half), 0)
        vals, idxs = [], []
        for _ in range(TOP_K):
            m = jnp.max(logits, axis=0, keepdims=True)
            ix = jnp.min(jnp.where(logits == m, rows, n_exp), axis=0, keepdims=True)
            vals.append(m)
            idxs.append(ix)
            logits = jnp.where(rows == ix, NEG_INF, logits)
        exps = [jnp.exp(v - vals[0]) for v in vals]
        denom = exps[0] + exps[1] + exps[2] + exps[3]
        idxs_halves.append(idxs)
        gates_halves.append([e / denom for e in exps])
    idxs = [jnp.concatenate([idxs_halves[0][k], idxs_halves[1][k]], axis=1) for k in range(TOP_K)]
    gates = [jnp.concatenate([gates_halves[0][k], gates_halves[1][k]], axis=1) for k in range(TOP_K)]
    for k in range(TOP_K):
        idx_ref[k:k + 1, :] = idxs[k]
    _expert_ranks(idxs, tri_ref, rank_ref, counts_ref, base_ref)
    lane_rows = lax.broadcasted_iota(jnp.int32, (V7X_LANES, t), 0)
    gates_t = jnp.zeros((V7X_LANES, t), F32)
    for k in range(TOP_K):
        gates_t = jnp.where(lane_rows == k, gates[k], gates_t)
    gcol_ref[...] = gates_t.T


def _pool_layer_kernel(dn_alpha, h_ref, mem_ref, w_in_ref, wbd_ref, pscale_ref, mem_kv_ref, w_out_ref, g_ref, b_ref,
                       rwt_ref, rb_ref, tri_ref, h_out_ref, hp_ref, idx_ref, rank_ref, counts_ref, gcol_ref,
                       ue_ref, w2_ref, w4_ref, w8_ref, kv_ref, cat_ref, base_ref):
    s = pl.program_id(1)
    ts = h_ref.shape[0]
    pool_width = wbd_ref.shape[0]
    hist = POOL_HISTORY

    @pl.when(s == 0)
    def _():
        ue_ref[0:hist, :] = jnp.zeros((hist, pool_width), F32)
        kv_ref[...] = _dot(mem_ref[0].astype(BF16), mem_kv_ref[...]).astype(BF16)

    x = h_ref[...]
    proj = _dot(x.astype(BF16), w_in_ref[...])
    u = proj[:, :pool_width]
    ue_ref[hist:, :] = u
    n = ts + hist
    w2_ref[8:n, :] = ue_ref[8:n, :] + ue_ref[7:n - 1, :]
    w4_ref[16:n, :] = w2_ref[16:n, :] + w2_ref[14:n - 2, :]
    w8_ref[24:n, :] = w4_ref[24:n, :] + w4_ref[20:n - 4, :]
    w16 = w8_ref[hist:n, :] + w8_ref[hist - 8:n - 8, :]
    group_dim = pool_width // len(POOL_WINDOWS)
    col = _lane_iota((ts, pool_width))
    wsum = jnp.where(col < group_dim, w2_ref[hist:n, :],
                     jnp.where(col < 2 * group_dim, w4_ref[hist:n, :],
                               jnp.where(col < 3 * group_dim, w8_ref[hist:n, :], w16)))
    window = jnp.where(col < group_dim, POOL_WINDOWS[0],
                       jnp.where(col < 2 * group_dim, POOL_WINDOWS[1],
                                 jnp.where(col < 3 * group_dim, POOL_WINDOWS[2], POOL_WINDOWS[3])))
    tpos = s * ts + lax.broadcasted_iota(jnp.int32, (ts, pool_width), 0) + 1
    cnt = jnp.minimum(tpos, window).astype(F32)
    d = wsum / cnt - u
    ue_ref[0:hist, :] = u[ts - hist:, :]
    pooled = _dot(d.astype(BF16), wbd_ref[...]) * pscale_ref[...]
    cat_ref[:, :pool_width] = pooled.astype(BF16)
    _memory_attention(proj, kv_ref, cat_ref, pool_width, pool_width)
    _mix_epilogue(x, cat_ref, w_out_ref, g_ref, b_ref, rwt_ref, rb_ref, tri_ref, dn_alpha, h_out_ref, hp_ref, idx_ref,
                  rank_ref, counts_ref, gcol_ref, base_ref)


def _bias_pieces(c):
    hi = c.astype(BF16).astype(F32)
    r = c - hi
    mid = r.astype(BF16).astype(F32)
    lo = (r - mid).astype(BF16).astype(F32)
    return hi, mid, lo


def _bias_lane_pieces(c3, n_heads):
    hi, mid, lo = _bias_pieces(c3)
    lane = _lane_iota(c3.shape)
    out = jnp.where(lane < n_heads, hi, jnp.where(lane < 2 * n_heads, mid, jnp.where(lane < 3 * n_heads, lo,
                    jnp.where(lane == 3 * n_heads, 1.0, 0.0))))
    return out.astype(BF16)


def _key_bias_selector(n_heads):
    sel = np.zeros((V7X_LANES, n_heads * V7X_LANES), np.float32)
    for h in range(n_heads):
        spare = h * V7X_LANES + HEAD_DIM
        for p in range(3):
            sel[3 * n_heads, spare + p] = 1.0
            sel[p * n_heads + h, spare + 3 + p] = -1.0
    return jnp.asarray(sel, BF16)


def _shared_kv_block(h, kvw_ref, fb_ref, tri_ref, sel_ref, k_ref, vt_ref, cq_ref, carry_ref):
    s = pl.program_id(1)
    v_width = vt_ref.shape[1]
    n_heads = k_ref.shape[1] // V7X_LANES

    @pl.when(s == 0)
    def _():
        carry_ref[...] = jnp.zeros(carry_ref.shape, F32)

    proj = _dot(h.astype(BF16), kvw_ref[...])
    z = proj[:, 2 * v_width:] + fb_ref[...]
    logf = jnp.minimum(z, 0.0) - jnp.log(1.0 + jnp.exp(-jnp.abs(z)))
    hi, mid, lo = _bias_pieces(logf)
    tri = tri_ref[...]
    c = _dot(tri, hi.astype(BF16)) + _dot(tri, mid.astype(BF16)) + _dot(tri, lo.astype(BF16)) + carry_ref[0:1, :]
    ts = c.shape[0]
    carry_ref[0:1, :] = c[ts - 1:ts, :]
    pieces = _bias_lane_pieces(c * LOG2_E, n_heads)
    cq_ref[...] = pieces
    bias_lanes = _dot(pieces, sel_ref[...])
    lane = _lane_iota((ts, V7X_LANES))
    for hd in range(n_heads):
        pair = proj[:, (hd // 2) * V7X_LANES:(hd // 2 + 1) * V7X_LANES]
        k_h = pair if hd % 2 == 0 else pltpu.roll(pair, HEAD_DIM, axis=1)
        cols = slice(hd * V7X_LANES, (hd + 1) * V7X_LANES)
        k_ref[:, cols] = jnp.where(lane < HEAD_DIM, k_h, bias_lanes[:, cols]).astype(BF16)
    vt_ref[0] = proj[:, v_width:2 * v_width].T.astype(BF16)


def _fox_layer_kernel(dn_alpha, h_ref, mem_ref, k_ref, vt_ref, cq_ref, w_in_ref, mem_kv_ref, w_out_ref,
                      g_ref, b_ref, rwt_ref, rb_ref, tri_ref, h_out_ref, hp_ref, idx_ref, rank_ref, counts_ref, gcol_ref,
                      kv_ref, cat_ref, qa_ref, m_ref, l_ref, acc_ref, base_ref):
    i = pl.program_id(1)
    tq = h_ref.shape[0]
    n_heads = k_ref.shape[2] // V7X_LANES
    fox_width = vt_ref.shape[1]

    @pl.when(i == 0)
    def _():
        kv_ref[...] = _dot(mem_ref[0].astype(BF16), mem_kv_ref[...]).astype(BF16)

    x = h_ref[...]
    proj = _dot(x.astype(BF16), w_in_ref[...])
    cq = cq_ref[...].astype(F32)
    lane = _lane_iota((tq, V7X_LANES))
    for h in range(n_heads):
        pair = proj[:, (h // 2) * V7X_LANES:(h // 2 + 1) * V7X_LANES] * (ATTN_SCALE * LOG2_E)
        q_h = pair if h % 2 == 0 else pltpu.roll(pair, HEAD_DIM, axis=1)
        piece = [cq[:, p * n_heads + h:p * n_heads + h + 1] for p in range(3)]
        spare = jnp.where(lane == HEAD_DIM, piece[0], jnp.where(lane == HEAD_DIM + 1, piece[1],
                          jnp.where(lane == HEAD_DIM + 2, piece[2], jnp.where(lane < HEAD_DIM + 6, 1.0, 0.0))))
        qa_ref[h] = jnp.where(lane < HEAD_DIM, q_h, spare).astype(BF16)
    m_ref[...] = jnp.full(m_ref.shape, NEG_INF, F32)
    l_ref[...] = jnp.zeros(l_ref.shape, F32)
    acc_ref[...] = jnp.zeros(acc_ref.shape, F32)
    causal_t = lax.broadcasted_iota(jnp.int32, (tq, tq), 0) <= lax.broadcasted_iota(jnp.int32, (tq, tq), 1)

    ones_rows = (lax.broadcasted_iota(jnp.int32, (V7X_BF16_SUBLANES, tq), 0) == 0).astype(BF16)

    def tile(j, masked):
        start = pl.multiple_of(j * tq, tq)

        def scores(h):
            kb = k_ref[0, pl.ds(start, tq), h * V7X_LANES:(h + 1) * V7X_LANES]
            return _dot_nt(kb, qa_ref[h])

        st_next = scores(0)
        for h in range(n_heads):
            st = st_next
            if h + 1 < n_heads:
                st_next = scores(h + 1)
            if masked:
                st = jnp.where(causal_t, st, NEG_INF)
            m_old = m_ref[h:h + 1, :]
            m_new = jnp.maximum(m_old, jnp.max(st, axis=0, keepdims=True))
            alpha = jnp.exp2(m_old - m_new)
            pt = jnp.exp2(st - m_new).astype(BF16)
            m_ref[h:h + 1, :] = m_new
            rows = slice(h * HEAD_DIM, (h + 1) * HEAD_DIM)
            vb = jnp.concatenate([vt_ref[0, rows, pl.ds(start, tq)], ones_rows], axis=0)
            pv = _dot(vb, pt)
            acc_ref[rows, :] = alpha * acc_ref[rows, :] + pv[:HEAD_DIM]
            l_ref[h:h + 1, :] = alpha * l_ref[h:h + 1, :] + pv[HEAD_DIM:HEAD_DIM + 1]

    tile(i, True)

    def off_diagonal(j, carry):
        tile(j, False)
        return carry

    lax.fori_loop(0, i, off_diagonal, 0)
    for h in range(n_heads):
        rows = slice(h * HEAD_DIM, (h + 1) * HEAD_DIM)
        acc_ref[rows, :] = acc_ref[rows, :] / l_ref[h:h + 1, :]
    cat_ref[:, :fox_width] = acc_ref[...].T.astype(BF16)
    _memory_attention(proj, kv_ref, cat_ref, fox_width, fox_width)
    _mix_epilogue(x, cat_ref, w_out_ref, g_ref, b_ref, rwt_ref, rb_ref, tri_ref, dn_alpha, h_out_ref, hp_ref, idx_ref,
                  rank_ref, counts_ref, gcol_ref, base_ref)


def _expert_kernel(layer, be_ref, bv_ref, bslot_ref, bnext_ref, nact_ref, x_ref, w1_hbm, b1_ref, w2_hbm, b2_ref, y_ref,
                   w1s_ref, w2s_ref, act_ref, w1f_ref, w2f_ref, sem_ref):
    i = pl.program_id(0)
    active = i < nact_ref[0]
    two_f = w1f_ref.shape[2]
    group = 2 * V7X_LANES
    n_groups = two_f // group

    def weight_copies(expert, slot):
        return (pltpu.make_async_copy(w1_hbm.at[layer, expert], w1f_ref.at[slot], sem_ref.at[0, slot]),
                pltpu.make_async_copy(w2_hbm.at[layer, expert], w2f_ref.at[slot], sem_ref.at[1, slot]))

    @pl.when(i == 0)
    def _():
        for cp in weight_copies(be_ref[0], bslot_ref[0]):
            cp.start()

    @pl.when(jnp.logical_and(active, jnp.logical_or(i == 0, be_ref[i] != be_ref[jnp.maximum(i - 1, 0)])))
    def _():
        slot = bslot_ref[i]
        for cp in weight_copies(be_ref[i], slot):
            cp.wait()

        @pl.when(bnext_ref[i] >= 0)
        def _():
            for cp in weight_copies(bnext_ref[i], 1 - slot):
                cp.start()

        src = lax.broadcasted_iota(jnp.int32, (group, group), 0)
        dst = lax.broadcasted_iota(jnp.int32, (group, group), 1)
        want = jnp.where(dst < V7X_LANES, 2 * dst, 2 * (dst - V7X_LANES) + 1)
        perm = (src == want).astype(BF16)
        for g in range(n_groups):
            cols = slice(g * group, (g + 1) * group)
            w1s_ref[:, cols] = _dot(w1f_ref[slot, :, cols].astype(BF16), perm).astype(BF16)
        w2s_ref[...] = w2f_ref[slot].astype(BF16)

    @pl.when(active)
    def _():
        row_id = lax.broadcasted_iota(jnp.int32, x_ref.shape, 0)
        x_low, x_high = _unpack_bf16_pairs(jnp.where(row_id < bv_ref[i], x_ref[...], jnp.uint32(0)))
        xb = jnp.concatenate([x_low.astype(BF16), x_high.astype(BF16)], axis=1)
        slab_groups = 2
        for s0 in range(0, n_groups, slab_groups):
            cols = slice(s0 * group, (s0 + slab_groups) * group)
            hdn = _dot(xb, w1s_ref[:, cols]) + b1_ref[0, :, cols]
            for g in range(slab_groups):
                x_glu = jnp.minimum(hdn[:, g * group:g * group + V7X_LANES], SWIGLU_LIMIT)
                x_lin = jnp.clip(hdn[:, g * group + V7X_LANES:(g + 1) * group], -SWIGLU_LIMIT, SWIGLU_LIMIT)
                act = x_glu * jax.nn.sigmoid(SWIGLU_ALPHA * x_glu) * (x_lin + 1.0)
                act_ref[:, (s0 + g) * V7X_LANES:(s0 + g + 1) * V7X_LANES] = act.astype(BF16)
        y_ref[...] = _pack_bf16_pairs(_dot(act_ref[...], w2s_ref[...]) + b2_ref[0])

    @pl.when(jnp.logical_not(active))
    def _():
        y_ref[...] = jnp.zeros(y_ref.shape, jnp.uint32)


def _combine_rows(dn_alpha, h_ref, y_ref, gcol_ref, g_ref, b_ref):
    gates = gcol_ref[...]
    low, high = _unpack_bf16_pairs(y_ref[0])
    low, high = gates[:, 0:1] * low, gates[:, 0:1] * high
    for k in range(1, TOP_K):
        low_k, high_k = _unpack_bf16_pairs(y_ref[k])
        low = low + gates[:, k:k + 1] * low_k
        high = high + gates[:, k:k + 1] * high_k
    ffn = jnp.concatenate([low, high], axis=1)
    return _layer_norm(dn_alpha * h_ref[...] + ffn, g_ref[...], b_ref[...])


def _combine_kernel(dn_alpha, h_ref, y_ref, gcol_ref, g_ref, b_ref, o_ref):
    o_ref[...] = _combine_rows(dn_alpha, h_ref, y_ref, gcol_ref, g_ref, b_ref)


def _combine_kv_kernel(dn_alpha, h_ref, y_ref, gcol_ref, g_ref, b_ref, kvw_ref, fb_ref, tri_ref, sel_ref,
                       o_ref, k_ref, vt_ref, cq_ref, carry_ref):
    h2 = _combine_rows(dn_alpha, h_ref, y_ref, gcol_ref, g_ref, b_ref)
    o_ref[...] = h2
    _shared_kv_block(h2, kvw_ref, fb_ref, tri_ref, sel_ref, k_ref, vt_ref, cq_ref, carry_ref)


def _sc_gather_rows(table, idx):
    n, = idx.shape
    _, d = table.shape
    info = plsc.get_sparse_core_info()
    n_workers = info.num_cores * info.num_subcores
    per_worker = n // n_workers
    n_chunks = per_worker // SC_CHUNK
    assert n % (n_workers * SC_CHUNK) == 0 and n_chunks % 2 == 0
    mesh = plsc.VectorSubcoreMesh(core_axis_name="core", subcore_axis_name="subcore")

    @functools.partial(
        pl.kernel,
        out_type=jax.ShapeDtypeStruct((n, d), table.dtype),
        mesh=mesh,
        scratch_types=[
            pltpu.VMEM((per_worker,), jnp.int32),
            pltpu.VMEM((2, SC_CHUNK, d), table.dtype),
            pltpu.SemaphoreType.DMA((2,)),
            pltpu.SemaphoreType.DMA((2,)),
        ],
    )
    def gather_kernel(table_hbm, idx_hbm, out_hbm, idx_v, rows_v, gather_sem, write_sem):
        worker = lax.axis_index("subcore") * info.num_cores + lax.axis_index("core")
        base = pl.multiple_of(worker * per_worker, SC_CHUNK)
        pltpu.sync_copy(idx_hbm.at[pl.ds(base, per_worker)], idx_v)

        def gather(c, buf):
            rows = idx_v.at[pl.ds(pl.multiple_of(c * SC_CHUNK, SC_CHUNK), SC_CHUNK)]
            return pltpu.make_async_copy(table_hbm.at[rows], rows_v.at[buf], gather_sem.at[buf])

        def write(c, buf):
            off = pl.multiple_of(base + c * SC_CHUNK, SC_CHUNK)
            return pltpu.make_async_copy(rows_v.at[buf], out_hbm.at[pl.ds(off, SC_CHUNK)], write_sem.at[buf])

        gather(0, 0).start()

        @pl.loop(0, n_chunks, step=2)
        def _(c0):
            for buf in range(2):
                c = c0 + buf
                gather(c, buf).wait()

                @pl.when(c >= 1)
                def _():
                    write(c - 1, 1 - buf).wait()

                @pl.when(c + 1 < n_chunks)
                def _():
                    gather(c + 1, 1 - buf).start()

                write(c, buf).start()

        write(n_chunks - 1, 1).wait()

    return gather_kernel(table, idx)


def _sc_scatter_rows(rows, pos, n_out):
    t, d = rows.shape
    top_k = pos.shape[0]
    pos_flat = pos.reshape(-1)
    info = plsc.get_sparse_core_info()
    n_workers = info.num_cores * info.num_subcores
    per_worker = t // n_workers
    n_chunks = per_worker // SC_CHUNK
    assert t % (n_workers * SC_CHUNK) == 0 and n_chunks % 2 == 0
    mesh = plsc.VectorSubcoreMesh(core_axis_name="core", subcore_axis_name="subcore")

    @functools.partial(
        pl.kernel,
        out_type=jax.ShapeDtypeStruct((n_out, d), rows.dtype),
        mesh=mesh,
        scratch_types=[pltpu.VMEM((2, SC_CHUNK, d), rows.dtype)]
        + [pltpu.VMEM((SC_CHUNK,), jnp.int32) for _ in range(2 * top_k)]
        + [pltpu.SemaphoreType.DMA((2,)), pltpu.SemaphoreType.DMA((2,))],
    )
    def scatter_kernel(rows_hbm, pos_hbm, out_hbm, rows_v, *rest):
        idx_bufs = rest[:2 * top_k]
        load_sem, write_sem = rest[2 * top_k:]
        worker = lax.axis_index("subcore") * info.num_cores + lax.axis_index("core")
        base = pl.multiple_of(worker * per_worker, SC_CHUNK)

        def loads(c, buf):
            off = pl.multiple_of(base + c * SC_CHUNK, SC_CHUNK)
            copies = [pltpu.make_async_copy(rows_hbm.at[pl.ds(off, SC_CHUNK)], rows_v.at[buf], load_sem.at[buf])]
            for k in range(top_k):
                src = pos_hbm.at[pl.ds(pl.multiple_of(k * t + off, SC_CHUNK), SC_CHUNK)]
                copies.append(pltpu.make_async_copy(src, idx_bufs[buf * top_k + k], load_sem.at[buf]))
            return copies

        def scatters(buf):
            return [pltpu.make_async_copy(rows_v.at[buf], out_hbm.at[idx_bufs[buf * top_k + k]], write_sem.at[buf])
                    for k in range(top_k)]

        for cp in loads(0, 0):
            cp.start()

        @pl.loop(0, n_chunks, step=2)
        def _(c0):
            for buf in range(2):
                c = c0 + buf
                for cp in loads(c, buf):
                    cp.wait()

                @pl.when(c >= 1)
                def _():
                    for cp in scatters(1 - buf):
                        cp.wait()

                @pl.when(c + 1 < n_chunks)
                def _():
                    for cp in loads(c + 1, 1 - buf):
                        cp.start()

                for cp in scatters(buf):
                    cp.start()

        for cp in scatters(1):
            cp.wait()

    return scatter_kernel(rows, pos_flat)


def _const_spec(shape):
    zeros = (0,) * len(shape)
    return pl.BlockSpec(shape, lambda *_: zeros, pipeline_mode=pl.Buffered(1))


def _tc_params(n_axes):
    return pltpu.CompilerParams(dimension_semantics=("arbitrary",) * n_axes,
                                vmem_limit_bytes=V7X_VMEM_LIMIT_BYTES)


def _mix_out_shapes(t, d, n_exp):
    return (jax.ShapeDtypeStruct((t, d), F32),
            jax.ShapeDtypeStruct((t, d // 2), jnp.uint32),
            jax.ShapeDtypeStruct((TOP_K, t), jnp.int32),
            jax.ShapeDtypeStruct((TOP_K, t), jnp.int32),
            jax.ShapeDtypeStruct((n_exp, V7X_LANES), F32),
            jax.ShapeDtypeStruct((t, V7X_LANES), F32))


def _mix_out_specs(rows, d, n_exp, row, tok_col):
    return (pl.BlockSpec((rows, d), row),
            pl.BlockSpec((rows, d // 2), row),
            pl.BlockSpec((TOP_K, rows), tok_col),
            pl.BlockSpec((TOP_K, rows), tok_col),
            pl.BlockSpec((n_exp, V7X_LANES), lambda *_: (0, 0)),
            pl.BlockSpec((rows, V7X_LANES), row))


def _strict_upper_tri(n):
    return (lax.broadcasted_iota(jnp.int32, (n, n), 0) < lax.broadcasted_iota(jnp.int32, (n, n), 1)).astype(BF16)


def _pool_layer(h, mem, w_in, wbd, pscale, mem_kv, w_out, ln_g, ln_b, rwt, rb, dn_alpha, seq):
    t, d = h.shape
    n_mem = mem.shape[1]
    pool_width = wbd.shape[0]
    ts = SEQ_BLOCK
    nsb = seq // ts
    row = lambda b, s: (b * nsb + s, 0)
    n_exp = rwt.shape[0]
    tri = _strict_upper_tri(ts)
    return pl.pallas_call(
        functools.partial(_pool_layer_kernel, dn_alpha),
        out_shape=_mix_out_shapes(t, d, n_exp),
        grid=(t // seq, nsb),
        in_specs=[
            pl.BlockSpec((ts, d), row),
            pl.BlockSpec((1, n_mem, d), lambda b, s: (b, 0, 0)),
            _const_spec(w_in.shape), _const_spec(wbd.shape), _const_spec(pscale.shape), _const_spec(mem_kv.shape),
            _const_spec(w_out.shape), _const_spec(ln_g.shape), _const_spec(ln_b.shape), _const_spec(rwt.shape),
            _const_spec(rb.shape), _const_spec(tri.shape),
        ],
        out_specs=_mix_out_specs(ts, d, n_exp, row, lambda b, s: (0, b * nsb + s)),
        scratch_shapes=[pltpu.VMEM((ts + POOL_HISTORY, pool_width), F32)] * 4
        + [pltpu.VMEM((n_mem, 2 * MEM_WIDTH), BF16), pltpu.VMEM((ts, d), BF16), pltpu.VMEM((n_exp, V7X_LANES), F32)],
        compiler_params=_tc_params(2),
        name="pool_layer",
    )(h, mem, w_in, wbd, pscale, mem_kv, w_out, ln_g, ln_b, rwt, rb, tri)


def _fox_layer(h, mem, k_aug, v_t, cq, w_in, mem_kv, w_out, ln_g, ln_b, rwt, rb, dn_alpha, seq):
    t, d = h.shape
    n_mem = mem.shape[1]
    k_width = k_aug.shape[1]
    v_width = v_t.shape[1]
    n_heads = k_width // V7X_LANES
    stat_rows = -(-n_heads // V7X_SUBLANES) * V7X_SUBLANES
    tq = Q_BLOCK
    nqb = seq // tq
    n_batch = t // seq
    row = lambda b, i: (b * nqb + i, 0)
    per_batch = lambda b, i: (b, 0, 0)
    n_exp = rwt.shape[0]
    tri = _strict_upper_tri(tq)
    return pl.pallas_call(
        functools.partial(_fox_layer_kernel, dn_alpha),
        out_shape=_mix_out_shapes(t, d, n_exp),
        grid=(n_batch, nqb),
        in_specs=[
            pl.BlockSpec((tq, d), row),
            pl.BlockSpec((1, n_mem, d), per_batch),
            pl.BlockSpec((1, seq, k_width), per_batch),
            pl.BlockSpec((1, v_width, seq), per_batch),
            pl.BlockSpec((tq, V7X_LANES), row),
            _const_spec(w_in.shape), _const_spec(mem_kv.shape), _const_spec(w_out.shape), _const_spec(ln_g.shape),
            _const_spec(ln_b.shape), _const_spec(rwt.shape), _const_spec(rb.shape), _const_spec(tri.shape),
        ],
        out_specs=_mix_out_specs(tq, d, n_exp, row, lambda b, i: (0, b * nqb + i)),
        scratch_shapes=[pltpu.VMEM((n_mem, 2 * MEM_WIDTH), BF16), pltpu.VMEM((tq, d), BF16),
                        pltpu.VMEM((n_heads, tq, V7X_LANES), BF16),
                        pltpu.VMEM((stat_rows, tq), F32), pltpu.VMEM((stat_rows, tq), F32),
                        pltpu.VMEM((v_width, tq), F32),
                        pltpu.VMEM((n_exp, V7X_LANES), F32)],
        compiler_params=_tc_params(2),
        name="fox_layer",
    )(h, mem, k_aug.reshape(n_batch, seq, k_width), v_t, cq, w_in, mem_kv, w_out, ln_g, ln_b, rwt, rb, tri)


def _slot_kernel(pad_start_ref, idx_ref, rank_ref, pos_ref):
    pos = rank_ref[...]
    idx = idx_ref[...]
    for e in range(pad_start_ref.shape[0]):
        pos = pos + jnp.where(idx == e, pad_start_ref[e], 0)
    pos_ref[...] = pos


def _route(idx_t, rank_t, counts, n_experts):
    top_k, t = idx_t.shape
    n_assign = top_k * t
    padded = ((counts + MOE_BLOCK - 1) // MOE_BLOCK) * MOE_BLOCK
    pad_end = jnp.cumsum(padded)
    pad_start = pad_end - padded
    n_blocks = n_assign // MOE_BLOCK + n_experts
    block_start = jnp.arange(n_blocks, dtype=jnp.int32) * MOE_BLOCK
    block_expert = jnp.minimum(jnp.sum((pad_end[None, :] <= block_start[:, None]).astype(jnp.int32), axis=1),
                               n_experts - 1)
    block_valid = jnp.clip(counts[block_expert] - (block_start - pad_start[block_expert]), 0, MOE_BLOCK)
    n_active = (pad_end[-1:] // MOE_BLOCK).astype(jnp.int32)
    expert_ids = jnp.arange(n_experts, dtype=jnp.int32)
    in_use = counts > 0
    expert_slot = (jnp.cumsum(in_use.astype(jnp.int32)) - in_use.astype(jnp.int32)) % 2
    later = lax.cummin(jnp.where(in_use, expert_ids, n_experts), axis=0, reverse=True)
    next_in_use = jnp.concatenate([later[1:], jnp.full((1,), n_experts, jnp.int32)])
    next_in_use = jnp.where(next_in_use < n_experts, next_in_use, -1)
    block_slot = expert_slot[block_expert].astype(jnp.int32)
    block_next = next_in_use[block_expert].astype(jnp.int32)
    cols = min(SLOT_COLS, t)
    pos = pl.pallas_call(
        _slot_kernel,
        out_shape=jax.ShapeDtypeStruct((top_k, t), jnp.int32),
        grid_spec=pltpu.PrefetchScalarGridSpec(
            num_scalar_prefetch=1,
            grid=(t // cols,),
            in_specs=[pl.BlockSpec((top_k, cols), lambda i, ps: (0, i))] * 2,
            out_specs=pl.BlockSpec((top_k, cols), lambda i, ps: (0, i)),
        ),
        compiler_params=_tc_params(1),
        name="slots",
    )(pad_start.astype(jnp.int32), idx_t, rank_t)
    return pos, (block_expert, block_valid.astype(jnp.int32), block_slot, block_next, n_active)


def _experts(xs, block_expert, block_valid, block_slot, block_next, n_active, w1_all, b1, w2_all, b2, layer):
    n_slots, half_d = xs.shape
    _, n_exp, d, two_f = w1_all.shape
    d_ff = two_f // 2
    assert d == 2 * half_d
    block = lambda i, *_: (i, 0)
    per_expert = lambda i, be, *_: (be[i], 0, 0)
    grid_spec = pltpu.PrefetchScalarGridSpec(
        num_scalar_prefetch=5,
        grid=(n_slots // MOE_BLOCK,),
        in_specs=[
            pl.BlockSpec((MOE_BLOCK, half_d), block),
            pl.BlockSpec(memory_space=pl.ANY),
            pl.BlockSpec((1, 1, two_f), per_expert),
            pl.BlockSpec(memory_space=pl.ANY),
            pl.BlockSpec((1, 1, d), per_expert),
        ],
        out_specs=pl.BlockSpec((MOE_BLOCK, half_d), block),
        scratch_shapes=[pltpu.VMEM((d, two_f), BF16), pltpu.VMEM((d_ff, d), BF16), pltpu.VMEM((MOE_BLOCK, d_ff), BF16),
                        pltpu.VMEM((2, d, two_f), F32), pltpu.VMEM((2, d_ff, d), F32),
                        pltpu.SemaphoreType.DMA((2, 2))],
    )
    return pl.pallas_call(
        functools.partial(_expert_kernel, layer),
        out_shape=jax.ShapeDtypeStruct((n_slots, half_d), jnp.uint32),
        grid_spec=grid_spec,
        compiler_params=_tc_params(1),
        name="experts",
    )(block_expert, block_valid, block_slot, block_next, n_active, xs, w1_all, b1.reshape(n_exp, 1, two_f), w2_all,
      b2.reshape(n_exp, 1, d))


def _combine(h, yg, gcol, ln_g, ln_b, dn_alpha, seq, shared_kv_weights=None):
    t, d = h.shape
    ts = SEQ_BLOCK
    nsb = seq // ts
    row = lambda b, s: (b * nsb + s, 0)
    in_specs = [pl.BlockSpec((ts, d), row), pl.BlockSpec((TOP_K, ts, d // 2), lambda b, s: (0, b * nsb + s, 0)),
                pl.BlockSpec((ts, V7X_LANES), row), _const_spec(ln_g.shape), _const_spec(ln_b.shape)]
    args = (h, yg.reshape(TOP_K, t, d // 2), gcol, ln_g, ln_b)
    if shared_kv_weights is None:
        return pl.pallas_call(
            functools.partial(_combine_kernel, dn_alpha),
            out_shape=jax.ShapeDtypeStruct((t, d), F32),
            grid=(t // seq, nsb),
            in_specs=in_specs,
            out_specs=pl.BlockSpec((ts, d), row),
            compiler_params=_tc_params(2),
            name="combine",
        )(*args), None
    kvw, fb, sel_k, n_heads = shared_kv_weights
    k_width = n_heads * V7X_LANES
    v_width = n_heads * HEAD_DIM
    tri = (lax.broadcasted_iota(jnp.int32, (ts, ts), 1) <= lax.broadcasted_iota(jnp.int32, (ts, ts), 0)).astype(BF16)
    out = pl.pallas_call(
        functools.partial(_combine_kv_kernel, dn_alpha),
        out_shape=(jax.ShapeDtypeStruct((t, d), F32),
                   jax.ShapeDtypeStruct((t, k_width), BF16),
                   jax.ShapeDtypeStruct((t // seq, v_width, seq), BF16),
                   jax.ShapeDtypeStruct((t, V7X_LANES), BF16)),
        grid=(t // seq, nsb),
        in_specs=in_specs + [_const_spec(kvw.shape), _const_spec(fb.shape), _const_spec(tri.shape),
                             _const_spec(sel_k.shape)],
        out_specs=(pl.BlockSpec((ts, d), row),
                   pl.BlockSpec((ts, k_width), row),
                   pl.BlockSpec((1, v_width, ts), lambda b, s: (b, 0, s)),
                   pl.BlockSpec((ts, V7X_LANES), row)),
        scratch_shapes=[pltpu.VMEM((V7X_SUBLANES, V7X_LANES), F32)],
        compiler_params=_tc_params(2),
        name="combine_shared_kv",
    )(*args, kvw, fb, tri, sel_k)
    return out[0], out[1:]


def _moe(mixed, w1_all, b1, w2_all, b2, layer, ln_g, ln_b, dn_alpha, seq, shared_kv_weights):
    h1, h1_packed, idx_t, rank_t, counts, gcol = mixed
    n_exp = w1_all.shape[1]
    d_ff = w2_all.shape[2]
    pos, block_meta = _route(idx_t, rank_t, counts[:, 0].astype(jnp.int32), n_exp)
    b1p = jnp.swapaxes(b1.reshape(n_exp, d_ff // V7X_LANES, V7X_LANES, 2), 2, 3).reshape(n_exp, 2 * d_ff)
    xs = _sc_scatter_rows(h1_packed, pos, block_meta[0].shape[0] * MOE_BLOCK)
    y = _experts(xs, *block_meta, w1_all, b1p, w2_all, b2, layer)
    yg = _sc_gather_rows(y, pos.reshape(-1))
    return _combine(h1, yg, gcol, ln_g, ln_b, dn_alpha, seq, shared_kv_weights)


def kernel(x, mem, ln1_g, ln1_b, ln2_g, ln2_b, a_w_in, pool_w, pool_scale, a_mem_kv, a_w_out, kv_w, fgate_b, b_w_in, b_mem_kv, b_w_out, router_w, router_b, moe_w1, moe_b1, moe_w2, moe_b2):
    n_batch, seq, d = x.shape
    t = n_batch * seq
    depth = ln1_g.shape[0]
    n_a = a_w_in.shape[0]
    dn_alpha = float((2 * depth) ** 0.25)
    n_exp = router_w.shape[2]
    fox_heads = fgate_b.shape[0]
    fox_width = fox_heads * HEAD_DIM
    assert seq % SEQ_BLOCK == 0 and seq % Q_BLOCK == 0
    assert 3 * fox_heads < V7X_LANES

    assert 1 <= n_a < depth
    pad = V7X_LANES - 3 * fox_heads
    kvw = jnp.concatenate([kv_w[:, :2 * fox_width],
                           jnp.pad(jnp.tile(kv_w[:, 2 * fox_width:], (1, 3)), ((0, 0), (0, pad)))],
                          axis=1).astype(BF16)
    fb = jnp.pad(jnp.tile(fgate_b, 3), (0, pad)).reshape(1, V7X_LANES)
    shared_kv_weights = (kvw, fb, _key_bias_selector(fox_heads), fox_heads)

    h = x.reshape(t, d)
    row = lambda a: a.reshape(1, -1)
    shared = None
    for l in range(depth):
        rwt = router_w[l].T
        rb = router_b[l].reshape(n_exp, 1)
        if l < n_a:
            wbd = jax.scipy.linalg.block_diag(*[pool_w[l, g] for g in range(pool_w.shape[1])]).astype(BF16)
            mixed = _pool_layer(
                h, mem, a_w_in[l].astype(BF16), wbd, row(pool_scale[l]), a_mem_kv[l].astype(BF16),
                a_w_out[l].astype(BF16), row(ln1_g[l]), row(ln1_b[l]), rwt, rb, dn_alpha, seq)
        else:
            j = l - n_a
            k_aug, v_t, cq = shared
            mixed = _fox_layer(
                h, mem, k_aug, v_t, cq, b_w_in[j].astype(BF16), b_mem_kv[j].astype(BF16),
                b_w_out[j].astype(BF16), row(ln1_g[l]), row(ln1_b[l]), rwt, rb, dn_alpha, seq)
        h, new_shared = _moe(mixed, moe_w1, moe_b1[l], moe_w2, moe_b2[l], l, row(ln2_g[l]), row(ln2_b[l]), dn_alpha,
                             seq, shared_kv_weights if l == n_a - 1 else None)
        if new_shared is not None:
            shared = new_shared
    return h.reshape(n_batch, seq, d)
```

```python
import functools
import numpy as np
import jax
import jax.numpy as jnp
from jax import lax
from jax.experimental import pallas as pl
from jax.experimental.pallas import tpu as pltpu
from jax.experimental.pallas import tpu_sc as plsc

HEAD_DIM = 64
MEM_HEADS = 4
MEM_WIDTH = MEM_HEADS * HEAD_DIM
POOL_WINDOWS = (2, 4, 8, 16)
TOP_K = 4
SWIGLU_ALPHA = 1.702
SWIGLU_LIMIT = 7.0
MOE_BLOCK = 1024
LN_EPS = 1e-5
ATTN_SCALE = HEAD_DIM ** -0.5
LOG2_E = 1.4426950408889634

V7X_LANES = 128
V7X_SUBLANES = 8
V7X_BF16_SUBLANES = 16
V7X_VMEM_LIMIT_BYTES = 56 * 1024 * 1024
POOL_HISTORY = 32
SC_CHUNK = 64
SEQ_BLOCK = 1024
Q_BLOCK = 512
SLOT_COLS = 8192

F32 = jnp.float32
BF16 = jnp.bfloat16
NEG_INF = float("-inf")


def _dot(a, b):
    return jnp.dot(a, b, preferred_element_type=F32)


def _dot_nt(a, b):
    return lax.dot_general(a, b, (((1,), (1,)), ((), ())), preferred_element_type=F32)


def _lane_iota(shape):
    return lax.broadcasted_iota(jnp.int32, shape, len(shape) - 1)


def _pair_attention(q2, keys, vals):
    lane = _lane_iota(q2.shape)
    first = lane < HEAD_DIM
    outs = []
    for sel in (first, jnp.logical_not(first)):
        qh = jnp.where(sel, q2, 0.0).astype(BF16)
        s = _dot_nt(qh, keys)
        m = jnp.max(s, axis=-1, keepdims=True)
        p = jnp.exp(s - m)
        l = jnp.sum(p, axis=-1, keepdims=True)
        outs.append(_dot(p.astype(BF16), vals) / l)
    return jnp.where(first, outs[0], outs[1])


def _memory_attention(proj, kv_ref, cat_ref, out_off, q_off):
    for p in range(MEM_HEADS // 2):
        lo = p * V7X_LANES
        q2 = proj[:, q_off + lo:q_off + lo + V7X_LANES] * ATTN_SCALE
        keys = kv_ref[:, lo:lo + V7X_LANES]
        vals = kv_ref[:, MEM_WIDTH + lo:MEM_WIDTH + lo + V7X_LANES]
        cat_ref[:, out_off + lo:out_off + lo + V7X_LANES] = _pair_attention(q2, keys, vals).astype(BF16)


def _layer_norm(z, g, b):
    mu = jnp.mean(z, axis=-1, keepdims=True)
    zc = z - mu
    var = jnp.mean(zc * zc, axis=-1, keepdims=True)
    return zc * lax.rsqrt(var + LN_EPS) * g + b


def _pack_bf16_pairs(x):
    w = x.shape[1] // 2
    bits = lax.bitcast_convert_type(x.astype(BF16).astype(F32), jnp.uint32)
    return lax.shift_right_logical(bits[:, :w], jnp.uint32(16)) | (bits[:, w:] & jnp.uint32(0xFFFF0000))


def _unpack_bf16_pairs(p):
    low = lax.bitcast_convert_type(lax.shift_left(p, jnp.uint32(16)), F32)
    high = lax.bitcast_convert_type(p & jnp.uint32(0xFFFF0000), F32)
    return low, high


def _expert_ranks(idxs, tri_ref, rank_ref, counts_ref, base_ref):
    @pl.when(jnp.logical_and(pl.program_id(0) == 0, pl.program_id(1) == 0))
    def _():
        base_ref[...] = jnp.zeros(base_ref.shape, F32)

    n_exp = base_ref.shape[0]
    t = idxs[0].shape[1]
    rows = lax.broadcasted_iota(jnp.int32, (n_exp, t), 0)
    running = base_ref[:, 0:1]
    for k in range(TOP_K):
        onehot = rows == idxs[k]
        prefix = _dot(jnp.where(onehot, 1.0, 0.0).astype(BF16), tri_ref[...])
        rank = jnp.sum(jnp.where(onehot, prefix + running, 0.0), axis=0, keepdims=True)
        rank_ref[k:k + 1, :] = rank.astype(jnp.int32)
        running = running + jnp.sum(jnp.where(onehot, 1.0, 0.0), axis=1, keepdims=True)
    base_ref[...] = jnp.broadcast_to(running, base_ref.shape)
    counts_ref[...] = base_ref[...]


def _mix_epilogue(x, cat_ref, w_out_ref, g_ref, b_ref, rwt_ref, rb_ref, tri_ref, dn_alpha, h_out_ref, hp_ref, idx_ref,
                  rank_ref, counts_ref, gcol_ref, base_ref):
    mix = _dot(cat_ref[...], w_out_ref[...])
    h1 = _layer_norm(dn_alpha * x + mix, g_ref[...], b_ref[...])
    h_hi = h1.astype(BF16)
    h_lo = (h1 - h_hi.astype(F32)).astype(BF16)
    rw = rwt_ref[...]
    n_exp = rw.shape[0]
    rw_hi = rw.astype(BF16)
    rw_lo = (rw - rw_hi.astype(F32)).astype(BF16)
    both = _dot_nt(jnp.concatenate([rw_hi, rw_lo], axis=0), h_hi)
    logits_all = both[:n_exp] + both[n_exp:] + _dot_nt(rw_hi, h_lo) + rb_ref[...]
    h_out_ref[...] = h1
    hp_ref[...] = _pack_bf16_pairs(h1)
    t = logits_all.shape[1]
    half = t // 2
    idxs_halves, gates_halves = [], []
    for part in range(2):
        logits = logits_all[:, part * half:(part + 1) * half]
        rows = lax.broadcasted_iota(jnp.int32, (n_exp, half), 0)
        vals, idxs = [], []
        for _ in range(TOP_K):
            m = jnp.max(logits, axis=0, keepdims=True)
            ix = jnp.min(jnp.where(logits == m, rows, n_exp), axis=0, keepdims=True)
            vals.append(m)
            idxs.append(ix)
            logits = jnp.where(rows == ix, NEG_INF, logits)
        exps = [jnp.exp(v - vals[0]) for v in vals]
        denom = exps[0] + exps[1] + exps[2] + exps[3]
        idxs_halves.append(idxs)
        gates_halves.append([e / denom for e in exps])
    idxs = [jnp.concatenate([idxs_halves[0][k], idxs_halves[1][k]], axis=1) for k in range(TOP_K)]
    gates = [jnp.concatenate([gates_halves[0][k], gates_halves[1][k]], axis=1) for k in range(TOP_K)]
    for k in range(TOP_K):
        idx_ref[k:k + 1, :] = idxs[k]
    _expert_ranks(idxs, tri_ref, rank_ref, counts_ref, base_ref)
    lane_rows = lax.broadcasted_iota(jnp.int32, (V7X_LANES, t), 0)
    gates_t = jnp.zeros((V7X_LANES, t), F32)
    for k in range(TOP_K):
        gates_t = jnp.where(lane_rows == k, gates[k], gates_t)
    gcol_ref[...] = gates_t.T


def _pool_layer_kernel(dn_alpha, h_ref, mem_ref, w_in_ref, wbd_ref, pscale_ref, mem_kv_ref, w_out_ref, g_ref, b_ref,
                       rwt_ref, rb_ref, tri_ref, h_out_ref, hp_ref, idx_ref, rank_ref, counts_ref, gcol_ref,
                       ue_ref, w2_ref, w4_ref, w8_ref, kv_ref, cat_ref, base_ref):
    s = pl.program_id(1)
    ts = h_ref.shape[0]
    pool_width = wbd_ref.shape[0]
    hist = POOL_HISTORY

    @pl.when(s == 0)
    def _():
        ue_ref[0:hist, :] = jnp.zeros((hist, pool_width), F32)
        kv_ref[...] = _dot(mem_ref[0].astype(BF16), mem_kv_ref[...]).astype(BF16)

    x = h_ref[...]
    proj = _dot(x.astype(BF16), w_in_ref[...])
    u = proj[:, :pool_width]
    ue_ref[hist:, :] = u
    n = ts + hist
    w2_ref[8:n, :] = ue_ref[8:n, :] + ue_ref[7:n - 1, :]
    w4_ref[16:n, :] = w2_ref[16:n, :] + w2_ref[14:n - 2, :]
    w8_ref[24:n, :] = w4_ref[24:n, :] + w4_ref[20:n - 4, :]
    w16 = w8_ref[hist:n, :] + w8_ref[hist - 8:n - 8, :]
    group_dim = pool_width // len(POOL_WINDOWS)
    col = _lane_iota((ts, pool_width))
    wsum = jnp.where(col < group_dim, w2_ref[hist:n, :],
                     jnp.where(col < 2 * group_dim, w4_ref[hist:n, :],
                               jnp.where(col < 3 * group_dim, w8_ref[hist:n, :], w16)))
    window = jnp.where(col < group_dim, POOL_WINDOWS[0],
                       jnp.where(col < 2 * group_dim, POOL_WINDOWS[1],
                                 jnp.where(col < 3 * group_dim, POOL_WINDOWS[2], POOL_WINDOWS[3])))
    tpos = s * ts + lax.broadcasted_iota(jnp.int32, (ts, pool_width), 0) + 1
    cnt = jnp.minimum(tpos, window).astype(F32)
    d = wsum / cnt - u
    ue_ref[0:hist, :] = u[ts - hist:, :]
    pooled = _dot(d.astype(BF16), wbd_ref[...]) * pscale_ref[...]
    cat_ref[:, :pool_width] = pooled.astype(BF16)
    _memory_attention(proj, kv_ref, cat_ref, pool_width, pool_width)
    _mix_epilogue(x, cat_ref, w_out_ref, g_ref, b_ref, rwt_ref, rb_ref, tri_ref, dn_alpha, h_out_ref, hp_ref, idx_ref,
                  rank_ref, counts_ref, gcol_ref, base_ref)


def _bias_pieces(c):
    hi = c.astype(BF16).astype(F32)
    r = c - hi
    mid = r.astype(BF16).astype(F32)
    lo = (r - mid).astype(BF16).astype(F32)
    return hi, mid, lo


def _bias_lane_pieces(c3, n_heads):
    hi, mid, lo = _bias_pieces(c3)
    lane = _lane_iota(c3.shape)
    out = jnp.where(lane < n_heads, hi, jnp.where(lane < 2 * n_heads, mid, jnp.where(lane < 3 * n_heads, lo,
                    jnp.where(lane == 3 * n_heads, 1.0, 0.0))))
    return out.astype(BF16)


def _key_bias_selector(n_heads):
    sel = np.zeros((V7X_LANES, n_heads * V7X_LANES), np.float32)
    for h in range(n_heads):
        spare = h * V7X_LANES + HEAD_DIM
        for p in range(3):
            sel[3 * n_heads, spare + p] = 1.0
            sel[p * n_heads + h, spare + 3 + p] = -1.0
    return jnp.asarray(sel, BF16)


def _shared_kv_block(h, kvw_ref, fb_ref, tri_ref, sel_ref, k_ref, vt_ref, cq_ref, carry_ref):
    s = pl.program_id(1)
    v_width = vt_ref.shape[1]
    n_heads = k_ref.shape[1] // V7X_LANES

    @pl.when(s == 0)
    def _():
        carry_ref[...] = jnp.zeros(carry_ref.shape, F32)

    proj = _dot(h.astype(BF16), kvw_ref[...])
    z = proj[:, 2 * v_width:] + fb_ref[...]
    logf = jnp.minimum(z, 0.0) - jnp.log(1.0 + jnp.exp(-jnp.abs(z)))
    hi, mid, lo = _bias_pieces(logf)
    tri = tri_ref[...]
    c = _dot(tri, hi.astype(BF16)) + _dot(tri, mid.astype(BF16)) + _dot(tri, lo.astype(BF16)) + carry_ref[0:1, :]
    ts = c.shape[0]
    carry_ref[0:1, :] = c[ts - 1:ts, :]
    pieces = _bias_lane_pieces(c * LOG2_E, n_heads)
    cq_ref[...] = pieces
    bias_lanes = _dot(pieces, sel_ref[...])
    lane = _lane_iota((ts, V7X_LANES))
    for hd in range(n_heads):
        pair = proj[:, (hd // 2) * V7X_LANES:(hd // 2 + 1) * V7X_LANES]
        k_h = pair if hd % 2 == 0 else pltpu.roll(pair, HEAD_DIM, axis=1)
        cols = slice(hd * V7X_LANES, (hd + 1) * V7X_LANES)
        k_ref[:, cols] = jnp.where(lane < HEAD_DIM, k_h, bias_lanes[:, cols]).astype(BF16)
    vt_ref[0] = proj[:, v_width:2 * v_width].T.astype(BF16)


def _fox_layer_kernel(dn_alpha, h_ref, mem_ref, k_ref, vt_ref, cq_ref, w_in_ref, mem_kv_ref, w_out_ref,
                      g_ref, b_ref, rwt_ref, rb_ref, tri_ref, h_out_ref, hp_ref, idx_ref, rank_ref, counts_ref, gcol_ref,
                      kv_ref, cat_ref, qa_ref, m_ref, l_ref, acc_ref, base_ref):
    i = pl.program_id(1)
    tq = h_ref.shape[0]
    n_heads = k_ref.shape[2] // V7X_LANES
    fox_width = vt_ref.shape[1]

    @pl.when(i == 0)
    def _():
        kv_ref[...] = _dot(mem_ref[0].astype(BF16), mem_kv_ref[...]).astype(BF16)

    x = h_ref[...]
    proj = _dot(x.astype(BF16), w_in_ref[...])
    cq = cq_ref[...].astype(F32)
    lane = _lane_iota((tq, V7X_LANES))
    for h in range(n_heads):
        pair = proj[:, (h // 2) * V7X_LANES:(h // 2 + 1) * V7X_LANES] * (ATTN_SCALE * LOG2_E)
        q_h = pair if h % 2 == 0 else pltpu.roll(pair, HEAD_DIM, axis=1)
        piece = [cq[:, p * n_heads + h:p * n_heads + h + 1] for p in range(3)]
        spare = jnp.where(lane == HEAD_DIM, piece[0], jnp.where(lane == HEAD_DIM + 1, piece[1],
                          jnp.where(lane == HEAD_DIM + 2, piece[2], jnp.where(lane < HEAD_DIM + 6, 1.0, 0.0))))
        qa_ref[h] = jnp.where(lane < HEAD_DIM, q_h, spare).astype(BF16)
    m_ref[...] = jnp.full(m_ref.shape, NEG_INF, F32)
    l_ref[...] = jnp.zeros(l_ref.shape, F32)
    acc_ref[...] = jnp.zeros(acc_ref.shape, F32)
    causal_t = lax.broadcasted_iota(jnp.int32, (tq, tq), 0) <= lax.broadcasted_iota(jnp.int32, (tq, tq), 1)

    ones_rows = (lax.broadcasted_iota(jnp.int32, (V7X_BF16_SUBLANES, tq), 0) == 0).astype(BF16)

    def tile(j, masked):
        start = pl.multiple_of(j * tq, tq)

        def scores(h):
            kb = k_ref[0, pl.ds(start, tq), h * V7X_LANES:(h + 1) * V7X_LANES]
            return _dot_nt(kb, qa_ref[h])

        st_next = scores(0)
        for h in range(n_heads):
            st = st_next
            if h + 1 < n_heads:
                st_next = scores(h + 1)
            if masked:
                st = jnp.where(causal_t, st, NEG_INF)
            m_old = m_ref[h:h + 1, :]
            m_new = jnp.maximum(m_old, jnp.max(st, axis=0, keepdims=True))
            alpha = jnp.exp2(m_old - m_new)
            pt = jnp.exp2(st - m_new).astype(BF16)
            m_ref[h:h + 1, :] = m_new
            rows = slice(h * HEAD_DIM, (h + 1) * HEAD_DIM)
            vb = jnp.concatenate([vt_ref[0, rows, pl.ds(start, tq)], ones_rows], axis=0)
            pv = _dot(vb, pt)
            acc_ref[rows, :] = alpha * acc_ref[rows, :] + pv[:HEAD_DIM]
            l_ref[h:h + 1, :] = alpha * l_ref[h:h + 1, :] + pv[HEAD_DIM:HEAD_DIM + 1]

    tile(i, True)

    def off_diagonal(j, carry):
        tile(j, False)
        return carry

    lax.fori_loop(0, i, off_diagonal, 0)
    for h in range(n_heads):
        rows = slice(h * HEAD_DIM, (h + 1) * HEAD_DIM)
        acc_ref[rows, :] = acc_ref[rows, :] / l_ref[h:h + 1, :]
    cat_ref[:, :fox_width] = acc_ref[...].T.astype(BF16)
    _memory_attention(proj, kv_ref, cat_ref, fox_width, fox_width)
    _mix_epilogue(x, cat_ref, w_out_ref, g_ref, b_ref, rwt_ref, rb_ref, tri_ref, dn_alpha, h_out_ref, hp_ref, idx_ref,
                  rank_ref, counts_ref, gcol_ref, base_ref)


def _expert_kernel(layer, be_ref, bv_ref, bslot_ref, bnext_ref, nact_ref, x_ref, w1_hbm, b1_ref, w2_hbm, b2_ref, y_ref,
                   w1s_ref, w2s_ref, act_ref, w1f_ref, w2f_ref, sem_ref):
    i = pl.program_id(0)
    active = i < nact_ref[0]
    two_f = w1f_ref.shape[2]
    group = 2 * V7X_LANES
    n_groups = two_f // group

    def weight_copies(expert, slot):
        return (pltpu.make_async_copy(w1_hbm.at[layer, expert], w1f_ref.at[slot], sem_ref.at[0, slot]),
                pltpu.make_async_copy(w2_hbm.at[layer, expert], w2f_ref.at[slot], sem_ref.at[1, slot]))

    @pl.when(i == 0)
    def _():
        for cp in weight_copies(be_ref[0], bslot_ref[0]):
            cp.start()

    @pl.when(jnp.logical_and(active, jnp.logical_or(i == 0, be_ref[i] != be_ref[jnp.maximum(i - 1, 0)])))
    def _():
        slot = bslot_ref[i]
        for cp in weight_copies(be_ref[i], slot):
            cp.wait()

        @pl.when(bnext_ref[i] >= 0)
        def _():
            for cp in weight_copies(bnext_ref[i], 1 - slot):
                cp.start()

        src = lax.broadcasted_iota(jnp.int32, (group, group), 0)
        dst = lax.broadcasted_iota(jnp.int32, (group, group), 1)
        want = jnp.where(dst < V7X_LANES, 2 * dst, 2 * (dst - V7X_LANES) + 1)
        perm = (src == want).astype(BF16)
        for g in range(n_groups):
            cols = slice(g * group, (g + 1) * group)
            w1s_ref[:, cols] = _dot(w1f_ref[slot, :, cols].astype(BF16), perm).astype(BF16)
        w2s_ref[...] = w2f_ref[slot].astype(BF16)

    @pl.when(active)
    def _():
        row_id = lax.broadcasted_iota(jnp.int32, x_ref.shape, 0)
        x_low, x_high = _unpack_bf16_pairs(jnp.where(row_id < bv_ref[i], x_ref[...], jnp.uint32(0)))
        xb = jnp.concatenate([x_low.astype(BF16), x_high.astype(BF16)], axis=1)
        slab_groups = 2
        for s0 in range(0, n_groups, slab_groups):
            cols = slice(s0 * group, (s0 + slab_groups) * group)
            hdn = _dot(xb, w1s_ref[:, cols]) + b1_ref[0, :, cols]
            for g in range(slab_groups):
                x_glu = jnp.minimum(hdn[:, g * group:g * group + V7X_LANES], SWIGLU_LIMIT)
                x_lin = jnp.clip(hdn[:, g * group + V7X_LANES:(g + 1) * group], -SWIGLU_LIMIT, SWIGLU_LIMIT)
                act = x_glu * jax.nn.sigmoid(SWIGLU_ALPHA * x_glu) * (x_lin + 1.0)
                act_ref[:, (s0 + g) * V7X_LANES:(s0 + g + 1) * V7X_LANES] = act.astype(BF16)
        y_ref[...] = _pack_bf16_pairs(_dot(act_ref[...], w2s_ref[...]) + b2_ref[0])

    @pl.when(jnp.logical_not(active))
    def _():
        y_ref[...] = jnp.zeros(y_ref.shape, jnp.uint32)


def _combine_rows(dn_alpha, h_ref, y_ref, gcol_ref, g_ref, b_ref):
    gates = gcol_ref[...]
    low, high = _unpack_bf16_pairs(y_ref[0])
    low, high = gates[:, 0:1] * low, gates[:, 0:1] * high
    for k in range(1, TOP_K):
        low_k, high_k = _unpack_bf16_pairs(y_ref[k])
        low = low + gates[:, k:k + 1] * low_k
        high = high + gates[:, k:k + 1] * high_k
    ffn = jnp.concatenate([low, high], axis=1)
    return _layer_norm(dn_alpha * h_ref[...] + ffn, g_ref[...], b_ref[...])


def _combine_kernel(dn_alpha, h_ref, y_ref, gcol_ref, g_ref, b_ref, o_ref):
    o_ref[...] = _combine_rows(dn_alpha, h_ref, y_ref, gcol_ref, g_ref, b_ref)


def _combine_kv_kernel(dn_alpha, h_ref, y_ref, gcol_ref, g_ref, b_ref, kvw_ref, fb_ref, tri_ref, sel_ref,
                       o_ref, k_ref, vt_ref, cq_ref, carry_ref):
    h2 = _combine_rows(dn_alpha, h_ref, y_ref, gcol_ref, g_ref, b_ref)
    o_ref[...] = h2
    _shared_kv_block(h2, kvw_ref, fb_ref, tri_ref, sel_ref, k_ref, vt_ref, cq_ref, carry_ref)


def _sc_gather_rows(table, idx):
    n, = idx.shape
    _, d = table.shape
    info = plsc.get_sparse_core_info()
    n_workers = info.num_cores * info.num_subcores
    per_worker = n // n_workers
    n_chunks = per_worker // SC_CHUNK
    assert n % (n_workers * SC_CHUNK) == 0 and n_chunks % 2 == 0
    mesh = plsc.VectorSubcoreMesh(core_axis_name="core", subcore_axis_name="subcore")

    @functools.partial(
        pl.kernel,
        out_type=jax.ShapeDtypeStruct((n, d), table.dtype),
        mesh=mesh,
        scratch_types=[
            pltpu.VMEM((per_worker,), jnp.int32),
            pltpu.VMEM((2, SC_CHUNK, d), table.dtype),
            pltpu.SemaphoreType.DMA((2,)),
            pltpu.SemaphoreType.DMA((2,)),
        ],
    )
    def gather_kernel(table_hbm, idx_hbm, out_hbm, idx_v, rows_v, gather_sem, write_sem):
        worker = lax.axis_index("subcore") * info.num_cores + lax.axis_index("core")
        base = pl.multiple_of(worker * per_worker, SC_CHUNK)
        pltpu.sync_copy(idx_hbm.at[pl.ds(base, per_worker)], idx_v)

        def gather(c, buf):
            rows = idx_v.at[pl.ds(pl.multiple_of(c * SC_CHUNK, SC_CHUNK), SC_CHUNK)]
            return pltpu.make_async_copy(table_hbm.at[rows], rows_v.at[buf], gather_sem.at[buf])

        def write(c, buf):
            off = pl.multiple_of(base + c * SC_CHUNK, SC_CHUNK)
            return pltpu.make_async_copy(rows_v.at[buf], out_hbm.at[pl.ds(off, SC_CHUNK)], write_sem.at[buf])

        gather(0, 0).start()

        @pl.loop(0, n_chunks, step=2)
        def _(c0):
            for buf in range(2):
                c = c0 + buf
                gather(c, buf).wait()

                @pl.when(c >= 1)
                def _():
                    write(c - 1, 1 - buf).wait()

                @pl.when(c + 1 < n_chunks)
                def _():
                    gather(c + 1, 1 - buf).start()

                write(c, buf).start()

        write(n_chunks - 1, 1).wait()

    return gather_kernel(table, idx)


def _sc_scatter_rows(rows, pos, n_out):
    t, d = rows.shape
    top_k = pos.shape[0]
    pos_flat = pos.reshape(-1)
    info = plsc.get_sparse_core_info()
    n_workers = info.num_cores * info.num_subcores
    per_worker = t // n_workers
    n_chunks = per_worker // SC_CHUNK
    assert t % (n_workers * SC_CHUNK) == 0 and n_chunks % 2 == 0
    mesh = plsc.VectorSubcoreMesh(core_axis_name="core", subcore_axis_name="subcore")

    @functools.partial(
        pl.kernel,
        out_type=jax.ShapeDtypeStruct((n_out, d), rows.dtype),
        mesh=mesh,
        scratch_types=[pltpu.VMEM((2, SC_CHUNK, d), rows.dtype)]
        + [pltpu.VMEM((SC_CHUNK,), jnp.int32) for _ in range(2 * top_k)]
        + [pltpu.SemaphoreType.DMA((2,)), pltpu.SemaphoreType.DMA((2,))],
    )
    def scatter_kernel(rows_hbm, pos_hbm, out_hbm, rows_v, *rest):
        idx_bufs = rest[:2 * top_k]
        load_sem, write_sem = rest[2 * top_k:]
        worker = lax.axis_index("subcore") * info.num_cores + lax.axis_index("core")
        base = pl.multiple_of(worker * per_worker, SC_CHUNK)

        def loads(c, buf):
            off = pl.multiple_of(base + c * SC_CHUNK, SC_CHUNK)
            copies = [pltpu.make_async_copy(rows_hbm.at[pl.ds(off, SC_CHUNK)], rows_v.at[buf], load_sem.at[buf])]
            for k in range(top_k):
                src = pos_hbm.at[pl.ds(pl.multiple_of(k * t + off, SC_CHUNK), SC_CHUNK)]
                copies.append(pltpu.make_async_copy(src, idx_bufs[buf * top_k + k], load_sem.at[buf]))
            return copies

        def scatters(buf):
            return [pltpu.make_async_copy(rows_v.at[buf], out_hbm.at[idx_bufs[buf * top_k + k]], write_sem.at[buf])
                    for k in range(top_k)]

        for cp in loads(0, 0):
            cp.start()

        @pl.loop(0, n_chunks, step=2)
        def _(c0):
            for buf in range(2):
                c = c0 + buf
                for cp in loads(c, buf):
                    cp.wait()

                @pl.when(c >= 1)
                def _():
                    for cp in scatters(1 - buf):
                        cp.wait()

                @pl.when(c + 1 < n_chunks)
                def _():
                    for cp in loads(c + 1, 1 - buf):
                        cp.start()

                for cp in scatters(buf):
                    cp.start()

        for cp in scatters(1):
            cp.wait()

    return scatter_kernel(rows, pos_flat)


def _const_spec(shape):
    zeros = (0,) * len(shape)
    return pl.BlockSpec(shape, lambda *_: zeros, pipeline_mode=pl.Buffered(1))


def _tc_params(n_axes):
    return pltpu.CompilerParams(dimension_semantics=("arbitrary",) * n_axes,
                                vmem_limit_bytes=V7X_VMEM_LIMIT_BYTES)


def _mix_out_shapes(t, d, n_exp):
    return (jax.ShapeDtypeStruct((t, d), F32),
            jax.ShapeDtypeStruct((t, d // 2), jnp.uint32),
            jax.ShapeDtypeStruct((TOP_K, t), jnp.int32),
            jax.ShapeDtypeStruct((TOP_K, t), jnp.int32),
            jax.ShapeDtypeStruct((n_exp, V7X_LANES), F32),
            jax.ShapeDtypeStruct((t, V7X_LANES), F32))


def _mix_out_specs(rows, d, n_exp, row, tok_col):
    return (pl.BlockSpec((rows, d), row),
            pl.BlockSpec((rows, d // 2), row),
            pl.BlockSpec((TOP_K, rows), tok_col),
            pl.BlockSpec((TOP_K, rows), tok_col),
            pl.BlockSpec((n_exp, V7X_LANES), lambda *_: (0, 0)),
            pl.BlockSpec((rows, V7X_LANES), row))


def _strict_upper_tri(n):
    return (lax.broadcasted_iota(jnp.int32, (n, n), 0) < lax.broadcasted_iota(jnp.int32, (n, n), 1)).astype(BF16)


def _pool_layer(h, mem, w_in, wbd, pscale, mem_kv, w_out, ln_g, ln_b, rwt, rb, dn_alpha, seq):
    t, d = h.shape
    n_mem = mem.shape[1]
    pool_width = wbd.shape[0]
    ts = SEQ_BLOCK
    nsb = seq // ts
    row = lambda b, s: (b * nsb + s, 0)
    n_exp = rwt.shape[0]
    tri = _strict_upper_tri(ts)
    return pl.pallas_call(
        functools.partial(_pool_layer_kernel, dn_alpha),
        out_shape=_mix_out_shapes(t, d, n_exp),
        grid=(t // seq, nsb),
        in_specs=[
            pl.BlockSpec((ts, d), row),
            pl.BlockSpec((1, n_mem, d), lambda b, s: (b, 0, 0)),
            _const_spec(w_in.shape), _const_spec(wbd.shape), _const_spec(pscale.shape), _const_spec(mem_kv.shape),
            _const_spec(w_out.shape), _const_spec(ln_g.shape), _const_spec(ln_b.shape), _const_spec(rwt.shape),
            _const_spec(rb.shape), _const_spec(tri.shape),
        ],
        out_specs=_mix_out_specs(ts, d, n_exp, row, lambda b, s: (0, b * nsb + s)),
        scratch_shapes=[pltpu.VMEM((ts + POOL_HISTORY, pool_width), F32)] * 4
        + [pltpu.VMEM((n_mem, 2 * MEM_WIDTH), BF16), pltpu.VMEM((ts, d), BF16), pltpu.VMEM((n_exp, V7X_LANES), F32)],
        compiler_params=_tc_params(2),
        name="pool_layer",
    )(h, mem, w_in, wbd, pscale, mem_kv, w_out, ln_g, ln_b, rwt, rb, tri)


def _fox_layer(h, mem, k_aug, v_t, cq, w_in, mem_kv, w_out, ln_g, ln_b, rwt, rb, dn_alpha, seq):
    t, d = h.shape
    n_mem = mem.shape[1]
    k_width = k_aug.shape[1]
    v_width = v_t.shape[1]
    n_heads = k_width // V7X_LANES
    stat_rows = -(-n_heads // V7X_SUBLANES) * V7X_SUBLANES
    tq = Q_BLOCK
    nqb = seq // tq
    n_batch = t // seq
    row = lambda b, i: (b * nqb + i, 0)
    per_batch = lambda b, i: (b, 0, 0)
    n_exp = rwt.shape[0]
    tri = _strict_upper_tri(tq)
    return pl.pallas_call(
        functools.partial(_fox_layer_kernel, dn_alpha),
        out_shape=_mix_out_shapes(t, d, n_exp),
        grid=(n_batch, nqb),
        in_specs=[
            pl.BlockSpec((tq, d), row),
            pl.BlockSpec((1, n_mem, d), per_batch),
            pl.BlockSpec((1, seq, k_width), per_batch),
            pl.BlockSpec((1, v_width, seq), per_batch),
            pl.BlockSpec((tq, V7X_LANES), row),
            _const_spec(w_in.shape), _const_spec(mem_kv.shape), _const_spec(w_out.shape), _const_spec(ln_g.shape),
            _const_spec(ln_b.shape), _const_spec(rwt.shape), _const_spec(rb.shape), _const_spec(tri.shape),
        ],
        out_specs=_mix_out_specs(tq, d, n_exp, row, lambda b, i: (0, b * nqb + i)),
        scratch_shapes=[pltpu.VMEM((n_mem, 2 * MEM_WIDTH), BF16), pltpu.VMEM((tq, d), BF16),
                        pltpu.VMEM((n_heads, tq, V7X_LANES), BF16),
                        pltpu.VMEM((stat_rows, tq), F32), pltpu.VMEM((stat_rows, tq), F32),
                        pltpu.VMEM((v_width, tq), F32),
                        pltpu.VMEM((n_exp, V7X_LANES), F32)],
        compiler_params=_tc_params(2),
        name="fox_layer",
    )(h, mem, k_aug.reshape(n_batch, seq, k_width), v_t, cq, w_in, mem_kv, w_out, ln_g, ln_b, rwt, rb, tri)


def _slot_kernel(pad_start_ref, idx_ref, rank_ref, pos_ref):
    pos = rank_ref[...]
    idx = idx_ref[...]
    for e in range(pad_start_ref.shape[0]):
        pos = pos + jnp.where(idx == e, pad_start_ref[e], 0)
    pos_ref[...] = pos


def _route(idx_t, rank_t, counts, n_experts):
    top_k, t = idx_t.shape
    n_assign = top_k * t
    padded = ((counts + MOE_BLOCK - 1) // MOE_BLOCK) * MOE_BLOCK
    pad_end = jnp.cumsum(padded)
    pad_start = pad_end - padded
    n_blocks = n_assign // MOE_BLOCK + n_experts
    block_start = jnp.arange(n_blocks, dtype=jnp.int32) * MOE_BLOCK
    block_expert = jnp.minimum(jnp.sum((pad_end[None, :] <= block_start[:, None]).astype(jnp.int32), axis=1),
                               n_experts - 1)
    block_valid = jnp.clip(counts[block_expert] - (block_start - pad_start[block_expert]), 0, MOE_BLOCK)
    n_active = (pad_end[-1:] // MOE_BLOCK).astype(jnp.int32)
    expert_ids = jnp.arange(n_experts, dtype=jnp.int32)
    in_use = counts > 0
    expert_slot = (jnp.cumsum(in_use.astype(jnp.int32)) - in_use.astype(jnp.int32)) % 2
    later = lax.cummin(jnp.where(in_use, expert_ids, n_experts), axis=0, reverse=True)
    next_in_use = jnp.concatenate([later[1:], jnp.full((1,), n_experts, jnp.int32)])
    next_in_use = jnp.where(next_in_use < n_experts, next_in_use, -1)
    block_slot = expert_slot[block_expert].astype(jnp.int32)
    block_next = next_in_use[block_expert].astype(jnp.int32)
    cols = min(SLOT_COLS, t)
    pos = pl.pallas_call(
        _slot_kernel,
        out_shape=jax.ShapeDtypeStruct((top_k, t), jnp.int32),
        grid_spec=pltpu.PrefetchScalarGridSpec(
            num_scalar_prefetch=1,
            grid=(t // cols,),
            in_specs=[pl.BlockSpec((top_k, cols), lambda i, ps: (0, i))] * 2,
            out_specs=pl.BlockSpec((top_k, cols), lambda i, ps: (0, i)),
        ),
        compiler_params=_tc_params(1),
        name="slots",
    )(pad_start.astype(jnp.int32), idx_t, rank_t)
    return pos, (block_expert, block_valid.astype(jnp.int32), block_slot, block_next, n_active)


def _experts(xs, block_expert, block_valid, block_slot, block_next, n_active, w1_all, b1, w2_all, b2, layer):
    n_slots, half_d = xs.shape
    _, n_exp, d, two_f = w1_all.shape
    d_ff = two_f // 2
    assert d == 2 * half_d
    block = lambda i, *_: (i, 0)
    per_expert = lambda i, be, *_: (be[i], 0, 0)
    grid_spec = pltpu.PrefetchScalarGridSpec(
        num_scalar_prefetch=5,
        grid=(n_slots // MOE_BLOCK,),
        in_specs=[
            pl.BlockSpec((MOE_BLOCK, half_d), block),
            pl.BlockSpec(memory_space=pl.ANY),
            pl.BlockSpec((1, 1, two_f), per_expert),
            pl.BlockSpec(memory_space=pl.ANY),
            pl.BlockSpec((1, 1, d), per_expert),
        ],
        out_specs=pl.BlockSpec((MOE_BLOCK, half_d), block),
        scratch_shapes=[pltpu.VMEM((d, two_f), BF16), pltpu.VMEM((d_ff, d), BF16), pltpu.VMEM((MOE_BLOCK, d_ff), BF16),
                        pltpu.VMEM((2, d, two_f), F32), pltpu.VMEM((2, d_ff, d), F32),
                        pltpu.SemaphoreType.DMA((2, 2))],
    )
    return pl.pallas_call(
        functools.partial(_expert_kernel, layer),
        out_shape=jax.ShapeDtypeStruct((n_slots, half_d), jnp.uint32),
        grid_spec=grid_spec,
        compiler_params=_tc_params(1),
        name="experts",
    )(block_expert, block_valid, block_slot, block_next, n_active, xs, w1_all, b1.reshape(n_exp, 1, two_f), w2_all,
      b2.reshape(n_exp, 1, d))


def _combine(h, yg, gcol, ln_g, ln_b, dn_alpha, seq, shared_kv_weights=None):
    t, d = h.shape
    ts = SEQ_BLOCK
    nsb = seq // ts
    row = lambda b, s: (b * nsb + s, 0)
    in_specs = [pl.BlockSpec((ts, d), row), pl.BlockSpec((TOP_K, ts, d // 2), lambda b, s: (0, b * nsb + s, 0)),
                pl.BlockSpec((ts, V7X_LANES), row), _const_spec(ln_g.shape), _const_spec(ln_b.shape)]
    args = (h, yg.reshape(TOP_K, t, d // 2), gcol, ln_g, ln_b)
    if shared_kv_weights is None:
        return pl.pallas_call(
            functools.partial(_combine_kernel, dn_alpha),
            out_shape=jax.ShapeDtypeStruct((t, d), F32),
            grid=(t // seq, nsb),
            in_specs=in_specs,
            out_specs=pl.BlockSpec((ts, d), row),
            compiler_params=_tc_params(2),
            name="combine",
        )(*args), None
    kvw, fb, sel_k, n_heads = shared_kv_weights
    k_width = n_heads * V7X_LANES
    v_width = n_heads * HEAD_DIM
    tri = (lax.broadcasted_iota(jnp.int32, (ts, ts), 1) <= lax.broadcasted_iota(jnp.int32, (ts, ts), 0)).astype(BF16)
    out = pl.pallas_call(
        functools.partial(_combine_kv_kernel, dn_alpha),
        out_shape=(jax.ShapeDtypeStruct((t, d), F32),
                   jax.ShapeDtypeStruct((t, k_width), BF16),
                   jax.ShapeDtypeStruct((t // seq, v_width, seq), BF16),
                   jax.ShapeDtypeStruct((t, V7X_LANES), BF16)),
        grid=(t // seq, nsb),
        in_specs=in_specs + [_const_spec(kvw.shape), _const_spec(fb.shape), _const_spec(tri.shape),
                             _const_spec(sel_k.shape)],
        out_specs=(pl.BlockSpec((ts, d), row),
                   pl.BlockSpec((ts, k_width), row),
                   pl.BlockSpec((1, v_width, ts), lambda b, s: (b, 0, s)),
                   pl.BlockSpec((ts, V7X_LANES), row)),
        scratch_shapes=[pltpu.VMEM((V7X_SUBLANES, V7X_LANES), F32)],
        compiler_params=_tc_params(2),
        name="combine_shared_kv",
    )(*args, kvw, fb, tri, sel_k)
    return out[0], out[1:]


def _moe(mixed, w1_all, b1, w2_all, b2, layer, ln_g, ln_b, dn_alpha, seq, shared_kv_weights):
    h1, h1_packed, idx_t, rank_t, counts, gcol = mixed
    n_exp = w1_all.shape[1]
    d_ff = w2_all.shape[2]
    pos, block_meta = _route(idx_t, rank_t, counts[:, 0].astype(jnp.int32), n_exp)
    b1p = jnp.swapaxes(b1.reshape(n_exp, d_ff // V7X_LANES, V7X_LANES, 2), 2, 3).reshape(n_exp, 2 * d_ff)
    xs = _sc_scatter_rows(h1_packed, pos, block_meta[0].shape[0] * MOE_BLOCK)
    y = _experts(xs, *block_meta, w1_all, b1p, w2_all, b2, layer)
    yg = _sc_gather_rows(y, pos.reshape(-1))
    return _combine(h1, yg, gcol, ln_g, ln_b, dn_alpha, seq, shared_kv_weights)


def kernel(x, mem, ln1_g, ln1_b, ln2_g, ln2_b, a_w_in, pool_w, pool_scale, a_mem_kv, a_w_out, kv_w, fgate_b, b_w_in, b_mem_kv, b_w_out, router_w, router_b, moe_w1, moe_b1, moe_w2, moe_b2):
    n_batch, seq, d = x.shape
    t = n_batch * seq
    depth = ln1_g.shape[0]
    n_a = a_w_in.shape[0]
    dn_alpha = float((2 * depth) ** 0.25)
    n_exp = router_w.shape[2]
    fox_heads = fgate_b.shape[0]
    fox_width = fox_heads * HEAD_DIM
    assert seq % SEQ_BLOCK == 0 and seq % Q_BLOCK == 0
    assert 3 * fox_heads < V7X_LANES

    assert 1 <= n_a < depth
    pad = V7X_LANES - 3 * fox_heads
    kvw = jnp.concatenate([kv_w[:, :2 * fox_width],
                           jnp.pad(jnp.tile(kv_w[:, 2 * fox_width:], (1, 3)), ((0, 0), (0, pad)))],
                          axis=1).astype(BF16)
    fb = jnp.pad(jnp.tile(fgate_b, 3), (0, pad)).reshape(1, V7X_LANES)
    shared_kv_weights = (kvw, fb, _key_bias_selector(fox_heads), fox_heads)

    h = x.reshape(t, d)
    row = lambda a: a.reshape(1, -1)
    shared = None
    for l in range(depth):
        rwt = router_w[l].T
        rb = router_b[l].reshape(n_exp, 1)
        if l < n_a:
            wbd = jax.scipy.linalg.block_diag(*[pool_w[l, g] for g in range(pool_w.shape[1])]).astype(BF16)
            mixed = _pool_layer(
                h, mem, a_w_in[l].astype(BF16), wbd, row(pool_scale[l]), a_mem_kv[l].astype(BF16),
                a_w_out[l].astype(BF16), row(ln1_g[l]), row(ln1_b[l]), rwt, rb, dn_alpha, seq)
        else:
            j = l - n_a
            k_aug, v_t, cq = shared
            mixed = _fox_layer(
                h, mem, k_aug, v_t, cq, b_w_in[j].astype(BF16), b_mem_kv[j].astype(BF16),
                b_w_out[j].astype(BF16), row(ln1_g[l]), row(ln1_b[l]), rwt, rb, dn_alpha, seq)
        h, new_shared = _moe(mixed, moe_w1, moe_b1[l], moe_w2, moe_b2[l], l, row(ln2_g[l]), row(ln2_b[l]), dn_alpha,
                             seq, shared_kv_weights if l == n_a - 1 else None)
        if new_shared is not None:
            shared = new_shared
    return h.reshape(n_batch, seq, d)
```

```python
import functools
import numpy as np
import jax
import jax.numpy as jnp
from jax import lax
from jax.experimental import pallas as pl
from jax.experimental.pallas import tpu as pltpu
from jax.experimental.pallas import tpu_sc as plsc

HEAD_DIM = 64
MEM_HEADS = 4
MEM_WIDTH = MEM_HEADS * HEAD_DIM
POOL_WINDOWS = (2, 4, 8, 16)
TOP_K = 4
SWIGLU_ALPHA = 1.702
SWIGLU_LIMIT = 7.0
MOE_BLOCK = 1024
LN_EPS = 1e-5
ATTN_SCALE = HEAD_DIM ** -0.5
LOG2_E = 1.4426950408889634

V7X_LANES = 128
V7X_SUBLANES = 8
V7X_BF16_SUBLANES = 16
V7X_VMEM_LIMIT_BYTES = 56 * 1024 * 1024
POOL_HISTORY = 32
SC_CHUNK = 64
SEQ_BLOCK = 1024
Q_BLOCK = 512
SLOT_COLS = 8192

F32 = jnp.float32
BF16 = jnp.bfloat16
NEG_INF = float("-inf")


def _dot(a, b):
    return jnp.dot(a, b, preferred_element_type=F32)


def _dot_nt(a, b):
    return lax.dot_general(a, b, (((1,), (1,)), ((), ())), preferred_element_type=F32)


def _lane_iota(shape):
    return lax.broadcasted_iota(jnp.int32, shape, len(shape) - 1)


def _pair_attention(q2, keys, vals):
    lane = _lane_iota(q2.shape)
    first = lane < HEAD_DIM
    outs = []
    for sel in (first, jnp.logical_not(first)):
        qh = jnp.where(sel, q2, 0.0).astype(BF16)
        s = _dot_nt(qh, keys)
        m = jnp.max(s, axis=-1, keepdims=True)
        p = jnp.exp(s - m)
        l = jnp.sum(p, axis=-1, keepdims=True)
        outs.append(_dot(p.astype(BF16), vals) / l)
    return jnp.where(first, outs[0], outs[1])


def _memory_attention(proj, kv_ref, cat_ref, out_off, q_off):
    for p in range(MEM_HEADS // 2):
        lo = p * V7X_LANES
        q2 = proj[:, q_off + lo:q_off + lo + V7X_LANES] * ATTN_SCALE
        keys = kv_ref[:, lo:lo + V7X_LANES]
        vals = kv_ref[:, MEM_WIDTH + lo:MEM_WIDTH + lo + V7X_LANES]
        cat_ref[:, out_off + lo:out_off + lo + V7X_LANES] = _pair_attention(q2, keys, vals).astype(BF16)


def _layer_norm(z, g, b):
    mu = jnp.mean(z, axis=-1, keepdims=True)
    zc = z - mu
    var = jnp.mean(zc * zc, axis=-1, keepdims=True)
    return zc * lax.rsqrt(var + LN_EPS) * g + b


def _pack_bf16_pairs(x):
    w = x.shape[1] // 2
    bits = lax.bitcast_convert_type(x.astype(BF16).astype(F32), jnp.uint32)
    return lax.shift_right_logical(bits[:, :w], jnp.uint32(16)) | (bits[:, w:] & jnp.uint32(0xFFFF0000))


def _unpack_bf16_pairs(p):
    low = lax.bitcast_convert_type(lax.shift_left(p, jnp.uint32(16)), F32)
    high = lax.bitcast_convert_type(p & jnp.uint32(0xFFFF0000), F32)
    return low, high


def _expert_ranks(idxs, tri_ref, rank_ref, counts_ref, base_ref):
    @pl.when(jnp.logical_and(pl.program_id(0) == 0, pl.program_id(1) == 0))
    def _():
        base_ref[...] = jnp.zeros(base_ref.shape, F32)

    n_exp = base_ref.shape[0]
    t = idxs[0].shape[1]
    rows = lax.broadcasted_iota(jnp.int32, (n_exp, t), 0)
    running = base_ref[:, 0:1]
    for k in range(TOP_K):
        onehot = rows == idxs[k]
        prefix = _dot(jnp.where(onehot, 1.0, 0.0).astype(BF16), tri_ref[...])
        rank = jnp.sum(jnp.where(onehot, prefix + running, 0.0), axis=0, keepdims=True)
        rank_ref[k:k + 1, :] = rank.astype(jnp.int32)
        running = running + jnp.sum(jnp.where(onehot, 1.0, 0.0), axis=1, keepdims=True)
    base_ref[...] = jnp.broadcast_to(running, base_ref.shape)
    counts_ref[...] = base_ref[...]


def _mix_epilogue(x, cat_ref, w_out_ref, g_ref, b_ref, rwt_ref, rb_ref, tri_ref, dn_alpha, h_out_ref, hp_ref, idx_ref,
                  rank_ref, counts_ref, gcol_ref, base_ref):
    mix = _dot(cat_ref[...], w_out_ref[...])
    h1 = _layer_norm(dn_alpha * x + mix, g_ref[...], b_ref[...])
    h_hi = h1.astype(BF16)
    h_lo = (h1 - h_hi.astype(F32)).astype(BF16)
    rw = rwt_ref[...]
    n_exp = rw.shape[0]
    rw_hi = rw.astype(BF16)
    rw_lo = (rw - rw_hi.astype(F32)).astype(BF16)
    both = _dot_nt(jnp.concatenate([rw_hi, rw_lo], axis=0), h_hi)
    logits_all = both[:n_exp] + both[n_exp:] + _dot_nt(rw_hi, h_lo) + rb_ref[...]
    h_out_ref[...] = h1
    hp_ref[...] = _pack_bf16_pairs(h1)
    t = logits_all.shape[1]
    half = t // 2
    idxs_halves, gates_halves = [], []
    for part in range(2):
        logits = logits_all[:, part * half:(part + 1) * half]
        rows = lax.broadcasted_iota(jnp.int32, (n_exp, half), 0)
        vals, idxs = [], []
        for _ in range(TOP_K):
            m = jnp.max(logits, axis=0, keepdims=True)
            ix = jnp.min(jnp.where(logits == m, rows, n_exp), axis=0, keepdims=True)
            vals.append(m)
            idxs.append(ix)
            logits = jnp.where(rows == ix, NEG_INF, logits)
        exps = [jnp.exp(v - vals[0]) for v in vals]
        denom = exps[0] + exps[1] + exps[2] + exps[3]
        idxs_halves.append(idxs)
        gates_halves.append([e / denom for e in exps])
    idxs = [jnp.concatenate([idxs_halves[0][k], idxs_halves[1][k]], axis=1) for k in range(TOP_K)]
    gates = [jnp.concatenate([gates_halves[0][k], gates_halves[1][k]], axis=1) for k in range(TOP_K)]
    for k in range(TOP_K):
        idx_ref[k:k + 1, :] = idxs[k]
    _expert_ranks(idxs, tri_ref, rank_ref, counts_ref, base_ref)
    lane_rows = lax.broadcasted_iota(jnp.int32, (V7X_LANES, t), 0)
    gates_t = jnp.zeros((V7X_LANES, t), F32)
    for k in range(TOP_K):
        gates_t = jnp.where(lane_rows == k, gates[k], gates_t)
    gcol_ref[...] = gates_t.T


def _pool_layer_kernel(dn_alpha, h_ref, mem_ref, w_in_ref, wbd_ref, pscale_ref, mem_kv_ref, w_out_ref, g_ref, b_ref,
                       rwt_ref, rb_ref, tri_ref, h_out_ref, hp_ref, idx_ref, rank_ref, counts_ref, gcol_ref,
                       ue_ref, w2_ref, w4_ref, w8_ref, kv_ref, cat_ref, base_ref):
    s = pl.program_id(1)
    ts = h_ref.shape[0]
    pool_width = wbd_ref.shape[0]
    hist = POOL_HISTORY

    @pl.when(s == 0)
    def _():
        ue_ref[0:hist, :] = jnp.zeros((hist, pool_width), F32)
        kv_ref[...] = _dot(mem_ref[0].astype(BF16), mem_kv_ref[...]).astype(BF16)

    x = h_ref[...]
    proj = _dot(x.astype(BF16), w_in_ref[...])
    u = proj[:, :pool_width]
    ue_ref[hist:, :] = u
    n = ts + hist
    w2_ref[8:n, :] = ue_ref[8:n, :] + ue_ref[7:n - 1, :]
    w4_ref[16:n, :] = w2_ref[16:n, :] + w2_ref[14:n - 2, :]
    w8_ref[24:n, :] = w4_ref[24:n, :] + w4_ref[20:n - 4, :]
    w16 = w8_ref[hist:n, :] + w8_ref[hist - 8:n - 8, :]
    group_dim = pool_width // len(POOL_WINDOWS)
    col = _lane_iota((ts, pool_width))
    wsum = jnp.where(col < group_dim, w2_ref[hist:n, :],
                     jnp.where(col < 2 * group_dim, w4_ref[hist:n, :],
                               jnp.where(col < 3 * group_dim, w8_ref[hist:n, :], w16)))
    window = jnp.where(col < group_dim, POOL_WINDOWS[0],
                       jnp.where(col < 2 * group_dim, POOL_WINDOWS[1],
                                 jnp.where(col < 3 * group_dim, POOL_WINDOWS[2], POOL_WINDOWS[3])))
    tpos = s * ts + lax.broadcasted_iota(jnp.int32, (ts, pool_width), 0) + 1
    cnt = jnp.minimum(tpos, window).astype(F32)
    d = wsum / cnt - u
    ue_ref[0:hist, :] = u[ts - hist:, :]
    pooled = _dot(d.astype(BF16), wbd_ref[...]) * pscale_ref[...]
    cat_ref[:, :pool_width] = pooled.astype(BF16)
    _memory_attention(proj, kv_ref, cat_ref, pool_width, pool_width)
    _mix_epilogue(x, cat_ref, w_out_ref, g_ref, b_ref, rwt_ref, rb_ref, tri_ref, dn_alpha, h_out_ref, hp_ref, idx_ref,
                  rank_ref, counts_ref, gcol_ref, base_ref)


def _bias_pieces(c):
    hi = c.astype(BF16).astype(F32)
    r = c - hi
    mid = r.astype(BF16).astype(F32)
    lo = (r - mid).astype(BF16).astype(F32)
    return hi, mid, lo


def _bias_lane_pieces(c3, n_heads):
    hi, mid, lo = _bias_pieces(c3)
    lane = _lane_iota(c3.shape)
    out = jnp.where(lane < n_heads, hi, jnp.where(lane < 2 * n_heads, mid, jnp.where(lane < 3 * n_heads, lo,
                    jnp.where(lane == 3 * n_heads, 1.0, 0.0))))
    return out.astype(BF16)


def _key_bias_selector(n_heads):
    sel = np.zeros((V7X_LANES, n_heads * V7X_LANES), np.float32)
    for h in range(n_heads):
        spare = h * V7X_LANES + HEAD_DIM
        for p in range(3):
            sel[3 * n_heads, spare + p] = 1.0
            sel[p * n_heads + h, spare + 3 + p] = -1.0
    return jnp.asarray(sel, BF16)


def _shared_kv_block(h, kvw_ref, fb_ref, tri_ref, sel_ref, k_ref, vt_ref, cq_ref, carry_ref):
    s = pl.program_id(1)
    v_width = vt_ref.shape[1]
    n_heads = k_ref.shape[1] // V7X_LANES

    @pl.when(s == 0)
    def _():
        carry_ref[...] = jnp.zeros(carry_ref.shape, F32)

    proj = _dot(h.astype(BF16), kvw_ref[...])
    z = proj[:, 2 * v_width:] + fb_ref[...]
    logf = jnp.minimum(z, 0.0) - jnp.log(1.0 + jnp.exp(-jnp.abs(z)))
    hi, mid, lo = _bias_pieces(logf)
    tri = tri_ref[...]
    c = _dot(tri, hi.astype(BF16)) + _dot(tri, mid.astype(BF16)) + _dot(tri, lo.astype(BF16)) + carry_ref[0:1, :]
    ts = c.shape[0]
    carry_ref[0:1, :] = c[ts - 1:ts, :]
    pieces = _bias_lane_pieces(c * LOG2_E, n_heads)
    cq_ref[...] = pieces
    bias_lanes = _dot(pieces, sel_ref[...])
    lane = _lane_iota((ts, V7X_LANES))
    for hd in range(n_heads):
        pair = proj[:, (hd // 2) * V7X_LANES:(hd // 2 + 1) * V7X_LANES]
        k_h = pair if hd % 2 == 0 else pltpu.roll(pair, HEAD_DIM, axis=1)
        cols = slice(hd * V7X_LANES, (hd + 1) * V7X_LANES)
        k_ref[:, cols] = jnp.where(lane < HEAD_DIM, k_h, bias_lanes[:, cols]).astype(BF16)
    vt_ref[0] = proj[:, v_width:2 * v_width].T.astype(BF16)


def _fox_layer_kernel(dn_alpha, h_ref, mem_ref, k_ref, vt_ref, cq_ref, w_in_ref, mem_kv_ref, w_out_ref,
                      g_ref, b_ref, rwt_ref, rb_ref, tri_ref, h_out_ref, hp_ref, idx_ref, rank_ref, counts_ref, gcol_ref,
                      kv_ref, cat_ref, qa_ref, m_ref, l_ref, acc_ref, base_ref):
    i = pl.program_id(1)
    tq = h_ref.shape[0]
    n_heads = k_ref.shape[2] // V7X_LANES
    fox_width = vt_ref.shape[1]

    @pl.when(i == 0)
    def _():
        kv_ref[...] = _dot(mem_ref[0].astype(BF16), mem_kv_ref[...]).astype(BF16)

    x = h_ref[...]
    proj = _dot(x.astype(BF16), w_in_ref[...])
    cq = cq_ref[...].astype(F32)
    lane = _lane_iota((tq, V7X_LANES))
    for h in range(n_heads):
        pair = proj[:, (h // 2) * V7X_LANES:(h // 2 + 1) * V7X_LANES] * (ATTN_SCALE * LOG2_E)
        q_h = pair if h % 2 == 0 else pltpu.roll(pair, HEAD_DIM, axis=1)
        piece = [cq[:, p * n_heads + h:p * n_heads + h + 1] for p in range(3)]
        spare = jnp.where(lane == HEAD_DIM, piece[0], jnp.where(lane == HEAD_DIM + 1, piece[1],
                          jnp.where(lane == HEAD_DIM + 2, piece[2], jnp.where(lane < HEAD_DIM + 6, 1.0, 0.0))))
        qa_ref[h] = jnp.where(lane < HEAD_DIM, q_h, spare).astype(BF16)
    m_ref[...] = jnp.full(m_ref.shape, NEG_INF, F32)
    l_ref[...] = jnp.zeros(l_ref.shape, F32)
    acc_ref[...] = jnp.zeros(acc_ref.shape, F32)
    causal_t = lax.broadcasted_iota(jnp.int32, (tq, tq), 0) <= lax.broadcasted_iota(jnp.int32, (tq, tq), 1)

    ones_rows = (lax.broadcasted_iota(jnp.int32, (V7X_BF16_SUBLANES, tq), 0) == 0).astype(BF16)

    def tile(j, masked):
        start = pl.multiple_of(j * tq, tq)

        def scores(h):
            kb = k_ref[0, pl.ds(start, tq), h * V7X_LANES:(h + 1) * V7X_LANES]
            return _dot_nt(kb, qa_ref[h])

        st_next = scores(0)
        for h in range(n_heads):
            st = st_next
            if h + 1 < n_heads:
                st_next = scores(h + 1)
            if masked:
                st = jnp.where(causal_t, st, NEG_INF)
            m_old = m_ref[h:h + 1, :]
            m_new = jnp.maximum(m_old, jnp.max(st, axis=0, keepdims=True))
            alpha = jnp.exp2(m_old - m_new)
            pt = jnp.exp2(st - m_new).astype(BF16)
            m_ref[h:h + 1, :] = m_new
            rows = slice(h * HEAD_DIM, (h + 1) * HEAD_DIM)
            vb = jnp.concatenate([vt_ref[0, rows, pl.ds(start, tq)], ones_rows], axis=0)
            pv = _dot(vb, pt)
            acc_ref[rows, :] = alpha * acc_ref[rows, :] + pv[:HEAD_DIM]
            l_ref[h:h + 1, :] = alpha * l_ref[h:h + 1, :] + pv[HEAD_DIM:HEAD_DIM + 1]

    tile(i, True)

    def off_diagonal(j, carry):
        tile(j, False)
        return carry

    lax.fori_loop(0, i, off_diagonal, 0)
    for h in range(n_heads):
        rows = slice(h * HEAD_DIM, (h + 1) * HEAD_DIM)
        acc_ref[rows, :] = acc_ref[rows, :] / l_ref[h:h + 1, :]
    cat_ref[:, :fox_width] = acc_ref[...].T.astype(BF16)
    _memory_attention(proj, kv_ref, cat_ref, fox_width, fox_width)
    _mix_epilogue(x, cat_ref, w_out_ref, g_ref, b_ref, rwt_ref, rb_ref, tri_ref, dn_alpha, h_out_ref, hp_ref, idx_ref,
                  rank_ref, counts_ref, gcol_ref, base_ref)


def _expert_kernel(layer, be_ref, bv_ref, bslot_ref, bnext_ref, nact_ref, x_ref, w1_hbm, b1_ref, w2_hbm, b2_ref, y_ref,
                   w1s_ref, w2s_ref, act_ref, w1f_ref, w2f_ref, sem_ref):
    i = pl.program_id(0)
    active = i < nact_ref[0]
    two_f = w1f_ref.shape[2]
    group = 2 * V7X_LANES
    n_groups = two_f // group

    def weight_copies(expert, slot):
        return (pltpu.make_async_copy(w1_hbm.at[layer, expert], w1f_ref.at[slot], sem_ref.at[0, slot]),
                pltpu.make_async_copy(w2_hbm.at[layer, expert], w2f_ref.at[slot], sem_ref.at[1, slot]))

    @pl.when(i == 0)
    def _():
        for cp in weight_copies(be_ref[0], bslot_ref[0]):
            cp.start()

    @pl.when(jnp.logical_and(active, jnp.logical_or(i == 0, be_ref[i] != be_ref[jnp.maximum(i - 1, 0)])))
    def _():
        slot = bslot_ref[i]
        for cp in weight_copies(be_ref[i], slot):
            cp.wait()

        @pl.when(bnext_ref[i] >= 0)
        def _():
            for cp in weight_copies(bnext_ref[i], 1 - slot):
                cp.start()

        src = lax.broadcasted_iota(jnp.int32, (group, group), 0)
        dst = lax.broadcasted_iota(jnp.int32, (group, group), 1)
        want = jnp.where(dst < V7X_LANES, 2 * dst, 2 * (dst - V7X_LANES) + 1)
        perm = (src == want).astype(BF16)
        for g in range(n_groups):
            cols = slice(g * group, (g + 1) * group)
            w1s_ref[:, cols] = _dot(w1f_ref[slot, :, cols].astype(BF16), perm).astype(BF16)
        w2s_ref[...] = w2f_ref[slot].astype(BF16)

    n_valid = bv_ref[i]
    half_rows = x_ref.shape[0] // 2

    def run_rows(rows):
        row_id = lax.broadcasted_iota(jnp.int32, (rows, x_ref.shape[1]), 0)
        x_low, x_high = _unpack_bf16_pairs(jnp.where(row_id < n_valid, x_ref[0:rows, :], jnp.uint32(0)))
        xb = jnp.concatenate([x_low.astype(BF16), x_high.astype(BF16)], axis=1)
        slab_groups = 2
        for s0 in range(0, n_groups, slab_groups):
            cols = slice(s0 * group, (s0 + slab_groups) * group)
            hdn = _dot(xb, w1s_ref[:, cols]) + b1_ref[0, :, cols]
            for g in range(slab_groups):
                x_glu = jnp.minimum(hdn[:, g * group:g * group + V7X_LANES], SWIGLU_LIMIT)
                x_lin = jnp.clip(hdn[:, g * group + V7X_LANES:(g + 1) * group], -SWIGLU_LIMIT, SWIGLU_LIMIT)
                act = x_glu * jax.nn.sigmoid(SWIGLU_ALPHA * x_glu) * (x_lin + 1.0)
                act_ref[0:rows, (s0 + g) * V7X_LANES:(s0 + g + 1) * V7X_LANES] = act.astype(BF16)
        y_ref[0:rows, :] = _pack_bf16_pairs(_dot(act_ref[0:rows, :], w2s_ref[...]) + b2_ref[0])

    @pl.when(jnp.logical_and(active, n_valid > half_rows))
    def _():
        run_rows(x_ref.shape[0])

    @pl.when(jnp.logical_and(active, n_valid <= half_rows))
    def _():
        run_rows(half_rows)
        y_ref[half_rows:, :] = jnp.zeros((half_rows, y_ref.shape[1]), jnp.uint32)

    @pl.when(jnp.logical_not(active))
    def _():
        y_ref[...] = jnp.zeros(y_ref.shape, jnp.uint32)


def _combine_rows(dn_alpha, h_ref, y_ref, gcol_ref, g_ref, b_ref):
    gates = gcol_ref[...]
    low, high = _unpack_bf16_pairs(y_ref[0])
    low, high = gates[:, 0:1] * low, gates[:, 0:1] * high
    for k in range(1, TOP_K):
        low_k, high_k = _unpack_bf16_pairs(y_ref[k])
        low = low + gates[:, k:k + 1] * low_k
        high = high + gates[:, k:k + 1] * high_k
    ffn = jnp.concatenate([low, high], axis=1)
    return _layer_norm(dn_alpha * h_ref[...] + ffn, g_ref[...], b_ref[...])


def _combine_kernel(dn_alpha, h_ref, y_ref, gcol_ref, g_ref, b_ref, o_ref):
    o_ref[...] = _combine_rows(dn_alpha, h_ref, y_ref, gcol_ref, g_ref, b_ref)


def _combine_kv_kernel(dn_alpha, h_ref, y_ref, gcol_ref, g_ref, b_ref, kvw_ref, fb_ref, tri_ref, sel_ref,
                       o_ref, k_ref, vt_ref, cq_ref, carry_ref):
    h2 = _combine_rows(dn_alpha, h_ref, y_ref, gcol_ref, g_ref, b_ref)
    o_ref[...] = h2
    _shared_kv_block(h2, kvw_ref, fb_ref, tri_ref, sel_ref, k_ref, vt_ref, cq_ref, carry_ref)


def _sc_gather_rows(table, idx):
    n, = idx.shape
    _, d = table.shape
    info = plsc.get_sparse_core_info()
    n_workers = info.num_cores * info.num_subcores
    per_worker = n // n_workers
    n_chunks = per_worker // SC_CHUNK
    assert n % (n_workers * SC_CHUNK) == 0 and n_chunks % 2 == 0
    mesh = plsc.VectorSubcoreMesh(core_axis_name="core", subcore_axis_name="subcore")

    @functools.partial(
        pl.kernel,
        out_type=jax.ShapeDtypeStruct((n, d), table.dtype),
        mesh=mesh,
        scratch_types=[
            pltpu.VMEM((per_worker,), jnp.int32),
            pltpu.VMEM((2, SC_CHUNK, d), table.dtype),
            pltpu.SemaphoreType.DMA((2,)),
            pltpu.SemaphoreType.DMA((2,)),
        ],
    )
    def gather_kernel(table_hbm, idx_hbm, out_hbm, idx_v, rows_v, gather_sem, write_sem):
        worker = lax.axis_index("subcore") * info.num_cores + lax.axis_index("core")
        base = pl.multiple_of(worker * per_worker, SC_CHUNK)
        pltpu.sync_copy(idx_hbm.at[pl.ds(base, per_worker)], idx_v)

        def gather(c, buf):
            rows = idx_v.at[pl.ds(pl.multiple_of(c * SC_CHUNK, SC_CHUNK), SC_CHUNK)]
            return pltpu.make_async_copy(table_hbm.at[rows], rows_v.at[buf], gather_sem.at[buf])

        def write(c, buf):
            off = pl.multiple_of(base + c * SC_CHUNK, SC_CHUNK)
            return pltpu.make_async_copy(rows_v.at[buf], out_hbm.at[pl.ds(off, SC_CHUNK)], write_sem.at[buf])

        gather(0, 0).start()

        @pl.loop(0, n_chunks, step=2)
        def _(c0):
            for buf in range(2):
                c = c0 + buf
                gather(c, buf).wait()

                @pl.when(c >= 1)
                def _():
                    write(c - 1, 1 - buf).wait()

                @pl.when(c + 1 < n_chunks)
                def _():
                    gather(c + 1, 1 - buf).start()

                write(c, buf).start()

        write(n_chunks - 1, 1).wait()

    return gather_kernel(table, idx)


def _sc_scatter_rows(rows, pos, n_out):
    t, d = rows.shape
    top_k = pos.shape[0]
    pos_flat = pos.reshape(-1)
    info = plsc.get_sparse_core_info()
    n_workers = info.num_cores * info.num_subcores
    per_worker = t // n_workers
    n_chunks = per_worker // SC_CHUNK
    assert t % (n_workers * SC_CHUNK) == 0 and n_chunks % 2 == 0
    mesh = plsc.VectorSubcoreMesh(core_axis_name="core", subcore_axis_name="subcore")

    @functools.partial(
        pl.kernel,
        out_type=jax.ShapeDtypeStruct((n_out, d), rows.dtype),
        mesh=mesh,
        scratch_types=[pltpu.VMEM((2, SC_CHUNK, d), rows.dtype)]
        + [pltpu.VMEM((SC_CHUNK,), jnp.int32) for _ in range(2 * top_k)]
        + [pltpu.SemaphoreType.DMA((2,)), pltpu.SemaphoreType.DMA((2,))],
    )
    def scatter_kernel(rows_hbm, pos_hbm, out_hbm, rows_v, *rest):
        idx_bufs = rest[:2 * top_k]
        load_sem, write_sem = rest[2 * top_k:]
        worker = lax.axis_index("subcore") * info.num_cores + lax.axis_index("core")
        base = pl.multiple_of(worker * per_worker, SC_CHUNK)

        def loads(c, buf):
            off = pl.multiple_of(base + c * SC_CHUNK, SC_CHUNK)
            copies = [pltpu.make_async_copy(rows_hbm.at[pl.ds(off, SC_CHUNK)], rows_v.at[buf], load_sem.at[buf])]
            for k in range(top_k):
                src = pos_hbm.at[pl.ds(pl.multiple_of(k * t + off, SC_CHUNK), SC_CHUNK)]
                copies.append(pltpu.make_async_copy(src, idx_bufs[buf * top_k + k], load_sem.at[buf]))
            return copies

        def scatters(buf):
            return [pltpu.make_async_copy(rows_v.at[buf], out_hbm.at[idx_bufs[buf * top_k + k]], write_sem.at[buf])
                    for k in range(top_k)]

        for cp in loads(0, 0):
            cp.start()

        @pl.loop(0, n_chunks, step=2)
        def _(c0):
            for buf in range(2):
                c = c0 + buf
                for cp in loads(c, buf):
                    cp.wait()

                @pl.when(c >= 1)
                def _():
                    for cp in scatters(1 - buf):
                        cp.wait()

                @pl.when(c + 1 < n_chunks)
                def _():
                    for cp in loads(c + 1, 1 - buf):
                        cp.start()

                for cp in scatters(buf):
                    cp.start()

        for cp in scatters(1):
            cp.wait()

    return scatter_kernel(rows, pos_flat)


def _const_spec(shape):
    zeros = (0,) * len(shape)
    return pl.BlockSpec(shape, lambda *_: zeros, pipeline_mode=pl.Buffered(1))


def _tc_params(n_axes):
    return pltpu.CompilerParams(dimension_semantics=("arbitrary",) * n_axes,
                                vmem_limit_bytes=V7X_VMEM_LIMIT_BYTES)


def _mix_out_shapes(t, d, n_exp):
    return (jax.ShapeDtypeStruct((t, d), F32),
            jax.ShapeDtypeStruct((t, d // 2), jnp.uint32),
            jax.ShapeDtypeStruct((TOP_K, t), jnp.int32),
            jax.ShapeDtypeStruct((TOP_K, t), jnp.int32),
            jax.ShapeDtypeStruct((n_exp, V7X_LANES), F32),
            jax.ShapeDtypeStruct((t, V7X_LANES), F32))


def _mix_out_specs(rows, d, n_exp, row, tok_col):
    return (pl.BlockSpec((rows, d), row),
            pl.BlockSpec((rows, d // 2), row),
            pl.BlockSpec((TOP_K, rows), tok_col),
            pl.BlockSpec((TOP_K, rows), tok_col),
            pl.BlockSpec((n_exp, V7X_LANES), lambda *_: (0, 0)),
            pl.BlockSpec((rows, V7X_LANES), row))


def _strict_upper_tri(n):
    return (lax.broadcasted_iota(jnp.int32, (n, n), 0) < lax.broadcasted_iota(jnp.int32, (n, n), 1)).astype(BF16)


def _pool_layer(h, mem, w_in, wbd, pscale, mem_kv, w_out, ln_g, ln_b, rwt, rb, dn_alpha, seq):
    t, d = h.shape
    n_mem = mem.shape[1]
    pool_width = wbd.shape[0]
    ts = SEQ_BLOCK
    nsb = seq // ts
    row = lambda b, s: (b * nsb + s, 0)
    n_exp = rwt.shape[0]
    tri = _strict_upper_tri(ts)
    return pl.pallas_call(
        functools.partial(_pool_layer_kernel, dn_alpha),
        out_shape=_mix_out_shapes(t, d, n_exp),
        grid=(t // seq, nsb),
        in_specs=[
            pl.BlockSpec((ts, d), row),
            pl.BlockSpec((1, n_mem, d), lambda b, s: (b, 0, 0)),
            _const_spec(w_in.shape), _const_spec(wbd.shape), _const_spec(pscale.shape), _const_spec(mem_kv.shape),
            _const_spec(w_out.shape), _const_spec(ln_g.shape), _const_spec(ln_b.shape), _const_spec(rwt.shape),
            _const_spec(rb.shape), _const_spec(tri.shape),
        ],
        out_specs=_mix_out_specs(ts, d, n_exp, row, lambda b, s: (0, b * nsb + s)),
        scratch_shapes=[pltpu.VMEM((ts + POOL_HISTORY, pool_width), F32)] * 4
        + [pltpu.VMEM((n_mem, 2 * MEM_WIDTH), BF16), pltpu.VMEM((ts, d), BF16), pltpu.VMEM((n_exp, V7X_LANES), F32)],
        compiler_params=_tc_params(2),
        name="pool_layer",
    )(h, mem, w_in, wbd, pscale, mem_kv, w_out, ln_g, ln_b, rwt, rb, tri)


def _fox_layer(h, mem, k_aug, v_t, cq, w_in, mem_kv, w_out, ln_g, ln_b, rwt, rb, dn_alpha, seq):
    t, d = h.shape
    n_mem = mem.shape[1]
    k_width = k_aug.shape[1]
    v_width = v_t.shape[1]
    n_heads = k_width // V7X_LANES
    stat_rows = -(-n_heads // V7X_SUBLANES) * V7X_SUBLANES
    tq = Q_BLOCK
    nqb = seq // tq
    n_batch = t // seq
    row = lambda b, i: (b * nqb + i, 0)
    per_batch = lambda b, i: (b, 0, 0)
    n_exp = rwt.shape[0]
    tri = _strict_upper_tri(tq)
    return pl.pallas_call(
        functools.partial(_fox_layer_kernel, dn_alpha),
        out_shape=_mix_out_shapes(t, d, n_exp),
        grid=(n_batch, nqb),
        in_specs=[
            pl.BlockSpec((tq, d), row),
            pl.BlockSpec((1, n_mem, d), per_batch),
            pl.BlockSpec((1, seq, k_width), per_batch),
            pl.BlockSpec((1, v_width, seq), per_batch),
            pl.BlockSpec((tq, V7X_LANES), row),
            _const_spec(w_in.shape), _const_spec(mem_kv.shape), _const_spec(w_out.shape), _const_spec(ln_g.shape),
            _const_spec(ln_b.shape), _const_spec(rwt.shape), _const_spec(rb.shape), _const_spec(tri.shape),
        ],
        out_specs=_mix_out_specs(tq, d, n_exp, row, lambda b, i: (0, b * nqb + i)),
        scratch_shapes=[pltpu.VMEM((n_mem, 2 * MEM_WIDTH), BF16), pltpu.VMEM((tq, d), BF16),
                        pltpu.VMEM((n_heads, tq, V7X_LANES), BF16),
                        pltpu.VMEM((stat_rows, tq), F32), pltpu.VMEM((stat_rows, tq), F32),
                        pltpu.VMEM((v_width, tq), F32),
                        pltpu.VMEM((n_exp, V7X_LANES), F32)],
        compiler_params=_tc_params(2),
        name="fox_layer",
    )(h, mem, k_aug.reshape(n_batch, seq, k_width), v_t, cq, w_in, mem_kv, w_out, ln_g, ln_b, rwt, rb, tri)


def _slot_kernel(pad_start_ref, idx_ref, rank_ref, pos_ref):
    pos = rank_ref[...]
    idx = idx_ref[...]
    for e in range(pad_start_ref.shape[0]):
        pos = pos + jnp.where(idx == e, pad_start_ref[e], 0)
    pos_ref[...] = pos


def _route(idx_t, rank_t, counts, n_experts):
    top_k, t = idx_t.shape
    n_assign = top_k * t
    padded = ((counts + MOE_BLOCK - 1) // MOE_BLOCK) * MOE_BLOCK
    pad_end = jnp.cumsum(padded)
    pad_start = pad_end - padded
    n_blocks = n_assign // MOE_BLOCK + n_experts
    block_start = jnp.arange(n_blocks, dtype=jnp.int32) * MOE_BLOCK
    block_expert = jnp.minimum(jnp.sum((pad_end[None, :] <= block_start[:, None]).astype(jnp.int32), axis=1),
                               n_experts - 1)
    block_valid = jnp.clip(counts[block_expert] - (block_start - pad_start[block_expert]), 0, MOE_BLOCK)
    n_active = (pad_end[-1:] // MOE_BLOCK).astype(jnp.int32)
    expert_ids = jnp.arange(n_experts, dtype=jnp.int32)
    in_use = counts > 0
    expert_slot = (jnp.cumsum(in_use.astype(jnp.int32)) - in_use.astype(jnp.int32)) % 2
    later = lax.cummin(jnp.where(in_use, expert_ids, n_experts), axis=0, reverse=True)
    next_in_use = jnp.concatenate([later[1:], jnp.full((1,), n_experts, jnp.int32)])
    next_in_use = jnp.where(next_in_use < n_experts, next_in_use, -1)
    block_slot = expert_slot[block_expert].astype(jnp.int32)
    block_next = next_in_use[block_expert].astype(jnp.int32)
    cols = min(SLOT_COLS, t)
    pos = pl.pallas_call(
        _slot_kernel,
        out_shape=jax.ShapeDtypeStruct((top_k, t), jnp.int32),
        grid_spec=pltpu.PrefetchScalarGridSpec(
            num_scalar_prefetch=1,
            grid=(t // cols,),
            in_specs=[pl.BlockSpec((top_k, cols), lambda i, ps: (0, i))] * 2,
            out_specs=pl.BlockSpec((top_k, cols), lambda i, ps: (0, i)),
        ),
        compiler_params=_tc_params(1),
        name="slots",
    )(pad_start.astype(jnp.int32), idx_t, rank_t)
    return pos, (block_expert, block_valid.astype(jnp.int32), block_slot, block_next, n_active)


def _experts(xs, block_expert, block_valid, block_slot, block_next, n_active, w1_all, b1, w2_all, b2, layer):
    n_slots, half_d = xs.shape
    _, n_exp, d, two_f = w1_all.shape
    d_ff = two_f // 2
    assert d == 2 * half_d
    block = lambda i, *_: (i, 0)
    per_expert = lambda i, be, *_: (be[i], 0, 0)
    grid_spec = pltpu.PrefetchScalarGridSpec(
        num_scalar_prefetch=5,
        grid=(n_slots // MOE_BLOCK,),
        in_specs=[
            pl.BlockSpec((MOE_BLOCK, half_d), block),
            pl.BlockSpec(memory_space=pl.ANY),
            pl.BlockSpec((1, 1, two_f), per_expert),
            pl.BlockSpec(memory_space=pl.ANY),
            pl.BlockSpec((1, 1, d), per_expert),
        ],
        out_specs=pl.BlockSpec((MOE_BLOCK, half_d), block),
        scratch_shapes=[pltpu.VMEM((d, two_f), BF16), pltpu.VMEM((d_ff, d), BF16), pltpu.VMEM((MOE_BLOCK, d_ff), BF16),
                        pltpu.VMEM((2, d, two_f), F32), pltpu.VMEM((2, d_ff, d), F32),
                        pltpu.SemaphoreType.DMA((2, 2))],
    )
    return pl.pallas_call(
        functools.partial(_expert_kernel, layer),
        out_shape=jax.ShapeDtypeStruct((n_slots, half_d), jnp.uint32),
        grid_spec=grid_spec,
        compiler_params=_tc_params(1),
        name="experts",
    )(block_expert, block_valid, block_slot, block_next, n_active, xs, w1_all, b1.reshape(n_exp, 1, two_f), w2_all,
      b2.reshape(n_exp, 1, d))


def _combine(h, yg, gcol, ln_g, ln_b, dn_alpha, seq, shared_kv_weights=None):
    t, d = h.shape
    ts = SEQ_BLOCK
    nsb = seq // ts
    row = lambda b, s: (b * nsb + s, 0)
    in_specs = [pl.BlockSpec((ts, d), row), pl.BlockSpec((TOP_K, ts, d // 2), lambda b, s: (0, b * nsb + s, 0)),
                pl.BlockSpec((ts, V7X_LANES), row), _const_spec(ln_g.shape), _const_spec(ln_b.shape)]
    args = (h, yg.reshape(TOP_K, t, d // 2), gcol, ln_g, ln_b)
    if shared_kv_weights is None:
        return pl.pallas_call(
            functools.partial(_combine_kernel, dn_alpha),
            out_shape=jax.ShapeDtypeStruct((t, d), F32),
            grid=(t // seq, nsb),
            in_specs=in_specs,
            out_specs=pl.BlockSpec((ts, d), row),
            compiler_params=_tc_params(2),
            name="combine",
        )(*args), None
    kvw, fb, sel_k, n_heads = shared_kv_weights
    k_width = n_heads * V7X_LANES
    v_width = n_heads * HEAD_DIM
    tri = (lax.broadcasted_iota(jnp.int32, (ts, ts), 1) <= lax.broadcasted_iota(jnp.int32, (ts, ts), 0)).astype(BF16)
    out = pl.pallas_call(
        functools.partial(_combine_kv_kernel, dn_alpha),
        out_shape=(jax.ShapeDtypeStruct((t, d), F32),
                   jax.ShapeDtypeStruct((t, k_width), BF16),
                   jax.ShapeDtypeStruct((t // seq, v_width, seq), BF16),
                   jax.ShapeDtypeStruct((t, V7X_LANES), BF16)),
        grid=(t // seq, nsb),
        in_specs=in_specs + [_const_spec(kvw.shape), _const_spec(fb.shape), _const_spec(tri.shape),
                             _const_spec(sel_k.shape)],
        out_specs=(pl.BlockSpec((ts, d), row),
                   pl.BlockSpec((ts, k_width), row),
                   pl.BlockSpec((1, v_width, ts), lambda b, s: (b, 0, s)),
                   pl.BlockSpec((ts, V7X_LANES), row)),
        scratch_shapes=[pltpu.VMEM((V7X_SUBLANES, V7X_LANES), F32)],
        compiler_params=_tc_params(2),
        name="combine_shared_kv",
    )(*args, kvw, fb, tri, sel_k)
    return out[0], out[1:]


def _moe(mixed, w1_all, b1, w2_all, b2, layer, ln_g, ln_b, dn_alpha, seq, shared_kv_weights):
    h1, h1_packed, idx_t, rank_t, counts, gcol = mixed
    n_exp = w1_all.shape[1]
    d_ff = w2_all.shape[2]
    pos, block_meta = _route(idx_t, rank_t, counts[:, 0].astype(jnp.int32), n_exp)
    b1p = jnp.swapaxes(b1.reshape(n_exp, d_ff // V7X_LANES, V7X_LANES, 2), 2, 3).reshape(n_exp, 2 * d_ff)
    xs = _sc_scatter_rows(h1_packed, pos, block_meta[0].shape[0] * MOE_BLOCK)
    y = _experts(xs, *block_meta, w1_all, b1p, w2_all, b2, layer)
    yg = _sc_gather_rows(y, pos.reshape(-1))
    return _combine(h1, yg, gcol, ln_g, ln_b, dn_alpha, seq, shared_kv_weights)


def kernel(x, mem, ln1_g, ln1_b, ln2_g, ln2_b, a_w_in, pool_w, pool_scale, a_mem_kv, a_w_out, kv_w, fgate_b, b_w_in, b_mem_kv, b_w_out, router_w, router_b, moe_w1, moe_b1, moe_w2, moe_b2):
    n_batch, seq, d = x.shape
    t = n_batch * seq
    depth = ln1_g.shape[0]
    n_a = a_w_in.shape[0]
    dn_alpha = float((2 * depth) ** 0.25)
    n_exp = router_w.shape[2]
    fox_heads = fgate_b.shape[0]
    fox_width = fox_heads * HEAD_DIM
    assert seq % SEQ_BLOCK == 0 and seq % Q_BLOCK == 0
    assert 3 * fox_heads < V7X_LANES

    assert 1 <= n_a < depth
    pad = V7X_LANES - 3 * fox_heads
    kvw = jnp.concatenate([kv_w[:, :2 * fox_width],
                           jnp.pad(jnp.tile(kv_w[:, 2 * fox_width:], (1, 3)), ((0, 0), (0, pad)))],
                          axis=1).astype(BF16)
    fb = jnp.pad(jnp.tile(fgate_b, 3), (0, pad)).reshape(1, V7X_LANES)
    shared_kv_weights = (kvw, fb, _key_bias_selector(fox_heads), fox_heads)

    h = x.reshape(t, d)
    row = lambda a: a.reshape(1, -1)
    shared = None
    for l in range(depth):
        rwt = router_w[l].T
        rb = router_b[l].reshape(n_exp, 1)
        if l < n_a:
            wbd = jax.scipy.linalg.block_diag(*[pool_w[l, g] for g in range(pool_w.shape[1])]).astype(BF16)
            mixed = _pool_layer(
                h, mem, a_w_in[l].astype(BF16), wbd, row(pool_scale[l]), a_mem_kv[l].astype(BF16),
                a_w_out[l].astype(BF16), row(ln1_g[l]), row(ln1_b[l]), rwt, rb, dn_alpha, seq)
        else:
            j = l - n_a
            k_aug, v_t, cq = shared
            mixed = _fox_layer(
                h, mem, k_aug, v_t, cq, b_w_in[j].astype(BF16), b_mem_kv[j].astype(BF16),
                b_w_out[j].astype(BF16), row(ln1_g[l]), row(ln1_b[l]), rwt, rb, dn_alpha, seq)
        h, new_shared = _moe(mixed, moe_w1, moe_b1[l], moe_w2, moe_b2[l], l, row(ln2_g[l]), row(ln2_b[l]), dn_alpha,
                             seq, shared_kv_weights if l == n_a - 1 else None)
        if new_shared is not None:
            shared = new_shared
    return h.reshape(n_batch, seq, d)
```

```python
import functools
import numpy as np
import jax
import jax.numpy as jnp
from jax import lax
from jax.experimental import pallas as pl
from jax.experimental.pallas import tpu as pltpu
from jax.experimental.pallas import tpu_sc as plsc

HEAD_DIM = 64
MEM_HEADS = 4
MEM_WIDTH = MEM_HEADS * HEAD_DIM
POOL_WINDOWS = (2, 4, 8, 16)
TOP_K = 4
SWIGLU_ALPHA = 1.702
SWIGLU_LIMIT = 7.0
MOE_BLOCK = 1024
LN_EPS = 1e-5
ATTN_SCALE = HEAD_DIM ** -0.5
LOG2_E = 1.4426950408889634

V7X_LANES = 128
V7X_SUBLANES = 8
V7X_BF16_SUBLANES = 16
V7X_VMEM_LIMIT_BYTES = 56 * 1024 * 1024
POOL_HISTORY = 32
SC_CHUNK = 64
SEQ_BLOCK = 1024
Q_BLOCK = 512
SLOT_COLS = 8192

F32 = jnp.float32
BF16 = jnp.bfloat16
NEG_INF = float("-inf")


def _dot(a, b):
    return jnp.dot(a, b, preferred_element_type=F32)


def _dot_nt(a, b):
    return lax.dot_general(a, b, (((1,), (1,)), ((), ())), preferred_element_type=F32)


def _lane_iota(shape):
    return lax.broadcasted_iota(jnp.int32, shape, len(shape) - 1)


def _pair_attention(q2, keys, vals):
    lane = _lane_iota(q2.shape)
    first = lane < HEAD_DIM
    outs = []
    for sel in (first, jnp.logical_not(first)):
        qh = jnp.where(sel, q2, 0.0).astype(BF16)
        s = _dot_nt(qh, keys)
        m = jnp.max(s, axis=-1, keepdims=True)
        p = jnp.exp(s - m)
        l = jnp.sum(p, axis=-1, keepdims=True)
        outs.append(_dot(p.astype(BF16), vals) / l)
    return jnp.where(first, outs[0], outs[1])


def _memory_attention(proj, kv_ref, cat_ref, out_off, q_off):
    for p in range(MEM_HEADS // 2):
        lo = p * V7X_LANES
        q2 = proj[:, q_off + lo:q_off + lo + V7X_LANES] * ATTN_SCALE
        keys = kv_ref[:, lo:lo + V7X_LANES]
        vals = kv_ref[:, MEM_WIDTH + lo:MEM_WIDTH + lo + V7X_LANES]
        cat_ref[:, out_off + lo:out_off + lo + V7X_LANES] = _pair_attention(q2, keys, vals).astype(BF16)


def _layer_norm(z, g, b):
    mu = jnp.mean(z, axis=-1, keepdims=True)
    zc = z - mu
    var = jnp.mean(zc * zc, axis=-1, keepdims=True)
    return zc * lax.rsqrt(var + LN_EPS) * g + b


def _pack_bf16_pairs(x):
    w = x.shape[1] // 2
    bits = lax.bitcast_convert_type(x.astype(BF16).astype(F32), jnp.uint32)
    return lax.shift_right_logical(bits[:, :w], jnp.uint32(16)) | (bits[:, w:] & jnp.uint32(0xFFFF0000))


def _unpack_bf16_pairs(p):
    low = lax.bitcast_convert_type(lax.shift_left(p, jnp.uint32(16)), F32)
    high = lax.bitcast_convert_type(p & jnp.uint32(0xFFFF0000), F32)
    return low, high


def _expert_ranks(idxs, tri_ref, rank_ref, counts_ref, base_ref):
    @pl.when(jnp.logical_and(pl.program_id(0) == 0, pl.program_id(1) == 0))
    def _():
        base_ref[...] = jnp.zeros(base_ref.shape, F32)

    n_exp = base_ref.shape[0]
    t = idxs[0].shape[1]
    rows = lax.broadcasted_iota(jnp.int32, (n_exp, t), 0)
    running = base_ref[:, 0:1]
    for k in range(TOP_K):
        onehot = rows == idxs[k]
        prefix = _dot(jnp.where(onehot, 1.0, 0.0).astype(BF16), tri_ref[...])
        rank = jnp.sum(jnp.where(onehot, prefix + running, 0.0), axis=0, keepdims=True)
        rank_ref[k:k + 1, :] = rank.astype(jnp.int32)
        running = running + jnp.sum(jnp.where(onehot, 1.0, 0.0), axis=1, keepdims=True)
    base_ref[...] = jnp.broadcast_to(running, base_ref.shape)
    counts_ref[...] = base_ref[...]


def _mix_epilogue(x, cat_ref, w_out_ref, g_ref, b_ref, rwt_ref, rb_ref, tri_ref, dn_alpha, h_out_ref, hp_ref, idx_ref,
                  rank_ref, counts_ref, gcol_ref, base_ref):
    mix = _dot(cat_ref[...], w_out_ref[...])
    h1 = _layer_norm(dn_alpha * x + mix, g_ref[...], b_ref[...])
    h_hi = h1.astype(BF16)
    h_lo = (h1 - h_hi.astype(F32)).astype(BF16)
    rw = rwt_ref[...]
    n_exp = rw.shape[0]
    rw_hi = rw.astype(BF16)
    rw_lo = (rw - rw_hi.astype(F32)).astype(BF16)
    both = _dot_nt(jnp.concatenate([rw_hi, rw_lo], axis=0), h_hi)
    logits_all = both[:n_exp] + both[n_exp:] + _dot_nt(rw_hi, h_lo) + rb_ref[...]
    h_out_ref[...] = h1
    hp_ref[...] = _pack_bf16_pairs(h1)
    t = logits_all.shape[1]
    half = t // 2
    idxs_halves, gates_halves = [], []
    for part in range(2):
        logits = logits_all[:, part * half:(part + 1) * half]
        rows = lax.broadcasted_iota(jnp.int32, (n_exp, half), 0)
        vals, idxs = [], []
        for _ in range(TOP_K):
            m = jnp.max(logits, axis=0, keepdims=True)
            ix = jnp.min(jnp.where(logits == m, rows, n_exp), axis=0, keepdims=True)
            vals.append(m)
            idxs.append(ix)
            logits = jnp.where(rows == ix, NEG_INF, logits)
        exps = [jnp.exp(v - vals[0]) for v in vals]
        denom = exps[0] + exps[1] + exps[2] + exps[3]
        idxs_halves.append(idxs)
        gates_halves.append([e / denom for e in exps])
    idxs = [jnp.concatenate([idxs_halves[0][k], idxs_halves[1][k]], axis=1) for k in range(TOP_K)]
    gates = [jnp.concatenate([gates_halves[0][k], gates_halves[1][k]], axis=1) for k in range(TOP_K)]
    for k in range(TOP_K):
        idx_ref[k:k + 1, :] = idxs[k]
    _expert_ranks(idxs, tri_ref, rank_ref, counts_ref, base_ref)
    lane_rows = lax.broadcasted_iota(jnp.int32, (V7X_LANES, t), 0)
    gates_t = jnp.zeros((V7X_LANES, t), F32)
    for k in range(TOP_K):
        gates_t = jnp.where(lane_rows == k, gates[k], gates_t)
    gcol_ref[...] = gates_t.T


def _pool_layer_kernel(dn_alpha, h_ref, mem_ref, w_in_ref, wbd_ref, pscale_ref, mem_kv_ref, w_out_ref, g_ref, b_ref,
                       rwt_ref, rb_ref, tri_ref, h_out_ref, hp_ref, idx_ref, rank_ref, counts_ref, gcol_ref,
                       ue_ref, w2_ref, w4_ref, w8_ref, kv_ref, cat_ref, base_ref):
    s = pl.program_id(1)
    ts = h_ref.shape[0]
    pool_width = wbd_ref.shape[0]
    hist = POOL_HISTORY

    @pl.when(s == 0)
    def _():
        ue_ref[0:hist, :] = jnp.zeros((hist, pool_width), F32)
        kv_ref[...] = _dot(mem_ref[0].astype(BF16), mem_kv_ref[...]).astype(BF16)

    x = h_ref[...]
    proj = _dot(x.astype(BF16), w_in_ref[...])
    u = proj[:, :pool_width]
    ue_ref[hist:, :] = u
    n = ts + hist
    w2_ref[8:n, :] = ue_ref[8:n, :] + ue_ref[7:n - 1, :]
    w4_ref[16:n, :] = w2_ref[16:n, :] + w2_ref[14:n - 2, :]
    w8_ref[24:n, :] = w4_ref[24:n, :] + w4_ref[20:n - 4, :]
    w16 = w8_ref[hist:n, :] + w8_ref[hist - 8:n - 8, :]
    group_dim = pool_width // len(POOL_WINDOWS)
    col = _lane_iota((ts, pool_width))
    wsum = jnp.where(col < group_dim, w2_ref[hist:n, :],
                     jnp.where(col < 2 * group_dim, w4_ref[hist:n, :],
                               jnp.where(col < 3 * group_dim, w8_ref[hist:n, :], w16)))
    window = jnp.where(col < group_dim, POOL_WINDOWS[0],
                       jnp.where(col < 2 * group_dim, POOL_WINDOWS[1],
                                 jnp.where(col < 3 * group_dim, POOL_WINDOWS[2], POOL_WINDOWS[3])))
    tpos = s * ts + lax.broadcasted_iota(jnp.int32, (ts, pool_width), 0) + 1
    cnt = jnp.minimum(tpos, window).astype(F32)
    d = wsum / cnt - u
    ue_ref[0:hist, :] = u[ts - hist:, :]
    pooled = _dot(d.astype(BF16), wbd_ref[...]) * pscale_ref[...]
    cat_ref[:, :pool_width] = pooled.astype(BF16)
    _memory_attention(proj, kv_ref, cat_ref, pool_width, pool_width)
    _mix_epilogue(x, cat_ref, w_out_ref, g_ref, b_ref, rwt_ref, rb_ref, tri_ref, dn_alpha, h_out_ref, hp_ref, idx_ref,
                  rank_ref, counts_ref, gcol_ref, base_ref)


def _bias_pieces(c):
    hi = c.astype(BF16).astype(F32)
    r = c - hi
    mid = r.astype(BF16).astype(F32)
    lo = (r - mid).astype(BF16).astype(F32)
    return hi, mid, lo


def _bias_lane_pieces(c3, n_heads):
    hi, mid, lo = _bias_pieces(c3)
    lane = _lane_iota(c3.shape)
    out = jnp.where(lane < n_heads, hi, jnp.where(lane < 2 * n_heads, mid, jnp.where(lane < 3 * n_heads, lo,
                    jnp.where(lane == 3 * n_heads, 1.0, 0.0))))
    return out.astype(BF16)


def _key_bias_selector(n_heads):
    sel = np.zeros((V7X_LANES, n_heads * V7X_LANES), np.float32)
    for h in range(n_heads):
        spare = h * V7X_LANES + HEAD_DIM
        for p in range(3):
            sel[3 * n_heads, spare + p] = 1.0
            sel[p * n_heads + h, spare + 3 + p] = -1.0
    return jnp.asarray(sel, BF16)


def _shared_kv_block(h, kvw_ref, fb_ref, tri_ref, sel_ref, k_ref, vt_ref, cq_ref, carry_ref):
    s = pl.program_id(1)
    v_width = vt_ref.shape[1]
    n_heads = k_ref.shape[1] // V7X_LANES

    @pl.when(s == 0)
    def _():
        carry_ref[...] = jnp.zeros(carry_ref.shape, F32)

    proj = _dot(h.astype(BF16), kvw_ref[...])
    z = proj[:, 2 * v_width:] + fb_ref[...]
    logf = jnp.minimum(z, 0.0) - jnp.log(1.0 + jnp.exp(-jnp.abs(z)))
    hi, mid, lo = _bias_pieces(logf)
    tri = tri_ref[...]
    c = _dot(tri, hi.astype(BF16)) + _dot(tri, mid.astype(BF16)) + _dot(tri, lo.astype(BF16)) + carry_ref[0:1, :]
    ts = c.shape[0]
    carry_ref[0:1, :] = c[ts - 1:ts, :]
    pieces = _bias_lane_pieces(c * LOG2_E, n_heads)
    cq_ref[...] = pieces
    bias_lanes = _dot(pieces, sel_ref[...])
    lane = _lane_iota((ts, V7X_LANES))
    for hd in range(n_heads):
        pair = proj[:, (hd // 2) * V7X_LANES:(hd // 2 + 1) * V7X_LANES]
        k_h = pair if hd % 2 == 0 else pltpu.roll(pair, HEAD_DIM, axis=1)
        cols = slice(hd * V7X_LANES, (hd + 1) * V7X_LANES)
        k_ref[:, cols] = jnp.where(lane < HEAD_DIM, k_h, bias_lanes[:, cols]).astype(BF16)
    vt_ref[0] = proj[:, v_width:2 * v_width].T.astype(BF16)


def _fox_layer_kernel(dn_alpha, h_ref, mem_ref, k_ref, vt_ref, cq_ref, w_in_ref, mem_kv_ref, w_out_ref,
                      g_ref, b_ref, rwt_ref, rb_ref, tri_ref, h_out_ref, hp_ref, idx_ref, rank_ref, counts_ref, gcol_ref,
                      kv_ref, cat_ref, qa_ref, m_ref, l_ref, acc_ref, base_ref):
    i = pl.program_id(1)
    tq = h_ref.shape[0]
    n_heads = k_ref.shape[2] // V7X_LANES
    fox_width = vt_ref.shape[1]

    @pl.when(i == 0)
    def _():
        kv_ref[...] = _dot(mem_ref[0].astype(BF16), mem_kv_ref[...]).astype(BF16)

    x = h_ref[...]
    proj = _dot(x.astype(BF16), w_in_ref[...])
    cq = cq_ref[...].astype(F32)
    lane = _lane_iota((tq, V7X_LANES))
    for h in range(n_heads):
        pair = proj[:, (h // 2) * V7X_LANES:(h // 2 + 1) * V7X_LANES] * (ATTN_SCALE * LOG2_E)
        q_h = pair if h % 2 == 0 else pltpu.roll(pair, HEAD_DIM, axis=1)
        piece = [cq[:, p * n_heads + h:p * n_heads + h + 1] for p in range(3)]
        spare = jnp.where(lane == HEAD_DIM, piece[0], jnp.where(lane == HEAD_DIM + 1, piece[1],
                          jnp.where(lane == HEAD_DIM + 2, piece[2], jnp.where(lane < HEAD_DIM + 6, 1.0, 0.0))))
        qa_ref[h] = jnp.where(lane < HEAD_DIM, q_h, spare).astype(BF16)
    m_ref[...] = jnp.full(m_ref.shape, NEG_INF, F32)
    l_ref[...] = jnp.zeros(l_ref.shape, F32)
    acc_ref[...] = jnp.zeros(acc_ref.shape, F32)
    causal_t = lax.broadcasted_iota(jnp.int32, (tq, tq), 0) <= lax.broadcasted_iota(jnp.int32, (tq, tq), 1)

    ones_rows = (lax.broadcasted_iota(jnp.int32, (V7X_BF16_SUBLANES, tq), 0) == 0).astype(BF16)

    def tile(j, masked):
        start = pl.multiple_of(j * tq, tq)

        def scores(h):
            kb = k_ref[0, pl.ds(start, tq), h * V7X_LANES:(h + 1) * V7X_LANES]
            return _dot_nt(kb, qa_ref[h])

        st_next = scores(0)
        for h in range(n_heads):
            st = st_next
            if h + 1 < n_heads:
                st_next = scores(h + 1)
            if masked:
                st = jnp.where(causal_t, st, NEG_INF)
            m_old = m_ref[h:h + 1, :]
            m_new = jnp.maximum(m_old, jnp.max(st, axis=0, keepdims=True))
            alpha = jnp.exp2(m_old - m_new)
            pt = jnp.exp2(st - m_new).astype(BF16)
            m_ref[h:h + 1, :] = m_new
            rows = slice(h * HEAD_DIM, (h + 1) * HEAD_DIM)
            vb = jnp.concatenate([vt_ref[0, rows, pl.ds(start, tq)], ones_rows], axis=0)
            pv = _dot(vb, pt)
            acc_ref[rows, :] = alpha * acc_ref[rows, :] + pv[:HEAD_DIM]
            l_ref[h:h + 1, :] = alpha * l_ref[h:h + 1, :] + pv[HEAD_DIM:HEAD_DIM + 1]

    tile(i, True)

    def off_diagonal(j, carry):
        tile(j, False)
        return carry

    lax.fori_loop(0, i, off_diagonal, 0)
    for h in range(n_heads):
        rows = slice(h * HEAD_DIM, (h + 1) * HEAD_DIM)
        acc_ref[rows, :] = acc_ref[rows, :] / l_ref[h:h + 1, :]
    cat_ref[:, :fox_width] = acc_ref[...].T.astype(BF16)
    _memory_attention(proj, kv_ref, cat_ref, fox_width, fox_width)
    _mix_epilogue(x, cat_ref, w_out_ref, g_ref, b_ref, rwt_ref, rb_ref, tri_ref, dn_alpha, h_out_ref, hp_ref, idx_ref,
                  rank_ref, counts_ref, gcol_ref, base_ref)


def _expert_kernel(layer, be_ref, bv_ref, bslot_ref, bnext_ref, nact_ref, x_ref, w1_hbm, b1_ref, w2_hbm, b2_ref, y_ref,
                   w1s_ref, w2s_ref, act_ref, w1f_ref, w2f_ref, sem_ref):
    i = pl.program_id(0)
    active = i < nact_ref[0]
    two_f = w1f_ref.shape[2]
    group = 2 * V7X_LANES
    n_groups = two_f // group

    def weight_copies(expert, slot):
        return (pltpu.make_async_copy(w1_hbm.at[layer, expert], w1f_ref.at[slot], sem_ref.at[0, slot]),
                pltpu.make_async_copy(w2_hbm.at[layer, expert], w2f_ref.at[slot], sem_ref.at[1, slot]))

    @pl.when(i == 0)
    def _():
        for cp in weight_copies(be_ref[0], bslot_ref[0]):
            cp.start()

    @pl.when(jnp.logical_and(active, jnp.logical_or(i == 0, be_ref[i] != be_ref[jnp.maximum(i - 1, 0)])))
    def _():
        slot = bslot_ref[i]
        for cp in weight_copies(be_ref[i], slot):
            cp.wait()

        @pl.when(bnext_ref[i] >= 0)
        def _():
            for cp in weight_copies(bnext_ref[i], 1 - slot):
                cp.start()

        src = lax.broadcasted_iota(jnp.int32, (group, group), 0)
        dst = lax.broadcasted_iota(jnp.int32, (group, group), 1)
        want = jnp.where(dst < V7X_LANES, 2 * dst, 2 * (dst - V7X_LANES) + 1)
        perm = (src == want).astype(BF16)
        for g in range(n_groups):
            cols = slice(g * group, (g + 1) * group)
            w1s_ref[:, cols] = _dot(w1f_ref[slot, :, cols].astype(BF16), perm).astype(BF16)
        w2s_ref[...] = w2f_ref[slot].astype(BF16)

    n_valid = bv_ref[i]
    half_rows = x_ref.shape[0] // 2

    def run_rows(rows):
        row_id = lax.broadcasted_iota(jnp.int32, (rows, x_ref.shape[1]), 0)
        x_low, x_high = _unpack_bf16_pairs(jnp.where(row_id < n_valid, x_ref[0:rows, :], jnp.uint32(0)))
        xb = jnp.concatenate([x_low.astype(BF16), x_high.astype(BF16)], axis=1)
        slab_groups = 2
        for s0 in range(0, n_groups, slab_groups):
            cols = slice(s0 * group, (s0 + slab_groups) * group)
            hdn = _dot(xb, w1s_ref[:, cols]) + b1_ref[0, :, cols]
            for g in range(slab_groups):
                x_glu = jnp.minimum(hdn[:, g * group:g * group + V7X_LANES], SWIGLU_LIMIT)
                x_lin = jnp.clip(hdn[:, g * group + V7X_LANES:(g + 1) * group], -SWIGLU_LIMIT, SWIGLU_LIMIT)
                act = x_glu * jax.nn.sigmoid(SWIGLU_ALPHA * x_glu) * (x_lin + 1.0)
                act_ref[0:rows, (s0 + g) * V7X_LANES:(s0 + g + 1) * V7X_LANES] = act.astype(BF16)
        y_ref[0:rows, :] = _pack_bf16_pairs(_dot(act_ref[0:rows, :], w2s_ref[...]) + b2_ref[0])

    @pl.when(jnp.logical_and(active, n_valid > half_rows))
    def _():
        run_rows(x_ref.shape[0])

    quarter_rows = half_rows // 2

    @pl.when(jnp.logical_and(active, jnp.logical_and(n_valid <= half_rows, n_valid > quarter_rows)))
    def _():
        run_rows(half_rows)
        y_ref[half_rows:, :] = jnp.zeros((half_rows, y_ref.shape[1]), jnp.uint32)

    @pl.when(jnp.logical_and(active, n_valid <= quarter_rows))
    def _():
        run_rows(quarter_rows)
        y_ref[quarter_rows:, :] = jnp.zeros((y_ref.shape[0] - quarter_rows, y_ref.shape[1]), jnp.uint32)

    @pl.when(jnp.logical_not(active))
    def _():
        y_ref[...] = jnp.zeros(y_ref.shape, jnp.uint32)


def _combine_rows(dn_alpha, h_ref, y_ref, gcol_ref, g_ref, b_ref):
    gates = gcol_ref[...]
    low, high = _unpack_bf16_pairs(y_ref[0])
    low, high = gates[:, 0:1] * low, gates[:, 0:1] * high
    for k in range(1, TOP_K):
        low_k, high_k = _unpack_bf16_pairs(y_ref[k])
        low = low + gates[:, k:k + 1] * low_k
        high = high + gates[:, k:k + 1] * high_k
    ffn = jnp.concatenate([low, high], axis=1)
    return _layer_norm(dn_alpha * h_ref[...] + ffn, g_ref[...], b_ref[...])


def _combine_kernel(dn_alpha, h_ref, y_ref, gcol_ref, g_ref, b_ref, o_ref):
    o_ref[...] = _combine_rows(dn_alpha, h_ref, y_ref, gcol_ref, g_ref, b_ref)


def _combine_kv_kernel(dn_alpha, h_ref, y_ref, gcol_ref, g_ref, b_ref, kvw_ref, fb_ref, tri_ref, sel_ref,
                       o_ref, k_ref, vt_ref, cq_ref, carry_ref):
    h2 = _combine_rows(dn_alpha, h_ref, y_ref, gcol_ref, g_ref, b_ref)
    o_ref[...] = h2
    _shared_kv_block(h2, kvw_ref, fb_ref, tri_ref, sel_ref, k_ref, vt_ref, cq_ref, carry_ref)


def _sc_gather_rows(table, idx):
    n, = idx.shape
    _, d = table.shape
    info = plsc.get_sparse_core_info()
    n_workers = info.num_cores * info.num_subcores
    per_worker = n // n_workers
    n_chunks = per_worker // SC_CHUNK
    assert n % (n_workers * SC_CHUNK) == 0 and n_chunks % 2 == 0
    mesh = plsc.VectorSubcoreMesh(core_axis_name="core", subcore_axis_name="subcore")

    @functools.partial(
        pl.kernel,
        out_type=jax.ShapeDtypeStruct((n, d), table.dtype),
        mesh=mesh,
        scratch_types=[
            pltpu.VMEM((per_worker,), jnp.int32),
            pltpu.VMEM((2, SC_CHUNK, d), table.dtype),
            pltpu.SemaphoreType.DMA((2,)),
            pltpu.SemaphoreType.DMA((2,)),
        ],
    )
    def gather_kernel(table_hbm, idx_hbm, out_hbm, idx_v, rows_v, gather_sem, write_sem):
        worker = lax.axis_index("subcore") * info.num_cores + lax.axis_index("core")
        base = pl.multiple_of(worker * per_worker, SC_CHUNK)
        pltpu.sync_copy(idx_hbm.at[pl.ds(base, per_worker)], idx_v)

        def gather(c, buf):
            rows = idx_v.at[pl.ds(pl.multiple_of(c * SC_CHUNK, SC_CHUNK), SC_CHUNK)]
            return pltpu.make_async_copy(table_hbm.at[rows], rows_v.at[buf], gather_sem.at[buf])

        def write(c, buf):
            off = pl.multiple_of(base + c * SC_CHUNK, SC_CHUNK)
            return pltpu.make_async_copy(rows_v.at[buf], out_hbm.at[pl.ds(off, SC_CHUNK)], write_sem.at[buf])

        gather(0, 0).start()

        @pl.loop(0, n_chunks, step=2)
        def _(c0):
            for buf in range(2):
                c = c0 + buf
                gather(c, buf).wait()

                @pl.when(c >= 1)
                def _():
                    write(c - 1, 1 - buf).wait()

                @pl.when(c + 1 < n_chunks)
                def _():
                    gather(c + 1, 1 - buf).start()

                write(c, buf).start()

        write(n_chunks - 1, 1).wait()

    return gather_kernel(table, idx)


def _sc_scatter_rows(rows, pos, n_out):
    t, d = rows.shape
    top_k = pos.shape[0]
    pos_flat = pos.reshape(-1)
    info = plsc.get_sparse_core_info()
    n_workers = info.num_cores * info.num_subcores
    per_worker = t // n_workers
    n_chunks = per_worker // SC_CHUNK
    assert t % (n_workers * SC_CHUNK) == 0 and n_chunks % 2 == 0
    mesh = plsc.VectorSubcoreMesh(core_axis_name="core", subcore_axis_name="subcore")

    @functools.partial(
        pl.kernel,
        out_type=jax.ShapeDtypeStruct((n_out, d), rows.dtype),
        mesh=mesh,
        scratch_types=[pltpu.VMEM((2, SC_CHUNK, d), rows.dtype)]
        + [pltpu.VMEM((SC_CHUNK,), jnp.int32) for _ in range(2 * top_k)]
        + [pltpu.SemaphoreType.DMA((2,)), pltpu.SemaphoreType.DMA((2,))],
    )
    def scatter_kernel(rows_hbm, pos_hbm, out_hbm, rows_v, *rest):
        idx_bufs = rest[:2 * top_k]
        load_sem, write_sem = rest[2 * top_k:]
        worker = lax.axis_index("subcore") * info.num_cores + lax.axis_index("core")
        base = pl.multiple_of(worker * per_worker, SC_CHUNK)

        def loads(c, buf):
            off = pl.multiple_of(base + c * SC_CHUNK, SC_CHUNK)
            copies = [pltpu.make_async_copy(rows_hbm.at[pl.ds(off, SC_CHUNK)], rows_v.at[buf], load_sem.at[buf])]
            for k in range(top_k):
                src = pos_hbm.at[pl.ds(pl.multiple_of(k * t + off, SC_CHUNK), SC_CHUNK)]
                copies.append(pltpu.make_async_copy(src, idx_bufs[buf * top_k + k], load_sem.at[buf]))
            return copies

        def scatters(buf):
            return [pltpu.make_async_copy(rows_v.at[buf], out_hbm.at[idx_bufs[buf * top_k + k]], write_sem.at[buf])
                    for k in range(top_k)]

        for cp in loads(0, 0):
            cp.start()

        @pl.loop(0, n_chunks, step=2)
        def _(c0):
            for buf in range(2):
                c = c0 + buf
                for cp in loads(c, buf):
                    cp.wait()

                @pl.when(c >= 1)
                def _():
                    for cp in scatters(1 - buf):
                        cp.wait()

                @pl.when(c + 1 < n_chunks)
                def _():
                    for cp in loads(c + 1, 1 - buf):
                        cp.start()

                for cp in scatters(buf):
                    cp.start()

        for cp in scatters(1):
            cp.wait()

    return scatter_kernel(rows, pos_flat)


def _const_spec(shape):
    zeros = (0,) * len(shape)
    return pl.BlockSpec(shape, lambda *_: zeros, pipeline_mode=pl.Buffered(1))


def _tc_params(n_axes):
    return pltpu.CompilerParams(dimension_semantics=("arbitrary",) * n_axes,
                                vmem_limit_bytes=V7X_VMEM_LIMIT_BYTES)


def _mix_out_shapes(t, d, n_exp):
    return (jax.ShapeDtypeStruct((t, d), F32),
            jax.ShapeDtypeStruct((t, d // 2), jnp.uint32),
            jax.ShapeDtypeStruct((TOP_K, t), jnp.int32),
            jax.ShapeDtypeStruct((TOP_K, t), jnp.int32),
            jax.ShapeDtypeStruct((n_exp, V7X_LANES), F32),
            jax.ShapeDtypeStruct((t, V7X_LANES), F32))


def _mix_out_specs(rows, d, n_exp, row, tok_col):
    return (pl.BlockSpec((rows, d), row),
            pl.BlockSpec((rows, d // 2), row),
            pl.BlockSpec((TOP_K, rows), tok_col),
            pl.BlockSpec((TOP_K, rows), tok_col),
            pl.BlockSpec((n_exp, V7X_LANES), lambda *_: (0, 0)),
            pl.BlockSpec((rows, V7X_LANES), row))


def _strict_upper_tri(n):
    return (lax.broadcasted_iota(jnp.int32, (n, n), 0) < lax.broadcasted_iota(jnp.int32, (n, n), 1)).astype(BF16)


def _pool_layer(h, mem, w_in, wbd, pscale, mem_kv, w_out, ln_g, ln_b, rwt, rb, dn_alpha, seq):
    t, d = h.shape
    n_mem = mem.shape[1]
    pool_width = wbd.shape[0]
    ts = SEQ_BLOCK
    nsb = seq // ts
    row = lambda b, s: (b * nsb + s, 0)
    n_exp = rwt.shape[0]
    tri = _strict_upper_tri(ts)
    return pl.pallas_call(
        functools.partial(_pool_layer_kernel, dn_alpha),
        out_shape=_mix_out_shapes(t, d, n_exp),
        grid=(t // seq, nsb),
        in_specs=[
            pl.BlockSpec((ts, d), row),
            pl.BlockSpec((1, n_mem, d), lambda b, s: (b, 0, 0)),
            _const_spec(w_in.shape), _const_spec(wbd.shape), _const_spec(pscale.shape), _const_spec(mem_kv.shape),
            _const_spec(w_out.shape), _const_spec(ln_g.shape), _const_spec(ln_b.shape), _const_spec(rwt.shape),
            _const_spec(rb.shape), _const_spec(tri.shape),
        ],
        out_specs=_mix_out_specs(ts, d, n_exp, row, lambda b, s: (0, b * nsb + s)),
        scratch_shapes=[pltpu.VMEM((ts + POOL_HISTORY, pool_width), F32)] * 4
        + [pltpu.VMEM((n_mem, 2 * MEM_WIDTH), BF16), pltpu.VMEM((ts, d), BF16), pltpu.VMEM((n_exp, V7X_LANES), F32)],
        compiler_params=_tc_params(2),
        name="pool_layer",
    )(h, mem, w_in, wbd, pscale, mem_kv, w_out, ln_g, ln_b, rwt, rb, tri)


def _fox_layer(h, mem, k_aug, v_t, cq, w_in, mem_kv, w_out, ln_g, ln_b, rwt, rb, dn_alpha, seq):
    t, d = h.shape
    n_mem = mem.shape[1]
    k_width = k_aug.shape[1]
    v_width = v_t.shape[1]
    n_heads = k_width // V7X_LANES
    stat_rows = -(-n_heads // V7X_SUBLANES) * V7X_SUBLANES
    tq = Q_BLOCK
    nqb = seq // tq
    n_batch = t // seq
    row = lambda b, i: (b * nqb + i, 0)
    per_batch = lambda b, i: (b, 0, 0)
    n_exp = rwt.shape[0]
    tri = _strict_upper_tri(tq)
    return pl.pallas_call(
        functools.partial(_fox_layer_kernel, dn_alpha),
        out_shape=_mix_out_shapes(t, d, n_exp),
        grid=(n_batch, nqb),
        in_specs=[
            pl.BlockSpec((tq, d), row),
            pl.BlockSpec((1, n_mem, d), per_batch),
            pl.BlockSpec((1, seq, k_width), per_batch),
            pl.BlockSpec((1, v_width, seq), per_batch),
            pl.BlockSpec((tq, V7X_LANES), row),
            _const_spec(w_in.shape), _const_spec(mem_kv.shape), _const_spec(w_out.shape), _const_spec(ln_g.shape),
            _const_spec(ln_b.shape), _const_spec(rwt.shape), _const_spec(rb.shape), _const_spec(tri.shape),
        ],
        out_specs=_mix_out_specs(tq, d, n_exp, row, lambda b, i: (0, b * nqb + i)),
        scratch_shapes=[pltpu.VMEM((n_mem, 2 * MEM_WIDTH), BF16), pltpu.VMEM((tq, d), BF16),
                        pltpu.VMEM((n_heads, tq, V7X_LANES), BF16),
                        pltpu.VMEM((stat_rows, tq), F32), pltpu.VMEM((stat_rows, tq), F32),
                        pltpu.VMEM((v_width, tq), F32),
                        pltpu.VMEM((n_exp, V7X_LANES), F32)],
        compiler_params=_tc_params(2),
        name="fox_layer",
    )(h, mem, k_aug.reshape(n_batch, seq, k_width), v_t, cq, w_in, mem_kv, w_out, ln_g, ln_b, rwt, rb, tri)


def _slot_kernel(pad_start_ref, idx_ref, rank_ref, pos_ref):
    pos = rank_ref[...]
    idx = idx_ref[...]
    for e in range(pad_start_ref.shape[0]):
        pos = pos + jnp.where(idx == e, pad_start_ref[e], 0)
    pos_ref[...] = pos


def _route(idx_t, rank_t, counts, n_experts):
    top_k, t = idx_t.shape
    n_assign = top_k * t
    padded = ((counts + MOE_BLOCK - 1) // MOE_BLOCK) * MOE_BLOCK
    pad_end = jnp.cumsum(padded)
    pad_start = pad_end - padded
    n_blocks = n_assign // MOE_BLOCK + n_experts
    block_start = jnp.arange(n_blocks, dtype=jnp.int32) * MOE_BLOCK
    block_expert = jnp.minimum(jnp.sum((pad_end[None, :] <= block_start[:, None]).astype(jnp.int32), axis=1),
                               n_experts - 1)
    block_valid = jnp.clip(counts[block_expert] - (block_start - pad_start[block_expert]), 0, MOE_BLOCK)
    n_active = (pad_end[-1:] // MOE_BLOCK).astype(jnp.int32)
    expert_ids = jnp.arange(n_experts, dtype=jnp.int32)
    in_use = counts > 0
    expert_slot = (jnp.cumsum(in_use.astype(jnp.int32)) - in_use.astype(jnp.int32)) % 2
    later = lax.cummin(jnp.where(in_use, expert_ids, n_experts), axis=0, reverse=True)
    next_in_use = jnp.concatenate([later[1:], jnp.full((1,), n_experts, jnp.int32)])
    next_in_use = jnp.where(next_in_use < n_experts, next_in_use, -1)
    block_slot = expert_slot[block_expert].astype(jnp.int32)
    block_next = next_in_use[block_expert].astype(jnp.int32)
    cols = min(SLOT_COLS, t)
    pos = pl.pallas_call(
        _slot_kernel,
        out_shape=jax.ShapeDtypeStruct((top_k, t), jnp.int32),
        grid_spec=pltpu.PrefetchScalarGridSpec(
            num_scalar_prefetch=1,
            grid=(t // cols,),
            in_specs=[pl.BlockSpec((top_k, cols), lambda i, ps: (0, i))] * 2,
            out_specs=pl.BlockSpec((top_k, cols), lambda i, ps: (0, i)),
        ),
        compiler_params=_tc_params(1),
        name="slots",
    )(pad_start.astype(jnp.int32), idx_t, rank_t)
    return pos, (block_expert, block_valid.astype(jnp.int32), block_slot, block_next, n_active)


def _experts(xs, block_expert, block_valid, block_slot, block_next, n_active, w1_all, b1, w2_all, b2, layer):
    n_slots, half_d = xs.shape
    _, n_exp, d, two_f = w1_all.shape
    d_ff = two_f // 2
    assert d == 2 * half_d
    block = lambda i, *_: (i, 0)
    per_expert = lambda i, be, *_: (be[i], 0, 0)
    grid_spec = pltpu.PrefetchScalarGridSpec(
        num_scalar_prefetch=5,
        grid=(n_slots // MOE_BLOCK,),
        in_specs=[
            pl.BlockSpec((MOE_BLOCK, half_d), block),
            pl.BlockSpec(memory_space=pl.ANY),
            pl.BlockSpec((1, 1, two_f), per_expert),
            pl.BlockSpec(memory_space=pl.ANY),
            pl.BlockSpec((1, 1, d), per_expert),
        ],
        out_specs=pl.BlockSpec((MOE_BLOCK, half_d), block),
        scratch_shapes=[pltpu.VMEM((d, two_f), BF16), pltpu.VMEM((d_ff, d), BF16), pltpu.VMEM((MOE_BLOCK, d_ff), BF16),
                        pltpu.VMEM((2, d, two_f), F32), pltpu.VMEM((2, d_ff, d), F32),
                        pltpu.SemaphoreType.DMA((2, 2))],
    )
    return pl.pallas_call(
        functools.partial(_expert_kernel, layer),
        out_shape=jax.ShapeDtypeStruct((n_slots, half_d), jnp.uint32),
        grid_spec=grid_spec,
        compiler_params=_tc_params(1),
        name="experts",
    )(block_expert, block_valid, block_slot, block_next, n_active, xs, w1_all, b1.reshape(n_exp, 1, two_f), w2_all,
      b2.reshape(n_exp, 1, d))


def _combine(h, yg, gcol, ln_g, ln_b, dn_alpha, seq, shared_kv_weights=None):
    t, d = h.shape
    ts = SEQ_BLOCK
    nsb = seq // ts
    row = lambda b, s: (b * nsb + s, 0)
    in_specs = [pl.BlockSpec((ts, d), row), pl.BlockSpec((TOP_K, ts, d // 2), lambda b, s: (0, b * nsb + s, 0)),
                pl.BlockSpec((ts, V7X_LANES), row), _const_spec(ln_g.shape), _const_spec(ln_b.shape)]
    args = (h, yg.reshape(TOP_K, t, d // 2), gcol, ln_g, ln_b)
    if shared_kv_weights is None:
        return pl.pallas_call(
            functools.partial(_combine_kernel, dn_alpha),
            out_shape=jax.ShapeDtypeStruct((t, d), F32),
            grid=(t // seq, nsb),
            in_specs=in_specs,
            out_specs=pl.BlockSpec((ts, d), row),
            compiler_params=_tc_params(2),
            name="combine",
        )(*args), None
    kvw, fb, sel_k, n_heads = shared_kv_weights
    k_width = n_heads * V7X_LANES
    v_width = n_heads * HEAD_DIM
    tri = (lax.broadcasted_iota(jnp.int32, (ts, ts), 1) <= lax.broadcasted_iota(jnp.int32, (ts, ts), 0)).astype(BF16)
    out = pl.pallas_call(
        functools.partial(_combine_kv_kernel, dn_alpha),
        out_shape=(jax.ShapeDtypeStruct((t, d), F32),
                   jax.ShapeDtypeStruct((t, k_width), BF16),
                   jax.ShapeDtypeStruct((t // seq, v_width, seq), BF16),
                   jax.ShapeDtypeStruct((t, V7X_LANES), BF16)),
        grid=(t // seq, nsb),
        in_specs=in_specs + [_const_spec(kvw.shape), _const_spec(fb.shape), _const_spec(tri.shape),
                             _const_spec(sel_k.shape)],
        out_specs=(pl.BlockSpec((ts, d), row),
                   pl.BlockSpec((ts, k_width), row),
                   pl.BlockSpec((1, v_width, ts), lambda b, s: (b, 0, s)),
                   pl.BlockSpec((ts, V7X_LANES), row)),
        scratch_shapes=[pltpu.VMEM((V7X_SUBLANES, V7X_LANES), F32)],
        compiler_params=_tc_params(2),
        name="combine_shared_kv",
    )(*args, kvw, fb, tri, sel_k)
    return out[0], out[1:]


def _moe(mixed, w1_all, b1, w2_all, b2, layer, ln_g, ln_b, dn_alpha, seq, shared_kv_weights):
    h1, h1_packed, idx_t, rank_t, counts, gcol = mixed
    n_exp = w1_all.shape[1]
    d_ff = w2_all.shape[2]
    pos, block_meta = _route(idx_t, rank_t, counts[:, 0].astype(jnp.int32), n_exp)
    b1p = jnp.swapaxes(b1.reshape(n_exp, d_ff // V7X_LANES, V7X_LANES, 2), 2, 3).reshape(n_exp, 2 * d_ff)
    xs = _sc_scatter_rows(h1_packed, pos, block_meta[0].shape[0] * MOE_BLOCK)
    y = _experts(xs, *block_meta, w1_all, b1p, w2_all, b2, layer)
    yg = _sc_gather_rows(y, pos.reshape(-1))
    return _combine(h1, yg, gcol, ln_g, ln_b, dn_alpha, seq, shared_kv_weights)


def kernel(x, mem, ln1_g, ln1_b, ln2_g, ln2_b, a_w_in, pool_w, pool_scale, a_mem_kv, a_w_out, kv_w, fgate_b, b_w_in, b_mem_kv, b_w_out, router_w, router_b, moe_w1, moe_b1, moe_w2, moe_b2):
    n_batch, seq, d = x.shape
    t = n_batch * seq
    depth = ln1_g.shape[0]
    n_a = a_w_in.shape[0]
    dn_alpha = float((2 * depth) ** 0.25)
    n_exp = router_w.shape[2]
    fox_heads = fgate_b.shape[0]
    fox_width = fox_heads * HEAD_DIM
    assert seq % SEQ_BLOCK == 0 and seq % Q_BLOCK == 0
    assert 3 * fox_heads < V7X_LANES

    assert 1 <= n_a < depth
    pad = V7X_LANES - 3 * fox_heads
    kvw = jnp.concatenate([kv_w[:, :2 * fox_width],
                           jnp.pad(jnp.tile(kv_w[:, 2 * fox_width:], (1, 3)), ((0, 0), (0, pad)))],
                          axis=1).astype(BF16)
    fb = jnp.pad(jnp.tile(fgate_b, 3), (0, pad)).reshape(1, V7X_LANES)
    shared_kv_weights = (kvw, fb, _key_bias_selector(fox_heads), fox_heads)

    h = x.reshape(t, d)
    row = lambda a: a.reshape(1, -1)
    shared = None
    for l in range(depth):
        rwt = router_w[l].T
        rb = router_b[l].reshape(n_exp, 1)
        if l < n_a:
            wbd = jax.scipy.linalg.block_diag(*[pool_w[l, g] for g in range(pool_w.shape[1])]).astype(BF16)
            mixed = _pool_layer(
                h, mem, a_w_in[l].astype(BF16), wbd, row(pool_scale[l]), a_mem_kv[l].astype(BF16),
                a_w_out[l].astype(BF16), row(ln1_g[l]), row(ln1_b[l]), rwt, rb, dn_alpha, seq)
        else:
            j = l - n_a
            k_aug, v_t, cq = shared
            mixed = _fox_layer(
                h, mem, k_aug, v_t, cq, b_w_in[j].astype(BF16), b_mem_kv[j].astype(BF16),
                b_w_out[j].astype(BF16), row(ln1_g[l]), row(ln1_b[l]), rwt, rb, dn_alpha, seq)
        h, new_shared = _moe(mixed, moe_w1, moe_b1[l], moe_w2, moe_b2[l], l, row(ln2_g[l]), row(ln2_b[l]), dn_alpha,
                             seq, shared_kv_weights if l == n_a - 1 else None)
        if new_shared is not None:
            shared = new_shared
    return h.reshape(n_batch, seq, d)
```
